```python
import jax, jax.numpy as jnp
from jax import lax
import numpy as np

D_MODEL = 1024
BATCH = 2
SEQ = 8192
DEPTH = 4

N_MIXERS = 3
CHUNK = 128
GM_WIDTH = 2 * D_MODEL
GM_GROUPS = 8
GM_GROUP_DIM = GM_WIDTH // GM_GROUPS
FOX_HEADS = 16
FOX_HEAD_DIM = D_MODEL // FOX_HEADS
Q_BLOCK = 128
POOL_WINDOWS = (2, 4, 8, 16)
POOL_GROUPS = len(POOL_WINDOWS)
POOL_WIDTH = D_MODEL
POOL_GROUP_DIM = POOL_WIDTH // POOL_GROUPS
D_FF = 7 * D_MODEL // 2
N_EXPERTS = 8
TOP_K = 2
EXPERT_BLOCK = 512
DN_ALPHA = (2 * DEPTH) ** 0.25
DN_BETA = (8 * DEPTH) ** -0.25
LN_EPS = 1e-5
MOD_SCALE = 0.25

N_GM = (DEPTH + 2) // 3
N_FOX = (DEPTH + 1) // 3
N_POOL = DEPTH // 3
N_DENSE = (DEPTH + 1) // 2
N_MOE = DEPTH // 2

kernel_name = 'hybrid_gmlp_fox_pool_moe_deepnorm_adaln'


def layer_norm(x, g, b):
    xf = x.astype(jnp.float32)
    mu = jnp.mean(xf, axis=-1, keepdims=True)
    var = jnp.mean(jnp.square(xf - mu), axis=-1, keepdims=True)
    y = (xf - mu) * lax.rsqrt(var + LN_EPS)
    return (y * g.astype(jnp.float32) + b.astype(jnp.float32)).astype(x.dtype)


def gmlp_chunk_mixer(h, w_in, ln_g, ln_b, w_s, b_s, w_out):
    B_, S_, _ = h.shape
    z = jax.nn.gelu(h @ w_in, approximate=False)
    u, v = jnp.split(z, 2, axis=-1)
    v = layer_norm(v, ln_g, ln_b)
    v = v.reshape(B_, S_ // CHUNK, CHUNK, GM_GROUPS, GM_GROUP_DIM)
    causal = jnp.tril(jnp.ones((CHUNK, CHUNK), dtype=bool))
    w = jnp.where(causal[None], w_s, 0).astype(v.dtype)
    mixed = jnp.einsum('gts,bnsgc->bntgc', w, v) + b_s.T[:, :, None].astype(v.dtype)
    mixed = mixed.reshape(B_, S_, GM_WIDTH)
    return (u * mixed) @ w_out


def forgetting_attention(h, w_in, b_f, w_out):
    B_, S_, _ = h.shape
    proj = h @ w_in
    q, k, v, f = jnp.split(proj, [D_MODEL, 2 * D_MODEL, 3 * D_MODEL], axis=-1)

    def to_heads(t):
        return t.reshape(B_, S_, FOX_HEADS, FOX_HEAD_DIM).transpose(0, 2, 1, 3)

    q, k, v = to_heads(q), to_heads(k), to_heads(v)
    log_f = jax.nn.log_sigmoid((f + b_f).astype(jnp.float32))
    cum = jnp.cumsum(log_f, axis=1).transpose(0, 2, 1)
    scale = FOX_HEAD_DIM ** -0.5
    key_pos = jnp.arange(S_)

    def query_block(i):
        start = i * Q_BLOCK
        qb = lax.dynamic_slice_in_dim(q, start, Q_BLOCK, axis=2)
        cq = lax.dynamic_slice_in_dim(cum, start, Q_BLOCK, axis=2)
        s = jnp.einsum('bhqd,bhkd->bhqk', qb, k).astype(jnp.float32) * scale
        s = s + cq[..., :, None] - cum[..., None, :]
        q_pos = start + jnp.arange(Q_BLOCK)
        mask = key_pos[None, :] <= q_pos[:, None]
        p = jax.nn.softmax(jnp.where(mask, s, -jnp.inf), axis=-1)
        return jnp.einsum('bhqk,bhkd->bhqd', p.astype(v.dtype), v)

    out = lax.map(query_block, jnp.arange(S_ // Q_BLOCK))
    out = out.transpose(1, 0, 3, 2, 4).reshape(B_, S_, D_MODEL)
    return out @ w_out


def multiscale_pool_mixer(h, w_in, w_grp, layer_scale, w_out):
    B_, S_, _ = h.shape
    z = h @ w_in
    zf = z.astype(jnp.float32)
    cs = jnp.pad(jnp.cumsum(zf, axis=1), ((0, 0), (1, 0), (0, 0)))
    pos = jnp.arange(S_)
    outs = []
    for g, win in enumerate(POOL_WINDOWS):
        lo_c, hi_c = g * POOL_GROUP_DIM, (g + 1) * POOL_GROUP_DIM
        cs_g = cs[..., lo_c:hi_c]
        lo = jnp.maximum(pos + 1 - win, 0)
        window_sum = cs_g[:, 1:, :] - cs_g[:, lo, :]
        count = jnp.minimum(pos + 1, win).astype(jnp.float32)
        pooled = window_sum / count[None, :, None] - zf[..., lo_c:hi_c]
        outs.append(jnp.einsum('bsc,cd->bsd', pooled.astype(z.dtype), w_grp[g]))
    y = jnp.concatenate(outs, axis=-1) * layer_scale
    return y @ w_out


def swiglu(h, w13, w2):
    a, b = jnp.split(h @ w13, 2, axis=-1)
    return (jax.nn.silu(a) * b) @ w2


def moe_swiglu(h, w_router, b_router, w13, w2):
    B_, S_, D = h.shape
    xt = h.reshape(-1, D)
    T = xt.shape[0]
    logits = (xt @ w_router).astype(jnp.float32) + b_router.astype(jnp.float32)
    top_val, top_idx = lax.top_k(logits, TOP_K)
    gates = jax.nn.softmax(top_val, axis=-1)
    n_assign = T * TOP_K
    e_flat = top_idx.reshape(-1).astype(jnp.int32)
    tok_flat = jnp.repeat(jnp.arange(T, dtype=jnp.int32), TOP_K)
    g_flat = gates.reshape(-1)
    order = jnp.argsort(e_flat * n_assign + jnp.arange(n_assign, dtype=jnp.int32))
    e_sorted, tok_sorted, g_sorted = e_flat[order], tok_flat[order], g_flat[order]
    counts = jnp.bincount(e_flat, length=N_EXPERTS)
    padded = (counts + EXPERT_BLOCK - 1) // EXPERT_BLOCK * EXPERT_BLOCK
    pad_end = jnp.cumsum(padded)
    pad_start = pad_end - padded
    raw_start = jnp.cumsum(counts) - counts
    dest = pad_start[e_sorted] + (jnp.arange(n_assign) - raw_start[e_sorted])
    capacity = (n_assign + EXPERT_BLOCK - 1) // EXPERT_BLOCK * EXPERT_BLOCK + N_EXPERTS * EXPERT_BLOCK
    n_blocks = capacity // EXPERT_BLOCK
    slot_tok = jnp.full((capacity,), T, dtype=jnp.int32).at[dest].set(tok_sorted)
    x_pad = jnp.concatenate([xt, jnp.zeros((1, D), xt.dtype)], axis=0)
    x_slots = x_pad[slot_tok].reshape(n_blocks, EXPERT_BLOCK, D)
    block_expert = jnp.minimum(
        jnp.searchsorted(pad_end, jnp.arange(n_blocks) * EXPERT_BLOCK, side='right'),
        N_EXPERTS - 1)

    def expert_block(args):
        xb, e = args
        return swiglu(xb, w13[e], w2[e])

    y_slots = lax.map(expert_block, (x_slots, block_expert)).reshape(capacity, D)
    y = jnp.zeros_like(xt).at[tok_sorted].add(y_slots[dest] * g_sorted[:, None].astype(xt.dtype))
    return y.reshape(B_, S_, D)


def setup_inputs(seed: int = 0) -> dict:
    key = jax.random.key(seed)
    ks = jax.random.split(key, 24)
    nrm = jax.random.normal
    D, F, E = D_MODEL, D_FF, N_EXPERTS
    return {
        'x': nrm(ks[0], (BATCH, SEQ, D), jnp.float32),
        'c': nrm(ks[1], (BATCH, D), jnp.float32),
        'mod_w': nrm(ks[2], (DEPTH, D, 6 * D), jnp.float32) * (MOD_SCALE * D ** -0.5),
        'mod_b': nrm(ks[3], (DEPTH, 6 * D), jnp.float32) * 0.02,
        'ln_g': 1.0 + 0.02 * nrm(ks[4], (DEPTH, 2, D), jnp.float32),
        'ln_b': 0.02 * nrm(ks[5], (DEPTH, 2, D), jnp.float32),
        'gm_w_in': nrm(ks[6], (N_GM, D, 2 * GM_WIDTH), jnp.float32) * D ** -0.5,
        'gm_ln_g': 1.0 + 0.02 * nrm(ks[7], (N_GM, GM_WIDTH), jnp.float32),
        'gm_ln_b': 0.02 * nrm(ks[8], (N_GM, GM_WIDTH), jnp.float32),
        'gm_w_s': nrm(ks[9], (N_GM, GM_GROUPS, CHUNK, CHUNK), jnp.float32) * CHUNK ** -0.5,
        'gm_b_s': 1.0 + 0.1 * nrm(ks[10], (N_GM, GM_GROUPS, CHUNK), jnp.float32),
        'gm_w_out': nrm(ks[11], (N_GM, GM_WIDTH, D), jnp.float32) * (DN_BETA * GM_WIDTH ** -0.5),
        'fox_w_in': nrm(ks[12], (N_FOX, D, 3 * D + FOX_HEADS), jnp.float32) * D ** -0.5,
        'fox_b_f': jax.random.uniform(ks[13], (N_FOX, FOX_HEADS), jnp.float32, 1.0, 4.0),
        'fox_w_out': nrm(ks[14], (N_FOX, D, D), jnp.float32) * (DN_BETA * D ** -0.5),
        'pool_w_in': nrm(ks[15], (N_POOL, D, POOL_WIDTH), jnp.float32) * D ** -0.5,
        'pool_w_grp': nrm(ks[16], (N_POOL, POOL_GROUPS, POOL_GROUP_DIM, POOL_GROUP_DIM), jnp.float32) * POOL_GROUP_DIM ** -0.5,
        'pool_scale': 1.0 + 0.1 * nrm(ks[17], (N_POOL, POOL_WIDTH), jnp.float32),
        'pool_w_out': nrm(ks[18], (N_POOL, POOL_WIDTH, D), jnp.float32) * (DN_BETA * POOL_WIDTH ** -0.5),
        'ffn_w13': nrm(ks[19], (N_DENSE, D, 2 * F), jnp.float32) * D ** -0.5,
        'ffn_w2': nrm(ks[20], (N_DENSE, F, D), jnp.float32) * (DN_BETA * F ** -0.5),
        'moe_w_router': nrm(ks[21], (N_MOE, D, E), jnp.float32) * D ** -0.5,
        'moe_b_router': 0.01 * nrm(ks[22], (N_MOE, E), jnp.float32),
        'moe_w13': nrm(ks[23], (N_MOE, E, D, 2 * F), jnp.float32) * D ** -0.5,
        'moe_w2': nrm(jax.random.fold_in(ks[23], 1), (N_MOE, E, F, D), jnp.float32) * (DN_BETA * F ** -0.5),
    }


def reference(x, c, mod_w, mod_b, ln_g, ln_b, gm_w_in, gm_ln_g, gm_ln_b, gm_w_s, gm_b_s, gm_w_out,
              fox_w_in, fox_b_f, fox_w_out, pool_w_in, pool_w_grp, pool_scale, pool_w_out,
              ffn_w13, ffn_w2, moe_w_router, moe_b_router, moe_w13, moe_w2):
    mod = jnp.einsum('bd,ldm->lbm', jax.nn.silu(c), mod_w) + mod_b[:, None, :]
    for i in range(DEPTH):
        sh1, sc1, g1, sh2, sc2, g2 = jnp.split(mod[i][:, None, :], 6, axis=-1)
        h = x * (1 + sc1) + sh1
        kind, j = i % N_MIXERS, i // N_MIXERS
        if kind == 0:
            y = gmlp_chunk_mixer(h, gm_w_in[j], gm_ln_g[j], gm_ln_b[j], gm_w_s[j], gm_b_s[j], gm_w_out[j])
        elif kind == 1:
            y = forgetting_attention(h, fox_w_in[j], fox_b_f[j], fox_w_out[j])
        else:
            y = multiscale_pool_mixer(h, pool_w_in[j], pool_w_grp[j], pool_scale[j], pool_w_out[j])
        x = layer_norm(DN_ALPHA * x + (1 + g1) * y, ln_g[i, 0], ln_b[i, 0])
        h = x * (1 + sc2) + sh2
        if i % 2 == 0:
            y = swiglu(h, ffn_w13[i // 2], ffn_w2[i // 2])
        else:
            y = moe_swiglu(h, moe_w_router[i // 2], moe_b_router[i // 2], moe_w13[i // 2], moe_w2[i // 2])
        x = layer_norm(DN_ALPHA * x + (1 + g2) * y, ln_g[i, 1], ln_b[i, 1])
    return x
```

```python
import functools

import numpy as np
import jax
import jax.numpy as jnp
from jax import lax
from jax.experimental import pallas as pl
from jax.experimental.pallas import tpu as pltpu

F32 = jnp.float32
BF16 = jnp.bfloat16

POOL_WINDOWS = (2, 4, 8, 16)
TOP_K = 2
EXPERT_BLOCK = 512
LN_EPS = 1e-5
LANES = 128
POOL_HALO = 16
VMEM_LIMIT = 56 * 1024 * 1024


def _cparams(sem):
    return pltpu.CompilerParams(dimension_semantics=sem, vmem_limit_bytes=VMEM_LIMIT)


def _const_spec(shape):
    nd = len(shape)
    return pl.BlockSpec(shape, lambda *_: (0,) * nd, pipeline_mode=pl.Buffered(1))


def _layer_norm(r, g, b):
    mu = jnp.mean(r, axis=-1, keepdims=True)
    xc = r - mu
    var = jnp.mean(xc * xc, axis=-1, keepdims=True)
    return xc * lax.rsqrt(var + LN_EPS) * g + b


def _modulate(x, m, off):
    return x * (1.0 + m[off + 1:off + 2]) + m[off:off + 1]


def _res_ln(x, y, m, off, g, b, alpha):
    return _layer_norm(alpha * x + (1.0 + m[off + 2:off + 3]) * y, g, b)


def _split3(a):
    hi = a.astype(BF16)
    r1 = a - hi.astype(F32)
    mid = r1.astype(BF16)
    lo = (r1 - mid.astype(F32)).astype(BF16)
    return hi, mid, lo


def _dot(a, b):
    return jnp.dot(a, b, preferred_element_type=F32)


def _mod_kernel(c_ref, w_ref, b_ref, o_ref):
    c = c_ref[...]
    s = c / (1.0 + jnp.exp(-c))
    s_hi, s_mid, _ = _split3(s)
    w = w_ref[0]
    w_hi = w.astype(BF16)
    w_lo = (w - w_hi.astype(F32)).astype(BF16)
    acc = _dot(s_hi, w_hi) + _dot(s_mid, w_hi) + _dot(s_hi, w_lo)
    o_ref[0] = acc + b_ref[0]


def _modulation(c, mod_w, mod_b):
    depth, d, n = mod_w.shape
    bsz = c.shape[0]
    rows = 8
    tn = n // 4
    c_pad = jnp.zeros((rows, d), F32).at[:bsz].set(c)
    out = pl.pallas_call(
        _mod_kernel,
        grid=(depth, n // tn),
        in_specs=[
            pl.BlockSpec((rows, d), lambda l, j: (0, 0)),
            pl.BlockSpec((1, d, tn), lambda l, j: (l, 0, j)),
            pl.BlockSpec((1, 1, tn), lambda l, j: (l, 0, j)),
        ],
        out_specs=pl.BlockSpec((1, rows, tn), lambda l, j: (l, 0, j)),
        out_shape=jax.ShapeDtypeStruct((depth, rows, n), F32),
        compiler_params=_cparams(("parallel", "parallel")),
        name="adaln_mod",
    )(c_pad, mod_w, mod_b.reshape(depth, 1, n))
    return out[:, :bsz, :].reshape(depth, bsz, 6, d)


def _gmlp_kernel(x_ref, mod_ref, win_ref, vg_ref, vb_ref, ws_ref, bst_ref, wout_ref, g_ref, b_ref,
                 o_ref, gated_ref, *, tm, chunk, groups, alpha):
    x = x_ref[...]
    m = mod_ref[0]
    h = _modulate(x, m, 0).astype(BF16)
    z = _dot(h, win_ref[...])
    z = 0.5 * z * (1.0 + lax.erf(z * (2.0 ** -0.5)))
    width = z.shape[1] // 2
    gd = width // groups
    u = z[:, :width]
    v = _layer_norm(z[:, width:], vg_ref[...], vb_ref[...]).astype(BF16)
    row = lax.broadcasted_iota(jnp.int32, (chunk, chunk), 0)
    col = lax.broadcasted_iota(jnp.int32, (chunk, chunk), 1)
    causal = row >= col
    bst = bst_ref[...]
    for g in range(groups):
        w = jnp.where(causal, ws_ref[g], 0.0).astype(BF16)
        bias = bst[:, g:g + 1]
        for c in range(tm // chunk):
            rs = slice(c * chunk, (c + 1) * chunk)
            cs = slice(g * gd, (g + 1) * gd)
            mixed = _dot(w, v[rs, cs]) + bias
            gated_ref[rs, cs] = (u[rs, cs] * mixed).astype(BF16)
    y = _dot(gated_ref[...], wout_ref[...])
    o_ref[...] = _res_ln(x, y, m, 0, g_ref[...], b_ref[...], alpha)


def _gmlp_layer(x, mod, w_in, v_g, v_b, w_s, b_s, w_out, ln_g, ln_b, *, seq, alpha, tm=256):
    t, d = x.shape
    groups, chunk, _ = w_s.shape
    width = w_out.shape[0]
    kern = functools.partial(_gmlp_kernel, tm=tm, chunk=chunk, groups=groups, alpha=alpha)
    return pl.pallas_call(
        kern,
        grid=(t // tm,),
        in_specs=[
            pl.BlockSpec((tm, d), lambda i: (i, 0)),
            pl.BlockSpec((1, 6, d), lambda i: (i * tm // seq, 0, 0)),
            _const_spec((d, 2 * width)),
            _const_spec((1, width)),
            _const_spec((1, width)),
            _const_spec((groups, chunk, chunk)),
            _const_spec((chunk, groups)),
            _const_spec((width, d)),
            _const_spec((1, d)),
            _const_spec((1, d)),
        ],
        out_specs=pl.BlockSpec((tm, d), lambda i: (i, 0)),
        out_shape=jax.ShapeDtypeStruct((t, d), F32),
        scratch_shapes=[pltpu.VMEM((tm, width), BF16)],
        compiler_params=_cparams(("parallel",)),
        name="gmlp_mixer",
    )(x, mod, w_in.astype(BF16), v_g.reshape(1, width), v_b.reshape(1, width), w_s, b_s.T,
      w_out.astype(BF16), ln_g.reshape(1, d), ln_b.reshape(1, d))


def _swiglu_step(h, w1_ref, w3_ref, w2_ref, acc_ref, j):
    a = _dot(h, w1_ref[...])
    b = _dot(h, w3_ref[...])
    t = (a / (1.0 + jnp.exp(-a)) * b).astype(BF16)
    part = _dot(t, w2_ref[...])

    @pl.when(j == 0)
    def _():
        acc_ref[...] = part

    @pl.when(j > 0)
    def _():
        acc_ref[...] += part


def _ffn_kernel(x_ref, mod_ref, w1_ref, w3_ref, w2_ref, g_ref, b_ref, o_ref, h_ref, acc_ref, *, alpha):
    j = pl.program_id(1)

    @pl.when(j == 0)
    def _():
        h_ref[...] = _modulate(x_ref[...], mod_ref[0], 3).astype(BF16)

    _swiglu_step(h_ref[...], w1_ref, w3_ref, w2_ref, acc_ref, j)

    @pl.when(j == pl.num_programs(1) - 1)
    def _():
        o_ref[...] = _res_ln(x_ref[...], acc_ref[...], mod_ref[0], 3, g_ref[...], b_ref[...], alpha)


def _ffn_layer(x, mod, w13, w2, ln_g, ln_b, *, seq, alpha, tm=512, tf=512):
    t, d = x.shape
    f = w2.shape[0]
    nf = f // tf
    return pl.pallas_call(
        functools.partial(_ffn_kernel, alpha=alpha),
        grid=(t // tm, nf),
        in_specs=[
            pl.BlockSpec((tm, d), lambda i, j: (i, 0)),
            pl.BlockSpec((1, 6, d), lambda i, j: (i * tm // seq, 0, 0)),
            pl.BlockSpec((d, tf), lambda i, j: (0, j)),
            pl.BlockSpec((d, tf), lambda i, j: (0, nf + j)),
            pl.BlockSpec((tf, d), lambda i, j: (j, 0)),
            _const_spec((1, d)),
            _const_spec((1, d)),
        ],
        out_specs=pl.BlockSpec((tm, d), lambda i, j: (i, 0)),
        out_shape=jax.ShapeDtypeStruct((t, d), F32),
        scratch_shapes=[pltpu.VMEM((tm, d), BF16), pltpu.VMEM((tm, d), F32)],
        compiler_params=_cparams(("parallel", "arbitrary")),
        name="swiglu_dense",
    )(x, mod, w13.astype(BF16), w13.astype(BF16), w2.astype(BF16), ln_g.reshape(1, d), ln_b.reshape(1, d))


def _moe_ffn_kernel(be_ref, nv_ref, h_ref, w1_ref, w3_ref, w2_ref, o_ref, acc_ref):
    i = pl.program_id(0)
    j = pl.program_id(1)
    last = pl.num_programs(1) - 1
    valid = i < nv_ref[0]

    @pl.when(valid)
    def _():
        _swiglu_step(h_ref[...], w1_ref.at[0], w3_ref.at[0], w2_ref.at[0], acc_ref, j)

    @pl.when(jnp.logical_and(valid, j == last))
    def _():
        o_ref[...] = acc_ref[...]

    @pl.when(jnp.logical_and(jnp.logical_not(valid), j == last))
    def _():
        o_ref[...] = jnp.zeros_like(o_ref)


def _moe_ffn(h_slots, block_expert, n_valid, w13, w2, *, tf=512):
    cap, d = h_slots.shape
    f = w2.shape[1]
    nf = f // tf
    tm = EXPERT_BLOCK

    def jj(i, j, nv):
        return jnp.where(i < nv[0], j, nf - 1)

    grid_spec = pltpu.PrefetchScalarGridSpec(
        num_scalar_prefetch=2,
        grid=(cap // tm, nf),
        in_specs=[
            pl.BlockSpec((tm, d), lambda i, j, be, nv: (i, 0)),
            pl.BlockSpec((1, d, tf), lambda i, j, be, nv: (be[i], 0, jj(i, j, nv))),
            pl.BlockSpec((1, d, tf), lambda i, j, be, nv: (be[i], 0, nf + jj(i, j, nv))),
            pl.BlockSpec((1, tf, d), lambda i, j, be, nv: (be[i], jj(i, j, nv), 0)),
        ],
        out_specs=pl.BlockSpec((tm, d), lambda i, j, be, nv: (i, 0)),
        scratch_shapes=[pltpu.VMEM((tm, d), F32)],
    )
    w13b = w13.astype(BF16)
    return pl.pallas_call(
        _moe_ffn_kernel,
        grid_spec=grid_spec,
        out_shape=jax.ShapeDtypeStruct((cap, d), F32),
        compiler_params=_cparams(("parallel", "arbitrary")),
        name="swiglu_experts",
    )(block_expert, n_valid, h_slots, w13b, w13b, w2.astype(BF16))


def _router_kernel(x_ref, mod_ref, wr_ref, br_ref, h_ref, meta_ref, cnt_ref, carry_ref, *, tm, n_exp):
    i = pl.program_id(0)

    @pl.when(i == 0)
    def _():
        carry_ref[...] = jnp.zeros_like(carry_ref)

    h = _modulate(x_ref[...], mod_ref[0], 3)
    h_ref[...] = h.astype(BF16)
    h_hi, h_mid, _ = _split3(h)
    w = wr_ref[...]
    w_hi = w.astype(BF16)
    w_lo = (w - w_hi.astype(F32)).astype(BF16)
    logits = _dot(h_hi, w_hi) + _dot(h_mid, w_hi) + _dot(h_hi, w_lo) + br_ref[...]
    lane = lax.broadcasted_iota(jnp.int32, (tm, LANES), 1)
    neg = jnp.float32(-jnp.inf)
    logits = jnp.where(lane < n_exp, logits, neg)
    v0 = jnp.max(logits, axis=1, keepdims=True)
    e0 = jnp.min(jnp.where(logits == v0, lane, LANES), axis=1, keepdims=True)
    rest = jnp.where(lane == e0, neg, logits)
    v1 = jnp.max(rest, axis=1, keepdims=True)
    e1 = jnp.min(jnp.where(rest == v1, lane, LANES), axis=1, keepdims=True)
    p = jnp.exp(v1 - v0)
    g0 = 1.0 / (1.0 + p)
    g1 = p / (1.0 + p)
    oh0 = lane == e0
    oh1 = lane == e1
    onehot = jnp.where(jnp.logical_or(oh0, oh1), 1.0, 0.0).astype(BF16)
    r = lax.broadcasted_iota(jnp.int32, (tm, tm), 0)
    c = lax.broadcasted_iota(jnp.int32, (tm, tm), 1)
    strict = jnp.where(r > c, 1.0, 0.0).astype(BF16)
    before = _dot(strict, onehot) + carry_ref[0:1, :]
    rank0 = jnp.sum(jnp.where(oh0, before, 0.0), axis=1, keepdims=True)
    rank1 = jnp.sum(jnp.where(oh1, before, 0.0), axis=1, keepdims=True)
    total = carry_ref[0:1, :] + jnp.sum(onehot.astype(F32), axis=0, keepdims=True)
    carry_ref[...] = jnp.broadcast_to(total, carry_ref.shape)
    cnt_ref[...] = jnp.broadcast_to(total, cnt_ref.shape)
    meta = jnp.where(lane == 0, e0.astype(F32), 0.0)
    meta = jnp.where(lane == 1, e1.astype(F32), meta)
    meta = jnp.where(lane == 2, g0, meta)
    meta = jnp.where(lane == 3, g1, meta)
    meta = jnp.where(lane == 4, rank0, meta)
    meta = jnp.where(lane == 5, rank1, meta)
    meta_ref[...] = meta


def _router(x, mod, w_router, b_router, *, seq, tm=512):
    t, d = x.shape
    n_exp = w_router.shape[1]
    wr = jnp.zeros((d, LANES), F32).at[:, :n_exp].set(w_router)
    br = jnp.zeros((1, LANES), F32).at[0, :n_exp].set(b_router)
    return pl.pallas_call(
        functools.partial(_router_kernel, tm=tm, n_exp=n_exp),
        grid=(t // tm,),
        in_specs=[
            pl.BlockSpec((tm, d), lambda i: (i, 0)),
            pl.BlockSpec((1, 6, d), lambda i: (i * tm // seq, 0, 0)),
            _const_spec((d, LANES)),
            _const_spec((1, LANES)),
        ],
        out_specs=[
            pl.BlockSpec((tm, d), lambda i: (i, 0)),
            pl.BlockSpec((tm, LANES), lambda i: (i, 0)),
            pl.BlockSpec((8, LANES), lambda i: (0, 0)),
        ],
        out_shape=[
            jax.ShapeDtypeStruct((t, d), BF16),
            jax.ShapeDtypeStruct((t, LANES), F32),
            jax.ShapeDtypeStruct((8, LANES), F32),
        ],
        scratch_shapes=[pltpu.VMEM((8, LANES), F32)],
        compiler_params=_cparams(("arbitrary",)),
        name="moe_router",
    )(x, mod, wr, br)


def _combine_kernel(x_ref, ya_ref, yb_ref, meta_ref, mod_ref, g_ref, b_ref, o_ref, *, alpha):
    meta = meta_ref[...]
    y = meta[:, 2:3] * ya_ref[...] + meta[:, 3:4] * yb_ref[...]
    o_ref[...] = _res_ln(x_ref[...], y, mod_ref[0], 3, g_ref[...], b_ref[...], alpha)


def _combine(x, ya, yb, meta, mod, ln_g, ln_b, *, seq, alpha, tm=512):
    t, d = x.shape
    row = pl.BlockSpec((tm, d), lambda i: (i, 0))
    return pl.pallas_call(
        functools.partial(_combine_kernel, alpha=alpha),
        grid=(t // tm,),
        in_specs=[row, row, row,
                  pl.BlockSpec((tm, LANES), lambda i: (i, 0)),
                  pl.BlockSpec((1, 6, d), lambda i: (i * tm // seq, 0, 0)),
                  _const_spec((1, d)), _const_spec((1, d))],
        out_specs=row,
        out_shape=jax.ShapeDtypeStruct((t, d), F32),
        compiler_params=_cparams(("parallel",)),
        name="moe_combine",
    )(x, ya, yb, meta, mod, ln_g.reshape(1, d), ln_b.reshape(1, d))


def _moe_layer(x, mod, w_router, b_router, w13, w2, ln_g, ln_b, *, seq, alpha):
    t, d = x.shape
    n_exp = w_router.shape[1]
    h, meta, cnt = _router(x, mod, w_router, b_router, seq=seq)
    e0 = meta[:, 0].astype(jnp.int32)
    e1 = meta[:, 1].astype(jnp.int32)
    rank0 = meta[:, 4].astype(jnp.int32)
    rank1 = meta[:, 5].astype(jnp.int32)
    counts = cnt[0, :n_exp].astype(jnp.int32)
    padded = (counts + EXPERT_BLOCK - 1) // EXPERT_BLOCK * EXPERT_BLOCK
    pad_end = jnp.cumsum(padded)
    pad_start = pad_end - padded
    dest0 = pad_start[e0] + rank0
    dest1 = pad_start[e1] + rank1
    cap = t * TOP_K + n_exp * EXPERT_BLOCK
    n_blocks = cap // EXPERT_BLOCK
    tok = jnp.arange(t, dtype=jnp.int32)
    slot_tok = jnp.zeros((cap,), jnp.int32).at[dest0].set(tok).at[dest1].set(tok)
    block_expert = jnp.minimum(
        jnp.searchsorted(pad_end, jnp.arange(n_blocks, dtype=jnp.int32) * EXPERT_BLOCK, side='right'),
        n_exp - 1).astype(jnp.int32)
    n_valid = (pad_end[-1:] // EXPERT_BLOCK).astype(jnp.int32)
    h_slots = jnp.take(h, slot_tok, axis=0)
    y_slots = _moe_ffn(h_slots, block_expert, n_valid, w13, w2)
    ya = jnp.take(y_slots, dest0, axis=0)
    yb = jnp.take(y_slots, dest1, axis=0)
    return _combine(x, ya, yb, meta, mod, ln_g, ln_b, seq=seq, alpha=alpha)


def _fox_proj_kernel(x_ref, mod_ref, w_ref, wf_ref, bf_ref, qkv_ref, fl_ref, h_ref, *, q_scale):
    j = pl.program_id(1)

    @pl.when(j == 0)
    def _():
        h = _modulate(x_ref[...], mod_ref[0], 0)
        h_hi, h_mid, _ = _split3(h)
        h_ref[...] = h_hi
        wf = wf_ref[...]
        wf_hi = wf.astype(BF16)
        wf_lo = (wf - wf_hi.astype(F32)).astype(BF16)
        fl_ref[...] = _dot(h_hi, wf_hi) + _dot(h_mid, wf_hi) + _dot(h_hi, wf_lo) + bf_ref[...]

    res = _dot(h_ref[...], w_ref[...])
    scale = jnp.where(j == 0, jnp.float32(q_scale), jnp.float32(1.0))
    qkv_ref[...] = (res * scale).astype(BF16)


def _fox_proj(x, mod, w_in, b_f, *, seq, heads, tm=512):
    t, d = x.shape
    hd = d // heads
    w_qkv = w_in[:, :3 * d].astype(BF16)
    wf = jnp.zeros((d, LANES), F32).at[:, :heads].set(w_in[:, 3 * d:])
    bf = jnp.zeros((1, LANES), F32).at[0, :heads].set(b_f)
    return pl.pallas_call(
        functools.partial(_fox_proj_kernel, q_scale=hd ** -0.5),
        grid=(t // tm, 3),
        in_specs=[
            pl.BlockSpec((tm, d), lambda i, j: (i, 0)),
            pl.BlockSpec((1, 6, d), lambda i, j: (i * tm // seq, 0, 0)),
            pl.BlockSpec((d, d), lambda i, j: (0, j)),
            _const_spec((d, LANES)),
            _const_spec((1, LANES)),
        ],
        out_specs=[
            pl.BlockSpec((tm, d), lambda i, j: (i, j)),
            pl.BlockSpec((tm, LANES), lambda i, j: (i, 0)),
        ],
        out_shape=[
            jax.ShapeDtypeStruct((t, 3 * d), BF16),
            jax.ShapeDtypeStruct((t, LANES), F32),
        ],
        scratch_shapes=[pltpu.VMEM((tm, d), BF16)],
        compiler_params=_cparams(("parallel", "arbitrary")),
        name="fox_proj",
    )(x, mod, w_qkv, wf, bf)


def _cum_kernel(fl_ref, o_ref, *, blk):
    n = fl_ref.shape[1] // blk
    r = lax.broadcasted_iota(jnp.int32, (blk, blk), 0)
    c = lax.broadcasted_iota(jnp.int32, (blk, blk), 1)
    tri = jnp.where(r >= c, 1.0, 0.0).astype(BF16)

    def body(k, carry):
        rows = pl.ds(pl.multiple_of(k * blk, blk), blk)
        f = fl_ref[0, rows, :]
        lf = jnp.minimum(f, 0.0) - jnp.log(1.0 + jnp.exp(-jnp.abs(f)))
        hi, mid, lo = _split3(lf)
        cs = _dot(tri, hi) + _dot(tri, mid) + _dot(tri, lo) + carry
        o_ref[0, rows, :] = cs
        return cs[blk - 1:blk, :]

    lax.fori_loop(0, n, body, jnp.zeros((1, LANES), F32))


def _forget_cumsum(flogit, *, bsz, seq):
    fl = flogit.reshape(bsz, seq, LANES)
    return pl.pallas_call(
        functools.partial(_cum_kernel, blk=128),
        grid=(bsz,),
        in_specs=[pl.BlockSpec((1, seq, LANES), lambda b: (b, 0, 0))],
        out_specs=pl.BlockSpec((1, seq, LANES), lambda b: (b, 0, 0)),
        out_shape=jax.ShapeDtypeStruct((bsz, seq, LANES), F32),
        compiler_params=_cparams(("parallel",)),
        name="fox_forget_cumsum",
    )(fl)


def _flash_kernel(qi_ref, kj_ref, q_ref, k_ref, v_ref, cq_ref, ck_ref, o_ref, m_ref, l_ref, acc_ref,
                  *, heads, hd, tq, tk):
    p_idx = pl.program_id(1)
    qi = qi_ref[p_idx]
    kj = kj_ref[p_idx]
    last_kj = ((qi + 1) * tq - 1) // tk

    @pl.when(kj == 0)
    def _():
        m_ref[...] = jnp.full(m_ref.shape, -jnp.inf, F32)
        l_ref[...] = jnp.zeros_like(l_ref)
        acc_ref[...] = jnp.zeros_like(acc_ref)

    def step(masked):
        cq = cq_ref[0]
        ck = ck_ref[0]
        if masked:
            rows = qi * tq + lax.broadcasted_iota(jnp.int32, (tq, tk), 0)
            cols = kj * tk + lax.broadcasted_iota(jnp.int32, (tq, tk), 1)
            keep = rows >= cols
        for h in range(heads):
            cs = slice(h * hd, (h + 1) * hd)
            s = lax.dot_general(q_ref[:, cs], k_ref[:, cs], (((1,), (1,)), ((), ())),
                                preferred_element_type=F32)
            s = s + cq[:, h:h + 1] - ck[h:h + 1, :]
            if masked:
                s = jnp.where(keep, s, -jnp.inf)
            m_prev = m_ref[h]
            l_prev = l_ref[h]
            m_next = jnp.maximum(m_prev, jnp.max(s, axis=1, keepdims=True))
            p = jnp.exp(s - pltpu.repeat(m_next, tk // LANES, 1))
            a = jnp.exp(m_prev - m_next)
            l_ref[h] = a * l_prev + jnp.sum(p, axis=1, keepdims=True)
            m_ref[h] = m_next
            pv = _dot(p.astype(BF16), v_ref[:, cs])
            acc_ref[h] = acc_ref[h] * a[:, :hd] + pv

    @pl.when(kj < last_kj)
    def _():
        step(False)

    @pl.when(kj == last_kj)
    def _():
        step(True)
        outs = [acc_ref[h] / l_ref[h][:, :hd] for h in range(heads)]
        o_ref[...] = jnp.concatenate(outs, axis=1).astype(BF16)


def _flash(qkv, cum, cum_t, *, bsz, seq, heads, tq=256, tk=256):
    t = qkv.shape[0]
    d = qkv.shape[1] // 3
    hd = d // heads
    nq = seq // tq
    pairs = [(i, j) for i in range(nq) for j in range(((i + 1) * tq - 1) // tk + 1)]
    qi = jnp.asarray(np.array([p[0] for p in pairs], np.int32))
    kj = jnp.asarray(np.array([p[1] for p in pairs], np.int32))
    grid_spec = pltpu.PrefetchScalarGridSpec(
        num_scalar_prefetch=2,
        grid=(bsz, len(pairs)),
        in_specs=[
            pl.BlockSpec((tq, d), lambda b, p, qi, kj: (b * (seq // tq) + qi[p], 0)),
            pl.BlockSpec((tk, d), lambda b, p, qi, kj: (b * (seq // tk) + kj[p], 1)),
            pl.BlockSpec((tk, d), lambda b, p, qi, kj: (b * (seq // tk) + kj[p], 2)),
            pl.BlockSpec((1, tq, LANES), lambda b, p, qi, kj: (b, qi[p], 0)),
            pl.BlockSpec((1, heads, tk), lambda b, p, qi, kj: (b, 0, kj[p])),
        ],
        out_specs=pl.BlockSpec((tq, d), lambda b, p, qi, kj: (b * (seq // tq) + qi[p], 0)),
        scratch_shapes=[
            pltpu.VMEM((heads, tq, LANES), F32),
            pltpu.VMEM((heads, tq, LANES), F32),
            pltpu.VMEM((heads, tq, hd), F32),
        ],
    )
    return pl.pallas_call(
        functools.partial(_flash_kernel, heads=heads, hd=hd, tq=tq, tk=tk),
        grid_spec=grid_spec,
        out_shape=jax.ShapeDtypeStruct((t, d), BF16),
        compiler_params=_cparams(("parallel", "arbitrary")),
        name="fox_attention",
    )(qi, kj, qkv, qkv, qkv, cum, cum_t)


def _mm_resln_kernel(a_ref, x_ref, mod_ref, w_ref, g_ref, b_ref, o_ref, *, alpha):
    y = _dot(a_ref[...], w_ref[...])
    o_ref[...] = _res_ln(x_ref[...], y, mod_ref[0], 0, g_ref[...], b_ref[...], alpha)


def _mm_resln(a, x, mod, w, ln_g, ln_b, *, seq, alpha, tm=512):
    t, d = x.shape
    k = a.shape[1]
    return pl.pallas_call(
        functools.partial(_mm_resln_kernel, alpha=alpha),
        grid=(t // tm,),
        in_specs=[
            pl.BlockSpec((tm, k), lambda i: (i, 0)),
            pl.BlockSpec((tm, d), lambda i: (i, 0)),
            pl.BlockSpec((1, 6, d), lambda i: (i * tm // seq, 0, 0)),
            _const_spec((k, d)),
            _const_spec((1, d)),
            _const_spec((1, d)),
        ],
        out_specs=pl.BlockSpec((tm, d), lambda i: (i, 0)),
        out_shape=jax.ShapeDtypeStruct((t, d), F32),
        compiler_params=_cparams(("parallel",)),
        name="proj_res_ln",
    )(a, x, mod, w.astype(BF16), ln_g.reshape(1, d), ln_b.reshape(1, d))


def _fox_layer(x, mod, w_in, b_f, w_out, ln_g, ln_b, *, bsz, seq, alpha):
    heads = b_f.shape[0]
    qkv, flogit = _fox_proj(x, mod, w_in, b_f, seq=seq, heads=heads)
    cum = _forget_cumsum(flogit, bsz=bsz, seq=seq)
    cum_t = jnp.transpose(cum[:, :, :heads], (0, 2, 1))
    attn = _flash(qkv, cum, cum_t, bsz=bsz, seq=seq, heads=heads)
    return _mm_resln(attn, x, mod, w_out, ln_g, ln_b, seq=seq, alpha=alpha)


def _pool_kernel(x_ref, xh_ref, mod_ref, win_ref, wg_ref, ls_ref, wout_ref, g_ref, b_ref, o_ref, pooled_ref,
                 *, tm, seq, alpha):
    i = pl.program_id(0)
    pos0 = (i * tm) % seq
    m = mod_ref[0]
    x = x_ref[...]
    halo_ok = jnp.where(pos0 > 0, jnp.float32(1.0), jnp.float32(0.0))
    xe = jnp.concatenate([xh_ref[...], x], axis=0)
    ze = _dot(_modulate(xe, m, 0).astype(BF16), win_ref[...])
    rows = lax.broadcasted_iota(jnp.int32, (tm + POOL_HALO, 1), 0)
    ze = ze * jnp.where(rows < POOL_HALO, halo_ok, jnp.float32(1.0))
    pos = (pos0 + lax.broadcasted_iota(jnp.int32, (tm, 1), 0) + 1).astype(F32)
    gd = ze.shape[1] // len(POOL_WINDOWS)
    for g, win in enumerate(POOL_WINDOWS):
        cs = slice(g * gd, (g + 1) * gd)
        zg = ze[:, cs]
        s = zg
        span = 1
        while span < win:
            s = s + pltpu.roll(s, span, 0)
            span *= 2
        count = jnp.minimum(pos, jnp.float32(win))
        pooled = s[POOL_HALO:, :] / count - zg[POOL_HALO:, :]
        pooled_ref[:, cs] = (_dot(pooled.astype(BF16), wg_ref[g]) * ls_ref[:, cs]).astype(BF16)
    y = _dot(pooled_ref[...], wout_ref[...])
    o_ref[...] = _res_ln(x, y, m, 0, g_ref[...], b_ref[...], alpha)


def _pool_layer(x, mod, w_in, w_grp, scale, w_out, ln_g, ln_b, *, seq, alpha, tm=512):
    t, d = x.shape
    ng, gd, _ = w_grp.shape
    hb = tm // POOL_HALO
    return pl.pallas_call(
        functools.partial(_pool_kernel, tm=tm, seq=seq, alpha=alpha),
        grid=(t // tm,),
        in_specs=[
            pl.BlockSpec((tm, d), lambda i: (i, 0)),
            pl.BlockSpec((POOL_HALO, d), lambda i: (jnp.maximum(i * hb - 1, 0), 0)),
            pl.BlockSpec((1, 6, d), lambda i: (i * tm // seq, 0, 0)),
            _const_spec((d, d)),
            _const_spec((ng, gd, gd)),
            _const_spec((1, d)),
            _const_spec((d, d)),
            _const_spec((1, d)),
            _const_spec((1, d)),
        ],
        out_specs=pl.BlockSpec((tm, d), lambda i: (i, 0)),
        out_shape=jax.ShapeDtypeStruct((t, d), F32),
        scratch_shapes=[pltpu.VMEM((tm, d), BF16)],
        compiler_params=_cparams(("parallel",)),
        name="pool_mixer",
    )(x, x, mod, w_in.astype(BF16), w_grp.astype(BF16), scale.reshape(1, d), w_out.astype(BF16),
      ln_g.reshape(1, d), ln_b.reshape(1, d))


def kernel(x, c, mod_w, mod_b, ln_g, ln_b, gm_w_in, gm_ln_g, gm_ln_b, gm_w_s, gm_b_s, gm_w_out, fox_w_in,
           fox_b_f, fox_w_out, pool_w_in, pool_w_grp, pool_scale, pool_w_out, ffn_w13, ffn_w2, moe_w_router,
           moe_b_router, moe_w13, moe_w2):
    bsz, seq, d = x.shape
    depth = mod_w.shape[0]
    alpha = (2 * depth) ** 0.25
    n_mixers = 3
    mod_all = _modulation(c, mod_w, mod_b)
    xt = x.reshape(bsz * seq, d)
    for i in range(depth):
        mod = mod_all[i]
        kind, j = i % n_mixers, i // n_mixers
        if kind == 0:
            xt = _gmlp_layer(xt, mod, gm_w_in[j], gm_ln_g[j], gm_ln_b[j], gm_w_s[j], gm_b_s[j], gm_w_out[j],
                             ln_g[i, 0], ln_b[i, 0], seq=seq, alpha=alpha)
        elif kind == 1:
            xt = _fox_layer(xt, mod, fox_w_in[j], fox_b_f[j], fox_w_out[j], ln_g[i, 0], ln_b[i, 0],
                            bsz=bsz, seq=seq, alpha=alpha)
        else:
            xt = _pool_layer(xt, mod, pool_w_in[j], pool_w_grp[j], pool_scale[j], pool_w_out[j],
                             ln_g[i, 0], ln_b[i, 0], seq=seq, alpha=alpha)
        if i % 2 == 0:
            xt = _ffn_layer(xt, mod, ffn_w13[i // 2], ffn_w2[i // 2], ln_g[i, 1], ln_b[i, 1],
                            seq=seq, alpha=alpha)
        else:
            xt = _moe_layer(xt, mod, moe_w_router[i // 2], moe_b_router[i // 2], moe_w13[i // 2],
                            moe_w2[i // 2], ln_g[i, 1], ln_b[i, 1], seq=seq, alpha=alpha)
    return xt.reshape(bsz, seq, d)
```

```python
import functools

import numpy as np
import jax
import jax.numpy as jnp
from jax import lax
from jax.experimental import pallas as pl
from jax.experimental.pallas import tpu as pltpu

F32 = jnp.float32
BF16 = jnp.bfloat16

POOL_WINDOWS = (2, 4, 8, 16)
TOP_K = 2
EXPERT_BLOCK = 512
LN_EPS = 1e-5
LANES = 128
POOL_HALO = 16
VMEM_LIMIT = 56 * 1024 * 1024


def _cparams(sem):
    return pltpu.CompilerParams(dimension_semantics=sem, vmem_limit_bytes=VMEM_LIMIT)


def _const_spec(shape):
    nd = len(shape)
    return pl.BlockSpec(shape, lambda *_: (0,) * nd, pipeline_mode=pl.Buffered(1))


def _layer_norm(r, g, b):
    mu = jnp.mean(r, axis=-1, keepdims=True)
    xc = r - mu
    var = jnp.mean(xc * xc, axis=-1, keepdims=True)
    return xc * lax.rsqrt(var + LN_EPS) * g + b


def _modulate(x, m, off):
    return x * (1.0 + m[off + 1:off + 2]) + m[off:off + 1]


def _res_ln(x, y, m, off, g, b, alpha):
    return _layer_norm(alpha * x + (1.0 + m[off + 2:off + 3]) * y, g, b)


def _split3(a):
    hi = a.astype(BF16)
    r1 = a - hi.astype(F32)
    mid = r1.astype(BF16)
    lo = (r1 - mid.astype(F32)).astype(BF16)
    return hi, mid, lo


def _dot(a, b):
    return jnp.dot(a, b, preferred_element_type=F32)


def _mod_kernel(c_ref, w_ref, b_ref, o_ref):
    c = c_ref[...]
    s = c / (1.0 + jnp.exp(-c))
    s_hi, s_mid, _ = _split3(s)
    w = w_ref[0]
    w_hi = w.astype(BF16)
    w_lo = (w - w_hi.astype(F32)).astype(BF16)
    acc = _dot(s_hi, w_hi) + _dot(s_mid, w_hi) + _dot(s_hi, w_lo)
    o_ref[0] = acc + b_ref[0]


def _modulation(c, mod_w, mod_b):
    depth, d, n = mod_w.shape
    bsz = c.shape[0]
    rows = 8
    tn = n // 4
    c_pad = jnp.zeros((rows, d), F32).at[:bsz].set(c)
    out = pl.pallas_call(
        _mod_kernel,
        grid=(depth, n // tn),
        in_specs=[
            pl.BlockSpec((rows, d), lambda l, j: (0, 0)),
            pl.BlockSpec((1, d, tn), lambda l, j: (l, 0, j)),
            pl.BlockSpec((1, 1, tn), lambda l, j: (l, 0, j)),
        ],
        out_specs=pl.BlockSpec((1, rows, tn), lambda l, j: (l, 0, j)),
        out_shape=jax.ShapeDtypeStruct((depth, rows, n), F32),
        compiler_params=_cparams(("parallel", "parallel")),
        name="adaln_mod",
    )(c_pad, mod_w, mod_b.reshape(depth, 1, n))
    return out[:, :bsz, :].reshape(depth, bsz, 6, d)


def _gmlp_kernel(x_ref, mod_ref, win_ref, vg_ref, vb_ref, ws_ref, bst_ref, wout_ref, g_ref, b_ref,
                 o_ref, gated_ref, *, tm, chunk, groups, alpha):
    x = x_ref[...]
    m = mod_ref[0]
    h = _modulate(x, m, 0).astype(BF16)
    z = _dot(h, win_ref[...])
    z = 0.5 * z * (1.0 + lax.erf(z * (2.0 ** -0.5)))
    width = z.shape[1] // 2
    gd = width // groups
    u = z[:, :width]
    v = _layer_norm(z[:, width:], vg_ref[...], vb_ref[...]).astype(BF16)
    row = lax.broadcasted_iota(jnp.int32, (chunk, chunk), 0)
    col = lax.broadcasted_iota(jnp.int32, (chunk, chunk), 1)
    causal = row >= col
    bst = bst_ref[...]
    for g in range(groups):
        w = jnp.where(causal, ws_ref[g], 0.0).astype(BF16)
        bias = bst[:, g:g + 1]
        for c in range(tm // chunk):
            rs = slice(c * chunk, (c + 1) * chunk)
            cs = slice(g * gd, (g + 1) * gd)
            mixed = _dot(w, v[rs, cs]) + bias
            gated_ref[rs, cs] = (u[rs, cs] * mixed).astype(BF16)
    y = _dot(gated_ref[...], wout_ref[...])
    o_ref[...] = _res_ln(x, y, m, 0, g_ref[...], b_ref[...], alpha)


def _gmlp_layer(x, mod, w_in, v_g, v_b, w_s, b_s, w_out, ln_g, ln_b, *, seq, alpha, tm=256):
    t, d = x.shape
    groups, chunk, _ = w_s.shape
    width = w_out.shape[0]
    kern = functools.partial(_gmlp_kernel, tm=tm, chunk=chunk, groups=groups, alpha=alpha)
    return pl.pallas_call(
        kern,
        grid=(t // tm,),
        in_specs=[
            pl.BlockSpec((tm, d), lambda i: (i, 0)),
            pl.BlockSpec((1, 6, d), lambda i: (i * tm // seq, 0, 0)),
            _const_spec((d, 2 * width)),
            _const_spec((1, width)),
            _const_spec((1, width)),
            _const_spec((groups, chunk, chunk)),
            _const_spec((chunk, groups)),
            _const_spec((width, d)),
            _const_spec((1, d)),
            _const_spec((1, d)),
        ],
        out_specs=pl.BlockSpec((tm, d), lambda i: (i, 0)),
        out_shape=jax.ShapeDtypeStruct((t, d), F32),
        scratch_shapes=[pltpu.VMEM((tm, width), BF16)],
        compiler_params=_cparams(("parallel",)),
        name="gmlp_mixer",
    )(x, mod, w_in.astype(BF16), v_g.reshape(1, width), v_b.reshape(1, width), w_s, b_s.T,
      w_out.astype(BF16), ln_g.reshape(1, d), ln_b.reshape(1, d))


def _swiglu_step(h, w1_ref, w3_ref, w2_ref, acc_ref, j):
    a = _dot(h, w1_ref[...])
    b = _dot(h, w3_ref[...])
    t = (a / (1.0 + jnp.exp(-a)) * b).astype(BF16)
    part = _dot(t, w2_ref[...])

    @pl.when(j == 0)
    def _():
        acc_ref[...] = part

    @pl.when(j > 0)
    def _():
        acc_ref[...] += part


def _ffn_kernel(x_ref, mod_ref, w1_ref, w3_ref, w2_ref, g_ref, b_ref, o_ref, h_ref, acc_ref, *, alpha):
    j = pl.program_id(1)

    @pl.when(j == 0)
    def _():
        h_ref[...] = _modulate(x_ref[...], mod_ref[0], 3).astype(BF16)

    _swiglu_step(h_ref[...], w1_ref, w3_ref, w2_ref, acc_ref, j)

    @pl.when(j == pl.num_programs(1) - 1)
    def _():
        o_ref[...] = _res_ln(x_ref[...], acc_ref[...], mod_ref[0], 3, g_ref[...], b_ref[...], alpha)


def _ffn_layer(x, mod, w13, w2, ln_g, ln_b, *, seq, alpha, tm=512, tf=512):
    t, d = x.shape
    f = w2.shape[0]
    nf = f // tf
    return pl.pallas_call(
        functools.partial(_ffn_kernel, alpha=alpha),
        grid=(t // tm, nf),
        in_specs=[
            pl.BlockSpec((tm, d), lambda i, j: (i, 0)),
            pl.BlockSpec((1, 6, d), lambda i, j: (i * tm // seq, 0, 0)),
            pl.BlockSpec((d, tf), lambda i, j: (0, j)),
            pl.BlockSpec((d, tf), lambda i, j: (0, nf + j)),
            pl.BlockSpec((tf, d), lambda i, j: (j, 0)),
            _const_spec((1, d)),
            _const_spec((1, d)),
        ],
        out_specs=pl.BlockSpec((tm, d), lambda i, j: (i, 0)),
        out_shape=jax.ShapeDtypeStruct((t, d), F32),
        scratch_shapes=[pltpu.VMEM((tm, d), BF16), pltpu.VMEM((tm, d), F32)],
        compiler_params=_cparams(("parallel", "arbitrary")),
        name="swiglu_dense",
    )(x, mod, w13.astype(BF16), w13.astype(BF16), w2.astype(BF16), ln_g.reshape(1, d), ln_b.reshape(1, d))


def _moe_ffn_kernel(be_ref, nv_ref, h_ref, w1_ref, w3_ref, w2_ref, o_ref, acc_ref):
    i = pl.program_id(0)
    j = pl.program_id(1)
    last = pl.num_programs(1) - 1
    valid = i < nv_ref[0]

    @pl.when(valid)
    def _():
        _swiglu_step(h_ref[...], w1_ref.at[0], w3_ref.at[0], w2_ref.at[0], acc_ref, j)

    @pl.when(jnp.logical_and(valid, j == last))
    def _():
        o_ref[...] = acc_ref[...]

    @pl.when(jnp.logical_and(jnp.logical_not(valid), j == last))
    def _():
        o_ref[...] = jnp.zeros_like(o_ref)


def _moe_ffn(h_slots, block_expert, n_valid, w13, w2, *, tf=512):
    cap, d = h_slots.shape
    f = w2.shape[1]
    nf = f // tf
    tm = EXPERT_BLOCK

    def jj(i, j, nv):
        return jnp.where(i < nv[0], j, nf - 1)

    grid_spec = pltpu.PrefetchScalarGridSpec(
        num_scalar_prefetch=2,
        grid=(cap // tm, nf),
        in_specs=[
            pl.BlockSpec((tm, d), lambda i, j, be, nv: (i, 0)),
            pl.BlockSpec((1, d, tf), lambda i, j, be, nv: (be[i], 0, jj(i, j, nv))),
            pl.BlockSpec((1, d, tf), lambda i, j, be, nv: (be[i], 0, nf + jj(i, j, nv))),
            pl.BlockSpec((1, tf, d), lambda i, j, be, nv: (be[i], jj(i, j, nv), 0)),
        ],
        out_specs=pl.BlockSpec((tm, d), lambda i, j, be, nv: (i, 0)),
        scratch_shapes=[pltpu.VMEM((tm, d), F32)],
    )
    w13b = w13.astype(BF16)
    return pl.pallas_call(
        _moe_ffn_kernel,
        grid_spec=grid_spec,
        out_shape=jax.ShapeDtypeStruct((cap, d), F32),
        compiler_params=_cparams(("parallel", "arbitrary")),
        name="swiglu_experts",
    )(block_expert, n_valid, h_slots, w13b, w13b, w2.astype(BF16))


def _router_kernel(x_ref, mod_ref, wr_ref, br_ref, h_ref, meta_ref, cnt_ref, carry_ref, *, tm, n_exp):
    i = pl.program_id(0)

    @pl.when(i == 0)
    def _():
        carry_ref[...] = jnp.zeros_like(carry_ref)

    h = _modulate(x_ref[...], mod_ref[0], 3)
    h_ref[...] = h.astype(BF16)
    h_hi, h_mid, _ = _split3(h)
    w = wr_ref[...]
    w_hi = w.astype(BF16)
    w_lo = (w - w_hi.astype(F32)).astype(BF16)
    logits = _dot(h_hi, w_hi) + _dot(h_mid, w_hi) + _dot(h_hi, w_lo) + br_ref[...]
    lane = lax.broadcasted_iota(jnp.int32, (tm, LANES), 1)
    neg = jnp.float32(-jnp.inf)
    logits = jnp.where(lane < n_exp, logits, neg)
    v0 = jnp.max(logits, axis=1, keepdims=True)
    e0 = jnp.min(jnp.where(logits == v0, lane, LANES), axis=1, keepdims=True)
    rest = jnp.where(lane == e0, neg, logits)
    v1 = jnp.max(rest, axis=1, keepdims=True)
    e1 = jnp.min(jnp.where(rest == v1, lane, LANES), axis=1, keepdims=True)
    p = jnp.exp(v1 - v0)
    g0 = 1.0 / (1.0 + p)
    g1 = p / (1.0 + p)
    oh0 = lane == e0
    oh1 = lane == e1
    onehot = jnp.where(jnp.logical_or(oh0, oh1), 1.0, 0.0).astype(BF16)
    r = lax.broadcasted_iota(jnp.int32, (tm, tm), 0)
    c = lax.broadcasted_iota(jnp.int32, (tm, tm), 1)
    strict = jnp.where(r > c, 1.0, 0.0).astype(BF16)
    before = _dot(strict, onehot) + carry_ref[0:1, :]
    rank0 = jnp.sum(jnp.where(oh0, before, 0.0), axis=1, keepdims=True)
    rank1 = jnp.sum(jnp.where(oh1, before, 0.0), axis=1, keepdims=True)
    total = carry_ref[0:1, :] + jnp.sum(onehot.astype(F32), axis=0, keepdims=True)
    carry_ref[...] = jnp.broadcast_to(total, carry_ref.shape)
    cnt_ref[...] = jnp.broadcast_to(total, cnt_ref.shape)
    meta = jnp.where(lane == 0, e0.astype(F32), 0.0)
    meta = jnp.where(lane == 1, e1.astype(F32), meta)
    meta = jnp.where(lane == 2, g0, meta)
    meta = jnp.where(lane == 3, g1, meta)
    meta = jnp.where(lane == 4, rank0, meta)
    meta = jnp.where(lane == 5, rank1, meta)
    meta_ref[...] = meta


def _router(x, mod, w_router, b_router, *, seq, tm=512):
    t, d = x.shape
    n_exp = w_router.shape[1]
    wr = jnp.zeros((d, LANES), F32).at[:, :n_exp].set(w_router)
    br = jnp.zeros((1, LANES), F32).at[0, :n_exp].set(b_router)
    return pl.pallas_call(
        functools.partial(_router_kernel, tm=tm, n_exp=n_exp),
        grid=(t // tm,),
        in_specs=[
            pl.BlockSpec((tm, d), lambda i: (i, 0)),
            pl.BlockSpec((1, 6, d), lambda i: (i * tm // seq, 0, 0)),
            _const_spec((d, LANES)),
            _const_spec((1, LANES)),
        ],
        out_specs=[
            pl.BlockSpec((tm, d), lambda i: (i, 0)),
            pl.BlockSpec((tm, LANES), lambda i: (i, 0)),
            pl.BlockSpec((8, LANES), lambda i: (0, 0)),
        ],
        out_shape=[
            jax.ShapeDtypeStruct((t, d), BF16),
            jax.ShapeDtypeStruct((t, LANES), F32),
            jax.ShapeDtypeStruct((8, LANES), F32),
        ],
        scratch_shapes=[pltpu.VMEM((8, LANES), F32)],
        compiler_params=_cparams(("arbitrary",)),
        name="moe_router",
    )(x, mod, wr, br)


def _combine_kernel(x_ref, ya_ref, yb_ref, meta_ref, mod_ref, g_ref, b_ref, o_ref, *, alpha):
    meta = meta_ref[...]
    y = meta[:, 2:3] * ya_ref[...] + meta[:, 3:4] * yb_ref[...]
    o_ref[...] = _res_ln(x_ref[...], y, mod_ref[0], 3, g_ref[...], b_ref[...], alpha)


def _combine(x, ya, yb, meta, mod, ln_g, ln_b, *, seq, alpha, tm=512):
    t, d = x.shape
    row = pl.BlockSpec((tm, d), lambda i: (i, 0))
    return pl.pallas_call(
        functools.partial(_combine_kernel, alpha=alpha),
        grid=(t // tm,),
        in_specs=[row, row, row,
                  pl.BlockSpec((tm, LANES), lambda i: (i, 0)),
                  pl.BlockSpec((1, 6, d), lambda i: (i * tm // seq, 0, 0)),
                  _const_spec((1, d)), _const_spec((1, d))],
        out_specs=row,
        out_shape=jax.ShapeDtypeStruct((t, d), F32),
        compiler_params=_cparams(("parallel",)),
        name="moe_combine",
    )(x, ya, yb, meta, mod, ln_g.reshape(1, d), ln_b.reshape(1, d))


def _moe_layer(x, mod, w_router, b_router, w13, w2, ln_g, ln_b, *, seq, alpha):
    t, d = x.shape
    n_exp = w_router.shape[1]
    h, meta, cnt = _router(x, mod, w_router, b_router, seq=seq)
    e0 = meta[:, 0].astype(jnp.int32)
    e1 = meta[:, 1].astype(jnp.int32)
    rank0 = meta[:, 4].astype(jnp.int32)
    rank1 = meta[:, 5].astype(jnp.int32)
    counts = cnt[0, :n_exp].astype(jnp.int32)
    padded = (counts + EXPERT_BLOCK - 1) // EXPERT_BLOCK * EXPERT_BLOCK
    pad_end = jnp.cumsum(padded)
    pad_start = pad_end - padded
    dest0 = pad_start[e0] + rank0
    dest1 = pad_start[e1] + rank1
    cap = t * TOP_K + n_exp * EXPERT_BLOCK
    n_blocks = cap // EXPERT_BLOCK
    tok = jnp.arange(t, dtype=jnp.int32)
    slot_tok = jnp.zeros((cap,), jnp.int32).at[dest0].set(tok).at[dest1].set(tok)
    block_expert = jnp.minimum(
        jnp.searchsorted(pad_end, jnp.arange(n_blocks, dtype=jnp.int32) * EXPERT_BLOCK, side='right'),
        n_exp - 1).astype(jnp.int32)
    n_valid = (pad_end[-1:] // EXPERT_BLOCK).astype(jnp.int32)
    h_slots = jnp.take(h, slot_tok, axis=0)
    y_slots = _moe_ffn(h_slots, block_expert, n_valid, w13, w2)
    ya = jnp.take(y_slots, dest0, axis=0)
    yb = jnp.take(y_slots, dest1, axis=0)
    return _combine(x, ya, yb, meta, mod, ln_g, ln_b, seq=seq, alpha=alpha)


def _gate_kernel(x_ref, mod_ref, wf_ref, bf_ref, o_ref, carry_ref, *, tm):
    @pl.when(pl.program_id(1) == 0)
    def _():
        carry_ref[...] = jnp.zeros_like(carry_ref)

    h = _modulate(x_ref[...], mod_ref[0], 0)
    h_hi, h_mid, _ = _split3(h)
    wf = wf_ref[...]
    wf_hi = wf.astype(BF16)
    wf_lo = (wf - wf_hi.astype(F32)).astype(BF16)
    f = _dot(h_hi, wf_hi) + _dot(h_mid, wf_hi) + _dot(h_hi, wf_lo) + bf_ref[...]
    lf = jnp.minimum(f, 0.0) - jnp.log(1.0 + jnp.exp(-jnp.abs(f)))
    r = lax.broadcasted_iota(jnp.int32, (tm, tm), 0)
    c = lax.broadcasted_iota(jnp.int32, (tm, tm), 1)
    tri = jnp.where(r >= c, 1.0, 0.0).astype(BF16)
    hi, mid, lo = _split3(lf)
    cs = _dot(tri, hi) + _dot(tri, mid) + _dot(tri, lo) + carry_ref[0:1, :]
    o_ref[...] = cs
    carry_ref[...] = jnp.broadcast_to(cs[tm - 1:tm, :], carry_ref.shape)


def _forget_cumsum(x, mod, w_f, b_f, *, bsz, seq, tm=512):
    t, d = x.shape
    heads = b_f.shape[0]
    nb = seq // tm
    wf = jnp.zeros((d, LANES), F32).at[:, :heads].set(w_f)
    bf = jnp.zeros((1, LANES), F32).at[0, :heads].set(b_f)
    return pl.pallas_call(
        functools.partial(_gate_kernel, tm=tm),
        grid=(bsz, nb),
        in_specs=[
            pl.BlockSpec((tm, d), lambda b, i: (b * nb + i, 0)),
            pl.BlockSpec((1, 6, d), lambda b, i: (b, 0, 0)),
            _const_spec((d, LANES)),
            _const_spec((1, LANES)),
        ],
        out_specs=pl.BlockSpec((tm, LANES), lambda b, i: (b * nb + i, 0)),
        out_shape=jax.ShapeDtypeStruct((t, LANES), F32),
        scratch_shapes=[pltpu.VMEM((8, LANES), F32)],
        compiler_params=_cparams(("parallel", "arbitrary")),
        name="fox_forget_cumsum",
    )(x, mod, wf, bf)


HEAD_PAD = LANES
GATE_PARTS = 3
VAL_PAD = 16
LOG2E = 1.4426950408889634


def _fox_proj_kernel(x_ref, mod_ref, cs_ref, wq_ref, wk_ref, wv_ref, selk_ref, onek_ref, selq_ref, oneq_ref,
                     onev_ref, k_ref, q_ref, v_ref, *, q_scale):
    h = _modulate(x_ref[...], mod_ref[0], 0).astype(BF16)
    cs = cs_ref[...] * LOG2E
    parts = _split3(cs)
    k = _dot(h, wk_ref[...]) + onek_ref[...]
    for n in range(GATE_PARTS):
        k = k + _dot(parts[n], selk_ref[n])
    k_ref[...] = k.astype(BF16)
    nt = (((1,), (1,)), ((), ()))
    parts_t = _split3(cs.T)
    q = lax.dot_general(wq_ref[...], h, nt, preferred_element_type=F32) * (q_scale * LOG2E) + oneq_ref[...]
    for n in range(GATE_PARTS):
        q = q + _dot(selq_ref[n], parts_t[n])
    q_ref[0] = q.astype(BF16)
    v = lax.dot_general(wv_ref[...], h, nt, preferred_element_type=F32) + onev_ref[...]
    v_ref[0] = v.astype(BF16)


def _fox_proj(x, mod, cs, w_in, *, bsz, seq, heads, tm=512):
    t, d = x.shape
    hd = d // heads
    nb = seq // tm
    wide = heads * HEAD_PAD
    pad = lambda w: jnp.pad(w.reshape(d, heads, hd), ((0, 0), (0, 0), (0, HEAD_PAD - hd))).reshape(d, wide)
    wq_t = pad(w_in[:, :d]).T.astype(BF16)
    wk = pad(w_in[:, d:2 * d]).astype(BF16)
    vrows = heads * (hd + VAL_PAD)
    wv_t = jnp.pad(w_in[:, 2 * d:3 * d].reshape(d, heads, hd),
                   ((0, 0), (0, 0), (0, VAL_PAD))).reshape(d, vrows).T.astype(BF16)
    sel_k = np.zeros((GATE_PARTS, LANES, wide), np.float32)
    sel_q = np.zeros((GATE_PARTS, wide, LANES), np.float32)
    one_k = np.zeros((1, wide), np.float32)
    one_q = np.zeros((wide, 1), np.float32)
    one_v = np.zeros((vrows, 1), np.float32)
    for hh in range(heads):
        base = hh * HEAD_PAD + hd
        one_v[hh * (hd + VAL_PAD) + hd, 0] = 1.0
        for n in range(GATE_PARTS):
            sel_k[n, hh, base + n] = -1.0
            one_q[base + n, 0] = 1.0
            one_k[0, base + GATE_PARTS + n] = 1.0
            sel_q[n, base + GATE_PARTS + n, hh] = 1.0
    return pl.pallas_call(
        functools.partial(_fox_proj_kernel, q_scale=hd ** -0.5),
        grid=(t // tm,),
        in_specs=[
            pl.BlockSpec((tm, d), lambda i: (i, 0)),
            pl.BlockSpec((1, 6, d), lambda i: (i * tm // seq, 0, 0)),
            pl.BlockSpec((tm, LANES), lambda i: (i, 0)),
            _const_spec((wide, d)),
            _const_spec((d, wide)),
            _const_spec((vrows, d)),
            _const_spec((GATE_PARTS, LANES, wide)),
            _const_spec((1, wide)),
            _const_spec((GATE_PARTS, wide, LANES)),
            _const_spec((wide, 1)),
            _const_spec((vrows, 1)),
        ],
        out_specs=[
            pl.BlockSpec((tm, wide), lambda i: (i, 0)),
            pl.BlockSpec((1, wide, tm), lambda i: (i // nb, 0, i % nb)),
            pl.BlockSpec((1, vrows, tm), lambda i: (i // nb, 0, i % nb)),
        ],
        out_shape=[
            jax.ShapeDtypeStruct((t, wide), BF16),
            jax.ShapeDtypeStruct((bsz, wide, seq), BF16),
            jax.ShapeDtypeStruct((bsz, vrows, seq), BF16),
        ],
        compiler_params=_cparams(("parallel",)),
        name="fox_proj",
    )(x, mod, cs, wq_t, wk, wv_t, jnp.asarray(sel_k, BF16), jnp.asarray(one_k), jnp.asarray(sel_q, BF16),
      jnp.asarray(one_q), jnp.asarray(one_v))


def _flash_kernel(qi_ref, kj_ref, k_ref, q_ref, v_ref, o_ref, m_ref, acc_ref, s_ref, *, heads, hd, blk):
    p_idx = pl.program_id(1)
    qi = qi_ref[p_idx]
    kj = kj_ref[p_idx]
    vp = hd + VAL_PAD

    @pl.when(kj == 0)
    def _():
        m_ref[...] = jnp.full(m_ref.shape, -jnp.inf, F32)
        acc_ref[...] = jnp.zeros_like(acc_ref)

    def step(masked):
        if masked:
            kpos = lax.broadcasted_iota(jnp.int32, (blk, blk), 0)
            qpos = lax.broadcasted_iota(jnp.int32, (blk, blk), 1)
            keep = kpos <= qpos

        def scores(h):
            hs = slice(h * HEAD_PAD, (h + 1) * HEAD_PAD)
            s = _dot(k_ref[:, hs], q_ref[0, hs, :])
            if masked:
                s = jnp.where(keep, s, -jnp.inf)
            s_ref[h % 2] = s
            return jnp.max(s, axis=0, keepdims=True)

        m_cur = scores(0)
        for h in range(heads):
            m_nxt = scores(h + 1) if h + 1 < heads else None
            m_prev = m_ref[h:h + 1, :]
            m_new = jnp.maximum(m_prev, m_cur)
            m_ref[h:h + 1, :] = m_new
            a = jnp.exp2(m_prev - m_new)
            p = jnp.exp2(s_ref[h % 2] - m_new).astype(BF16)
            rs = slice(h * vp, (h + 1) * vp)
            acc_ref[rs, :] = acc_ref[rs, :] * a + _dot(v_ref[0, rs, :], p)
            m_cur = m_nxt

    @pl.when(kj < qi)
    def _():
        step(False)

    @pl.when(kj == qi)
    def _():
        step(True)
        per = LANES // hd
        for g in range(heads // per):
            parts = []
            for n in range(per):
                r0 = (g * per + n) * vp
                parts.append(acc_ref[r0:r0 + hd, :] / acc_ref[r0 + hd:r0 + hd + 1, :])
            o_ref[:, g * LANES:(g + 1) * LANES] = jnp.concatenate(parts, axis=0).T.astype(BF16)


def _flash(k_aug, q_aug_t, v_t, *, bsz, seq, heads, blk=512):
    t, wide = k_aug.shape
    vrows = v_t.shape[1]
    hd = vrows // heads - VAL_PAD
    d = heads * hd
    nq = seq // blk
    pairs = [(i, j) for i in range(nq) for j in range(i + 1)]
    qi = jnp.asarray(np.array([p[0] for p in pairs], np.int32))
    kj = jnp.asarray(np.array([p[1] for p in pairs], np.int32))
    grid_spec = pltpu.PrefetchScalarGridSpec(
        num_scalar_prefetch=2,
        grid=(bsz, len(pairs)),
        in_specs=[
            pl.BlockSpec((blk, wide), lambda b, p, qi, kj: (b * nq + kj[p], 0)),
            pl.BlockSpec((1, wide, blk), lambda b, p, qi, kj: (b, 0, qi[p])),
            pl.BlockSpec((1, vrows, blk), lambda b, p, qi, kj: (b, 0, kj[p])),
        ],
        out_specs=pl.BlockSpec((blk, d), lambda b, p, qi, kj: (b * nq + qi[p], 0)),
        scratch_shapes=[
            pltpu.VMEM((heads, blk), F32),
            pltpu.VMEM((vrows, blk), F32),
            pltpu.VMEM((2, blk, blk), F32),
        ],
    )
    return pl.pallas_call(
        functools.partial(_flash_kernel, heads=heads, hd=hd, blk=blk),
        grid_spec=grid_spec,
        out_shape=jax.ShapeDtypeStruct((t, d), BF16),
        compiler_params=_cparams(("parallel", "arbitrary")),
        name="fox_attention",
    )(qi, kj, k_aug, q_aug_t, v_t)


def _mm_resln_kernel(a_ref, x_ref, mod_ref, w_ref, g_ref, b_ref, o_ref, *, alpha):
    y = _dot(a_ref[...], w_ref[...])
    o_ref[...] = _res_ln(x_ref[...], y, mod_ref[0], 0, g_ref[...], b_ref[...], alpha)


def _mm_resln(a, x, mod, w, ln_g, ln_b, *, seq, alpha, tm=512):
    t, d = x.shape
    k = a.shape[1]
    return pl.pallas_call(
        functools.partial(_mm_resln_kernel, alpha=alpha),
        grid=(t // tm,),
        in_specs=[
            pl.BlockSpec((tm, k), lambda i: (i, 0)),
            pl.BlockSpec((tm, d), lambda i: (i, 0)),
            pl.BlockSpec((1, 6, d), lambda i: (i * tm // seq, 0, 0)),
            _const_spec((k, d)),
            _const_spec((1, d)),
            _const_spec((1, d)),
        ],
        out_specs=pl.BlockSpec((tm, d), lambda i: (i, 0)),
        out_shape=jax.ShapeDtypeStruct((t, d), F32),
        compiler_params=_cparams(("parallel",)),
        name="proj_res_ln",
    )(a, x, mod, w.astype(BF16), ln_g.reshape(1, d), ln_b.reshape(1, d))


def _fox_layer(x, mod, w_in, b_f, w_out, ln_g, ln_b, *, bsz, seq, alpha):
    heads = b_f.shape[0]
    d = x.shape[1]
    cs = _forget_cumsum(x, mod, w_in[:, 3 * d:], b_f, bsz=bsz, seq=seq)
    k_aug, q_aug_t, v_t = _fox_proj(x, mod, cs, w_in, bsz=bsz, seq=seq, heads=heads)
    attn = _flash(k_aug, q_aug_t, v_t, bsz=bsz, seq=seq, heads=heads)
    return _mm_resln(attn, x, mod, w_out, ln_g, ln_b, seq=seq, alpha=alpha)


def _pool_kernel(x_ref, xh_ref, mod_ref, win_ref, wg_ref, ls_ref, wout_ref, g_ref, b_ref, o_ref, pooled_ref,
                 *, tm, seq, alpha):
    i = pl.program_id(0)
    pos0 = (i * tm) % seq
    m = mod_ref[0]
    x = x_ref[...]
    halo_ok = jnp.where(pos0 > 0, jnp.float32(1.0), jnp.float32(0.0))
    xe = jnp.concatenate([xh_ref[...], x], axis=0)
    ze = _dot(_modulate(xe, m, 0).astype(BF16), win_ref[...])
    rows = lax.broadcasted_iota(jnp.int32, (tm + POOL_HALO, 1), 0)
    ze = ze * jnp.where(rows < POOL_HALO, halo_ok, jnp.float32(1.0))
    pos = (pos0 + lax.broadcasted_iota(jnp.int32, (tm, 1), 0) + 1).astype(F32)
    gd = ze.shape[1] // len(POOL_WINDOWS)
    for g, win in enumerate(POOL_WINDOWS):
        cs = slice(g * gd, (g + 1) * gd)
        zg = ze[:, cs]
        s = zg
        span = 1
        while span < win:
            s = s + pltpu.roll(s, span, 0)
            span *= 2
        count = jnp.minimum(pos, jnp.float32(win))
        pooled = s[POOL_HALO:, :] / count - zg[POOL_HALO:, :]
        pooled_ref[:, cs] = (_dot(pooled.astype(BF16), wg_ref[g]) * ls_ref[:, cs]).astype(BF16)
    y = _dot(pooled_ref[...], wout_ref[...])
    o_ref[...] = _res_ln(x, y, m, 0, g_ref[...], b_ref[...], alpha)


def _pool_layer(x, mod, w_in, w_grp, scale, w_out, ln_g, ln_b, *, seq, alpha, tm=512):
    t, d = x.shape
    ng, gd, _ = w_grp.shape
    hb = tm // POOL_HALO
    return pl.pallas_call(
        functools.partial(_pool_kernel, tm=tm, seq=seq, alpha=alpha),
        grid=(t // tm,),
        in_specs=[
            pl.BlockSpec((tm, d), lambda i: (i, 0)),
            pl.BlockSpec((POOL_HALO, d), lambda i: (jnp.maximum(i * hb - 1, 0), 0)),
            pl.BlockSpec((1, 6, d), lambda i: (i * tm // seq, 0, 0)),
            _const_spec((d, d)),
            _const_spec((ng, gd, gd)),
            _const_spec((1, d)),
            _const_spec((d, d)),
            _const_spec((1, d)),
            _const_spec((1, d)),
        ],
        out_specs=pl.BlockSpec((tm, d), lambda i: (i, 0)),
        out_shape=jax.ShapeDtypeStruct((t, d), F32),
        scratch_shapes=[pltpu.VMEM((tm, d), BF16)],
        compiler_params=_cparams(("parallel",)),
        name="pool_mixer",
    )(x, x, mod, w_in.astype(BF16), w_grp.astype(BF16), scale.reshape(1, d), w_out.astype(BF16),
      ln_g.reshape(1, d), ln_b.reshape(1, d))


def kernel(x, c, mod_w, mod_b, ln_g, ln_b, gm_w_in, gm_ln_g, gm_ln_b, gm_w_s, gm_b_s, gm_w_out, fox_w_in,
           fox_b_f, fox_w_out, pool_w_in, pool_w_grp, pool_scale, pool_w_out, ffn_w13, ffn_w2, moe_w_router,
           moe_b_router, moe_w13, moe_w2):
    bsz, seq, d = x.shape
    depth = mod_w.shape[0]
    alpha = (2 * depth) ** 0.25
    n_mixers = 3
    mod_all = _modulation(c, mod_w, mod_b)
    xt = x.reshape(bsz * seq, d)
    for i in range(depth):
        mod = mod_all[i]
        kind, j = i % n_mixers, i // n_mixers
        if kind == 0:
            xt = _gmlp_layer(xt, mod, gm_w_in[j], gm_ln_g[j], gm_ln_b[j], gm_w_s[j], gm_b_s[j], gm_w_out[j],
                             ln_g[i, 0], ln_b[i, 0], seq=seq, alpha=alpha)
        elif kind == 1:
            xt = _fox_layer(xt, mod, fox_w_in[j], fox_b_f[j], fox_w_out[j], ln_g[i, 0], ln_b[i, 0],
                            bsz=bsz, seq=seq, alpha=alpha)
        else:
            xt = _pool_layer(xt, mod, pool_w_in[j], pool_w_grp[j], pool_scale[j], pool_w_out[j],
                             ln_g[i, 0], ln_b[i, 0], seq=seq, alpha=alpha)
        if i % 2 == 0:
            xt = _ffn_layer(xt, mod, ffn_w13[i // 2], ffn_w2[i // 2], ln_g[i, 1], ln_b[i, 1],
                            seq=seq, alpha=alpha)
        else:
            xt = _moe_layer(xt, mod, moe_w_router[i // 2], moe_b_router[i // 2], moe_w13[i // 2],
                            moe_w2[i // 2], ln_g[i, 1], ln_b[i, 1], seq=seq, alpha=alpha)
    return xt.reshape(bsz, seq, d)
```

```python
import functools

import numpy as np
import jax
import jax.numpy as jnp
from jax import lax
from jax.experimental import pallas as pl
from jax.experimental.pallas import tpu as pltpu
from jax.experimental.pallas import tpu_sc as plsc

F32 = jnp.float32
BF16 = jnp.bfloat16

POOL_WINDOWS = (2, 4, 8, 16)
TOP_K = 2
EXPERT_BLOCK = 512
LN_EPS = 1e-5
LANES = 128
POOL_HALO = 16
VMEM_LIMIT = 56 * 1024 * 1024
SC_CORES = 2
SC_SUBCORES = 16
SC_GATHER_ROWS = 64


def _cparams(sem):
    return pltpu.CompilerParams(dimension_semantics=sem, vmem_limit_bytes=VMEM_LIMIT)


def _const_spec(shape):
    nd = len(shape)
    return pl.BlockSpec(shape, lambda *_: (0,) * nd, pipeline_mode=pl.Buffered(1))


def _layer_norm(r, g, b):
    mu = jnp.mean(r, axis=-1, keepdims=True)
    xc = r - mu
    var = jnp.mean(xc * xc, axis=-1, keepdims=True)
    return xc * lax.rsqrt(var + LN_EPS) * g + b


def _modulate(x, m, off):
    return x * (1.0 + m[off + 1:off + 2]) + m[off:off + 1]


def _res_ln(x, y, m, off, g, b, alpha):
    return _layer_norm(alpha * x + (1.0 + m[off + 2:off + 3]) * y, g, b)


def _split3(a):
    hi = a.astype(BF16)
    r1 = a - hi.astype(F32)
    mid = r1.astype(BF16)
    lo = (r1 - mid.astype(F32)).astype(BF16)
    return hi, mid, lo


def _dot(a, b):
    return jnp.dot(a, b, preferred_element_type=F32)


def _mod_kernel(c_ref, w_ref, b_ref, o_ref):
    c = c_ref[...]
    s = c / (1.0 + jnp.exp(-c))
    s_hi, s_mid, _ = _split3(s)
    w = w_ref[0]
    w_hi = w.astype(BF16)
    w_lo = (w - w_hi.astype(F32)).astype(BF16)
    acc = _dot(s_hi, w_hi) + _dot(s_mid, w_hi) + _dot(s_hi, w_lo)
    o_ref[0] = acc + b_ref[0]


def _modulation(c, mod_w, mod_b):
    depth, d, n = mod_w.shape
    bsz = c.shape[0]
    rows = 8
    tn = n // 4
    c_pad = jnp.zeros((rows, d), F32).at[:bsz].set(c)
    out = pl.pallas_call(
        _mod_kernel,
        grid=(depth, n // tn),
        in_specs=[
            pl.BlockSpec((rows, d), lambda l, j: (0, 0)),
            pl.BlockSpec((1, d, tn), lambda l, j: (l, 0, j)),
            pl.BlockSpec((1, 1, tn), lambda l, j: (l, 0, j)),
        ],
        out_specs=pl.BlockSpec((1, rows, tn), lambda l, j: (l, 0, j)),
        out_shape=jax.ShapeDtypeStruct((depth, rows, n), F32),
        compiler_params=_cparams(("parallel", "parallel")),
        name="adaln_mod",
    )(c_pad, mod_w, mod_b.reshape(depth, 1, n))
    return out[:, :bsz, :].reshape(depth, bsz, 6, d)


def _gmlp_kernel(x_ref, mod_ref, win_ref, vg_ref, vb_ref, ws_ref, bst_ref, wout_ref, g_ref, b_ref,
                 o_ref, gated_ref, *, tm, chunk, groups, alpha):
    x = x_ref[...]
    m = mod_ref[0]
    h = _modulate(x, m, 0).astype(BF16)
    z = _dot(h, win_ref[...])
    z = 0.5 * z * (1.0 + lax.erf(z * (2.0 ** -0.5)))
    width = z.shape[1] // 2
    gd = width // groups
    u = z[:, :width]
    v = _layer_norm(z[:, width:], vg_ref[...], vb_ref[...]).astype(BF16)
    row = lax.broadcasted_iota(jnp.int32, (chunk, chunk), 0)
    col = lax.broadcasted_iota(jnp.int32, (chunk, chunk), 1)
    causal = row >= col
    bst = bst_ref[...]
    for g in range(groups):
        w = jnp.where(causal, ws_ref[g], 0.0).astype(BF16)
        bias = bst[:, g:g + 1]
        for c in range(tm // chunk):
            rs = slice(c * chunk, (c + 1) * chunk)
            cs = slice(g * gd, (g + 1) * gd)
            mixed = _dot(w, v[rs, cs]) + bias
            gated_ref[rs, cs] = (u[rs, cs] * mixed).astype(BF16)
    y = _dot(gated_ref[...], wout_ref[...])
    o_ref[...] = _res_ln(x, y, m, 0, g_ref[...], b_ref[...], alpha)


def _gmlp_layer(x, mod, w_in, v_g, v_b, w_s, b_s, w_out, ln_g, ln_b, *, seq, alpha, tm=256):
    t, d = x.shape
    groups, chunk, _ = w_s.shape
    width = w_out.shape[0]
    kern = functools.partial(_gmlp_kernel, tm=tm, chunk=chunk, groups=groups, alpha=alpha)
    return pl.pallas_call(
        kern,
        grid=(t // tm,),
        in_specs=[
            pl.BlockSpec((tm, d), lambda i: (i, 0)),
            pl.BlockSpec((1, 6, d), lambda i: (i * tm // seq, 0, 0)),
            _const_spec((d, 2 * width)),
            _const_spec((1, width)),
            _const_spec((1, width)),
            _const_spec((groups, chunk, chunk)),
            _const_spec((chunk, groups)),
            _const_spec((width, d)),
            _const_spec((1, d)),
            _const_spec((1, d)),
        ],
        out_specs=pl.BlockSpec((tm, d), lambda i: (i, 0)),
        out_shape=jax.ShapeDtypeStruct((t, d), F32),
        scratch_shapes=[pltpu.VMEM((tm, width), BF16)],
        compiler_params=_cparams(("parallel",)),
        name="gmlp_mixer",
    )(x, mod, w_in.astype(BF16), v_g.reshape(1, width), v_b.reshape(1, width), w_s, b_s.T,
      w_out.astype(BF16), ln_g.reshape(1, d), ln_b.reshape(1, d))


def _swiglu_step(h, w1_ref, w3_ref, w2_ref, acc_ref, j):
    a = _dot(h, w1_ref[...])
    b = _dot(h, w3_ref[...])
    t = (a / (1.0 + jnp.exp(-a)) * b).astype(BF16)
    part = _dot(t, w2_ref[...])

    @pl.when(j == 0)
    def _():
        acc_ref[...] = part

    @pl.when(j > 0)
    def _():
        acc_ref[...] += part


def _ffn_kernel(x_ref, mod_ref, w1_ref, w3_ref, w2_ref, g_ref, b_ref, o_ref, h_ref, acc_ref, *, alpha):
    j = pl.program_id(1)

    @pl.when(j == 0)
    def _():
        h_ref[...] = _modulate(x_ref[...], mod_ref[0], 3).astype(BF16)

    _swiglu_step(h_ref[...], w1_ref, w3_ref, w2_ref, acc_ref, j)

    @pl.when(j == pl.num_programs(1) - 1)
    def _():
        o_ref[...] = _res_ln(x_ref[...], acc_ref[...], mod_ref[0], 3, g_ref[...], b_ref[...], alpha)


def _ffn_layer(x, mod, w13, w2, ln_g, ln_b, *, seq, alpha, tm=512, tf=512):
    t, d = x.shape
    f = w2.shape[0]
    nf = f // tf
    return pl.pallas_call(
        functools.partial(_ffn_kernel, alpha=alpha),
        grid=(t // tm, nf),
        in_specs=[
            pl.BlockSpec((tm, d), lambda i, j: (i, 0)),
            pl.BlockSpec((1, 6, d), lambda i, j: (i * tm // seq, 0, 0)),
            pl.BlockSpec((d, tf), lambda i, j: (0, j)),
            pl.BlockSpec((d, tf), lambda i, j: (0, nf + j)),
            pl.BlockSpec((tf, d), lambda i, j: (j, 0)),
            _const_spec((1, d)),
            _const_spec((1, d)),
        ],
        out_specs=pl.BlockSpec((tm, d), lambda i, j: (i, 0)),
        out_shape=jax.ShapeDtypeStruct((t, d), F32),
        scratch_shapes=[pltpu.VMEM((tm, d), BF16), pltpu.VMEM((tm, d), F32)],
        compiler_params=_cparams(("parallel", "arbitrary")),
        name="swiglu_dense",
    )(x, mod, w13.astype(BF16), w13.astype(BF16), w2.astype(BF16), ln_g.reshape(1, d), ln_b.reshape(1, d))


def _moe_ffn_kernel(be_ref, nv_ref, h_ref, w1_ref, w3_ref, w2_ref, o_ref, acc_ref):
    i = pl.program_id(0)
    j = pl.program_id(1)
    last = pl.num_programs(1) - 1
    valid = i < nv_ref[0]

    @pl.when(valid)
    def _():
        _swiglu_step(h_ref[...].astype(BF16), w1_ref.at[0], w3_ref.at[0], w2_ref.at[0], acc_ref, j)

    @pl.when(jnp.logical_and(valid, j == last))
    def _():
        o_ref[...] = acc_ref[...]

    @pl.when(jnp.logical_and(jnp.logical_not(valid), j == last))
    def _():
        o_ref[...] = jnp.zeros_like(o_ref)


def _moe_ffn(h_slots, block_expert, n_valid, w13, w2, *, tf=512):
    cap, d = h_slots.shape
    f = w2.shape[1]
    nf = f // tf
    tm = EXPERT_BLOCK

    def jj(i, j, nv):
        return jnp.where(i < nv[0], j, nf - 1)

    grid_spec = pltpu.PrefetchScalarGridSpec(
        num_scalar_prefetch=2,
        grid=(cap // tm, nf),
        in_specs=[
            pl.BlockSpec((tm, d), lambda i, j, be, nv: (i, 0)),
            pl.BlockSpec((1, d, tf), lambda i, j, be, nv: (be[i], 0, jj(i, j, nv))),
            pl.BlockSpec((1, d, tf), lambda i, j, be, nv: (be[i], 0, nf + jj(i, j, nv))),
            pl.BlockSpec((1, tf, d), lambda i, j, be, nv: (be[i], jj(i, j, nv), 0)),
        ],
        out_specs=pl.BlockSpec((tm, d), lambda i, j, be, nv: (i, 0)),
        scratch_shapes=[pltpu.VMEM((tm, d), F32)],
    )
    w13b = w13.astype(BF16)
    return pl.pallas_call(
        _moe_ffn_kernel,
        grid_spec=grid_spec,
        out_shape=jax.ShapeDtypeStruct((cap, d), F32),
        compiler_params=_cparams(("parallel", "arbitrary")),
        name="swiglu_experts",
    )(block_expert, n_valid, h_slots, w13b, w13b, w2.astype(BF16))


def _router_kernel(x_ref, mod_ref, wr_ref, br_ref, h_ref, meta_ref, cnt_ref, carry_ref, *, tm, n_exp):
    i = pl.program_id(0)

    @pl.when(i == 0)
    def _():
        carry_ref[...] = jnp.zeros_like(carry_ref)

    h = _modulate(x_ref[...], mod_ref[0], 3)
    h_ref[...] = h
    h_hi, h_mid, _ = _split3(h)
    w = wr_ref[...]
    w_hi = w.astype(BF16)
    w_lo = (w - w_hi.astype(F32)).astype(BF16)
    logits = _dot(h_hi, w_hi) + _dot(h_mid, w_hi) + _dot(h_hi, w_lo) + br_ref[...]
    lane = lax.broadcasted_iota(jnp.int32, (tm, LANES), 1)
    neg = jnp.float32(-jnp.inf)
    logits = jnp.where(lane < n_exp, logits, neg)
    v0 = jnp.max(logits, axis=1, keepdims=True)
    e0 = jnp.min(jnp.where(logits == v0, lane, LANES), axis=1, keepdims=True)
    rest = jnp.where(lane == e0, neg, logits)
    v1 = jnp.max(rest, axis=1, keepdims=True)
    e1 = jnp.min(jnp.where(rest == v1, lane, LANES), axis=1, keepdims=True)
    p = jnp.exp(v1 - v0)
    g0 = 1.0 / (1.0 + p)
    g1 = p / (1.0 + p)
    oh0 = lane == e0
    oh1 = lane == e1
    onehot = jnp.where(jnp.logical_or(oh0, oh1), 1.0, 0.0).astype(BF16)
    r = lax.broadcasted_iota(jnp.int32, (tm, tm), 0)
    c = lax.broadcasted_iota(jnp.int32, (tm, tm), 1)
    strict = jnp.where(r > c, 1.0, 0.0).astype(BF16)
    before = _dot(strict, onehot) + carry_ref[0:1, :]
    rank0 = jnp.sum(jnp.where(oh0, before, 0.0), axis=1, keepdims=True)
    rank1 = jnp.sum(jnp.where(oh1, before, 0.0), axis=1, keepdims=True)
    total = carry_ref[0:1, :] + jnp.sum(onehot.astype(F32), axis=0, keepdims=True)
    carry_ref[...] = jnp.broadcast_to(total, carry_ref.shape)
    cnt_ref[...] = jnp.broadcast_to(total, cnt_ref.shape)
    meta = jnp.where(lane == 0, e0.astype(F32), 0.0)
    meta = jnp.where(lane == 1, e1.astype(F32), meta)
    meta = jnp.where(lane == 2, g0, meta)
    meta = jnp.where(lane == 3, g1, meta)
    meta = jnp.where(lane == 4, rank0, meta)
    meta = jnp.where(lane == 5, rank1, meta)
    meta_ref[...] = meta


def _router(x, mod, w_router, b_router, *, seq, tm=512):
    t, d = x.shape
    n_exp = w_router.shape[1]
    wr = jnp.zeros((d, LANES), F32).at[:, :n_exp].set(w_router)
    br = jnp.zeros((1, LANES), F32).at[0, :n_exp].set(b_router)
    return pl.pallas_call(
        functools.partial(_router_kernel, tm=tm, n_exp=n_exp),
        grid=(t // tm,),
        in_specs=[
            pl.BlockSpec((tm, d), lambda i: (i, 0)),
            pl.BlockSpec((1, 6, d), lambda i: (i * tm // seq, 0, 0)),
            _const_spec((d, LANES)),
            _const_spec((1, LANES)),
        ],
        out_specs=[
            pl.BlockSpec((tm, d), lambda i: (i, 0)),
            pl.BlockSpec((tm, LANES), lambda i: (i, 0)),
            pl.BlockSpec((8, LANES), lambda i: (0, 0)),
        ],
        out_shape=[
            jax.ShapeDtypeStruct((t, d), F32),
            jax.ShapeDtypeStruct((t, LANES), F32),
            jax.ShapeDtypeStruct((8, LANES), F32),
        ],
        scratch_shapes=[pltpu.VMEM((8, LANES), F32)],
        compiler_params=_cparams(("arbitrary",)),
        name="moe_router",
    )(x, mod, wr, br)


def _combine_kernel(x_ref, ya_ref, yb_ref, meta_ref, mod_ref, g_ref, b_ref, o_ref, *, alpha):
    meta = meta_ref[...]
    y = meta[:, 2:3] * ya_ref[...] + meta[:, 3:4] * yb_ref[...]
    o_ref[...] = _res_ln(x_ref[...], y, mod_ref[0], 3, g_ref[...], b_ref[...], alpha)


def _combine(x, ya, yb, meta, mod, ln_g, ln_b, *, seq, alpha, tm=512):
    t, d = x.shape
    row = pl.BlockSpec((tm, d), lambda i: (i, 0))
    return pl.pallas_call(
        functools.partial(_combine_kernel, alpha=alpha),
        grid=(t // tm,),
        in_specs=[row, row, row,
                  pl.BlockSpec((tm, LANES), lambda i: (i, 0)),
                  pl.BlockSpec((1, 6, d), lambda i: (i * tm // seq, 0, 0)),
                  _const_spec((1, d)), _const_spec((1, d))],
        out_specs=row,
        out_shape=jax.ShapeDtypeStruct((t, d), F32),
        compiler_params=_cparams(("parallel",)),
        name="moe_combine",
    )(x, ya, yb, meta, mod, ln_g.reshape(1, d), ln_b.reshape(1, d))


def _sc_gather_rows(table, idx):
    n = idx.shape[0]
    width = table.shape[1]
    workers = SC_CORES * SC_SUBCORES
    per_worker = n // workers
    steps = per_worker // SC_GATHER_ROWS
    assert steps * SC_GATHER_ROWS * workers == n
    mesh = plsc.VectorSubcoreMesh(core_axis_name="c", subcore_axis_name="s", num_cores=SC_CORES,
                                  num_subcores=SC_SUBCORES)

    def body(table_hbm, idx_hbm, out_hbm, idx_v, rows_v, sem):
        wid = lax.axis_index("s") * SC_CORES + lax.axis_index("c")

        @pl.loop(0, steps)
        def _(j):
            off = pl.multiple_of(wid * per_worker + j * SC_GATHER_ROWS, SC_GATHER_ROWS)
            pltpu.sync_copy(idx_hbm.at[pl.ds(off, SC_GATHER_ROWS)], idx_v)
            pltpu.async_copy(table_hbm.at[idx_v], rows_v, sem).wait()
            pltpu.sync_copy(rows_v, out_hbm.at[pl.ds(off, SC_GATHER_ROWS)])

    return pl.kernel(
        body,
        out_type=jax.ShapeDtypeStruct((n, width), table.dtype),
        mesh=mesh,
        scratch_types=[
            pltpu.VMEM((SC_GATHER_ROWS,), jnp.int32),
            pltpu.VMEM((SC_GATHER_ROWS, width), table.dtype),
            pltpu.SemaphoreType.DMA,
        ],
        name="sc_row_gather",
    )(table, idx)


def _moe_layer(x, mod, w_router, b_router, w13, w2, ln_g, ln_b, *, seq, alpha):
    t, d = x.shape
    n_exp = w_router.shape[1]
    h, meta, cnt = _router(x, mod, w_router, b_router, seq=seq)
    e0 = meta[:, 0].astype(jnp.int32)
    e1 = meta[:, 1].astype(jnp.int32)
    rank0 = meta[:, 4].astype(jnp.int32)
    rank1 = meta[:, 5].astype(jnp.int32)
    counts = cnt[0, :n_exp].astype(jnp.int32)
    padded = (counts + EXPERT_BLOCK - 1) // EXPERT_BLOCK * EXPERT_BLOCK
    pad_end = jnp.cumsum(padded)
    pad_start = pad_end - padded
    dest0 = pad_start[e0] + rank0
    dest1 = pad_start[e1] + rank1
    cap = t * TOP_K + n_exp * EXPERT_BLOCK
    n_blocks = cap // EXPERT_BLOCK
    tok = jnp.arange(t, dtype=jnp.int32)
    slot_tok = jnp.zeros((cap,), jnp.int32).at[dest0].set(tok).at[dest1].set(tok)
    block_expert = jnp.minimum(
        jnp.searchsorted(pad_end, jnp.arange(n_blocks, dtype=jnp.int32) * EXPERT_BLOCK, side='right'),
        n_exp - 1).astype(jnp.int32)
    n_valid = (pad_end[-1:] // EXPERT_BLOCK).astype(jnp.int32)
    h_slots = _sc_gather_rows(h, slot_tok)
    y_slots = _moe_ffn(h_slots, block_expert, n_valid, w13, w2)
    ya = _sc_gather_rows(y_slots, dest0)
    yb = _sc_gather_rows(y_slots, dest1)
    return _combine(x, ya, yb, meta, mod, ln_g, ln_b, seq=seq, alpha=alpha)


def _gate_kernel(x_ref, mod_ref, wf_ref, bf_ref, o_ref, carry_ref, *, tm):
    @pl.when(pl.program_id(1) == 0)
    def _():
        carry_ref[...] = jnp.zeros_like(carry_ref)

    h = _modulate(x_ref[...], mod_ref[0], 0)
    h_hi, h_mid, _ = _split3(h)
    wf = wf_ref[...]
    wf_hi = wf.astype(BF16)
    wf_lo = (wf - wf_hi.astype(F32)).astype(BF16)
    f = _dot(h_hi, wf_hi) + _dot(h_mid, wf_hi) + _dot(h_hi, wf_lo) + bf_ref[...]
    lf = jnp.minimum(f, 0.0) - jnp.log(1.0 + jnp.exp(-jnp.abs(f)))
    r = lax.broadcasted_iota(jnp.int32, (tm, tm), 0)
    c = lax.broadcasted_iota(jnp.int32, (tm, tm), 1)
    tri = jnp.where(r >= c, 1.0, 0.0).astype(BF16)
    hi, mid, lo = _split3(lf)
    cs = _dot(tri, hi) + _dot(tri, mid) + _dot(tri, lo) + carry_ref[0:1, :]
    o_ref[...] = cs
    carry_ref[...] = jnp.broadcast_to(cs[tm - 1:tm, :], carry_ref.shape)


def _forget_cumsum(x, mod, w_f, b_f, *, bsz, seq, tm=512):
    t, d = x.shape
    heads = b_f.shape[0]
    nb = seq // tm
    wf = jnp.zeros((d, LANES), F32).at[:, :heads].set(w_f)
    bf = jnp.zeros((1, LANES), F32).at[0, :heads].set(b_f)
    return pl.pallas_call(
        functools.partial(_gate_kernel, tm=tm),
        grid=(bsz, nb),
        in_specs=[
            pl.BlockSpec((tm, d), lambda b, i: (b * nb + i, 0)),
            pl.BlockSpec((1, 6, d), lambda b, i: (b, 0, 0)),
            _const_spec((d, LANES)),
            _const_spec((1, LANES)),
        ],
        out_specs=pl.BlockSpec((tm, LANES), lambda b, i: (b * nb + i, 0)),
        out_shape=jax.ShapeDtypeStruct((t, LANES), F32),
        scratch_shapes=[pltpu.VMEM((8, LANES), F32)],
        compiler_params=_cparams(("parallel", "arbitrary")),
        name="fox_forget_cumsum",
    )(x, mod, wf, bf)


HEAD_PAD = LANES
GATE_PARTS = 3
VAL_PAD = 16
LOG2E = 1.4426950408889634


def _fox_proj_kernel(x_ref, mod_ref, cs_ref, wq_ref, wk_ref, wv_ref, selk_ref, onek_ref, selq_ref, oneq_ref,
                     onev_ref, k_ref, q_ref, v_ref, *, q_scale):
    h = _modulate(x_ref[...], mod_ref[0], 0).astype(BF16)
    cs = cs_ref[...] * LOG2E
    parts = _split3(cs)
    k = _dot(h, wk_ref[...]) + onek_ref[...]
    for n in range(GATE_PARTS):
        k = k + _dot(parts[n], selk_ref[n])
    k_ref[...] = k.astype(BF16)
    nt = (((1,), (1,)), ((), ()))
    parts_t = _split3(cs.T)
    q = lax.dot_general(wq_ref[...], h, nt, preferred_element_type=F32) * (q_scale * LOG2E) + oneq_ref[...]
    for n in range(GATE_PARTS):
        q = q + _dot(selq_ref[n], parts_t[n])
    q_ref[0] = q.astype(BF16)
    v = lax.dot_general(wv_ref[...], h, nt, preferred_element_type=F32) + onev_ref[...]
    v_ref[0] = v.astype(BF16)


def _fox_proj(x, mod, cs, w_in, *, bsz, seq, heads, tm=512):
    t, d = x.shape
    hd = d // heads
    nb = seq // tm
    wide = heads * HEAD_PAD
    pad = lambda w: jnp.pad(w.reshape(d, heads, hd), ((0, 0), (0, 0), (0, HEAD_PAD - hd))).reshape(d, wide)
    wq_t = pad(w_in[:, :d]).T.astype(BF16)
    wk = pad(w_in[:, d:2 * d]).astype(BF16)
    vrows = heads * (hd + VAL_PAD)
    wv_t = jnp.pad(w_in[:, 2 * d:3 * d].reshape(d, heads, hd),
                   ((0, 0), (0, 0), (0, VAL_PAD))).reshape(d, vrows).T.astype(BF16)
    sel_k = np.zeros((GATE_PARTS, LANES, wide), np.float32)
    sel_q = np.zeros((GATE_PARTS, wide, LANES), np.float32)
    one_k = np.zeros((1, wide), np.float32)
    one_q = np.zeros((wide, 1), np.float32)
    one_v = np.zeros((vrows, 1), np.float32)
    for hh in range(heads):
        base = hh * HEAD_PAD + hd
        one_v[hh * (hd + VAL_PAD) + hd, 0] = 1.0
        for n in range(GATE_PARTS):
            sel_k[n, hh, base + n] = -1.0
            one_q[base + n, 0] = 1.0
            one_k[0, base + GATE_PARTS + n] = 1.0
            sel_q[n, base + GATE_PARTS + n, hh] = 1.0
    return pl.pallas_call(
        functools.partial(_fox_proj_kernel, q_scale=hd ** -0.5),
        grid=(t // tm,),
        in_specs=[
            pl.BlockSpec((tm, d), lambda i: (i, 0)),
            pl.BlockSpec((1, 6, d), lambda i: (i * tm // seq, 0, 0)),
            pl.BlockSpec((tm, LANES), lambda i: (i, 0)),
            _const_spec((wide, d)),
            _const_spec((d, wide)),
            _const_spec((vrows, d)),
            _const_spec((GATE_PARTS, LANES, wide)),
            _const_spec((1, wide)),
            _const_spec((GATE_PARTS, wide, LANES)),
            _const_spec((wide, 1)),
            _const_spec((vrows, 1)),
        ],
        out_specs=[
            pl.BlockSpec((tm, wide), lambda i: (i, 0)),
            pl.BlockSpec((1, wide, tm), lambda i: (i // nb, 0, i % nb)),
            pl.BlockSpec((1, vrows, tm), lambda i: (i // nb, 0, i % nb)),
        ],
        out_shape=[
            jax.ShapeDtypeStruct((t, wide), BF16),
            jax.ShapeDtypeStruct((bsz, wide, seq), BF16),
            jax.ShapeDtypeStruct((bsz, vrows, seq), BF16),
        ],
        compiler_params=_cparams(("parallel",)),
        name="fox_proj",
    )(x, mod, cs, wq_t, wk, wv_t, jnp.asarray(sel_k, BF16), jnp.asarray(one_k), jnp.asarray(sel_q, BF16),
      jnp.asarray(one_q), jnp.asarray(one_v))


def _flash_kernel(qi_ref, kj_ref, k_ref, q_ref, v_ref, o_ref, m_ref, acc_ref, s_ref, *, heads, hd, blk):
    p_idx = pl.program_id(1)
    qi = qi_ref[p_idx]
    kj = kj_ref[p_idx]
    vp = hd + VAL_PAD

    @pl.when(kj == 0)
    def _():
        m_ref[...] = jnp.full(m_ref.shape, -jnp.inf, F32)
        acc_ref[...] = jnp.zeros_like(acc_ref)

    def step(masked):
        if masked:
            kpos = lax.broadcasted_iota(jnp.int32, (blk, blk), 0)
            qpos = lax.broadcasted_iota(jnp.int32, (blk, blk), 1)
            keep = kpos <= qpos

        def scores(h):
            hs = slice(h * HEAD_PAD, (h + 1) * HEAD_PAD)
            s = _dot(k_ref[:, hs], q_ref[0, hs, :])
            if masked:
                s = jnp.where(keep, s, -jnp.inf)
            s_ref[h % 2] = s
            return jnp.max(s, axis=0, keepdims=True)

        m_cur = scores(0)
        for h in range(heads):
            m_nxt = scores(h + 1) if h + 1 < heads else None
            m_prev = m_ref[h:h + 1, :]
            m_new = jnp.maximum(m_prev, m_cur)
            m_ref[h:h + 1, :] = m_new
            a = jnp.exp2(m_prev - m_new)
            p = jnp.exp2(s_ref[h % 2] - m_new).astype(BF16)
            rs = slice(h * vp, (h + 1) * vp)
            acc_ref[rs, :] = acc_ref[rs, :] * a + _dot(v_ref[0, rs, :], p)
            m_cur = m_nxt

    @pl.when(kj < qi)
    def _():
        step(False)

    @pl.when(kj == qi)
    def _():
        step(True)
        per = LANES // hd
        for g in range(heads // per):
            parts = []
            for n in range(per):
                r0 = (g * per + n) * vp
                parts.append(acc_ref[r0:r0 + hd, :] / acc_ref[r0 + hd:r0 + hd + 1, :])
            o_ref[:, g * LANES:(g + 1) * LANES] = jnp.concatenate(parts, axis=0).T.astype(BF16)


def _flash(k_aug, q_aug_t, v_t, *, bsz, seq, heads, blk=512):
    t, wide = k_aug.shape
    vrows = v_t.shape[1]
    hd = vrows // heads - VAL_PAD
    d = heads * hd
    nq = seq // blk
    pairs = [(i, j) for i in range(nq) for j in range(i + 1)]
    qi = jnp.asarray(np.array([p[0] for p in pairs], np.int32))
    kj = jnp.asarray(np.array([p[1] for p in pairs], np.int32))
    grid_spec = pltpu.PrefetchScalarGridSpec(
        num_scalar_prefetch=2,
        grid=(bsz, len(pairs)),
        in_specs=[
            pl.BlockSpec((blk, wide), lambda b, p, qi, kj: (b * nq + kj[p], 0)),
            pl.BlockSpec((1, wide, blk), lambda b, p, qi, kj: (b, 0, qi[p])),
            pl.BlockSpec((1, vrows, blk), lambda b, p, qi, kj: (b, 0, kj[p])),
        ],
        out_specs=pl.BlockSpec((blk, d), lambda b, p, qi, kj: (b * nq + qi[p], 0)),
        scratch_shapes=[
            pltpu.VMEM((heads, blk), F32),
            pltpu.VMEM((vrows, blk), F32),
            pltpu.VMEM((2, blk, blk), F32),
        ],
    )
    return pl.pallas_call(
        functools.partial(_flash_kernel, heads=heads, hd=hd, blk=blk),
        grid_spec=grid_spec,
        out_shape=jax.ShapeDtypeStruct((t, d), BF16),
        compiler_params=_cparams(("parallel", "arbitrary")),
        name="fox_attention",
    )(qi, kj, k_aug, q_aug_t, v_t)


def _mm_resln_kernel(a_ref, x_ref, mod_ref, w_ref, g_ref, b_ref, o_ref, *, alpha):
    y = _dot(a_ref[...], w_ref[...])
    o_ref[...] = _res_ln(x_ref[...], y, mod_ref[0], 0, g_ref[...], b_ref[...], alpha)


def _mm_resln(a, x, mod, w, ln_g, ln_b, *, seq, alpha, tm=512):
    t, d = x.shape
    k = a.shape[1]
    return pl.pallas_call(
        functools.partial(_mm_resln_kernel, alpha=alpha),
        grid=(t // tm,),
        in_specs=[
            pl.BlockSpec((tm, k), lambda i: (i, 0)),
            pl.BlockSpec((tm, d), lambda i: (i, 0)),
            pl.BlockSpec((1, 6, d), lambda i: (i * tm // seq, 0, 0)),
            _const_spec((k, d)),
            _const_spec((1, d)),
            _const_spec((1, d)),
        ],
        out_specs=pl.BlockSpec((tm, d), lambda i: (i, 0)),
        out_shape=jax.ShapeDtypeStruct((t, d), F32),
        compiler_params=_cparams(("parallel",)),
        name="proj_res_ln",
    )(a, x, mod, w.astype(BF16), ln_g.reshape(1, d), ln_b.reshape(1, d))


def _fox_layer(x, mod, w_in, b_f, w_out, ln_g, ln_b, *, bsz, seq, alpha):
    heads = b_f.shape[0]
    d = x.shape[1]
    cs = _forget_cumsum(x, mod, w_in[:, 3 * d:], b_f, bsz=bsz, seq=seq)
    k_aug, q_aug_t, v_t = _fox_proj(x, mod, cs, w_in, bsz=bsz, seq=seq, heads=heads)
    attn = _flash(k_aug, q_aug_t, v_t, bsz=bsz, seq=seq, heads=heads)
    return _mm_resln(attn, x, mod, w_out, ln_g, ln_b, seq=seq, alpha=alpha)


def _pool_kernel(x_ref, xh_ref, mod_ref, win_ref, wg_ref, ls_ref, wout_ref, g_ref, b_ref, o_ref, pooled_ref,
                 *, tm, seq, alpha):
    i = pl.program_id(0)
    pos0 = (i * tm) % seq
    m = mod_ref[0]
    x = x_ref[...]
    halo_ok = jnp.where(pos0 > 0, jnp.float32(1.0), jnp.float32(0.0))
    xe = jnp.concatenate([xh_ref[...], x], axis=0)
    ze = _dot(_modulate(xe, m, 0).astype(BF16), win_ref[...])
    rows = lax.broadcasted_iota(jnp.int32, (tm + POOL_HALO, 1), 0)
    ze = ze * jnp.where(rows < POOL_HALO, halo_ok, jnp.float32(1.0))
    pos = (pos0 + lax.broadcasted_iota(jnp.int32, (tm, 1), 0) + 1).astype(F32)
    gd = ze.shape[1] // len(POOL_WINDOWS)
    for g, win in enumerate(POOL_WINDOWS):
        cs = slice(g * gd, (g + 1) * gd)
        zg = ze[:, cs]
        s = zg
        span = 1
        while span < win:
            s = s + pltpu.roll(s, span, 0)
            span *= 2
        count = jnp.minimum(pos, jnp.float32(win))
        pooled = s[POOL_HALO:, :] / count - zg[POOL_HALO:, :]
        pooled_ref[:, cs] = (_dot(pooled.astype(BF16), wg_ref[g]) * ls_ref[:, cs]).astype(BF16)
    y = _dot(pooled_ref[...], wout_ref[...])
    o_ref[...] = _res_ln(x, y, m, 0, g_ref[...], b_ref[...], alpha)


def _pool_layer(x, mod, w_in, w_grp, scale, w_out, ln_g, ln_b, *, seq, alpha, tm=512):
    t, d = x.shape
    ng, gd, _ = w_grp.shape
    hb = tm // POOL_HALO
    return pl.pallas_call(
        functools.partial(_pool_kernel, tm=tm, seq=seq, alpha=alpha),
        grid=(t // tm,),
        in_specs=[
            pl.BlockSpec((tm, d), lambda i: (i, 0)),
            pl.BlockSpec((POOL_HALO, d), lambda i: (jnp.maximum(i * hb - 1, 0), 0)),
            pl.BlockSpec((1, 6, d), lambda i: (i * tm // seq, 0, 0)),
            _const_spec((d, d)),
            _const_spec((ng, gd, gd)),
            _const_spec((1, d)),
            _const_spec((d, d)),
            _const_spec((1, d)),
            _const_spec((1, d)),
        ],
        out_specs=pl.BlockSpec((tm, d), lambda i: (i, 0)),
        out_shape=jax.ShapeDtypeStruct((t, d), F32),
        scratch_shapes=[pltpu.VMEM((tm, d), BF16)],
        compiler_params=_cparams(("parallel",)),
        name="pool_mixer",
    )(x, x, mod, w_in.astype(BF16), w_grp.astype(BF16), scale.reshape(1, d), w_out.astype(BF16),
      ln_g.reshape(1, d), ln_b.reshape(1, d))


def kernel(x, c, mod_w, mod_b, ln_g, ln_b, gm_w_in, gm_ln_g, gm_ln_b, gm_w_s, gm_b_s, gm_w_out, fox_w_in,
           fox_b_f, fox_w_out, pool_w_in, pool_w_grp, pool_scale, pool_w_out, ffn_w13, ffn_w2, moe_w_router,
           moe_b_router, moe_w13, moe_w2):
    bsz, seq, d = x.shape
    depth = mod_w.shape[0]
    alpha = (2 * depth) ** 0.25
    n_mixers = 3
    mod_all = _modulation(c, mod_w, mod_b)
    xt = x.reshape(bsz * seq, d)
    for i in range(depth):
        mod = mod_all[i]
        kind, j = i % n_mixers, i // n_mixers
        if kind == 0:
            xt = _gmlp_layer(xt, mod, gm_w_in[j], gm_ln_g[j], gm_ln_b[j], gm_w_s[j], gm_b_s[j], gm_w_out[j],
                             ln_g[i, 0], ln_b[i, 0], seq=seq, alpha=alpha)
        elif kind == 1:
            xt = _fox_layer(xt, mod, fox_w_in[j], fox_b_f[j], fox_w_out[j], ln_g[i, 0], ln_b[i, 0],
                            bsz=bsz, seq=seq, alpha=alpha)
        else:
            xt = _pool_layer(xt, mod, pool_w_in[j], pool_w_grp[j], pool_scale[j], pool_w_out[j],
                             ln_g[i, 0], ln_b[i, 0], seq=seq, alpha=alpha)
        if i % 2 == 0:
            xt = _ffn_layer(xt, mod, ffn_w13[i // 2], ffn_w2[i // 2], ln_g[i, 1], ln_b[i, 1],
                            seq=seq, alpha=alpha)
        else:
            xt = _moe_layer(xt, mod, moe_w_router[i // 2], moe_b_router[i // 2], moe_w13[i // 2],
                            moe_w2[i // 2], ln_g[i, 1], ln_b[i, 1], seq=seq, alpha=alpha)
    return xt.reshape(bsz, seq, d)
```

```python
import functools

import numpy as np
import jax
import jax.numpy as jnp
from jax import lax
from jax.experimental import pallas as pl
from jax.experimental.pallas import tpu as pltpu
from jax.experimental.pallas import tpu_sc as plsc

F32 = jnp.float32
BF16 = jnp.bfloat16

POOL_WINDOWS = (2, 4, 8, 16)
TOP_K = 2
EXPERT_BLOCK = 512
LN_EPS = 1e-5
LANES = 128
POOL_HALO = 16
VMEM_LIMIT = 56 * 1024 * 1024
SC_CORES = 2
SC_SUBCORES = 16
SC_GATHER_ROWS = 64


def _cparams(sem):
    return pltpu.CompilerParams(dimension_semantics=sem, vmem_limit_bytes=VMEM_LIMIT)


def _const_spec(shape):
    nd = len(shape)
    return pl.BlockSpec(shape, lambda *_: (0,) * nd, pipeline_mode=pl.Buffered(1))


def _layer_norm(r, g, b):
    mu = jnp.mean(r, axis=-1, keepdims=True)
    xc = r - mu
    var = jnp.mean(xc * xc, axis=-1, keepdims=True)
    return xc * lax.rsqrt(var + LN_EPS) * g + b


def _modulate(x, m, off):
    return x * (1.0 + m[off + 1:off + 2]) + m[off:off + 1]


def _res_ln(x, y, m, off, g, b, alpha):
    return _layer_norm(alpha * x + (1.0 + m[off + 2:off + 3]) * y, g, b)


def _split3(a):
    hi = a.astype(BF16)
    r1 = a - hi.astype(F32)
    mid = r1.astype(BF16)
    lo = (r1 - mid.astype(F32)).astype(BF16)
    return hi, mid, lo


def _dot(a, b):
    return jnp.dot(a, b, preferred_element_type=F32)


def _mod_kernel(c_ref, w_ref, b_ref, o_ref):
    c = c_ref[...]
    s = c / (1.0 + jnp.exp(-c))
    s_hi, s_mid, _ = _split3(s)
    w = w_ref[0]
    w_hi = w.astype(BF16)
    w_lo = (w - w_hi.astype(F32)).astype(BF16)
    acc = _dot(s_hi, w_hi) + _dot(s_mid, w_hi) + _dot(s_hi, w_lo)
    o_ref[0] = acc + b_ref[0]


def _modulation(c, mod_w, mod_b):
    depth, d, n = mod_w.shape
    bsz = c.shape[0]
    rows = 8
    tn = n // 4
    c_pad = jnp.zeros((rows, d), F32).at[:bsz].set(c)
    out = pl.pallas_call(
        _mod_kernel,
        grid=(depth, n // tn),
        in_specs=[
            pl.BlockSpec((rows, d), lambda l, j: (0, 0)),
            pl.BlockSpec((1, d, tn), lambda l, j: (l, 0, j)),
            pl.BlockSpec((1, 1, tn), lambda l, j: (l, 0, j)),
        ],
        out_specs=pl.BlockSpec((1, rows, tn), lambda l, j: (l, 0, j)),
        out_shape=jax.ShapeDtypeStruct((depth, rows, n), F32),
        compiler_params=_cparams(("parallel", "parallel")),
        name="adaln_mod",
    )(c_pad, mod_w, mod_b.reshape(depth, 1, n))
    return out[:, :bsz, :].reshape(depth, bsz, 6, d)


def _gmlp_kernel(x_ref, mod_ref, win_ref, vg_ref, vb_ref, ws_ref, bst_ref, wout_ref, g_ref, b_ref,
                 o_ref, gated_ref, *, tm, chunk, groups, alpha):
    x = x_ref[...]
    m = mod_ref[0]
    h = _modulate(x, m, 0).astype(BF16)
    z = _dot(h, win_ref[...])
    z = 0.5 * z * (1.0 + lax.erf(z * (2.0 ** -0.5)))
    width = z.shape[1] // 2
    gd = width // groups
    u = z[:, :width]
    v = _layer_norm(z[:, width:], vg_ref[...], vb_ref[...]).astype(BF16)
    row = lax.broadcasted_iota(jnp.int32, (chunk, chunk), 0)
    col = lax.broadcasted_iota(jnp.int32, (chunk, chunk), 1)
    causal = row >= col
    bst = bst_ref[...]
    for g in range(groups):
        w = jnp.where(causal, ws_ref[g], 0.0).astype(BF16)
        bias = bst[:, g:g + 1]
        for c in range(tm // chunk):
            rs = slice(c * chunk, (c + 1) * chunk)
            cs = slice(g * gd, (g + 1) * gd)
            mixed = _dot(w, v[rs, cs]) + bias
            gated_ref[rs, cs] = (u[rs, cs] * mixed).astype(BF16)
    y = _dot(gated_ref[...], wout_ref[...])
    o_ref[...] = _res_ln(x, y, m, 0, g_ref[...], b_ref[...], alpha)


def _gmlp_layer(x, mod, w_in, v_g, v_b, w_s, b_s, w_out, ln_g, ln_b, *, seq, alpha, tm=256):
    t, d = x.shape
    groups, chunk, _ = w_s.shape
    width = w_out.shape[0]
    kern = functools.partial(_gmlp_kernel, tm=tm, chunk=chunk, groups=groups, alpha=alpha)
    return pl.pallas_call(
        kern,
        grid=(t // tm,),
        in_specs=[
            pl.BlockSpec((tm, d), lambda i: (i, 0)),
            pl.BlockSpec((1, 6, d), lambda i: (i * tm // seq, 0, 0)),
            _const_spec((d, 2 * width)),
            _const_spec((1, width)),
            _const_spec((1, width)),
            _const_spec((groups, chunk, chunk)),
            _const_spec((chunk, groups)),
            _const_spec((width, d)),
            _const_spec((1, d)),
            _const_spec((1, d)),
        ],
        out_specs=pl.BlockSpec((tm, d), lambda i: (i, 0)),
        out_shape=jax.ShapeDtypeStruct((t, d), F32),
        scratch_shapes=[pltpu.VMEM((tm, width), BF16)],
        compiler_params=_cparams(("parallel",)),
        name="gmlp_mixer",
    )(x, mod, w_in.astype(BF16), v_g.reshape(1, width), v_b.reshape(1, width), w_s, b_s.T,
      w_out.astype(BF16), ln_g.reshape(1, d), ln_b.reshape(1, d))


def _swiglu_rows(h, w1, w3, w2):
    a = _dot(h, w1)
    b = _dot(h, w3)
    t = (a / (1.0 + jnp.exp(-a)) * b).astype(BF16)
    return _dot(t, w2)


def _ffn_kernel(x_ref, mod_ref, w1_ref, w3_ref, w2_ref, g_ref, b_ref, o_ref, h_ref, *, alpha, sub):
    j = pl.program_id(1)
    last = pl.num_programs(1) - 1
    subs = [slice(s, s + sub) for s in range(0, x_ref.shape[0], sub)]

    @pl.when(j == 0)
    def _():
        for rows in subs:
            h = _modulate(x_ref[rows, :], mod_ref[0], 3).astype(BF16)
            h_ref[rows, :] = h
            o_ref[rows, :] = _swiglu_rows(h, w1_ref[...], w3_ref[...], w2_ref[...])

    @pl.when(jnp.logical_and(j > 0, j < last))
    def _():
        for rows in subs:
            o_ref[rows, :] += _swiglu_rows(h_ref[rows, :], w1_ref[...], w3_ref[...], w2_ref[...])

    @pl.when(j == last)
    def _():
        for rows in subs:
            y = o_ref[rows, :] + _swiglu_rows(h_ref[rows, :], w1_ref[...], w3_ref[...], w2_ref[...])
            o_ref[rows, :] = _res_ln(x_ref[rows, :], y, mod_ref[0], 3, g_ref[...], b_ref[...], alpha)


def _ffn_layer(x, mod, w13, w2, ln_g, ln_b, *, seq, alpha, tm=1024, tf=512):
    t, d = x.shape
    f = w2.shape[0]
    nf = f // tf
    assert nf >= 2
    w13b = w13.astype(BF16)
    return pl.pallas_call(
        functools.partial(_ffn_kernel, alpha=alpha, sub=tm // 2),
        grid=(t // tm, nf),
        in_specs=[
            pl.BlockSpec((tm, d), lambda i, j: (i, 0)),
            pl.BlockSpec((1, 6, d), lambda i, j: (i * tm // seq, 0, 0)),
            pl.BlockSpec((d, tf), lambda i, j: (0, j)),
            pl.BlockSpec((d, tf), lambda i, j: (0, nf + j)),
            pl.BlockSpec((tf, d), lambda i, j: (j, 0)),
            _const_spec((1, d)),
            _const_spec((1, d)),
        ],
        out_specs=pl.BlockSpec((tm, d), lambda i, j: (i, 0)),
        out_shape=jax.ShapeDtypeStruct((t, d), F32),
        scratch_shapes=[pltpu.VMEM((tm, d), BF16)],
        compiler_params=_cparams(("parallel", "arbitrary")),
        name="swiglu_dense",
    )(x, mod, w13b, w13b, w2.astype(BF16), ln_g.reshape(1, d), ln_b.reshape(1, d))


MOE_GROUP = 2


def _moe_ffn_kernel(ge_ref, gv_ref, h_ref, w1_ref, w3_ref, w2_ref, o_ref, w1b_ref, w3b_ref, w2b_ref):
    g = pl.program_id(0)
    j = pl.program_id(1)
    last = pl.num_programs(1) - 1
    nv = gv_ref[g]

    @pl.when(nv > 0)
    def _():
        w1b_ref[...] = w1_ref[0, 0].astype(BF16)
        w3b_ref[...] = w3_ref[0, 0].astype(BF16)
        w2b_ref[...] = w2_ref[0, 0].astype(BF16)

    for blk in range(MOE_GROUP):
        rows = slice(blk * EXPERT_BLOCK, (blk + 1) * EXPERT_BLOCK)

        @pl.when(blk < nv)
        def _():
            part = _swiglu_rows(h_ref[rows, :].astype(BF16), w1b_ref[...], w3b_ref[...], w2b_ref[...])

            @pl.when(j == 0)
            def _():
                o_ref[rows, :] = part

            @pl.when(j > 0)
            def _():
                o_ref[rows, :] += part

        @pl.when(jnp.logical_and(blk >= nv, j == last))
        def _():
            o_ref[rows, :] = jnp.zeros((EXPERT_BLOCK, o_ref.shape[1]), F32)


def _moe_ffn(h_slots, group_expert, group_valid, w13, w2, layer, *, tf=512):
    cap, d = h_slots.shape
    f = w2.shape[2]
    nf = f // tf
    tm = MOE_GROUP * EXPERT_BLOCK

    def jj(g, j, gv):
        return jnp.where(gv[g] > 0, j, nf - 1)

    grid_spec = pltpu.PrefetchScalarGridSpec(
        num_scalar_prefetch=2,
        grid=(cap // tm, nf),
        in_specs=[
            pl.BlockSpec((tm, d), lambda g, j, ge, gv: (g, 0)),
            pl.BlockSpec((1, 1, d, tf), lambda g, j, ge, gv: (layer, ge[g], 0, jj(g, j, gv))),
            pl.BlockSpec((1, 1, d, tf), lambda g, j, ge, gv: (layer, ge[g], 0, nf + jj(g, j, gv))),
            pl.BlockSpec((1, 1, tf, d), lambda g, j, ge, gv: (layer, ge[g], jj(g, j, gv), 0)),
        ],
        out_specs=pl.BlockSpec((tm, d), lambda g, j, ge, gv: (g, 0)),
        scratch_shapes=[pltpu.VMEM((d, tf), BF16), pltpu.VMEM((d, tf), BF16), pltpu.VMEM((tf, d), BF16)],
    )
    return pl.pallas_call(
        _moe_ffn_kernel,
        grid_spec=grid_spec,
        out_shape=jax.ShapeDtypeStruct((cap, d), F32),
        compiler_params=_cparams(("parallel", "arbitrary")),
        name="swiglu_experts",
    )(group_expert, group_valid, h_slots, w13, w13, w2)


def _router_kernel(x_ref, mod_ref, wr_ref, br_ref, h_ref, meta_ref, cnt_ref, carry_ref, *, tm, n_exp):
    i = pl.program_id(0)

    @pl.when(i == 0)
    def _():
        carry_ref[...] = jnp.zeros_like(carry_ref)

    h = _modulate(x_ref[...], mod_ref[0], 3)
    h_ref[...] = h
    h_hi, h_mid, _ = _split3(h)
    w = wr_ref[...]
    w_hi = w.astype(BF16)
    w_lo = (w - w_hi.astype(F32)).astype(BF16)
    logits = _dot(h_hi, w_hi) + _dot(h_mid, w_hi) + _dot(h_hi, w_lo) + br_ref[...]
    lane = lax.broadcasted_iota(jnp.int32, (tm, LANES), 1)
    neg = jnp.float32(-jnp.inf)
    logits = jnp.where(lane < n_exp, logits, neg)
    v0 = jnp.max(logits, axis=1, keepdims=True)
    e0 = jnp.min(jnp.where(logits == v0, lane, LANES), axis=1, keepdims=True)
    rest = jnp.where(lane == e0, neg, logits)
    v1 = jnp.max(rest, axis=1, keepdims=True)
    e1 = jnp.min(jnp.where(rest == v1, lane, LANES), axis=1, keepdims=True)
    p = jnp.exp(v1 - v0)
    g0 = 1.0 / (1.0 + p)
    g1 = p / (1.0 + p)
    oh0 = lane == e0
    oh1 = lane == e1
    onehot = jnp.where(jnp.logical_or(oh0, oh1), 1.0, 0.0).astype(BF16)
    r = lax.broadcasted_iota(jnp.int32, (tm, tm), 0)
    c = lax.broadcasted_iota(jnp.int32, (tm, tm), 1)
    strict = jnp.where(r > c, 1.0, 0.0).astype(BF16)
    before = _dot(strict, onehot) + carry_ref[0:1, :]
    rank0 = jnp.sum(jnp.where(oh0, before, 0.0), axis=1, keepdims=True)
    rank1 = jnp.sum(jnp.where(oh1, before, 0.0), axis=1, keepdims=True)
    total = carry_ref[0:1, :] + jnp.sum(onehot.astype(F32), axis=0, keepdims=True)
    carry_ref[...] = jnp.broadcast_to(total, carry_ref.shape)
    cnt_ref[...] = jnp.broadcast_to(total, cnt_ref.shape)
    meta = jnp.where(lane == 0, e0.astype(F32), 0.0)
    meta = jnp.where(lane == 1, e1.astype(F32), meta)
    meta = jnp.where(lane == 2, g0, meta)
    meta = jnp.where(lane == 3, g1, meta)
    meta = jnp.where(lane == 4, rank0, meta)
    meta = jnp.where(lane == 5, rank1, meta)
    meta_ref[...] = meta


def _router(x, mod, w_router, b_router, *, seq, tm=512):
    t, d = x.shape
    n_exp = w_router.shape[1]
    wr = jnp.zeros((d, LANES), F32).at[:, :n_exp].set(w_router)
    br = jnp.zeros((1, LANES), F32).at[0, :n_exp].set(b_router)
    return pl.pallas_call(
        functools.partial(_router_kernel, tm=tm, n_exp=n_exp),
        grid=(t // tm,),
        in_specs=[
            pl.BlockSpec((tm, d), lambda i: (i, 0)),
            pl.BlockSpec((1, 6, d), lambda i: (i * tm // seq, 0, 0)),
            _const_spec((d, LANES)),
            _const_spec((1, LANES)),
        ],
        out_specs=[
            pl.BlockSpec((tm, d), lambda i: (i, 0)),
            pl.BlockSpec((tm, LANES), lambda i: (i, 0)),
            pl.BlockSpec((8, LANES), lambda i: (0, 0)),
        ],
        out_shape=[
            jax.ShapeDtypeStruct((t, d), F32),
            jax.ShapeDtypeStruct((t, LANES), F32),
            jax.ShapeDtypeStruct((8, LANES), F32),
        ],
        scratch_shapes=[pltpu.VMEM((8, LANES), F32)],
        compiler_params=_cparams(("arbitrary",)),
        name="moe_router",
    )(x, mod, wr, br)


def _combine_kernel(x_ref, ya_ref, yb_ref, meta_ref, mod_ref, g_ref, b_ref, o_ref, *, alpha):
    meta = meta_ref[...]
    y = meta[:, 2:3] * ya_ref[...] + meta[:, 3:4] * yb_ref[...]
    o_ref[...] = _res_ln(x_ref[...], y, mod_ref[0], 3, g_ref[...], b_ref[...], alpha)


def _combine(x, ya, yb, meta, mod, ln_g, ln_b, *, seq, alpha, tm=512):
    t, d = x.shape
    row = pl.BlockSpec((tm, d), lambda i: (i, 0))
    return pl.pallas_call(
        functools.partial(_combine_kernel, alpha=alpha),
        grid=(t // tm,),
        in_specs=[row, row, row,
                  pl.BlockSpec((tm, LANES), lambda i: (i, 0)),
                  pl.BlockSpec((1, 6, d), lambda i: (i * tm // seq, 0, 0)),
                  _const_spec((1, d)), _const_spec((1, d))],
        out_specs=row,
        out_shape=jax.ShapeDtypeStruct((t, d), F32),
        compiler_params=_cparams(("parallel",)),
        name="moe_combine",
    )(x, ya, yb, meta, mod, ln_g.reshape(1, d), ln_b.reshape(1, d))


def _sc_gather_rows(table, idx):
    n = idx.shape[0]
    width = table.shape[1]
    workers = SC_CORES * SC_SUBCORES
    per_worker = n // workers
    steps = per_worker // SC_GATHER_ROWS
    assert steps * SC_GATHER_ROWS * workers == n
    mesh = plsc.VectorSubcoreMesh(core_axis_name="c", subcore_axis_name="s", num_cores=SC_CORES,
                                  num_subcores=SC_SUBCORES)

    def body(table_hbm, idx_hbm, out_hbm, idx_v, rows_v, sem):
        wid = lax.axis_index("s") * SC_CORES + lax.axis_index("c")

        @pl.loop(0, steps)
        def _(j):
            off = pl.multiple_of(wid * per_worker + j * SC_GATHER_ROWS, SC_GATHER_ROWS)
            pltpu.sync_copy(idx_hbm.at[pl.ds(off, SC_GATHER_ROWS)], idx_v)
            pltpu.async_copy(table_hbm.at[idx_v], rows_v, sem).wait()
            pltpu.sync_copy(rows_v, out_hbm.at[pl.ds(off, SC_GATHER_ROWS)])

    return pl.kernel(
        body,
        out_type=jax.ShapeDtypeStruct((n, width), table.dtype),
        mesh=mesh,
        scratch_types=[
            pltpu.VMEM((SC_GATHER_ROWS,), jnp.int32),
            pltpu.VMEM((SC_GATHER_ROWS, width), table.dtype),
            pltpu.SemaphoreType.DMA,
        ],
        name="sc_row_gather",
    )(table, idx)


def _moe_layer(x, mod, w_router, b_router, w13, w2, layer, ln_g, ln_b, *, seq, alpha):
    t, d = x.shape
    n_exp = w_router.shape[1]
    h, meta, cnt = _router(x, mod, w_router, b_router, seq=seq)
    e0 = meta[:, 0].astype(jnp.int32)
    e1 = meta[:, 1].astype(jnp.int32)
    rank0 = meta[:, 4].astype(jnp.int32)
    rank1 = meta[:, 5].astype(jnp.int32)
    counts = cnt[0, :n_exp].astype(jnp.int32)
    group_rows = MOE_GROUP * EXPERT_BLOCK
    padded = (counts + group_rows - 1) // group_rows * group_rows
    pad_end = jnp.cumsum(padded)
    pad_start = pad_end - padded
    dest0 = pad_start[e0] + rank0
    dest1 = pad_start[e1] + rank1
    cap = t * TOP_K + n_exp * group_rows
    tok = jnp.arange(t, dtype=jnp.int32)
    slot_tok = jnp.zeros((cap,), jnp.int32).at[dest0].set(tok).at[dest1].set(tok)
    group_start = jnp.arange(cap // group_rows, dtype=jnp.int32) * group_rows
    group_expert = jnp.minimum(jnp.searchsorted(pad_end, group_start, side='right'), n_exp - 1).astype(jnp.int32)
    rows_left = counts[group_expert] - (group_start - pad_start[group_expert])
    group_valid = jnp.clip((rows_left + EXPERT_BLOCK - 1) // EXPERT_BLOCK, 0, MOE_GROUP).astype(jnp.int32)
    h_slots = _sc_gather_rows(h, slot_tok)
    y_slots = _moe_ffn(h_slots, group_expert, group_valid, w13, w2, layer)
    ya = _sc_gather_rows(y_slots, dest0)
    yb = _sc_gather_rows(y_slots, dest1)
    return _combine(x, ya, yb, meta, mod, ln_g, ln_b, seq=seq, alpha=alpha)


def _gate_kernel(x_ref, mod_ref, wf_ref, bf_ref, o_ref, carry_ref, *, tm):
    @pl.when(pl.program_id(1) == 0)
    def _():
        carry_ref[...] = jnp.zeros_like(carry_ref)

    h = _modulate(x_ref[...], mod_ref[0], 0)
    h_hi, h_mid, _ = _split3(h)
    wf = wf_ref[...]
    wf_hi = wf.astype(BF16)
    wf_lo = (wf - wf_hi.astype(F32)).astype(BF16)
    f = _dot(h_hi, wf_hi) + _dot(h_mid, wf_hi) + _dot(h_hi, wf_lo) + bf_ref[...]
    lf = jnp.minimum(f, 0.0) - jnp.log(1.0 + jnp.exp(-jnp.abs(f)))
    r = lax.broadcasted_iota(jnp.int32, (tm, tm), 0)
    c = lax.broadcasted_iota(jnp.int32, (tm, tm), 1)
    tri = jnp.where(r >= c, 1.0, 0.0).astype(BF16)
    hi, mid, lo = _split3(lf)
    cs = _dot(tri, hi) + _dot(tri, mid) + _dot(tri, lo) + carry_ref[0:1, :]
    o_ref[...] = cs
    carry_ref[...] = jnp.broadcast_to(cs[tm - 1:tm, :], carry_ref.shape)


def _forget_cumsum(x, mod, w_f, b_f, *, bsz, seq, tm=512):
    t, d = x.shape
    heads = b_f.shape[0]
    nb = seq // tm
    wf = jnp.zeros((d, LANES), F32).at[:, :heads].set(w_f)
    bf = jnp.zeros((1, LANES), F32).at[0, :heads].set(b_f)
    return pl.pallas_call(
        functools.partial(_gate_kernel, tm=tm),
        grid=(bsz, nb),
        in_specs=[
            pl.BlockSpec((tm, d), lambda b, i: (b * nb + i, 0)),
            pl.BlockSpec((1, 6, d), lambda b, i: (b, 0, 0)),
            _const_spec((d, LANES)),
            _const_spec((1, LANES)),
        ],
        out_specs=pl.BlockSpec((tm, LANES), lambda b, i: (b * nb + i, 0)),
        out_shape=jax.ShapeDtypeStruct((t, LANES), F32),
        scratch_shapes=[pltpu.VMEM((8, LANES), F32)],
        compiler_params=_cparams(("parallel", "arbitrary")),
        name="fox_forget_cumsum",
    )(x, mod, wf, bf)


HEAD_PAD = LANES
GATE_PARTS = 3
VAL_PAD = 16
LOG2E = 1.4426950408889634


def _fox_proj_kernel(x_ref, mod_ref, cs_ref, wq_ref, wk_ref, wv_ref, selk_ref, onek_ref, selq_ref, oneq_ref,
                     onev_ref, k_ref, q_ref, v_ref, *, q_scale):
    h = _modulate(x_ref[...], mod_ref[0], 0).astype(BF16)
    cs = cs_ref[...] * LOG2E
    parts = _split3(cs)
    k = _dot(h, wk_ref[...]) + onek_ref[...]
    for n in range(GATE_PARTS):
        k = k + _dot(parts[n], selk_ref[n])
    k_ref[...] = k.astype(BF16)
    nt = (((1,), (1,)), ((), ()))
    parts_t = _split3(cs.T)
    q = lax.dot_general(wq_ref[...], h, nt, preferred_element_type=F32) * (q_scale * LOG2E) + oneq_ref[...]
    for n in range(GATE_PARTS):
        q = q + _dot(selq_ref[n], parts_t[n])
    q_ref[0] = q.astype(BF16)
    v = lax.dot_general(wv_ref[...], h, nt, preferred_element_type=F32) + onev_ref[...]
    v_ref[0] = v.astype(BF16)


def _fox_proj(x, mod, cs, w_in, *, bsz, seq, heads, tm=512):
    t, d = x.shape
    hd = d // heads
    nb = seq // tm
    wide = heads * HEAD_PAD
    pad = lambda w: jnp.pad(w.reshape(d, heads, hd), ((0, 0), (0, 0), (0, HEAD_PAD - hd))).reshape(d, wide)
    wq_t = pad(w_in[:, :d]).T.astype(BF16)
    wk = pad(w_in[:, d:2 * d]).astype(BF16)
    vrows = heads * (hd + VAL_PAD)
    wv_t = jnp.pad(w_in[:, 2 * d:3 * d].reshape(d, heads, hd),
                   ((0, 0), (0, 0), (0, VAL_PAD))).reshape(d, vrows).T.astype(BF16)
    sel_k = np.zeros((GATE_PARTS, LANES, wide), np.float32)
    sel_q = np.zeros((GATE_PARTS, wide, LANES), np.float32)
    one_k = np.zeros((1, wide), np.float32)
    one_q = np.zeros((wide, 1), np.float32)
    one_v = np.zeros((vrows, 1), np.float32)
    for hh in range(heads):
        base = hh * HEAD_PAD + hd
        one_v[hh * (hd + VAL_PAD) + hd, 0] = 1.0
        for n in range(GATE_PARTS):
            sel_k[n, hh, base + n] = -1.0
            one_q[base + n, 0] = 1.0
            one_k[0, base + GATE_PARTS + n] = 1.0
            sel_q[n, base + GATE_PARTS + n, hh] = 1.0
    return pl.pallas_call(
        functools.partial(_fox_proj_kernel, q_scale=hd ** -0.5),
        grid=(t // tm,),
        in_specs=[
            pl.BlockSpec((tm, d), lambda i: (i, 0)),
            pl.BlockSpec((1, 6, d), lambda i: (i * tm // seq, 0, 0)),
            pl.BlockSpec((tm, LANES), lambda i: (i, 0)),
            _const_spec((wide, d)),
            _const_spec((d, wide)),
            _const_spec((vrows, d)),
            _const_spec((GATE_PARTS, LANES, wide)),
            _const_spec((1, wide)),
            _const_spec((GATE_PARTS, wide, LANES)),
            _const_spec((wide, 1)),
            _const_spec((vrows, 1)),
        ],
        out_specs=[
            pl.BlockSpec((tm, wide), lambda i: (i, 0)),
            pl.BlockSpec((1, wide, tm), lambda i: (i // nb, 0, i % nb)),
            pl.BlockSpec((1, vrows, tm), lambda i: (i // nb, 0, i % nb)),
        ],
        out_shape=[
            jax.ShapeDtypeStruct((t, wide), BF16),
            jax.ShapeDtypeStruct((bsz, wide, seq), BF16),
            jax.ShapeDtypeStruct((bsz, vrows, seq), BF16),
        ],
        compiler_params=_cparams(("parallel",)),
        name="fox_proj",
    )(x, mod, cs, wq_t, wk, wv_t, jnp.asarray(sel_k, BF16), jnp.asarray(one_k), jnp.asarray(sel_q, BF16),
      jnp.asarray(one_q), jnp.asarray(one_v))


def _flash_kernel(qi_ref, kj_ref, k_ref, q_ref, v_ref, o_ref, m_ref, acc_ref, s_ref, *, heads, hd, blk):
    p_idx = pl.program_id(1)
    qi = qi_ref[p_idx]
    kj = kj_ref[p_idx]
    vp = hd + VAL_PAD

    @pl.when(kj == 0)
    def _():
        m_ref[...] = jnp.full(m_ref.shape, -jnp.inf, F32)
        acc_ref[...] = jnp.zeros_like(acc_ref)

    def step(masked):
        if masked:
            kpos = lax.broadcasted_iota(jnp.int32, (blk, blk), 0)
            qpos = lax.broadcasted_iota(jnp.int32, (blk, blk), 1)
            keep = kpos <= qpos

        def scores(h):
            hs = slice(h * HEAD_PAD, (h + 1) * HEAD_PAD)
            s = _dot(k_ref[:, hs], q_ref[0, hs, :])
            if masked:
                s = jnp.where(keep, s, -jnp.inf)
            s_ref[h % 2] = s
            return jnp.max(s, axis=0, keepdims=True)

        m_cur = scores(0)
        for h in range(heads):
            m_nxt = scores(h + 1) if h + 1 < heads else None
            m_prev = m_ref[h:h + 1, :]
            m_new = jnp.maximum(m_prev, m_cur)
            m_ref[h:h + 1, :] = m_new
            a = jnp.exp2(m_prev - m_new)
            p = jnp.exp2(s_ref[h % 2] - m_new).astype(BF16)
            rs = slice(h * vp, (h + 1) * vp)
            acc_ref[rs, :] = acc_ref[rs, :] * a + _dot(v_ref[0, rs, :], p)
            m_cur = m_nxt

    @pl.when(kj < qi)
    def _():
        step(False)

    @pl.when(kj == qi)
    def _():
        step(True)
        per = LANES // hd
        for g in range(heads // per):
            parts = []
            for n in range(per):
                r0 = (g * per + n) * vp
                parts.append(acc_ref[r0:r0 + hd, :] / acc_ref[r0 + hd:r0 + hd + 1, :])
            o_ref[:, g * LANES:(g + 1) * LANES] = jnp.concatenate(parts, axis=0).T.astype(BF16)


def _flash(k_aug, q_aug_t, v_t, *, bsz, seq, heads, blk=512):
    t, wide = k_aug.shape
    vrows = v_t.shape[1]
    hd = vrows // heads - VAL_PAD
    d = heads * hd
    nq = seq // blk
    pairs = [(i, j) for i in range(nq) for j in range(i + 1)]
    qi = jnp.asarray(np.array([p[0] for p in pairs], np.int32))
    kj = jnp.asarray(np.array([p[1] for p in pairs], np.int32))
    grid_spec = pltpu.PrefetchScalarGridSpec(
        num_scalar_prefetch=2,
        grid=(bsz, len(pairs)),
        in_specs=[
            pl.BlockSpec((blk, wide), lambda b, p, qi, kj: (b * nq + kj[p], 0)),
            pl.BlockSpec((1, wide, blk), lambda b, p, qi, kj: (b, 0, qi[p])),
            pl.BlockSpec((1, vrows, blk), lambda b, p, qi, kj: (b, 0, kj[p])),
        ],
        out_specs=pl.BlockSpec((blk, d), lambda b, p, qi, kj: (b * nq + qi[p], 0)),
        scratch_shapes=[
            pltpu.VMEM((heads, blk), F32),
            pltpu.VMEM((vrows, blk), F32),
            pltpu.VMEM((2, blk, blk), F32),
        ],
    )
    return pl.pallas_call(
        functools.partial(_flash_kernel, heads=heads, hd=hd, blk=blk),
        grid_spec=grid_spec,
        out_shape=jax.ShapeDtypeStruct((t, d), BF16),
        compiler_params=_cparams(("parallel", "arbitrary")),
        name="fox_attention",
    )(qi, kj, k_aug, q_aug_t, v_t)


def _mm_resln_kernel(a_ref, x_ref, mod_ref, w_ref, g_ref, b_ref, o_ref, *, alpha):
    y = _dot(a_ref[...], w_ref[...])
    o_ref[...] = _res_ln(x_ref[...], y, mod_ref[0], 0, g_ref[...], b_ref[...], alpha)


def _mm_resln(a, x, mod, w, ln_g, ln_b, *, seq, alpha, tm=512):
    t, d = x.shape
    k = a.shape[1]
    return pl.pallas_call(
        functools.partial(_mm_resln_kernel, alpha=alpha),
        grid=(t // tm,),
        in_specs=[
            pl.BlockSpec((tm, k), lambda i: (i, 0)),
            pl.BlockSpec((tm, d), lambda i: (i, 0)),
            pl.BlockSpec((1, 6, d), lambda i: (i * tm // seq, 0, 0)),
            _const_spec((k, d)),
            _const_spec((1, d)),
            _const_spec((1, d)),
        ],
        out_specs=pl.BlockSpec((tm, d), lambda i: (i, 0)),
        out_shape=jax.ShapeDtypeStruct((t, d), F32),
        compiler_params=_cparams(("parallel",)),
        name="proj_res_ln",
    )(a, x, mod, w.astype(BF16), ln_g.reshape(1, d), ln_b.reshape(1, d))


def _fox_layer(x, mod, w_in, b_f, w_out, ln_g, ln_b, *, bsz, seq, alpha):
    heads = b_f.shape[0]
    d = x.shape[1]
    cs = _forget_cumsum(x, mod, w_in[:, 3 * d:], b_f, bsz=bsz, seq=seq)
    k_aug, q_aug_t, v_t = _fox_proj(x, mod, cs, w_in, bsz=bsz, seq=seq, heads=heads)
    attn = _flash(k_aug, q_aug_t, v_t, bsz=bsz, seq=seq, heads=heads)
    return _mm_resln(attn, x, mod, w_out, ln_g, ln_b, seq=seq, alpha=alpha)


def _pool_kernel(x_ref, xh_ref, mod_ref, win_ref, wg_ref, ls_ref, wout_ref, g_ref, b_ref, o_ref, pooled_ref,
                 *, tm, seq, alpha):
    i = pl.program_id(0)
    pos0 = (i * tm) % seq
    m = mod_ref[0]
    x = x_ref[...]
    halo_ok = jnp.where(pos0 > 0, jnp.float32(1.0), jnp.float32(0.0))
    xe = jnp.concatenate([xh_ref[...], x], axis=0)
    ze = _dot(_modulate(xe, m, 0).astype(BF16), win_ref[...])
    rows = lax.broadcasted_iota(jnp.int32, (tm + POOL_HALO, 1), 0)
    ze = ze * jnp.where(rows < POOL_HALO, halo_ok, jnp.float32(1.0))
    pos = (pos0 + lax.broadcasted_iota(jnp.int32, (tm, 1), 0) + 1).astype(F32)
    gd = ze.shape[1] // len(POOL_WINDOWS)
    for g, win in enumerate(POOL_WINDOWS):
        cs = slice(g * gd, (g + 1) * gd)
        zg = ze[:, cs]
        s = zg
        span = 1
        while span < win:
            s = s + pltpu.roll(s, span, 0)
            span *= 2
        count = jnp.minimum(pos, jnp.float32(win))
        pooled = s[POOL_HALO:, :] / count - zg[POOL_HALO:, :]
        pooled_ref[:, cs] = (_dot(pooled.astype(BF16), wg_ref[g]) * ls_ref[:, cs]).astype(BF16)
    y = _dot(pooled_ref[...], wout_ref[...])
    o_ref[...] = _res_ln(x, y, m, 0, g_ref[...], b_ref[...], alpha)


def _pool_layer(x, mod, w_in, w_grp, scale, w_out, ln_g, ln_b, *, seq, alpha, tm=512):
    t, d = x.shape
    ng, gd, _ = w_grp.shape
    hb = tm // POOL_HALO
    return pl.pallas_call(
        functools.partial(_pool_kernel, tm=tm, seq=seq, alpha=alpha),
        grid=(t // tm,),
        in_specs=[
            pl.BlockSpec((tm, d), lambda i: (i, 0)),
            pl.BlockSpec((POOL_HALO, d), lambda i: (jnp.maximum(i * hb - 1, 0), 0)),
            pl.BlockSpec((1, 6, d), lambda i: (i * tm // seq, 0, 0)),
            _const_spec((d, d)),
            _const_spec((ng, gd, gd)),
            _const_spec((1, d)),
            _const_spec((d, d)),
            _const_spec((1, d)),
            _const_spec((1, d)),
        ],
        out_specs=pl.BlockSpec((tm, d), lambda i: (i, 0)),
        out_shape=jax.ShapeDtypeStruct((t, d), F32),
        scratch_shapes=[pltpu.VMEM((tm, d), BF16)],
        compiler_params=_cparams(("parallel",)),
        name="pool_mixer",
    )(x, x, mod, w_in.astype(BF16), w_grp.astype(BF16), scale.reshape(1, d), w_out.astype(BF16),
      ln_g.reshape(1, d), ln_b.reshape(1, d))


def kernel(x, c, mod_w, mod_b, ln_g, ln_b, gm_w_in, gm_ln_g, gm_ln_b, gm_w_s, gm_b_s, gm_w_out, fox_w_in,
           fox_b_f, fox_w_out, pool_w_in, pool_w_grp, pool_scale, pool_w_out, ffn_w13, ffn_w2, moe_w_router,
           moe_b_router, moe_w13, moe_w2):
    bsz, seq, d = x.shape
    depth = mod_w.shape[0]
    alpha = (2 * depth) ** 0.25
    n_mixers = 3
    mod_all = _modulation(c, mod_w, mod_b)
    xt = x.reshape(bsz * seq, d)
    for i in range(depth):
        mod = mod_all[i]
        kind, j = i % n_mixers, i // n_mixers
        if kind == 0:
            xt = _gmlp_layer(xt, mod, gm_w_in[j], gm_ln_g[j], gm_ln_b[j], gm_w_s[j], gm_b_s[j], gm_w_out[j],
                             ln_g[i, 0], ln_b[i, 0], seq=seq, alpha=alpha)
        elif kind == 1:
            xt = _fox_layer(xt, mod, fox_w_in[j], fox_b_f[j], fox_w_out[j], ln_g[i, 0], ln_b[i, 0],
                            bsz=bsz, seq=seq, alpha=alpha)
        else:
            xt = _pool_layer(xt, mod, pool_w_in[j], pool_w_grp[j], pool_scale[j], pool_w_out[j],
                             ln_g[i, 0], ln_b[i, 0], seq=seq, alpha=alpha)
        if i % 2 == 0:
            xt = _ffn_layer(xt, mod, ffn_w13[i // 2], ffn_w2[i // 2], ln_g[i, 1], ln_b[i, 1],
                            seq=seq, alpha=alpha)
        else:
            xt = _moe_layer(xt, mod, moe_w_router[i // 2], moe_b_router[i // 2], moe_w13, moe_w2, i // 2,
                            ln_g[i, 1], ln_b[i, 1], seq=seq, alpha=alpha)
    return xt.reshape(bsz, seq, d)
```

```python
import functools

import numpy as np
import jax
import jax.numpy as jnp
from jax import lax
from jax.experimental import pallas as pl
from jax.experimental.pallas import tpu as pltpu
from jax.experimental.pallas import tpu_sc as plsc

F32 = jnp.float32
BF16 = jnp.bfloat16

POOL_WINDOWS = (2, 4, 8, 16)
TOP_K = 2
EXPERT_BLOCK = 512
LN_EPS = 1e-5
LANES = 128
POOL_HALO = 16
VMEM_LIMIT = 56 * 1024 * 1024
SC_CORES = 2
SC_SUBCORES = 16
SC_GATHER_ROWS = 64


def _cparams(sem):
    return pltpu.CompilerParams(dimension_semantics=sem, vmem_limit_bytes=VMEM_LIMIT)


def _const_spec(shape):
    nd = len(shape)
    return pl.BlockSpec(shape, lambda *_: (0,) * nd, pipeline_mode=pl.Buffered(1))


def _layer_norm(r, g, b):
    mu = jnp.mean(r, axis=-1, keepdims=True)
    xc = r - mu
    var = jnp.mean(xc * xc, axis=-1, keepdims=True)
    return xc * lax.rsqrt(var + LN_EPS) * g + b


def _modulate(x, m, off):
    return x * (1.0 + m[off + 1:off + 2]) + m[off:off + 1]


def _res_ln(x, y, m, off, g, b, alpha):
    return _layer_norm(alpha * x + (1.0 + m[off + 2:off + 3]) * y, g, b)


def _split3(a):
    hi = a.astype(BF16)
    r1 = a - hi.astype(F32)
    mid = r1.astype(BF16)
    lo = (r1 - mid.astype(F32)).astype(BF16)
    return hi, mid, lo


def _dot(a, b):
    return jnp.dot(a, b, preferred_element_type=F32)


def _mod_kernel(c_ref, w_ref, b_ref, o_ref):
    c = c_ref[...]
    s = c / (1.0 + jnp.exp(-c))
    s_hi, s_mid, _ = _split3(s)
    w = w_ref[0]
    w_hi = w.astype(BF16)
    w_lo = (w - w_hi.astype(F32)).astype(BF16)
    acc = _dot(s_hi, w_hi) + _dot(s_mid, w_hi) + _dot(s_hi, w_lo)
    o_ref[0] = acc + b_ref[0]


def _modulation(c, mod_w, mod_b):
    depth, d, n = mod_w.shape
    bsz = c.shape[0]
    rows = 8
    tn = n // 4
    c_pad = jnp.zeros((rows, d), F32).at[:bsz].set(c)
    out = pl.pallas_call(
        _mod_kernel,
        grid=(depth, n // tn),
        in_specs=[
            pl.BlockSpec((rows, d), lambda l, j: (0, 0)),
            pl.BlockSpec((1, d, tn), lambda l, j: (l, 0, j)),
            pl.BlockSpec((1, 1, tn), lambda l, j: (l, 0, j)),
        ],
        out_specs=pl.BlockSpec((1, rows, tn), lambda l, j: (l, 0, j)),
        out_shape=jax.ShapeDtypeStruct((depth, rows, n), F32),
        compiler_params=_cparams(("parallel", "parallel")),
        name="adaln_mod",
    )(c_pad, mod_w, mod_b.reshape(depth, 1, n))
    return out[:, :bsz, :].reshape(depth, bsz, 6, d)


def _gmlp_kernel(x_ref, mod_ref, win_ref, vg_ref, vb_ref, ws_ref, bst_ref, wout_ref, g_ref, b_ref,
                 o_ref, gated_ref, *, tm, sub, chunk, groups, alpha):
    m = mod_ref[0]
    row = lax.broadcasted_iota(jnp.int32, (chunk, chunk), 0)
    col = lax.broadcasted_iota(jnp.int32, (chunk, chunk), 1)
    causal = row >= col
    bst = bst_ref[...]
    w_mix = [jnp.where(causal, ws_ref[g], 0.0).astype(BF16) for g in range(groups)]
    n = tm // sub

    def project(s):
        x = x_ref[s * sub:(s + 1) * sub, :]
        return x, _dot(_modulate(x, m, 0).astype(BF16), win_ref[...])

    def activate(z):
        z = 0.5 * z * (1.0 + lax.erf(z * (2.0 ** -0.5)))
        width = z.shape[1] // 2
        return z[:, :width], _layer_norm(z[:, width:], vg_ref[...], vb_ref[...]).astype(BF16)

    def mix(s, u, v):
        gd = u.shape[1] // groups
        for g in range(groups):
            cs = slice(g * gd, (g + 1) * gd)
            for c in range(sub // chunk):
                rs = slice(c * chunk, (c + 1) * chunk)
                mixed = _dot(w_mix[g], v[rs, cs]) + bst[:, g:g + 1]
                gated_ref[s * sub + c * chunk:s * sub + (c + 1) * chunk, cs] = (u[rs, cs] * mixed).astype(BF16)
        return _dot(gated_ref[s * sub:(s + 1) * sub, :], wout_ref[...])

    def finish(s, x, y):
        o_ref[s * sub:(s + 1) * sub, :] = _res_ln(x, y, m, 0, g_ref[...], b_ref[...], alpha)

    nxt = project(0)
    pending = None
    for s in range(n):
        x, z = nxt
        if s + 1 < n:
            nxt = project(s + 1)
        u, v = activate(z)
        if pending is not None:
            finish(*pending)
        pending = (s, x, mix(s, u, v))
    finish(*pending)


def _gmlp_layer(x, mod, w_in, v_g, v_b, w_s, b_s, w_out, ln_g, ln_b, *, seq, alpha, tm=512, sub=256):
    t, d = x.shape
    groups, chunk, _ = w_s.shape
    width = w_out.shape[0]
    kern = functools.partial(_gmlp_kernel, tm=tm, sub=sub, chunk=chunk, groups=groups, alpha=alpha)
    return pl.pallas_call(
        kern,
        grid=(t // tm,),
        in_specs=[
            pl.BlockSpec((tm, d), lambda i: (i, 0)),
            pl.BlockSpec((1, 6, d), lambda i: (i * tm // seq, 0, 0)),
            _const_spec((d, 2 * width)),
            _const_spec((1, width)),
            _const_spec((1, width)),
            _const_spec((groups, chunk, chunk)),
            _const_spec((chunk, groups)),
            _const_spec((width, d)),
            _const_spec((1, d)),
            _const_spec((1, d)),
        ],
        out_specs=pl.BlockSpec((tm, d), lambda i: (i, 0)),
        out_shape=jax.ShapeDtypeStruct((t, d), F32),
        scratch_shapes=[pltpu.VMEM((tm, width), BF16)],
        compiler_params=_cparams(("parallel",)),
        name="gmlp_mixer",
    )(x, mod, w_in.astype(BF16), v_g.reshape(1, width), v_b.reshape(1, width), w_s, b_s.T,
      w_out.astype(BF16), ln_g.reshape(1, d), ln_b.reshape(1, d))


def _swiglu_pipelined(hs, w1, w3, w2, emit):
    up = lambda h: (_dot(h, w1), _dot(h, w3))
    nxt = up(hs[0])
    for s in range(len(hs)):
        a, b = nxt
        if s + 1 < len(hs):
            nxt = up(hs[s + 1])
        t = (a / (1.0 + jnp.exp(-a)) * b).astype(BF16)
        emit(s, _dot(t, w2))


def _ffn_kernel(x_ref, mod_ref, w1_ref, w3_ref, w2_ref, g_ref, b_ref, o_ref, h_ref, *, alpha, sub):
    j = pl.program_id(1)
    last = pl.num_programs(1) - 1
    subs = [slice(s, s + sub) for s in range(0, x_ref.shape[0], sub)]
    weights = lambda: (w1_ref[...], w3_ref[...], w2_ref[...])

    @pl.when(j == 0)
    def _():
        hs = []
        for rows in subs:
            h = _modulate(x_ref[rows, :], mod_ref[0], 3).astype(BF16)
            h_ref[rows, :] = h
            hs.append(h)

        def emit(s, part):
            o_ref[subs[s], :] = part

        _swiglu_pipelined(hs, *weights(), emit)

    @pl.when(jnp.logical_and(j > 0, j < last))
    def _():
        def emit(s, part):
            o_ref[subs[s], :] += part

        _swiglu_pipelined([h_ref[rows, :] for rows in subs], *weights(), emit)

    @pl.when(j == last)
    def _():
        def emit(s, part):
            rows = subs[s]
            y = o_ref[rows, :] + part
            o_ref[rows, :] = _res_ln(x_ref[rows, :], y, mod_ref[0], 3, g_ref[...], b_ref[...], alpha)

        _swiglu_pipelined([h_ref[rows, :] for rows in subs], *weights(), emit)


def _ffn_layer(x, mod, w13, w2, ln_g, ln_b, *, seq, alpha, tm=1024, tf=512):
    t, d = x.shape
    f = w2.shape[0]
    nf = f // tf
    assert nf >= 2
    w13b = w13.astype(BF16)
    return pl.pallas_call(
        functools.partial(_ffn_kernel, alpha=alpha, sub=tm // 2),
        grid=(t // tm, nf),
        in_specs=[
            pl.BlockSpec((tm, d), lambda i, j: (i, 0)),
            pl.BlockSpec((1, 6, d), lambda i, j: (i * tm // seq, 0, 0)),
            pl.BlockSpec((d, tf), lambda i, j: (0, j)),
            pl.BlockSpec((d, tf), lambda i, j: (0, nf + j)),
            pl.BlockSpec((tf, d), lambda i, j: (j, 0)),
            _const_spec((1, d)),
            _const_spec((1, d)),
        ],
        out_specs=pl.BlockSpec((tm, d), lambda i, j: (i, 0)),
        out_shape=jax.ShapeDtypeStruct((t, d), F32),
        scratch_shapes=[pltpu.VMEM((tm, d), BF16)],
        compiler_params=_cparams(("parallel", "arbitrary")),
        name="swiglu_dense",
    )(x, mod, w13b, w13b, w2.astype(BF16), ln_g.reshape(1, d), ln_b.reshape(1, d))


MOE_GROUP = 2


def _moe_ffn_kernel(ge_ref, gv_ref, h_ref, w1_ref, w3_ref, w2_ref, o_ref, w1b_ref, w3b_ref, w2b_ref, *, sub):
    g = pl.program_id(0)
    j = pl.program_id(1)
    last = pl.num_programs(1) - 1
    nv = gv_ref[g]
    d = o_ref.shape[1]

    @pl.when(nv > 0)
    def _():
        w1b_ref[...] = w1_ref[0, 0].astype(BF16)
        w3b_ref[...] = w3_ref[0, 0].astype(BF16)
        w2b_ref[...] = w2_ref[0, 0].astype(BF16)

    def run(n_blocks, first):
        subs = [slice(s, s + sub) for s in range(0, n_blocks * EXPERT_BLOCK, sub)]

        def emit(s, part):
            if first:
                o_ref[subs[s], :] = part
            else:
                o_ref[subs[s], :] += part

        _swiglu_pipelined([h_ref[rows, :].astype(BF16) for rows in subs], w1b_ref[...], w3b_ref[...],
                          w2b_ref[...], emit)

    for n_blocks in range(1, MOE_GROUP + 1):
        @pl.when(jnp.logical_and(nv == n_blocks, j == 0))
        def _():
            run(n_blocks, True)

        @pl.when(jnp.logical_and(nv == n_blocks, j > 0))
        def _():
            run(n_blocks, False)

    for blk in range(MOE_GROUP):
        @pl.when(jnp.logical_and(blk >= nv, j == last))
        def _():
            o_ref[blk * EXPERT_BLOCK:(blk + 1) * EXPERT_BLOCK, :] = jnp.zeros((EXPERT_BLOCK, d), F32)


def _moe_ffn(h_slots, group_expert, group_valid, w13, w2, layer, *, tf=512):
    cap, d = h_slots.shape
    f = w2.shape[2]
    nf = f // tf
    tm = MOE_GROUP * EXPERT_BLOCK

    def jj(g, j, gv):
        return jnp.where(gv[g] > 0, j, nf - 1)

    grid_spec = pltpu.PrefetchScalarGridSpec(
        num_scalar_prefetch=2,
        grid=(cap // tm, nf),
        in_specs=[
            pl.BlockSpec((tm, d), lambda g, j, ge, gv: (g, 0)),
            pl.BlockSpec((1, 1, d, tf), lambda g, j, ge, gv: (layer, ge[g], 0, jj(g, j, gv))),
            pl.BlockSpec((1, 1, d, tf), lambda g, j, ge, gv: (layer, ge[g], 0, nf + jj(g, j, gv))),
            pl.BlockSpec((1, 1, tf, d), lambda g, j, ge, gv: (layer, ge[g], jj(g, j, gv), 0)),
        ],
        out_specs=pl.BlockSpec((tm, d), lambda g, j, ge, gv: (g, 0)),
        scratch_shapes=[pltpu.VMEM((d, tf), BF16), pltpu.VMEM((d, tf), BF16), pltpu.VMEM((tf, d), BF16)],
    )
    return pl.pallas_call(
        functools.partial(_moe_ffn_kernel, sub=EXPERT_BLOCK),
        grid_spec=grid_spec,
        out_shape=jax.ShapeDtypeStruct((cap, d), F32),
        compiler_params=_cparams(("parallel", "arbitrary")),
        name="swiglu_experts",
    )(group_expert, group_valid, h_slots, w13, w13, w2)


def _router_kernel(x_ref, mod_ref, wr_ref, br_ref, h_ref, meta_ref, cnt_ref, carry_ref, *, tm, n_exp):
    i = pl.program_id(0)

    @pl.when(i == 0)
    def _():
        carry_ref[...] = jnp.zeros_like(carry_ref)

    h = _modulate(x_ref[...], mod_ref[0], 3)
    h_ref[...] = h
    h_hi, h_mid, _ = _split3(h)
    w = wr_ref[...]
    w_hi = w.astype(BF16)
    w_lo = (w - w_hi.astype(F32)).astype(BF16)
    logits = _dot(h_hi, w_hi) + _dot(h_mid, w_hi) + _dot(h_hi, w_lo) + br_ref[...]
    lane = lax.broadcasted_iota(jnp.int32, (tm, LANES), 1)
    neg = jnp.float32(-jnp.inf)
    logits = jnp.where(lane < n_exp, logits, neg)
    v0 = jnp.max(logits, axis=1, keepdims=True)
    e0 = jnp.min(jnp.where(logits == v0, lane, LANES), axis=1, keepdims=True)
    rest = jnp.where(lane == e0, neg, logits)
    v1 = jnp.max(rest, axis=1, keepdims=True)
    e1 = jnp.min(jnp.where(rest == v1, lane, LANES), axis=1, keepdims=True)
    p = jnp.exp(v1 - v0)
    g0 = 1.0 / (1.0 + p)
    g1 = p / (1.0 + p)
    oh0 = lane == e0
    oh1 = lane == e1
    onehot = jnp.where(jnp.logical_or(oh0, oh1), 1.0, 0.0).astype(BF16)
    r = lax.broadcasted_iota(jnp.int32, (tm, tm), 0)
    c = lax.broadcasted_iota(jnp.int32, (tm, tm), 1)
    strict = jnp.where(r > c, 1.0, 0.0).astype(BF16)
    before = _dot(strict, onehot) + carry_ref[0:1, :]
    rank0 = jnp.sum(jnp.where(oh0, before, 0.0), axis=1, keepdims=True)
    rank1 = jnp.sum(jnp.where(oh1, before, 0.0), axis=1, keepdims=True)
    total = carry_ref[0:1, :] + jnp.sum(onehot.astype(F32), axis=0, keepdims=True)
    carry_ref[...] = jnp.broadcast_to(total, carry_ref.shape)
    cnt_ref[...] = jnp.broadcast_to(total, cnt_ref.shape)
    meta = jnp.where(lane == 0, e0.astype(F32), 0.0)
    meta = jnp.where(lane == 1, e1.astype(F32), meta)
    meta = jnp.where(lane == 2, g0, meta)
    meta = jnp.where(lane == 3, g1, meta)
    meta = jnp.where(lane == 4, rank0, meta)
    meta = jnp.where(lane == 5, rank1, meta)
    meta_ref[...] = meta


def _router(x, mod, w_router, b_router, *, seq, tm=512):
    t, d = x.shape
    n_exp = w_router.shape[1]
    wr = jnp.zeros((d, LANES), F32).at[:, :n_exp].set(w_router)
    br = jnp.zeros((1, LANES), F32).at[0, :n_exp].set(b_router)
    return pl.pallas_call(
        functools.partial(_router_kernel, tm=tm, n_exp=n_exp),
        grid=(t // tm,),
        in_specs=[
            pl.BlockSpec((tm, d), lambda i: (i, 0)),
            pl.BlockSpec((1, 6, d), lambda i: (i * tm // seq, 0, 0)),
            _const_spec((d, LANES)),
            _const_spec((1, LANES)),
        ],
        out_specs=[
            pl.BlockSpec((tm, d), lambda i: (i, 0)),
            pl.BlockSpec((tm, LANES), lambda i: (i, 0)),
            pl.BlockSpec((8, LANES), lambda i: (0, 0)),
        ],
        out_shape=[
            jax.ShapeDtypeStruct((t, d), F32),
            jax.ShapeDtypeStruct((t, LANES), F32),
            jax.ShapeDtypeStruct((8, LANES), F32),
        ],
        scratch_shapes=[pltpu.VMEM((8, LANES), F32)],
        compiler_params=_cparams(("arbitrary",)),
        name="moe_router",
    )(x, mod, wr, br)


def _combine_kernel(x_ref, ya_ref, yb_ref, meta_ref, mod_ref, g_ref, b_ref, o_ref, *, alpha):
    meta = meta_ref[...]
    y = meta[:, 2:3] * ya_ref[...] + meta[:, 3:4] * yb_ref[...]
    o_ref[...] = _res_ln(x_ref[...], y, mod_ref[0], 3, g_ref[...], b_ref[...], alpha)


def _combine(x, ya, yb, meta, mod, ln_g, ln_b, *, seq, alpha, tm=512):
    t, d = x.shape
    row = pl.BlockSpec((tm, d), lambda i: (i, 0))
    return pl.pallas_call(
        functools.partial(_combine_kernel, alpha=alpha),
        grid=(t // tm,),
        in_specs=[row, row, row,
                  pl.BlockSpec((tm, LANES), lambda i: (i, 0)),
                  pl.BlockSpec((1, 6, d), lambda i: (i * tm // seq, 0, 0)),
                  _const_spec((1, d)), _const_spec((1, d))],
        out_specs=row,
        out_shape=jax.ShapeDtypeStruct((t, d), F32),
        compiler_params=_cparams(("parallel",)),
        name="moe_combine",
    )(x, ya, yb, meta, mod, ln_g.reshape(1, d), ln_b.reshape(1, d))


def _sc_gather_rows(table, idx):
    n = idx.shape[0]
    width = table.shape[1]
    workers = SC_CORES * SC_SUBCORES
    per_worker = n // workers
    steps = per_worker // SC_GATHER_ROWS
    assert steps * SC_GATHER_ROWS * workers == n
    mesh = plsc.VectorSubcoreMesh(core_axis_name="c", subcore_axis_name="s", num_cores=SC_CORES,
                                  num_subcores=SC_SUBCORES)

    def body(table_hbm, idx_hbm, out_hbm, idx_v, rows_v, sem):
        wid = lax.axis_index("s") * SC_CORES + lax.axis_index("c")

        @pl.loop(0, steps)
        def _(j):
            off = pl.multiple_of(wid * per_worker + j * SC_GATHER_ROWS, SC_GATHER_ROWS)
            pltpu.sync_copy(idx_hbm.at[pl.ds(off, SC_GATHER_ROWS)], idx_v)
            pltpu.async_copy(table_hbm.at[idx_v], rows_v, sem).wait()
            pltpu.sync_copy(rows_v, out_hbm.at[pl.ds(off, SC_GATHER_ROWS)])

    return pl.kernel(
        body,
        out_type=jax.ShapeDtypeStruct((n, width), table.dtype),
        mesh=mesh,
        scratch_types=[
            pltpu.VMEM((SC_GATHER_ROWS,), jnp.int32),
            pltpu.VMEM((SC_GATHER_ROWS, width), table.dtype),
            pltpu.SemaphoreType.DMA,
        ],
        name="sc_row_gather",
    )(table, idx)


def _moe_layer(x, mod, w_router, b_router, w13, w2, layer, ln_g, ln_b, *, seq, alpha):
    t, d = x.shape
    n_exp = w_router.shape[1]
    h, meta, cnt = _router(x, mod, w_router, b_router, seq=seq)
    e0 = meta[:, 0].astype(jnp.int32)
    e1 = meta[:, 1].astype(jnp.int32)
    rank0 = meta[:, 4].astype(jnp.int32)
    rank1 = meta[:, 5].astype(jnp.int32)
    counts = cnt[0, :n_exp].astype(jnp.int32)
    group_rows = MOE_GROUP * EXPERT_BLOCK
    padded = (counts + group_rows - 1) // group_rows * group_rows
    pad_end = jnp.cumsum(padded)
    pad_start = pad_end - padded
    dest0 = pad_start[e0] + rank0
    dest1 = pad_start[e1] + rank1
    cap = t * TOP_K + n_exp * group_rows
    tok = jnp.arange(t, dtype=jnp.int32)
    slot_tok = (jnp.arange(cap, dtype=jnp.int32) % t).at[jnp.concatenate([dest0, dest1])].set(
        jnp.concatenate([tok, tok]))
    group_start = jnp.arange(cap // group_rows, dtype=jnp.int32) * group_rows
    group_expert = jnp.minimum(jnp.searchsorted(pad_end, group_start, side='right'), n_exp - 1).astype(jnp.int32)
    rows_left = counts[group_expert] - (group_start - pad_start[group_expert])
    group_valid = jnp.clip((rows_left + EXPERT_BLOCK - 1) // EXPERT_BLOCK, 0, MOE_GROUP).astype(jnp.int32)
    h_slots = _sc_gather_rows(h, slot_tok)
    y_slots = _moe_ffn(h_slots, group_expert, group_valid, w13, w2, layer)
    ya = _sc_gather_rows(y_slots, dest0)
    yb = _sc_gather_rows(y_slots, dest1)
    return _combine(x, ya, yb, meta, mod, ln_g, ln_b, seq=seq, alpha=alpha)


def _gate_kernel(x_ref, mod_ref, wf_ref, bf_ref, o_ref, carry_ref, *, tm):
    @pl.when(pl.program_id(1) == 0)
    def _():
        carry_ref[...] = jnp.zeros_like(carry_ref)

    h = _modulate(x_ref[...], mod_ref[0], 0)
    h_hi, h_mid, _ = _split3(h)
    wf = wf_ref[...]
    wf_hi = wf.astype(BF16)
    wf_lo = (wf - wf_hi.astype(F32)).astype(BF16)
    f = _dot(h_hi, wf_hi) + _dot(h_mid, wf_hi) + _dot(h_hi, wf_lo) + bf_ref[...]
    lf = jnp.minimum(f, 0.0) - jnp.log(1.0 + jnp.exp(-jnp.abs(f)))
    r = lax.broadcasted_iota(jnp.int32, (tm, tm), 0)
    c = lax.broadcasted_iota(jnp.int32, (tm, tm), 1)
    tri = jnp.where(r >= c, 1.0, 0.0).astype(BF16)
    hi, mid, lo = _split3(lf)
    cs = _dot(tri, hi) + _dot(tri, mid) + _dot(tri, lo) + carry_ref[0:1, :]
    o_ref[...] = cs
    carry_ref[...] = jnp.broadcast_to(cs[tm - 1:tm, :], carry_ref.shape)


def _forget_cumsum(x, mod, w_f, b_f, *, bsz, seq, tm=512):
    t, d = x.shape
    heads = b_f.shape[0]
    nb = seq // tm
    wf = jnp.zeros((d, LANES), F32).at[:, :heads].set(w_f)
    bf = jnp.zeros((1, LANES), F32).at[0, :heads].set(b_f)
    return pl.pallas_call(
        functools.partial(_gate_kernel, tm=tm),
        grid=(bsz, nb),
        in_specs=[
            pl.BlockSpec((tm, d), lambda b, i: (b * nb + i, 0)),
            pl.BlockSpec((1, 6, d), lambda b, i: (b, 0, 0)),
            _const_spec((d, LANES)),
            _const_spec((1, LANES)),
        ],
        out_specs=pl.BlockSpec((tm, LANES), lambda b, i: (b * nb + i, 0)),
        out_shape=jax.ShapeDtypeStruct((t, LANES), F32),
        scratch_shapes=[pltpu.VMEM((8, LANES), F32)],
        compiler_params=_cparams(("parallel", "arbitrary")),
        name="fox_forget_cumsum",
    )(x, mod, wf, bf)


HEAD_PAD = LANES
GATE_PARTS = 3
VAL_PAD = 16
LOG2E = 1.4426950408889634


def _fox_proj_kernel(x_ref, mod_ref, cs_ref, wq_ref, wk_ref, wv_ref, selk_ref, onek_ref, selq_ref, oneq_ref,
                     onev_ref, k_ref, q_ref, v_ref, *, q_scale):
    h = _modulate(x_ref[...], mod_ref[0], 0).astype(BF16)
    cs = cs_ref[...] * LOG2E
    parts = _split3(cs)
    k = _dot(h, wk_ref[...]) + onek_ref[...]
    for n in range(GATE_PARTS):
        k = k + _dot(parts[n], selk_ref[n])
    k_ref[...] = k.astype(BF16)
    nt = (((1,), (1,)), ((), ()))
    parts_t = _split3(cs.T)
    q = lax.dot_general(wq_ref[...], h, nt, preferred_element_type=F32) * (q_scale * LOG2E) + oneq_ref[...]
    for n in range(GATE_PARTS):
        q = q + _dot(selq_ref[n], parts_t[n])
    q_ref[0] = q.astype(BF16)
    v = lax.dot_general(wv_ref[...], h, nt, preferred_element_type=F32) + onev_ref[...]
    v_ref[0] = v.astype(BF16)


def _fox_proj(x, mod, cs, w_in, *, bsz, seq, heads, tm=512):
    t, d = x.shape
    hd = d // heads
    nb = seq // tm
    wide = heads * HEAD_PAD
    pad = lambda w: jnp.pad(w.reshape(d, heads, hd), ((0, 0), (0, 0), (0, HEAD_PAD - hd))).reshape(d, wide)
    wq_t = pad(w_in[:, :d]).T.astype(BF16)
    wk = pad(w_in[:, d:2 * d]).astype(BF16)
    vrows = heads * (hd + VAL_PAD)
    wv_t = jnp.pad(w_in[:, 2 * d:3 * d].reshape(d, heads, hd),
                   ((0, 0), (0, 0), (0, VAL_PAD))).reshape(d, vrows).T.astype(BF16)
    sel_k = np.zeros((GATE_PARTS, LANES, wide), np.float32)
    sel_q = np.zeros((GATE_PARTS, wide, LANES), np.float32)
    one_k = np.zeros((1, wide), np.float32)
    one_q = np.zeros((wide, 1), np.float32)
    one_v = np.zeros((vrows, 1), np.float32)
    for hh in range(heads):
        base = hh * HEAD_PAD + hd
        one_v[hh * (hd + VAL_PAD) + hd, 0] = 1.0
        for n in range(GATE_PARTS):
            sel_k[n, hh, base + n] = -1.0
            one_q[base + n, 0] = 1.0
            one_k[0, base + GATE_PARTS + n] = 1.0
            sel_q[n, base + GATE_PARTS + n, hh] = 1.0
    return pl.pallas_call(
        functools.partial(_fox_proj_kernel, q_scale=hd ** -0.5),
        grid=(t // tm,),
        in_specs=[
            pl.BlockSpec((tm, d), lambda i: (i, 0)),
            pl.BlockSpec((1, 6, d), lambda i: (i * tm // seq, 0, 0)),
            pl.BlockSpec((tm, LANES), lambda i: (i, 0)),
            _const_spec((wide, d)),
            _const_spec((d, wide)),
            _const_spec((vrows, d)),
            _const_spec((GATE_PARTS, LANES, wide)),
            _const_spec((1, wide)),
            _const_spec((GATE_PARTS, wide, LANES)),
            _const_spec((wide, 1)),
            _const_spec((vrows, 1)),
        ],
        out_specs=[
            pl.BlockSpec((tm, wide), lambda i: (i, 0)),
            pl.BlockSpec((1, wide, tm), lambda i: (i // nb, 0, i % nb)),
            pl.BlockSpec((1, vrows, tm), lambda i: (i // nb, 0, i % nb)),
        ],
        out_shape=[
            jax.ShapeDtypeStruct((t, wide), BF16),
            jax.ShapeDtypeStruct((bsz, wide, seq), BF16),
            jax.ShapeDtypeStruct((bsz, vrows, seq), BF16),
        ],
        compiler_params=_cparams(("parallel",)),
        name="fox_proj",
    )(x, mod, cs, wq_t, wk, wv_t, jnp.asarray(sel_k, BF16), jnp.asarray(one_k), jnp.asarray(sel_q, BF16),
      jnp.asarray(one_q), jnp.asarray(one_v))


def _flash_kernel(qi_ref, kj_ref, k_ref, q_ref, v_ref, o_ref, m_ref, acc_ref, s_ref, *, heads, hd, blk):
    p_idx = pl.program_id(1)
    qi = qi_ref[p_idx]
    kj = kj_ref[p_idx]
    vp = hd + VAL_PAD

    @pl.when(kj == 0)
    def _():
        m_ref[...] = jnp.full(m_ref.shape, -jnp.inf, F32)
        acc_ref[...] = jnp.zeros_like(acc_ref)

    def step(masked):
        if masked:
            kpos = lax.broadcasted_iota(jnp.int32, (blk, blk), 0)
            qpos = lax.broadcasted_iota(jnp.int32, (blk, blk), 1)
            keep = kpos <= qpos

        def scores(h):
            hs = slice(h * HEAD_PAD, (h + 1) * HEAD_PAD)
            s = _dot(k_ref[:, hs], q_ref[0, hs, :])
            if masked:
                s = jnp.where(keep, s, -jnp.inf)
            s_ref[h % 2] = s
            return jnp.max(s, axis=0, keepdims=True)

        m_cur = scores(0)
        for h in range(heads):
            m_nxt = scores(h + 1) if h + 1 < heads else None
            m_prev = m_ref[h:h + 1, :]
            m_new = jnp.maximum(m_prev, m_cur)
            m_ref[h:h + 1, :] = m_new
            a = jnp.exp2(m_prev - m_new)
            p = jnp.exp2(s_ref[h % 2] - m_new).astype(BF16)
            rs = slice(h * vp, (h + 1) * vp)
            acc_ref[rs, :] = acc_ref[rs, :] * a + _dot(v_ref[0, rs, :], p)
            m_cur = m_nxt

    @pl.when(kj < qi)
    def _():
        step(False)

    @pl.when(kj == qi)
    def _():
        step(True)
        per = LANES // hd
        for g in range(heads // per):
            parts = []
            for n in range(per):
                r0 = (g * per + n) * vp
                parts.append(acc_ref[r0:r0 + hd, :] / acc_ref[r0 + hd:r0 + hd + 1, :])
            o_ref[:, g * LANES:(g + 1) * LANES] = jnp.concatenate(parts, axis=0).T.astype(BF16)


def _flash(k_aug, q_aug_t, v_t, *, bsz, seq, heads, blk=512):
    t, wide = k_aug.shape
    vrows = v_t.shape[1]
    hd = vrows // heads - VAL_PAD
    d = heads * hd
    nq = seq // blk
    pairs = [(i, j) for i in range(nq) for j in range(i + 1)]
    qi = jnp.asarray(np.array([p[0] for p in pairs], np.int32))
    kj = jnp.asarray(np.array([p[1] for p in pairs], np.int32))
    grid_spec = pltpu.PrefetchScalarGridSpec(
        num_scalar_prefetch=2,
        grid=(bsz, len(pairs)),
        in_specs=[
            pl.BlockSpec((blk, wide), lambda b, p, qi, kj: (b * nq + kj[p], 0)),
            pl.BlockSpec((1, wide, blk), lambda b, p, qi, kj: (b, 0, qi[p])),
            pl.BlockSpec((1, vrows, blk), lambda b, p, qi, kj: (b, 0, kj[p])),
        ],
        out_specs=pl.BlockSpec((blk, d), lambda b, p, qi, kj: (b * nq + qi[p], 0)),
        scratch_shapes=[
            pltpu.VMEM((heads, blk), F32),
            pltpu.VMEM((vrows, blk), F32),
            pltpu.VMEM((2, blk, blk), F32),
        ],
    )
    return pl.pallas_call(
        functools.partial(_flash_kernel, heads=heads, hd=hd, blk=blk),
        grid_spec=grid_spec,
        out_shape=jax.ShapeDtypeStruct((t, d), BF16),
        compiler_params=_cparams(("parallel", "arbitrary")),
        name="fox_attention",
    )(qi, kj, k_aug, q_aug_t, v_t)


def _mm_resln_kernel(a_ref, x_ref, mod_ref, w_ref, g_ref, b_ref, o_ref, *, alpha):
    y = _dot(a_ref[...], w_ref[...])
    o_ref[...] = _res_ln(x_ref[...], y, mod_ref[0], 0, g_ref[...], b_ref[...], alpha)


def _mm_resln(a, x, mod, w, ln_g, ln_b, *, seq, alpha, tm=512):
    t, d = x.shape
    k = a.shape[1]
    return pl.pallas_call(
        functools.partial(_mm_resln_kernel, alpha=alpha),
        grid=(t // tm,),
        in_specs=[
            pl.BlockSpec((tm, k), lambda i: (i, 0)),
            pl.BlockSpec((tm, d), lambda i: (i, 0)),
            pl.BlockSpec((1, 6, d), lambda i: (i * tm // seq, 0, 0)),
            _const_spec((k, d)),
            _const_spec((1, d)),
            _const_spec((1, d)),
        ],
        out_specs=pl.BlockSpec((tm, d), lambda i: (i, 0)),
        out_shape=jax.ShapeDtypeStruct((t, d), F32),
        compiler_params=_cparams(("parallel",)),
        name="proj_res_ln",
    )(a, x, mod, w.astype(BF16), ln_g.reshape(1, d), ln_b.reshape(1, d))


def _fox_layer(x, mod, w_in, b_f, w_out, ln_g, ln_b, *, bsz, seq, alpha):
    heads = b_f.shape[0]
    d = x.shape[1]
    cs = _forget_cumsum(x, mod, w_in[:, 3 * d:], b_f, bsz=bsz, seq=seq)
    k_aug, q_aug_t, v_t = _fox_proj(x, mod, cs, w_in, bsz=bsz, seq=seq, heads=heads)
    attn = _flash(k_aug, q_aug_t, v_t, bsz=bsz, seq=seq, heads=heads)
    return _mm_resln(attn, x, mod, w_out, ln_g, ln_b, seq=seq, alpha=alpha)


def _pool_kernel(x_ref, xh_ref, mod_ref, win_ref, wg_ref, ls_ref, wout_ref, g_ref, b_ref, o_ref, pooled_ref,
                 *, tm, seq, alpha):
    i = pl.program_id(0)
    pos0 = (i * tm) % seq
    m = mod_ref[0]
    x = x_ref[...]
    halo_ok = jnp.where(pos0 > 0, jnp.float32(1.0), jnp.float32(0.0))
    xe = jnp.concatenate([xh_ref[...], x], axis=0)
    ze = _dot(_modulate(xe, m, 0).astype(BF16), win_ref[...])
    rows = lax.broadcasted_iota(jnp.int32, (tm + POOL_HALO, 1), 0)
    ze = ze * jnp.where(rows < POOL_HALO, halo_ok, jnp.float32(1.0))
    pos = (pos0 + lax.broadcasted_iota(jnp.int32, (tm, 1), 0) + 1).astype(F32)
    gd = ze.shape[1] // len(POOL_WINDOWS)
    for g, win in enumerate(POOL_WINDOWS):
        cs = slice(g * gd, (g + 1) * gd)
        zg = ze[:, cs]
        s = zg
        span = 1
        while span < win:
            s = s + pltpu.roll(s, span, 0)
            span *= 2
        count = jnp.minimum(pos, jnp.float32(win))
        pooled = s[POOL_HALO:, :] / count - zg[POOL_HALO:, :]
        pooled_ref[:, cs] = (_dot(pooled.astype(BF16), wg_ref[g]) * ls_ref[:, cs]).astype(BF16)
    y = _dot(pooled_ref[...], wout_ref[...])
    o_ref[...] = _res_ln(x, y, m, 0, g_ref[...], b_ref[...], alpha)


def _pool_layer(x, mod, w_in, w_grp, scale, w_out, ln_g, ln_b, *, seq, alpha, tm=512):
    t, d = x.shape
    ng, gd, _ = w_grp.shape
    hb = tm // POOL_HALO
    return pl.pallas_call(
        functools.partial(_pool_kernel, tm=tm, seq=seq, alpha=alpha),
        grid=(t // tm,),
        in_specs=[
            pl.BlockSpec((tm, d), lambda i: (i, 0)),
            pl.BlockSpec((POOL_HALO, d), lambda i: (jnp.maximum(i * hb - 1, 0), 0)),
            pl.BlockSpec((1, 6, d), lambda i: (i * tm // seq, 0, 0)),
            _const_spec((d, d)),
            _const_spec((ng, gd, gd)),
            _const_spec((1, d)),
            _const_spec((d, d)),
            _const_spec((1, d)),
            _const_spec((1, d)),
        ],
        out_specs=pl.BlockSpec((tm, d), lambda i: (i, 0)),
        out_shape=jax.ShapeDtypeStruct((t, d), F32),
        scratch_shapes=[pltpu.VMEM((tm, d), BF16)],
        compiler_params=_cparams(("parallel",)),
        name="pool_mixer",
    )(x, x, mod, w_in.astype(BF16), w_grp.astype(BF16), scale.reshape(1, d), w_out.astype(BF16),
      ln_g.reshape(1, d), ln_b.reshape(1, d))


def kernel(x, c, mod_w, mod_b, ln_g, ln_b, gm_w_in, gm_ln_g, gm_ln_b, gm_w_s, gm_b_s, gm_w_out, fox_w_in,
           fox_b_f, fox_w_out, pool_w_in, pool_w_grp, pool_scale, pool_w_out, ffn_w13, ffn_w2, moe_w_router,
           moe_b_router, moe_w13, moe_w2):
    bsz, seq, d = x.shape
    depth = mod_w.shape[0]
    alpha = (2 * depth) ** 0.25
    n_mixers = 3
    mod_all = _modulation(c, mod_w, mod_b)
    xt = x.reshape(bsz * seq, d)
    for i in range(depth):
        mod = mod_all[i]
        kind, j = i % n_mixers, i // n_mixers
        if kind == 0:
            xt = _gmlp_layer(xt, mod, gm_w_in[j], gm_ln_g[j], gm_ln_b[j], gm_w_s[j], gm_b_s[j], gm_w_out[j],
                             ln_g[i, 0], ln_b[i, 0], seq=seq, alpha=alpha)
        elif kind == 1:
            xt = _fox_layer(xt, mod, fox_w_in[j], fox_b_f[j], fox_w_out[j], ln_g[i, 0], ln_b[i, 0],
                            bsz=bsz, seq=seq, alpha=alpha)
        else:
            xt = _pool_layer(xt, mod, pool_w_in[j], pool_w_grp[j], pool_scale[j], pool_w_out[j],
                             ln_g[i, 0], ln_b[i, 0], seq=seq, alpha=alpha)
        if i % 2 == 0:
            xt = _ffn_layer(xt, mod, ffn_w13[i // 2], ffn_w2[i // 2], ln_g[i, 1], ln_b[i, 1],
                            seq=seq, alpha=alpha)
        else:
            xt = _moe_layer(xt, mod, moe_w_router[i // 2], moe_b_router[i // 2], moe_w13, moe_w2, i // 2,
                            ln_g[i, 1], ln_b[i, 1], seq=seq, alpha=alpha)
    return xt.reshape(bsz, seq, d)
```

```python
import functools

import numpy as np
import jax
import jax.numpy as jnp
from jax import lax
from jax.experimental import pallas as pl
from jax.experimental.pallas import tpu as pltpu
from jax.experimental.pallas import tpu_sc as plsc

F32 = jnp.float32
BF16 = jnp.bfloat16

POOL_WINDOWS = (2, 4, 8, 16)
TOP_K = 2
EXPERT_BLOCK = 512
LN_EPS = 1e-5
LANES = 128
POOL_HALO = 16
VMEM_LIMIT = 56 * 1024 * 1024
SC_CORES = 2
SC_SUBCORES = 16
SC_ROWS = 128


def _cparams(sem):
    return pltpu.CompilerParams(dimension_semantics=sem, vmem_limit_bytes=VMEM_LIMIT)


def _const_spec(shape):
    nd = len(shape)
    return pl.BlockSpec(shape, lambda *_: (0,) * nd, pipeline_mode=pl.Buffered(1))


def _layer_norm(r, g, b):
    mu = jnp.mean(r, axis=-1, keepdims=True)
    xc = r - mu
    var = jnp.mean(xc * xc, axis=-1, keepdims=True)
    return xc * lax.rsqrt(var + LN_EPS) * g + b


def _modulate(x, m, off):
    return x * (1.0 + m[off + 1:off + 2]) + m[off:off + 1]


def _res_ln(x, y, m, off, g, b, alpha):
    return _layer_norm(alpha * x + (1.0 + m[off + 2:off + 3]) * y, g, b)


def _split3(a):
    hi = a.astype(BF16)
    r1 = a - hi.astype(F32)
    mid = r1.astype(BF16)
    lo = (r1 - mid.astype(F32)).astype(BF16)
    return hi, mid, lo


def _dot(a, b):
    return jnp.dot(a, b, preferred_element_type=F32)


def _pack_bf16_pair(x):
    n = x.shape[1] // 2
    hi = lax.bitcast_convert_type(x[:, :n].astype(BF16).astype(F32), jnp.uint32)
    lo = lax.bitcast_convert_type(x[:, n:].astype(BF16).astype(F32), jnp.uint32)
    return hi | (lo >> 16)


def _unpack_bf16_pair(w):
    hi = lax.bitcast_convert_type(w & jnp.uint32(0xFFFF0000), F32)
    lo = lax.bitcast_convert_type(w << 16, F32)
    return jnp.concatenate([hi, lo], axis=1)


def _mod_kernel(c_ref, w_ref, b_ref, o_ref):
    c = c_ref[...]
    s = c / (1.0 + jnp.exp(-c))
    s_hi, s_mid, _ = _split3(s)
    w = w_ref[0]
    w_hi = w.astype(BF16)
    w_lo = (w - w_hi.astype(F32)).astype(BF16)
    acc = _dot(s_hi, w_hi) + _dot(s_mid, w_hi) + _dot(s_hi, w_lo)
    o_ref[0] = acc + b_ref[0]


def _modulation(c, mod_w, mod_b):
    depth, d, n = mod_w.shape
    bsz = c.shape[0]
    rows = 8
    tn = n // 4
    c_pad = jnp.zeros((rows, d), F32).at[:bsz].set(c)
    out = pl.pallas_call(
        _mod_kernel,
        grid=(depth, n // tn),
        in_specs=[
            pl.BlockSpec((rows, d), lambda l, j: (0, 0)),
            pl.BlockSpec((1, d, tn), lambda l, j: (l, 0, j)),
            pl.BlockSpec((1, 1, tn), lambda l, j: (l, 0, j)),
        ],
        out_specs=pl.BlockSpec((1, rows, tn), lambda l, j: (l, 0, j)),
        out_shape=jax.ShapeDtypeStruct((depth, rows, n), F32),
        compiler_params=_cparams(("parallel", "parallel")),
        name="adaln_mod",
    )(c_pad, mod_w, mod_b.reshape(depth, 1, n))
    return out[:, :bsz, :].reshape(depth, bsz, 6, d)


def _gmlp_kernel(x_ref, mod_ref, win_ref, vg_ref, vb_ref, ws_ref, bst_ref, wout_ref, g_ref, b_ref,
                 o_ref, gated_ref, *, tm, sub, chunk, groups, alpha):
    m = mod_ref[0]
    row = lax.broadcasted_iota(jnp.int32, (chunk, chunk), 0)
    col = lax.broadcasted_iota(jnp.int32, (chunk, chunk), 1)
    causal = row >= col
    bst = bst_ref[...]
    w_mix = [jnp.where(causal, ws_ref[g], 0.0).astype(BF16) for g in range(groups)]
    n = tm // sub

    def project(s):
        x = x_ref[s * sub:(s + 1) * sub, :]
        return x, _dot(_modulate(x, m, 0).astype(BF16), win_ref[...])

    def activate(z):
        z = 0.5 * z * (1.0 + lax.erf(z * (2.0 ** -0.5)))
        width = z.shape[1] // 2
        return z[:, :width], _layer_norm(z[:, width:], vg_ref[...], vb_ref[...]).astype(BF16)

    def mix(s, u, v):
        gd = u.shape[1] // groups
        for g in range(groups):
            cs = slice(g * gd, (g + 1) * gd)
            for c in range(sub // chunk):
                rs = slice(c * chunk, (c + 1) * chunk)
                mixed = _dot(w_mix[g], v[rs, cs]) + bst[:, g:g + 1]
                gated_ref[s * sub + c * chunk:s * sub + (c + 1) * chunk, cs] = (u[rs, cs] * mixed).astype(BF16)
        return _dot(gated_ref[s * sub:(s + 1) * sub, :], wout_ref[...])

    def finish(s, x, y):
        o_ref[s * sub:(s + 1) * sub, :] = _res_ln(x, y, m, 0, g_ref[...], b_ref[...], alpha)

    nxt = project(0)
    pending = None
    for s in range(n):
        x, z = nxt
        if s + 1 < n:
            nxt = project(s + 1)
        u, v = activate(z)
        if pending is not None:
            finish(*pending)
        pending = (s, x, mix(s, u, v))
    finish(*pending)


def _gmlp_layer(x, mod, w_in, v_g, v_b, w_s, b_s, w_out, ln_g, ln_b, *, seq, alpha, tm=512, sub=256):
    t, d = x.shape
    groups, chunk, _ = w_s.shape
    width = w_out.shape[0]
    kern = functools.partial(_gmlp_kernel, tm=tm, sub=sub, chunk=chunk, groups=groups, alpha=alpha)
    return pl.pallas_call(
        kern,
        grid=(t // tm,),
        in_specs=[
            pl.BlockSpec((tm, d), lambda i: (i, 0)),
            pl.BlockSpec((1, 6, d), lambda i: (i * tm // seq, 0, 0)),
            _const_spec((d, 2 * width)),
            _const_spec((1, width)),
            _const_spec((1, width)),
            _const_spec((groups, chunk, chunk)),
            _const_spec((chunk, groups)),
            _const_spec((width, d)),
            _const_spec((1, d)),
            _const_spec((1, d)),
        ],
        out_specs=pl.BlockSpec((tm, d), lambda i: (i, 0)),
        out_shape=jax.ShapeDtypeStruct((t, d), F32),
        scratch_shapes=[pltpu.VMEM((tm, width), BF16)],
        compiler_params=_cparams(("parallel",)),
        name="gmlp_mixer",
    )(x, mod, w_in.astype(BF16), v_g.reshape(1, width), v_b.reshape(1, width), w_s, b_s.T,
      w_out.astype(BF16), ln_g.reshape(1, d), ln_b.reshape(1, d))


def _swiglu_pipelined(hs, w1, w3, w2, emit):
    up = lambda h: (_dot(h, w1), _dot(h, w3))
    nxt = up(hs[0])
    for s in range(len(hs)):
        a, b = nxt
        if s + 1 < len(hs):
            nxt = up(hs[s + 1])
        t = (a / (1.0 + jnp.exp(-a)) * b).astype(BF16)
        emit(s, _dot(t, w2))


def _ffn_kernel(x_ref, mod_ref, w1_ref, w3_ref, w2_ref, g_ref, b_ref, o_ref, h_ref, *, alpha, sub):
    j = pl.program_id(1)
    last = pl.num_programs(1) - 1
    subs = [slice(s, s + sub) for s in range(0, x_ref.shape[0], sub)]
    weights = lambda: (w1_ref[...], w3_ref[...], w2_ref[...])

    @pl.when(j == 0)
    def _():
        hs = []
        for rows in subs:
            h = _modulate(x_ref[rows, :], mod_ref[0], 3).astype(BF16)
            h_ref[rows, :] = h
            hs.append(h)

        def emit(s, part):
            o_ref[subs[s], :] = part

        _swiglu_pipelined(hs, *weights(), emit)

    @pl.when(jnp.logical_and(j > 0, j < last))
    def _():
        def emit(s, part):
            o_ref[subs[s], :] += part

        _swiglu_pipelined([h_ref[rows, :] for rows in subs], *weights(), emit)

    @pl.when(j == last)
    def _():
        def emit(s, part):
            rows = subs[s]
            y = o_ref[rows, :] + part
            o_ref[rows, :] = _res_ln(x_ref[rows, :], y, mod_ref[0], 3, g_ref[...], b_ref[...], alpha)

        _swiglu_pipelined([h_ref[rows, :] for rows in subs], *weights(), emit)


def _ffn_layer(x, mod, w13, w2, ln_g, ln_b, *, seq, alpha, tm=1024, tf=512):
    t, d = x.shape
    f = w2.shape[0]
    nf = f // tf
    assert nf >= 2
    w13b = w13.astype(BF16)
    return pl.pallas_call(
        functools.partial(_ffn_kernel, alpha=alpha, sub=tm // 2),
        grid=(t // tm, nf),
        in_specs=[
            pl.BlockSpec((tm, d), lambda i, j: (i, 0)),
            pl.BlockSpec((1, 6, d), lambda i, j: (i * tm // seq, 0, 0)),
            pl.BlockSpec((d, tf), lambda i, j: (0, j)),
            pl.BlockSpec((d, tf), lambda i, j: (0, nf + j)),
            pl.BlockSpec((tf, d), lambda i, j: (j, 0)),
            _const_spec((1, d)),
            _const_spec((1, d)),
        ],
        out_specs=pl.BlockSpec((tm, d), lambda i, j: (i, 0)),
        out_shape=jax.ShapeDtypeStruct((t, d), F32),
        scratch_shapes=[pltpu.VMEM((tm, d), BF16)],
        compiler_params=_cparams(("parallel", "arbitrary")),
        name="swiglu_dense",
    )(x, mod, w13b, w13b, w2.astype(BF16), ln_g.reshape(1, d), ln_b.reshape(1, d))


MOE_GROUP = 2


def _moe_ffn_kernel(ge_ref, gr_ref, h_ref, w1_ref, w3_ref, w2_ref, o_ref, w1b_ref, w3b_ref, w2b_ref, acc_ref,
                    *, sub):
    g = pl.program_id(0)
    j = pl.program_id(1)
    last = pl.num_programs(1) - 1
    n_rows = gr_ref[g]
    nv = (n_rows + EXPERT_BLOCK - 1) // EXPERT_BLOCK

    @pl.when(nv > 0)
    def _():
        w1b_ref[...] = w1_ref[0, 0].astype(BF16)
        w3b_ref[...] = w3_ref[0, 0].astype(BF16)
        w2b_ref[...] = w2_ref[0, 0].astype(BF16)

    def run(n_blocks, phase):
        subs = [slice(s, s + sub) for s in range(0, n_blocks * EXPERT_BLOCK, sub)]
        row = lax.broadcasted_iota(jnp.int32, (sub, 1), 0)

        def rows_in(rows):
            w = jnp.where(row + rows.start < n_rows, h_ref[rows, :], jnp.uint32(0))
            return _unpack_bf16_pair(w).astype(BF16)

        def emit(s, part):
            if phase == "first":
                acc_ref[subs[s], :] = part
            elif phase == "middle":
                acc_ref[subs[s], :] += part
            else:
                o_ref[subs[s], :] = _pack_bf16_pair(acc_ref[subs[s], :] + part)

        _swiglu_pipelined([rows_in(rows) for rows in subs], w1b_ref[...], w3b_ref[...], w2b_ref[...], emit)

    for n_blocks in range(1, MOE_GROUP + 1):
        @pl.when(jnp.logical_and(nv == n_blocks, j == 0))
        def _():
            run(n_blocks, "first")

        @pl.when(jnp.logical_and(nv == n_blocks, jnp.logical_and(j > 0, j < last)))
        def _():
            run(n_blocks, "middle")

        @pl.when(jnp.logical_and(nv == n_blocks, j == last))
        def _():
            run(n_blocks, "last")

    for blk in range(MOE_GROUP):
        @pl.when(jnp.logical_and(blk >= nv, j == last))
        def _():
            o_ref[blk * EXPERT_BLOCK:(blk + 1) * EXPERT_BLOCK, :] = jnp.zeros((EXPERT_BLOCK, o_ref.shape[1]),
                                                                              jnp.uint32)


def _moe_ffn(h_slots, group_expert, group_rows, w13, w2, layer, *, tf=512):
    cap, half = h_slots.shape
    d = 2 * half
    f = w2.shape[2]
    nf = f // tf
    assert nf >= 2
    tm = MOE_GROUP * EXPERT_BLOCK

    def jj(g, j, gr):
        return jnp.where(gr[g] > 0, j, nf - 1)

    grid_spec = pltpu.PrefetchScalarGridSpec(
        num_scalar_prefetch=2,
        grid=(cap // tm, nf),
        in_specs=[
            pl.BlockSpec((tm, half), lambda g, j, ge, gr: (g, 0)),
            pl.BlockSpec((1, 1, d, tf), lambda g, j, ge, gr: (layer, ge[g], 0, jj(g, j, gr))),
            pl.BlockSpec((1, 1, d, tf), lambda g, j, ge, gr: (layer, ge[g], 0, nf + jj(g, j, gr))),
            pl.BlockSpec((1, 1, tf, d), lambda g, j, ge, gr: (layer, ge[g], jj(g, j, gr), 0)),
        ],
        out_specs=pl.BlockSpec((tm, half), lambda g, j, ge, gr: (g, 0)),
        scratch_shapes=[pltpu.VMEM((d, tf), BF16), pltpu.VMEM((d, tf), BF16), pltpu.VMEM((tf, d), BF16),
                        pltpu.VMEM((tm, d), F32)],
    )
    return pl.pallas_call(
        functools.partial(_moe_ffn_kernel, sub=EXPERT_BLOCK),
        grid_spec=grid_spec,
        out_shape=jax.ShapeDtypeStruct((cap, half), jnp.uint32),
        compiler_params=_cparams(("parallel", "arbitrary")),
        name="swiglu_experts",
    )(group_expert, group_rows, h_slots, w13, w13, w2)


def _router_kernel(x_ref, mod_ref, wr_ref, br_ref, h_ref, meta_ref, cnt_ref, carry_ref, *, tm, n_exp):
    i = pl.program_id(0)

    @pl.when(i == 0)
    def _():
        carry_ref[...] = jnp.zeros_like(carry_ref)

    h = _modulate(x_ref[...], mod_ref[0], 3)
    h_ref[...] = _pack_bf16_pair(h)
    h_hi, h_mid, _ = _split3(h)
    w = wr_ref[...]
    w_hi = w.astype(BF16)
    w_lo = (w - w_hi.astype(F32)).astype(BF16)
    logits = _dot(h_hi, w_hi) + _dot(h_mid, w_hi) + _dot(h_hi, w_lo) + br_ref[...]
    lane = lax.broadcasted_iota(jnp.int32, (tm, LANES), 1)
    neg = jnp.float32(-jnp.inf)
    logits = jnp.where(lane < n_exp, logits, neg)
    v0 = jnp.max(logits, axis=1, keepdims=True)
    e0 = jnp.min(jnp.where(logits == v0, lane, LANES), axis=1, keepdims=True)
    rest = jnp.where(lane == e0, neg, logits)
    v1 = jnp.max(rest, axis=1, keepdims=True)
    e1 = jnp.min(jnp.where(rest == v1, lane, LANES), axis=1, keepdims=True)
    p = jnp.exp(v1 - v0)
    g0 = 1.0 / (1.0 + p)
    g1 = p / (1.0 + p)
    oh0 = lane == e0
    oh1 = lane == e1
    onehot = jnp.where(jnp.logical_or(oh0, oh1), 1.0, 0.0).astype(BF16)
    r = lax.broadcasted_iota(jnp.int32, (tm, tm), 0)
    c = lax.broadcasted_iota(jnp.int32, (tm, tm), 1)
    strict = jnp.where(r > c, 1.0, 0.0).astype(BF16)
    before = _dot(strict, onehot) + carry_ref[0:1, :]
    rank0 = jnp.sum(jnp.where(oh0, before, 0.0), axis=1, keepdims=True)
    rank1 = jnp.sum(jnp.where(oh1, before, 0.0), axis=1, keepdims=True)
    total = carry_ref[0:1, :] + jnp.sum(onehot.astype(F32), axis=0, keepdims=True)
    carry_ref[...] = jnp.broadcast_to(total, carry_ref.shape)
    cnt_ref[...] = jnp.broadcast_to(total, cnt_ref.shape)
    meta = jnp.where(lane == 0, e0.astype(F32), 0.0)
    meta = jnp.where(lane == 1, e1.astype(F32), meta)
    meta = jnp.where(lane == 2, g0, meta)
    meta = jnp.where(lane == 3, g1, meta)
    meta = jnp.where(lane == 4, rank0, meta)
    meta = jnp.where(lane == 5, rank1, meta)
    meta_ref[...] = meta


def _router(x, mod, w_router, b_router, *, seq, tm=512):
    t, d = x.shape
    n_exp = w_router.shape[1]
    wr = jnp.zeros((d, LANES), F32).at[:, :n_exp].set(w_router)
    br = jnp.zeros((1, LANES), F32).at[0, :n_exp].set(b_router)
    return pl.pallas_call(
        functools.partial(_router_kernel, tm=tm, n_exp=n_exp),
        grid=(t // tm,),
        in_specs=[
            pl.BlockSpec((tm, d), lambda i: (i, 0)),
            pl.BlockSpec((1, 6, d), lambda i: (i * tm // seq, 0, 0)),
            _const_spec((d, LANES)),
            _const_spec((1, LANES)),
        ],
        out_specs=[
            pl.BlockSpec((tm, d // 2), lambda i: (i, 0)),
            pl.BlockSpec((tm, LANES), lambda i: (i, 0)),
            pl.BlockSpec((8, LANES), lambda i: (0, 0)),
        ],
        out_shape=[
            jax.ShapeDtypeStruct((t, d // 2), jnp.uint32),
            jax.ShapeDtypeStruct((t, LANES), F32),
            jax.ShapeDtypeStruct((8, LANES), F32),
        ],
        scratch_shapes=[pltpu.VMEM((8, LANES), F32)],
        compiler_params=_cparams(("arbitrary",)),
        name="moe_router",
    )(x, mod, wr, br)


def _combine_kernel(x_ref, ya_ref, yb_ref, meta_ref, mod_ref, g_ref, b_ref, o_ref, *, alpha):
    meta = meta_ref[...]
    y = meta[:, 2:3] * _unpack_bf16_pair(ya_ref[...]) + meta[:, 3:4] * _unpack_bf16_pair(yb_ref[...])
    o_ref[...] = _res_ln(x_ref[...], y, mod_ref[0], 3, g_ref[...], b_ref[...], alpha)


def _combine(x, ya, yb, meta, mod, ln_g, ln_b, *, seq, alpha, tm=512):
    t, d = x.shape
    row = pl.BlockSpec((tm, d), lambda i: (i, 0))
    packed = pl.BlockSpec((tm, d // 2), lambda i: (i, 0))
    return pl.pallas_call(
        functools.partial(_combine_kernel, alpha=alpha),
        grid=(t // tm,),
        in_specs=[row, packed, packed,
                  pl.BlockSpec((tm, LANES), lambda i: (i, 0)),
                  pl.BlockSpec((1, 6, d), lambda i: (i * tm // seq, 0, 0)),
                  _const_spec((1, d)), _const_spec((1, d))],
        out_specs=row,
        out_shape=jax.ShapeDtypeStruct((t, d), F32),
        compiler_params=_cparams(("parallel",)),
        name="moe_combine",
    )(x, ya, yb, meta, mod, ln_g.reshape(1, d), ln_b.reshape(1, d))


def _sc_mesh():
    return plsc.VectorSubcoreMesh(core_axis_name="c", subcore_axis_name="s", num_cores=SC_CORES,
                                  num_subcores=SC_SUBCORES)


def _sc_worker_share(n):
    workers = SC_CORES * SC_SUBCORES
    per_worker = n // workers
    steps = per_worker // SC_ROWS
    assert steps * SC_ROWS * workers == n
    return per_worker, steps


def _sc_gather_rows(table, idx):
    n = idx.shape[0]
    width = table.shape[1]
    per_worker, steps = _sc_worker_share(n)

    def body(table_hbm, idx_hbm, out_hbm, idx_v, rows_v, sem):
        wid = lax.axis_index("s") * SC_CORES + lax.axis_index("c")

        @pl.loop(0, steps)
        def _(j):
            off = pl.multiple_of(wid * per_worker + j * SC_ROWS, SC_ROWS)
            pltpu.sync_copy(idx_hbm.at[pl.ds(off, SC_ROWS)], idx_v)
            pltpu.async_copy(table_hbm.at[idx_v], rows_v, sem).wait()
            pltpu.sync_copy(rows_v, out_hbm.at[pl.ds(off, SC_ROWS)])

    return pl.kernel(
        body,
        out_type=jax.ShapeDtypeStruct((n, width), table.dtype),
        mesh=_sc_mesh(),
        scratch_types=[
            pltpu.VMEM((SC_ROWS,), jnp.int32),
            pltpu.VMEM((SC_ROWS, width), table.dtype),
            pltpu.SemaphoreType.DMA,
        ],
        name="sc_row_gather",
    )(table, idx)


def _sc_scatter_rows(rows, dest_a, dest_b, n_out):
    n, width = rows.shape
    per_worker, steps = _sc_worker_share(n)

    def body(rows_hbm, da_hbm, db_hbm, out_hbm, ia_v, ib_v, rows_v, sem):
        wid = lax.axis_index("s") * SC_CORES + lax.axis_index("c")

        @pl.loop(0, steps)
        def _(j):
            off = pl.multiple_of(wid * per_worker + j * SC_ROWS, SC_ROWS)
            pltpu.sync_copy(da_hbm.at[pl.ds(off, SC_ROWS)], ia_v)
            pltpu.sync_copy(db_hbm.at[pl.ds(off, SC_ROWS)], ib_v)
            pltpu.sync_copy(rows_hbm.at[pl.ds(off, SC_ROWS)], rows_v)
            pltpu.async_copy(rows_v, out_hbm.at[ia_v], sem).wait()
            pltpu.async_copy(rows_v, out_hbm.at[ib_v], sem).wait()

    return pl.kernel(
        body,
        out_type=jax.ShapeDtypeStruct((n_out, width), rows.dtype),
        mesh=_sc_mesh(),
        scratch_types=[
            pltpu.VMEM((SC_ROWS,), jnp.int32),
            pltpu.VMEM((SC_ROWS,), jnp.int32),
            pltpu.VMEM((SC_ROWS, width), rows.dtype),
            pltpu.SemaphoreType.DMA,
        ],
        name="sc_row_scatter",
    )(rows, dest_a, dest_b)


def _moe_layer(x, mod, w_router, b_router, w13, w2, layer, ln_g, ln_b, *, seq, alpha):
    t, d = x.shape
    n_exp = w_router.shape[1]
    h, meta, cnt = _router(x, mod, w_router, b_router, seq=seq)
    e0 = meta[:, 0].astype(jnp.int32)
    e1 = meta[:, 1].astype(jnp.int32)
    rank0 = meta[:, 4].astype(jnp.int32)
    rank1 = meta[:, 5].astype(jnp.int32)
    counts = cnt[0, :n_exp].astype(jnp.int32)
    group_rows = MOE_GROUP * EXPERT_BLOCK
    padded = (counts + group_rows - 1) // group_rows * group_rows
    pad_end = jnp.cumsum(padded)
    pad_start = pad_end - padded
    dest0 = pad_start[e0] + rank0
    dest1 = pad_start[e1] + rank1
    cap = t * TOP_K + n_exp * group_rows
    group_start = jnp.arange(cap // group_rows, dtype=jnp.int32) * group_rows
    group_expert = jnp.minimum(jnp.searchsorted(pad_end, group_start, side='right'), n_exp - 1).astype(jnp.int32)
    rows_left = counts[group_expert] - (group_start - pad_start[group_expert])
    rows_in_group = jnp.clip(rows_left, 0, group_rows).astype(jnp.int32)
    h_slots = _sc_scatter_rows(h, dest0, dest1, cap)
    y_slots = _moe_ffn(h_slots, group_expert, rows_in_group, w13, w2, layer)
    ya = _sc_gather_rows(y_slots, dest0)
    yb = _sc_gather_rows(y_slots, dest1)
    return _combine(x, ya, yb, meta, mod, ln_g, ln_b, seq=seq, alpha=alpha)


def _gate_kernel(x_ref, mod_ref, wf_ref, bf_ref, o_ref, carry_ref, *, tm):
    @pl.when(pl.program_id(1) == 0)
    def _():
        carry_ref[...] = jnp.zeros_like(carry_ref)

    h = _modulate(x_ref[...], mod_ref[0], 0)
    h_hi, h_mid, _ = _split3(h)
    wf = wf_ref[...]
    wf_hi = wf.astype(BF16)
    wf_lo = (wf - wf_hi.astype(F32)).astype(BF16)
    f = _dot(h_hi, wf_hi) + _dot(h_mid, wf_hi) + _dot(h_hi, wf_lo) + bf_ref[...]
    lf = jnp.minimum(f, 0.0) - jnp.log(1.0 + jnp.exp(-jnp.abs(f)))
    r = lax.broadcasted_iota(jnp.int32, (tm, tm), 0)
    c = lax.broadcasted_iota(jnp.int32, (tm, tm), 1)
    tri = jnp.where(r >= c, 1.0, 0.0).astype(BF16)
    hi, mid, lo = _split3(lf)
    cs = _dot(tri, hi) + _dot(tri, mid) + _dot(tri, lo) + carry_ref[0:1, :]
    o_ref[...] = cs
    carry_ref[...] = jnp.broadcast_to(cs[tm - 1:tm, :], carry_ref.shape)


def _forget_cumsum(x, mod, w_f, b_f, *, bsz, seq, tm=512):
    t, d = x.shape
    heads = b_f.shape[0]
    nb = seq // tm
    wf = jnp.zeros((d, LANES), F32).at[:, :heads].set(w_f)
    bf = jnp.zeros((1, LANES), F32).at[0, :heads].set(b_f)
    return pl.pallas_call(
        functools.partial(_gate_kernel, tm=tm),
        grid=(bsz, nb),
        in_specs=[
            pl.BlockSpec((tm, d), lambda b, i: (b * nb + i, 0)),
            pl.BlockSpec((1, 6, d), lambda b, i: (b, 0, 0)),
            _const_spec((d, LANES)),
            _const_spec((1, LANES)),
        ],
        out_specs=pl.BlockSpec((tm, LANES), lambda b, i: (b * nb + i, 0)),
        out_shape=jax.ShapeDtypeStruct((t, LANES), F32),
        scratch_shapes=[pltpu.VMEM((8, LANES), F32)],
        compiler_params=_cparams(("parallel", "arbitrary")),
        name="fox_forget_cumsum",
    )(x, mod, wf, bf)


HEAD_PAD = LANES
GATE_PARTS = 3
VAL_PAD = 16
LOG2E = 1.4426950408889634


def _fox_proj_kernel(x_ref, mod_ref, cs_ref, wq_ref, wk_ref, wv_ref, selk_ref, onek_ref, selq_ref, oneq_ref,
                     onev_ref, k_ref, q_ref, v_ref, *, q_scale):
    h = _modulate(x_ref[...], mod_ref[0], 0).astype(BF16)
    cs = cs_ref[...] * LOG2E
    parts = _split3(cs)
    k = _dot(h, wk_ref[...]) + onek_ref[...]
    for n in range(GATE_PARTS):
        k = k + _dot(parts[n], selk_ref[n])
    k_ref[...] = k.astype(BF16)
    nt = (((1,), (1,)), ((), ()))
    parts_t = _split3(cs.T)
    q = lax.dot_general(wq_ref[...], h, nt, preferred_element_type=F32) * (q_scale * LOG2E) + oneq_ref[...]
    for n in range(GATE_PARTS):
        q = q + _dot(selq_ref[n], parts_t[n])
    q_ref[0] = q.astype(BF16)
    v = lax.dot_general(wv_ref[...], h, nt, preferred_element_type=F32) + onev_ref[...]
    v_ref[0] = v.astype(BF16)


def _fox_proj(x, mod, cs, w_in, *, bsz, seq, heads, tm=512):
    t, d = x.shape
    hd = d // heads
    nb = seq // tm
    wide = heads * HEAD_PAD
    pad = lambda w: jnp.pad(w.reshape(d, heads, hd), ((0, 0), (0, 0), (0, HEAD_PAD - hd))).reshape(d, wide)
    wq_t = pad(w_in[:, :d]).T.astype(BF16)
    wk = pad(w_in[:, d:2 * d]).astype(BF16)
    vrows = heads * (hd + VAL_PAD)
    wv_t = jnp.pad(w_in[:, 2 * d:3 * d].reshape(d, heads, hd),
                   ((0, 0), (0, 0), (0, VAL_PAD))).reshape(d, vrows).T.astype(BF16)
    sel_k = np.zeros((GATE_PARTS, LANES, wide), np.float32)
    sel_q = np.zeros((GATE_PARTS, wide, LANES), np.float32)
    one_k = np.zeros((1, wide), np.float32)
    one_q = np.zeros((wide, 1), np.float32)
    one_v = np.zeros((vrows, 1), np.float32)
    for hh in range(heads):
        base = hh * HEAD_PAD + hd
        one_v[hh * (hd + VAL_PAD) + hd, 0] = 1.0
        for n in range(GATE_PARTS):
            sel_k[n, hh, base + n] = -1.0
            one_q[base + n, 0] = 1.0
            one_k[0, base + GATE_PARTS + n] = 1.0
            sel_q[n, base + GATE_PARTS + n, hh] = 1.0
    return pl.pallas_call(
        functools.partial(_fox_proj_kernel, q_scale=hd ** -0.5),
        grid=(t // tm,),
        in_specs=[
            pl.BlockSpec((tm, d), lambda i: (i, 0)),
            pl.BlockSpec((1, 6, d), lambda i: (i * tm // seq, 0, 0)),
            pl.BlockSpec((tm, LANES), lambda i: (i, 0)),
            _const_spec((wide, d)),
            _const_spec((d, wide)),
            _const_spec((vrows, d)),
            _const_spec((GATE_PARTS, LANES, wide)),
            _const_spec((1, wide)),
            _const_spec((GATE_PARTS, wide, LANES)),
            _const_spec((wide, 1)),
            _const_spec((vrows, 1)),
        ],
        out_specs=[
            pl.BlockSpec((tm, wide), lambda i: (i, 0)),
            pl.BlockSpec((1, wide, tm), lambda i: (i // nb, 0, i % nb)),
            pl.BlockSpec((1, vrows, tm), lambda i: (i // nb, 0, i % nb)),
        ],
        out_shape=[
            jax.ShapeDtypeStruct((t, wide), BF16),
            jax.ShapeDtypeStruct((bsz, wide, seq), BF16),
            jax.ShapeDtypeStruct((bsz, vrows, seq), BF16),
        ],
        compiler_params=_cparams(("parallel",)),
        name="fox_proj",
    )(x, mod, cs, wq_t, wk, wv_t, jnp.asarray(sel_k, BF16), jnp.asarray(one_k), jnp.asarray(sel_q, BF16),
      jnp.asarray(one_q), jnp.asarray(one_v))


def _flash_kernel(qi_ref, kj_ref, k_ref, q_ref, v_ref, o_ref, m_ref, acc_ref, s_ref, *, heads, hd, blk):
    p_idx = pl.program_id(1)
    qi = qi_ref[p_idx]
    kj = kj_ref[p_idx]
    vp = hd + VAL_PAD

    @pl.when(kj == 0)
    def _():
        m_ref[...] = jnp.full(m_ref.shape, -jnp.inf, F32)
        acc_ref[...] = jnp.zeros_like(acc_ref)

    def step(masked):
        if masked:
            kpos = lax.broadcasted_iota(jnp.int32, (blk, blk), 0)
            qpos = lax.broadcasted_iota(jnp.int32, (blk, blk), 1)
            keep = kpos <= qpos

        def scores(h):
            hs = slice(h * HEAD_PAD, (h + 1) * HEAD_PAD)
            s = _dot(k_ref[:, hs], q_ref[0, hs, :])
            if masked:
                s = jnp.where(keep, s, -jnp.inf)
            s_ref[h % 2] = s
            return jnp.max(s, axis=0, keepdims=True)

        m_cur = scores(0)
        for h in range(heads):
            m_nxt = scores(h + 1) if h + 1 < heads else None
            m_prev = m_ref[h:h + 1, :]
            m_new = jnp.maximum(m_prev, m_cur)
            m_ref[h:h + 1, :] = m_new
            a = jnp.exp2(m_prev - m_new)
            p = jnp.exp2(s_ref[h % 2] - m_new).astype(BF16)
            rs = slice(h * vp, (h + 1) * vp)
            acc_ref[rs, :] = acc_ref[rs, :] * a + _dot(v_ref[0, rs, :], p)
            m_cur = m_nxt

    @pl.when(kj < qi)
    def _():
        step(False)

    @pl.when(kj == qi)
    def _():
        step(True)
        per = LANES // hd
        for g in range(heads // per):
            parts = []
            for n in range(per):
                r0 = (g * per + n) * vp
                parts.append(acc_ref[r0:r0 + hd, :] / acc_ref[r0 + hd:r0 + hd + 1, :])
            o_ref[:, g * LANES:(g + 1) * LANES] = jnp.concatenate(parts, axis=0).T.astype(BF16)


def _flash(k_aug, q_aug_t, v_t, *, bsz, seq, heads, blk=512):
    t, wide = k_aug.shape
    vrows = v_t.shape[1]
    hd = vrows // heads - VAL_PAD
    d = heads * hd
    nq = seq // blk
    pairs = [(i, j) for i in range(nq) for j in range(i + 1)]
    qi = jnp.asarray(np.array([p[0] for p in pairs], np.int32))
    kj = jnp.asarray(np.array([p[1] for p in pairs], np.int32))
    grid_spec = pltpu.PrefetchScalarGridSpec(
        num_scalar_prefetch=2,
        grid=(bsz, len(pairs)),
        in_specs=[
            pl.BlockSpec((blk, wide), lambda b, p, qi, kj: (b * nq + kj[p], 0)),
            pl.BlockSpec((1, wide, blk), lambda b, p, qi, kj: (b, 0, qi[p])),
            pl.BlockSpec((1, vrows, blk), lambda b, p, qi, kj: (b, 0, kj[p])),
        ],
        out_specs=pl.BlockSpec((blk, d), lambda b, p, qi, kj: (b * nq + qi[p], 0)),
        scratch_shapes=[
            pltpu.VMEM((heads, blk), F32),
            pltpu.VMEM((vrows, blk), F32),
            pltpu.VMEM((2, blk, blk), F32),
        ],
    )
    return pl.pallas_call(
        functools.partial(_flash_kernel, heads=heads, hd=hd, blk=blk),
        grid_spec=grid_spec,
        out_shape=jax.ShapeDtypeStruct((t, d), BF16),
        compiler_params=_cparams(("parallel", "arbitrary")),
        name="fox_attention",
    )(qi, kj, k_aug, q_aug_t, v_t)


def _mm_resln_kernel(a_ref, x_ref, mod_ref, w_ref, g_ref, b_ref, o_ref, *, alpha):
    y = _dot(a_ref[...], w_ref[...])
    o_ref[...] = _res_ln(x_ref[...], y, mod_ref[0], 0, g_ref[...], b_ref[...], alpha)


def _mm_resln(a, x, mod, w, ln_g, ln_b, *, seq, alpha, tm=512):
    t, d = x.shape
    k = a.shape[1]
    return pl.pallas_call(
        functools.partial(_mm_resln_kernel, alpha=alpha),
        grid=(t // tm,),
        in_specs=[
            pl.BlockSpec((tm, k), lambda i: (i, 0)),
            pl.BlockSpec((tm, d), lambda i: (i, 0)),
            pl.BlockSpec((1, 6, d), lambda i: (i * tm // seq, 0, 0)),
            _const_spec((k, d)),
            _const_spec((1, d)),
            _const_spec((1, d)),
        ],
        out_specs=pl.BlockSpec((tm, d), lambda i: (i, 0)),
        out_shape=jax.ShapeDtypeStruct((t, d), F32),
        compiler_params=_cparams(("parallel",)),
        name="proj_res_ln",
    )(a, x, mod, w.astype(BF16), ln_g.reshape(1, d), ln_b.reshape(1, d))


def _fox_layer(x, mod, w_in, b_f, w_out, ln_g, ln_b, *, bsz, seq, alpha):
    heads = b_f.shape[0]
    d = x.shape[1]
    cs = _forget_cumsum(x, mod, w_in[:, 3 * d:], b_f, bsz=bsz, seq=seq)
    k_aug, q_aug_t, v_t = _fox_proj(x, mod, cs, w_in, bsz=bsz, seq=seq, heads=heads)
    attn = _flash(k_aug, q_aug_t, v_t, bsz=bsz, seq=seq, heads=heads)
    return _mm_resln(attn, x, mod, w_out, ln_g, ln_b, seq=seq, alpha=alpha)


def _pool_kernel(x_ref, xh_ref, mod_ref, win_ref, wg_ref, ls_ref, wout_ref, g_ref, b_ref, o_ref, pooled_ref,
                 *, tm, seq, alpha):
    i = pl.program_id(0)
    pos0 = (i * tm) % seq
    m = mod_ref[0]
    x = x_ref[...]
    halo_ok = jnp.where(pos0 > 0, jnp.float32(1.0), jnp.float32(0.0))
    xe = jnp.concatenate([xh_ref[...], x], axis=0)
    ze = _dot(_modulate(xe, m, 0).astype(BF16), win_ref[...])
    rows = lax.broadcasted_iota(jnp.int32, (tm + POOL_HALO, 1), 0)
    ze = ze * jnp.where(rows < POOL_HALO, halo_ok, jnp.float32(1.0))
    pos = (pos0 + lax.broadcasted_iota(jnp.int32, (tm, 1), 0) + 1).astype(F32)
    gd = ze.shape[1] // len(POOL_WINDOWS)
    for g, win in enumerate(POOL_WINDOWS):
        cs = slice(g * gd, (g + 1) * gd)
        zg = ze[:, cs]
        s = zg
        span = 1
        while span < win:
            s = s + pltpu.roll(s, span, 0)
            span *= 2
        count = jnp.minimum(pos, jnp.float32(win))
        pooled = s[POOL_HALO:, :] / count - zg[POOL_HALO:, :]
        pooled_ref[:, cs] = (_dot(pooled.astype(BF16), wg_ref[g]) * ls_ref[:, cs]).astype(BF16)
    y = _dot(pooled_ref[...], wout_ref[...])
    o_ref[...] = _res_ln(x, y, m, 0, g_ref[...], b_ref[...], alpha)


def _pool_layer(x, mod, w_in, w_grp, scale, w_out, ln_g, ln_b, *, seq, alpha, tm=512):
    t, d = x.shape
    ng, gd, _ = w_grp.shape
    hb = tm // POOL_HALO
    return pl.pallas_call(
        functools.partial(_pool_kernel, tm=tm, seq=seq, alpha=alpha),
        grid=(t // tm,),
        in_specs=[
            pl.BlockSpec((tm, d), lambda i: (i, 0)),
            pl.BlockSpec((POOL_HALO, d), lambda i: (jnp.maximum(i * hb - 1, 0), 0)),
            pl.BlockSpec((1, 6, d), lambda i: (i * tm // seq, 0, 0)),
            _const_spec((d, d)),
            _const_spec((ng, gd, gd)),
            _const_spec((1, d)),
            _const_spec((d, d)),
            _const_spec((1, d)),
            _const_spec((1, d)),
        ],
        out_specs=pl.BlockSpec((tm, d), lambda i: (i, 0)),
        out_shape=jax.ShapeDtypeStruct((t, d), F32),
        scratch_shapes=[pltpu.VMEM((tm, d), BF16)],
        compiler_params=_cparams(("parallel",)),
        name="pool_mixer",
    )(x, x, mod, w_in.astype(BF16), w_grp.astype(BF16), scale.reshape(1, d), w_out.astype(BF16),
      ln_g.reshape(1, d), ln_b.reshape(1, d))


def kernel(x, c, mod_w, mod_b, ln_g, ln_b, gm_w_in, gm_ln_g, gm_ln_b, gm_w_s, gm_b_s, gm_w_out, fox_w_in,
           fox_b_f, fox_w_out, pool_w_in, pool_w_grp, pool_scale, pool_w_out, ffn_w13, ffn_w2, moe_w_router,
           moe_b_router, moe_w13, moe_w2):
    bsz, seq, d = x.shape
    depth = mod_w.shape[0]
    alpha = (2 * depth) ** 0.25
    n_mixers = 3
    mod_all = _modulation(c, mod_w, mod_b)
    xt = x.reshape(bsz * seq, d)
    for i in range(depth):
        mod = mod_all[i]
        kind, j = i % n_mixers, i // n_mixers
        if kind == 0:
            xt = _gmlp_layer(xt, mod, gm_w_in[j], gm_ln_g[j], gm_ln_b[j], gm_w_s[j], gm_b_s[j], gm_w_out[j],
                             ln_g[i, 0], ln_b[i, 0], seq=seq, alpha=alpha)
        elif kind == 1:
            xt = _fox_layer(xt, mod, fox_w_in[j], fox_b_f[j], fox_w_out[j], ln_g[i, 0], ln_b[i, 0],
                            bsz=bsz, seq=seq, alpha=alpha)
        else:
            xt = _pool_layer(xt, mod, pool_w_in[j], pool_w_grp[j], pool_scale[j], pool_w_out[j],
                             ln_g[i, 0], ln_b[i, 0], seq=seq, alpha=alpha)
        if i % 2 == 0:
            xt = _ffn_layer(xt, mod, ffn_w13[i // 2], ffn_w2[i // 2], ln_g[i, 1], ln_b[i, 1],
                            seq=seq, alpha=alpha)
        else:
            xt = _moe_layer(xt, mod, moe_w_router[i // 2], moe_b_router[i // 2], moe_w13, moe_w2, i // 2,
                            ln_g[i, 1], ln_b[i, 1], seq=seq, alpha=alpha)
    return xt.reshape(bsz, seq, d)
```

```python
import functools

import numpy as np
import jax
import jax.numpy as jnp
from jax import lax
from jax.experimental import pallas as pl
from jax.experimental.pallas import tpu as pltpu
from jax.experimental.pallas import tpu_sc as plsc

F32 = jnp.float32
BF16 = jnp.bfloat16

POOL_WINDOWS = (2, 4, 8, 16)
TOP_K = 2
EXPERT_BLOCK = 512
LN_EPS = 1e-5
LANES = 128
POOL_HALO = 16
VMEM_LIMIT = 56 * 1024 * 1024
SC_CORES = 2
SC_SUBCORES = 16
SC_ROWS = 128


def _cparams(sem):
    return pltpu.CompilerParams(dimension_semantics=sem, vmem_limit_bytes=VMEM_LIMIT)


def _const_spec(shape):
    nd = len(shape)
    return pl.BlockSpec(shape, lambda *_: (0,) * nd, pipeline_mode=pl.Buffered(1))


def _layer_norm(r, g, b):
    mu = jnp.mean(r, axis=-1, keepdims=True)
    xc = r - mu
    var = jnp.mean(xc * xc, axis=-1, keepdims=True)
    return xc * lax.rsqrt(var + LN_EPS) * g + b


def _modulate(x, m, off):
    return x * (1.0 + m[off + 1:off + 2]) + m[off:off + 1]


def _res_ln(x, y, m, off, g, b, alpha):
    return _layer_norm(alpha * x + (1.0 + m[off + 2:off + 3]) * y, g, b)


def _split3(a):
    hi = a.astype(BF16)
    r1 = a - hi.astype(F32)
    mid = r1.astype(BF16)
    lo = (r1 - mid.astype(F32)).astype(BF16)
    return hi, mid, lo


def _dot(a, b):
    return jnp.dot(a, b, preferred_element_type=F32)


def _pack_bf16_pair(x):
    n = x.shape[1] // 2
    hi = lax.bitcast_convert_type(x[:, :n].astype(BF16).astype(F32), jnp.uint32)
    lo = lax.bitcast_convert_type(x[:, n:].astype(BF16).astype(F32), jnp.uint32)
    return hi | (lo >> 16)


def _unpack_bf16_pair(w):
    hi = lax.bitcast_convert_type(w & jnp.uint32(0xFFFF0000), F32)
    lo = lax.bitcast_convert_type(w << 16, F32)
    return jnp.concatenate([hi, lo], axis=1)


def _mod_kernel(c_ref, w_ref, b_ref, o_ref):
    c = c_ref[...]
    s = c / (1.0 + jnp.exp(-c))
    s_hi, s_mid, _ = _split3(s)
    w = w_ref[0]
    w_hi = w.astype(BF16)
    w_lo = (w - w_hi.astype(F32)).astype(BF16)
    acc = _dot(s_hi, w_hi) + _dot(s_mid, w_hi) + _dot(s_hi, w_lo)
    o_ref[0] = acc + b_ref[0]


def _modulation(c, mod_w, mod_b):
    depth, d, n = mod_w.shape
    bsz = c.shape[0]
    rows = 8
    tn = n // 4
    c_pad = jnp.zeros((rows, d), F32).at[:bsz].set(c)
    out = pl.pallas_call(
        _mod_kernel,
        grid=(depth, n // tn),
        in_specs=[
            pl.BlockSpec((rows, d), lambda l, j: (0, 0)),
            pl.BlockSpec((1, d, tn), lambda l, j: (l, 0, j)),
            pl.BlockSpec((1, 1, tn), lambda l, j: (l, 0, j)),
        ],
        out_specs=pl.BlockSpec((1, rows, tn), lambda l, j: (l, 0, j)),
        out_shape=jax.ShapeDtypeStruct((depth, rows, n), F32),
        compiler_params=_cparams(("parallel", "parallel")),
        name="adaln_mod",
    )(c_pad, mod_w, mod_b.reshape(depth, 1, n))
    return out[:, :bsz, :].reshape(depth, bsz, 6, d)


def _gmlp_kernel(x_ref, mod_ref, win_ref, vg_ref, vb_ref, ws_ref, bst_ref, wout_ref, g_ref, b_ref,
                 o_ref, gated_ref, *, tm, sub, chunk, groups, alpha):
    m = mod_ref[0]
    row = lax.broadcasted_iota(jnp.int32, (chunk, chunk), 0)
    col = lax.broadcasted_iota(jnp.int32, (chunk, chunk), 1)
    causal = row >= col
    bst = bst_ref[...]
    w_mix = [jnp.where(causal, ws_ref[g], 0.0).astype(BF16) for g in range(groups)]
    n = tm // sub

    def project(s):
        x = x_ref[s * sub:(s + 1) * sub, :]
        return x, _dot(_modulate(x, m, 0).astype(BF16), win_ref[...])

    def activate(z):
        z = 0.5 * z * (1.0 + lax.erf(z * (2.0 ** -0.5)))
        width = z.shape[1] // 2
        return z[:, :width], _layer_norm(z[:, width:], vg_ref[...], vb_ref[...]).astype(BF16)

    def mix(s, u, v):
        gd = u.shape[1] // groups
        for g in range(groups):
            cs = slice(g * gd, (g + 1) * gd)
            for c in range(sub // chunk):
                rs = slice(c * chunk, (c + 1) * chunk)
                mixed = _dot(w_mix[g], v[rs, cs]) + bst[:, g:g + 1]
                gated_ref[s * sub + c * chunk:s * sub + (c + 1) * chunk, cs] = (u[rs, cs] * mixed).astype(BF16)
        return _dot(gated_ref[s * sub:(s + 1) * sub, :], wout_ref[...])

    def finish(s, x, y):
        o_ref[s * sub:(s + 1) * sub, :] = _res_ln(x, y, m, 0, g_ref[...], b_ref[...], alpha)

    nxt = project(0)
    pending = None
    for s in range(n):
        x, z = nxt
        if s + 1 < n:
            nxt = project(s + 1)
        u, v = activate(z)
        if pending is not None:
            finish(*pending)
        pending = (s, x, mix(s, u, v))
    finish(*pending)


def _gmlp_layer(x, mod, w_in, v_g, v_b, w_s, b_s, w_out, ln_g, ln_b, *, seq, alpha, tm=512, sub=256):
    t, d = x.shape
    groups, chunk, _ = w_s.shape
    width = w_out.shape[0]
    kern = functools.partial(_gmlp_kernel, tm=tm, sub=sub, chunk=chunk, groups=groups, alpha=alpha)
    return pl.pallas_call(
        kern,
        grid=(t // tm,),
        in_specs=[
            pl.BlockSpec((tm, d), lambda i: (i, 0)),
            pl.BlockSpec((1, 6, d), lambda i: (i * tm // seq, 0, 0)),
            _const_spec((d, 2 * width)),
            _const_spec((1, width)),
            _const_spec((1, width)),
            _const_spec((groups, chunk, chunk)),
            _const_spec((chunk, groups)),
            _const_spec((width, d)),
            _const_spec((1, d)),
            _const_spec((1, d)),
        ],
        out_specs=pl.BlockSpec((tm, d), lambda i: (i, 0)),
        out_shape=jax.ShapeDtypeStruct((t, d), F32),
        scratch_shapes=[pltpu.VMEM((tm, width), BF16)],
        compiler_params=_cparams(("parallel",)),
        name="gmlp_mixer",
    )(x, mod, w_in.astype(BF16), v_g.reshape(1, width), v_b.reshape(1, width), w_s, b_s.T,
      w_out.astype(BF16), ln_g.reshape(1, d), ln_b.reshape(1, d))


def _swiglu_pipelined(hs, w1, w3, w2, emit):
    up = lambda h: (_dot(h, w1), _dot(h, w3))
    nxt = up(hs[0])
    for s in range(len(hs)):
        a, b = nxt
        if s + 1 < len(hs):
            nxt = up(hs[s + 1])
        t = (a / (1.0 + jnp.exp(-a)) * b).astype(BF16)
        emit(s, _dot(t, w2))


def _ffn_kernel(x_ref, mod_ref, w1_ref, w3_ref, w2_ref, g_ref, b_ref, o_ref, h_ref, *, alpha, sub):
    j = pl.program_id(1)
    last = pl.num_programs(1) - 1
    subs = [slice(s, s + sub) for s in range(0, x_ref.shape[0], sub)]
    weights = lambda: (w1_ref[...], w3_ref[...], w2_ref[...])

    @pl.when(j == 0)
    def _():
        hs = []
        for rows in subs:
            h = _modulate(x_ref[rows, :], mod_ref[0], 3).astype(BF16)
            h_ref[rows, :] = h
            hs.append(h)

        def emit(s, part):
            o_ref[subs[s], :] = part

        _swiglu_pipelined(hs, *weights(), emit)

    @pl.when(jnp.logical_and(j > 0, j < last))
    def _():
        def emit(s, part):
            o_ref[subs[s], :] += part

        _swiglu_pipelined([h_ref[rows, :] for rows in subs], *weights(), emit)

    @pl.when(j == last)
    def _():
        def emit(s, part):
            rows = subs[s]
            y = o_ref[rows, :] + part
            o_ref[rows, :] = _res_ln(x_ref[rows, :], y, mod_ref[0], 3, g_ref[...], b_ref[...], alpha)

        _swiglu_pipelined([h_ref[rows, :] for rows in subs], *weights(), emit)


def _ffn_layer(x, mod, w13, w2, ln_g, ln_b, *, seq, alpha, tm=1024, tf=512):
    t, d = x.shape
    f = w2.shape[0]
    nf = f // tf
    assert nf >= 2
    w13b = w13.astype(BF16)
    return pl.pallas_call(
        functools.partial(_ffn_kernel, alpha=alpha, sub=tm // 2),
        grid=(t // tm, nf),
        in_specs=[
            pl.BlockSpec((tm, d), lambda i, j: (i, 0)),
            pl.BlockSpec((1, 6, d), lambda i, j: (i * tm // seq, 0, 0)),
            pl.BlockSpec((d, tf), lambda i, j: (0, j)),
            pl.BlockSpec((d, tf), lambda i, j: (0, nf + j)),
            pl.BlockSpec((tf, d), lambda i, j: (j, 0)),
            _const_spec((1, d)),
            _const_spec((1, d)),
        ],
        out_specs=pl.BlockSpec((tm, d), lambda i, j: (i, 0)),
        out_shape=jax.ShapeDtypeStruct((t, d), F32),
        scratch_shapes=[pltpu.VMEM((tm, d), BF16)],
        compiler_params=_cparams(("parallel", "arbitrary")),
        name="swiglu_dense",
    )(x, mod, w13b, w13b, w2.astype(BF16), ln_g.reshape(1, d), ln_b.reshape(1, d))


MOE_GROUP = 2


def _moe_ffn_kernel(ge_ref, gr_ref, h_ref, w1_ref, w3_ref, w2_ref, o_ref, acc_ref, *, sub):
    g = pl.program_id(0)
    j = pl.program_id(1)
    last = pl.num_programs(1) - 1
    n_rows = gr_ref[g]
    nv = (n_rows + EXPERT_BLOCK - 1) // EXPERT_BLOCK

    def run(n_blocks, phase):
        subs = [slice(s, s + sub) for s in range(0, n_blocks * EXPERT_BLOCK, sub)]
        row = lax.broadcasted_iota(jnp.int32, (sub, 1), 0)

        def rows_in(rows):
            w = jnp.where(row + rows.start < n_rows, h_ref[rows, :], jnp.uint32(0))
            return _unpack_bf16_pair(w).astype(BF16)

        def emit(s, part):
            if phase == "first":
                acc_ref[subs[s], :] = part
            elif phase == "middle":
                acc_ref[subs[s], :] += part
            else:
                o_ref[subs[s], :] = _pack_bf16_pair(acc_ref[subs[s], :] + part)

        _swiglu_pipelined([rows_in(rows) for rows in subs], w1_ref[0, 0].astype(BF16), w3_ref[0, 0].astype(BF16),
                          w2_ref[0, 0].astype(BF16), emit)

    for n_blocks in range(1, MOE_GROUP + 1):
        @pl.when(jnp.logical_and(nv == n_blocks, j == 0))
        def _():
            run(n_blocks, "first")

        @pl.when(jnp.logical_and(nv == n_blocks, jnp.logical_and(j > 0, j < last)))
        def _():
            run(n_blocks, "middle")

        @pl.when(jnp.logical_and(nv == n_blocks, j == last))
        def _():
            run(n_blocks, "last")

    for blk in range(MOE_GROUP):
        @pl.when(jnp.logical_and(blk >= nv, j == last))
        def _():
            o_ref[blk * EXPERT_BLOCK:(blk + 1) * EXPERT_BLOCK, :] = jnp.zeros((EXPERT_BLOCK, o_ref.shape[1]),
                                                                              jnp.uint32)


def _moe_ffn(h_slots, group_expert, group_rows, w13, w2, layer, *, tf=512):
    cap, half = h_slots.shape
    d = 2 * half
    f = w2.shape[2]
    nf = f // tf
    assert nf >= 2
    tm = MOE_GROUP * EXPERT_BLOCK

    def jj(g, j, gr):
        return jnp.where(gr[g] > 0, j, nf - 1)

    grid_spec = pltpu.PrefetchScalarGridSpec(
        num_scalar_prefetch=2,
        grid=(cap // tm, nf),
        in_specs=[
            pl.BlockSpec((tm, half), lambda g, j, ge, gr: (g, 0)),
            pl.BlockSpec((1, 1, d, tf), lambda g, j, ge, gr: (layer, ge[g], 0, jj(g, j, gr))),
            pl.BlockSpec((1, 1, d, tf), lambda g, j, ge, gr: (layer, ge[g], 0, nf + jj(g, j, gr))),
            pl.BlockSpec((1, 1, tf, d), lambda g, j, ge, gr: (layer, ge[g], jj(g, j, gr), 0)),
        ],
        out_specs=pl.BlockSpec((tm, half), lambda g, j, ge, gr: (g, 0)),
        scratch_shapes=[pltpu.VMEM((tm, d), F32)],
    )
    return pl.pallas_call(
        functools.partial(_moe_ffn_kernel, sub=EXPERT_BLOCK),
        grid_spec=grid_spec,
        out_shape=jax.ShapeDtypeStruct((cap, half), jnp.uint32),
        compiler_params=_cparams(("parallel", "arbitrary")),
        name="swiglu_experts",
    )(group_expert, group_rows, h_slots, w13, w13, w2)


def _router_kernel(x_ref, mod_ref, wr_ref, br_ref, h_ref, meta_ref, cnt_ref, carry_ref, *, tm, n_exp):
    i = pl.program_id(0)

    @pl.when(i == 0)
    def _():
        carry_ref[...] = jnp.zeros_like(carry_ref)

    h = _modulate(x_ref[...], mod_ref[0], 3)
    h_ref[...] = _pack_bf16_pair(h)
    h_hi, h_mid, _ = _split3(h)
    w = wr_ref[...]
    w_hi = w.astype(BF16)
    w_lo = (w - w_hi.astype(F32)).astype(BF16)
    logits = _dot(h_hi, w_hi) + _dot(h_mid, w_hi) + _dot(h_hi, w_lo) + br_ref[...]
    lane = lax.broadcasted_iota(jnp.int32, (tm, LANES), 1)
    neg = jnp.float32(-jnp.inf)
    logits = jnp.where(lane < n_exp, logits, neg)
    v0 = jnp.max(logits, axis=1, keepdims=True)
    e0 = jnp.min(jnp.where(logits == v0, lane, LANES), axis=1, keepdims=True)
    rest = jnp.where(lane == e0, neg, logits)
    v1 = jnp.max(rest, axis=1, keepdims=True)
    e1 = jnp.min(jnp.where(rest == v1, lane, LANES), axis=1, keepdims=True)
    p = jnp.exp(v1 - v0)
    g0 = 1.0 / (1.0 + p)
    g1 = p / (1.0 + p)
    oh0 = lane == e0
    oh1 = lane == e1
    onehot = jnp.where(jnp.logical_or(oh0, oh1), 1.0, 0.0).astype(BF16)
    r = lax.broadcasted_iota(jnp.int32, (tm, tm), 0)
    c = lax.broadcasted_iota(jnp.int32, (tm, tm), 1)
    strict = jnp.where(r > c, 1.0, 0.0).astype(BF16)
    before = _dot(strict, onehot) + carry_ref[0:1, :]
    rank0 = jnp.sum(jnp.where(oh0, before, 0.0), axis=1, keepdims=True)
    rank1 = jnp.sum(jnp.where(oh1, before, 0.0), axis=1, keepdims=True)
    total = carry_ref[0:1, :] + jnp.sum(onehot.astype(F32), axis=0, keepdims=True)
    carry_ref[...] = jnp.broadcast_to(total, carry_ref.shape)
    cnt_ref[...] = jnp.broadcast_to(total, cnt_ref.shape)
    meta = jnp.where(lane == 0, e0.astype(F32), 0.0)
    meta = jnp.where(lane == 1, e1.astype(F32), meta)
    meta = jnp.where(lane == 2, g0, meta)
    meta = jnp.where(lane == 3, g1, meta)
    meta = jnp.where(lane == 4, rank0, meta)
    meta = jnp.where(lane == 5, rank1, meta)
    meta_ref[...] = meta


def _router(x, mod, w_router, b_router, *, seq, tm=512):
    t, d = x.shape
    n_exp = w_router.shape[1]
    wr = jnp.zeros((d, LANES), F32).at[:, :n_exp].set(w_router)
    br = jnp.zeros((1, LANES), F32).at[0, :n_exp].set(b_router)
    return pl.pallas_call(
        functools.partial(_router_kernel, tm=tm, n_exp=n_exp),
        grid=(t // tm,),
        in_specs=[
            pl.BlockSpec((tm, d), lambda i: (i, 0)),
            pl.BlockSpec((1, 6, d), lambda i: (i * tm // seq, 0, 0)),
            _const_spec((d, LANES)),
            _const_spec((1, LANES)),
        ],
        out_specs=[
            pl.BlockSpec((tm, d // 2), lambda i: (i, 0)),
            pl.BlockSpec((tm, LANES), lambda i: (i, 0)),
            pl.BlockSpec((8, LANES), lambda i: (0, 0)),
        ],
        out_shape=[
            jax.ShapeDtypeStruct((t, d // 2), jnp.uint32),
            jax.ShapeDtypeStruct((t, LANES), F32),
            jax.ShapeDtypeStruct((8, LANES), F32),
        ],
        scratch_shapes=[pltpu.VMEM((8, LANES), F32)],
        compiler_params=_cparams(("arbitrary",)),
        name="moe_router",
    )(x, mod, wr, br)


def _combine_kernel(x_ref, ya_ref, yb_ref, meta_ref, mod_ref, g_ref, b_ref, o_ref, *, alpha):
    meta = meta_ref[...]
    y = meta[:, 2:3] * _unpack_bf16_pair(ya_ref[...]) + meta[:, 3:4] * _unpack_bf16_pair(yb_ref[...])
    o_ref[...] = _res_ln(x_ref[...], y, mod_ref[0], 3, g_ref[...], b_ref[...], alpha)


def _combine(x, ya, yb, meta, mod, ln_g, ln_b, *, seq, alpha, tm=512):
    t, d = x.shape
    row = pl.BlockSpec((tm, d), lambda i: (i, 0))
    packed = pl.BlockSpec((tm, d // 2), lambda i: (i, 0))
    return pl.pallas_call(
        functools.partial(_combine_kernel, alpha=alpha),
        grid=(t // tm,),
        in_specs=[row, packed, packed,
                  pl.BlockSpec((tm, LANES), lambda i: (i, 0)),
                  pl.BlockSpec((1, 6, d), lambda i: (i * tm // seq, 0, 0)),
                  _const_spec((1, d)), _const_spec((1, d))],
        out_specs=row,
        out_shape=jax.ShapeDtypeStruct((t, d), F32),
        compiler_params=_cparams(("parallel",)),
        name="moe_combine",
    )(x, ya, yb, meta, mod, ln_g.reshape(1, d), ln_b.reshape(1, d))


def _sc_mesh():
    return plsc.VectorSubcoreMesh(core_axis_name="c", subcore_axis_name="s", num_cores=SC_CORES,
                                  num_subcores=SC_SUBCORES)


def _sc_worker_share(n):
    workers = SC_CORES * SC_SUBCORES
    per_worker = n // workers
    steps = per_worker // SC_ROWS
    assert steps * SC_ROWS * workers == n
    return per_worker, steps


def _sc_gather_rows(table, idx):
    n = idx.shape[0]
    width = table.shape[1]
    per_worker, steps = _sc_worker_share(n)

    def body(table_hbm, idx_hbm, out_hbm, idx_v, rows_v, sem):
        wid = lax.axis_index("s") * SC_CORES + lax.axis_index("c")

        @pl.loop(0, steps)
        def _(j):
            off = pl.multiple_of(wid * per_worker + j * SC_ROWS, SC_ROWS)
            pltpu.sync_copy(idx_hbm.at[pl.ds(off, SC_ROWS)], idx_v)
            pltpu.async_copy(table_hbm.at[idx_v], rows_v, sem).wait()
            pltpu.sync_copy(rows_v, out_hbm.at[pl.ds(off, SC_ROWS)])

    return pl.kernel(
        body,
        out_type=jax.ShapeDtypeStruct((n, width), table.dtype),
        mesh=_sc_mesh(),
        scratch_types=[
            pltpu.VMEM((SC_ROWS,), jnp.int32),
            pltpu.VMEM((SC_ROWS, width), table.dtype),
            pltpu.SemaphoreType.DMA,
        ],
        name="sc_row_gather",
    )(table, idx)


def _sc_scatter_rows(rows, dest_a, dest_b, n_out):
    n, width = rows.shape
    per_worker, steps = _sc_worker_share(n)

    def body(rows_hbm, da_hbm, db_hbm, out_hbm, ia_v, ib_v, rows_v, sem):
        wid = lax.axis_index("s") * SC_CORES + lax.axis_index("c")

        @pl.loop(0, steps)
        def _(j):
            off = pl.multiple_of(wid * per_worker + j * SC_ROWS, SC_ROWS)
            pltpu.sync_copy(da_hbm.at[pl.ds(off, SC_ROWS)], ia_v)
            pltpu.sync_copy(db_hbm.at[pl.ds(off, SC_ROWS)], ib_v)
            pltpu.sync_copy(rows_hbm.at[pl.ds(off, SC_ROWS)], rows_v)
            pltpu.async_copy(rows_v, out_hbm.at[ia_v], sem).wait()
            pltpu.async_copy(rows_v, out_hbm.at[ib_v], sem).wait()

    return pl.kernel(
        body,
        out_type=jax.ShapeDtypeStruct((n_out, width), rows.dtype),
        mesh=_sc_mesh(),
        scratch_types=[
            pltpu.VMEM((SC_ROWS,), jnp.int32),
            pltpu.VMEM((SC_ROWS,), jnp.int32),
            pltpu.VMEM((SC_ROWS, width), rows.dtype),
            pltpu.SemaphoreType.DMA,
        ],
        name="sc_row_scatter",
    )(rows, dest_a, dest_b)


def _moe_layer(x, mod, w_router, b_router, w13, w2, layer, ln_g, ln_b, *, seq, alpha):
    t, d = x.shape
    n_exp = w_router.shape[1]
    h, meta, cnt = _router(x, mod, w_router, b_router, seq=seq)
    e0 = meta[:, 0].astype(jnp.int32)
    e1 = meta[:, 1].astype(jnp.int32)
    rank0 = meta[:, 4].astype(jnp.int32)
    rank1 = meta[:, 5].astype(jnp.int32)
    counts = cnt[0, :n_exp].astype(jnp.int32)
    group_rows = MOE_GROUP * EXPERT_BLOCK
    padded = (counts + group_rows - 1) // group_rows * group_rows
    pad_end = jnp.cumsum(padded)
    pad_start = pad_end - padded
    dest0 = pad_start[e0] + rank0
    dest1 = pad_start[e1] + rank1
    cap = t * TOP_K + n_exp * group_rows
    group_start = jnp.arange(cap // group_rows, dtype=jnp.int32) * group_rows
    group_expert = jnp.minimum(jnp.sum(pad_end[None, :] <= group_start[:, None], axis=1), n_exp - 1).astype(jnp.int32)
    rows_left = counts[group_expert] - (group_start - pad_start[group_expert])
    rows_in_group = jnp.clip(rows_left, 0, group_rows).astype(jnp.int32)
    h_slots = _sc_scatter_rows(h, dest0, dest1, cap)
    y_slots = _moe_ffn(h_slots, group_expert, rows_in_group, w13, w2, layer)
    ya = _sc_gather_rows(y_slots, dest0)
    yb = _sc_gather_rows(y_slots, dest1)
    return _combine(x, ya, yb, meta, mod, ln_g, ln_b, seq=seq, alpha=alpha)


def _gate_kernel(x_ref, mod_ref, wf_ref, bf_ref, o_ref, carry_ref, *, tm):
    @pl.when(pl.program_id(1) == 0)
    def _():
        carry_ref[...] = jnp.zeros_like(carry_ref)

    h = _modulate(x_ref[...], mod_ref[0], 0)
    h_hi, h_mid, _ = _split3(h)
    wf = wf_ref[...]
    wf_hi = wf.astype(BF16)
    wf_lo = (wf - wf_hi.astype(F32)).astype(BF16)
    f = _dot(h_hi, wf_hi) + _dot(h_mid, wf_hi) + _dot(h_hi, wf_lo) + bf_ref[...]
    lf = jnp.minimum(f, 0.0) - jnp.log(1.0 + jnp.exp(-jnp.abs(f)))
    r = lax.broadcasted_iota(jnp.int32, (tm, tm), 0)
    c = lax.broadcasted_iota(jnp.int32, (tm, tm), 1)
    tri = jnp.where(r >= c, 1.0, 0.0).astype(BF16)
    hi, mid, lo = _split3(lf)
    cs = _dot(tri, hi) + _dot(tri, mid) + _dot(tri, lo) + carry_ref[0:1, :]
    o_ref[...] = cs
    carry_ref[...] = jnp.broadcast_to(cs[tm - 1:tm, :], carry_ref.shape)


def _forget_cumsum(x, mod, w_f, b_f, *, bsz, seq, tm=512):
    t, d = x.shape
    heads = b_f.shape[0]
    nb = seq // tm
    wf = jnp.zeros((d, LANES), F32).at[:, :heads].set(w_f)
    bf = jnp.zeros((1, LANES), F32).at[0, :heads].set(b_f)
    return pl.pallas_call(
        functools.partial(_gate_kernel, tm=tm),
        grid=(bsz, nb),
        in_specs=[
            pl.BlockSpec((tm, d), lambda b, i: (b * nb + i, 0)),
            pl.BlockSpec((1, 6, d), lambda b, i: (b, 0, 0)),
            _const_spec((d, LANES)),
            _const_spec((1, LANES)),
        ],
        out_specs=pl.BlockSpec((tm, LANES), lambda b, i: (b * nb + i, 0)),
        out_shape=jax.ShapeDtypeStruct((t, LANES), F32),
        scratch_shapes=[pltpu.VMEM((8, LANES), F32)],
        compiler_params=_cparams(("parallel", "arbitrary")),
        name="fox_forget_cumsum",
    )(x, mod, wf, bf)


HEAD_PAD = LANES
GATE_PARTS = 3
VAL_PAD = 16
LOG2E = 1.4426950408889634


def _fox_proj_kernel(x_ref, mod_ref, cs_ref, wq_ref, wk_ref, wv_ref, selk_ref, onek_ref, selq_ref, oneq_ref,
                     onev_ref, k_ref, q_ref, v_ref, *, q_scale):
    h = _modulate(x_ref[...], mod_ref[0], 0).astype(BF16)
    cs = cs_ref[...] * LOG2E
    parts = _split3(cs)
    k = _dot(h, wk_ref[...]) + onek_ref[...]
    for n in range(GATE_PARTS):
        k = k + _dot(parts[n], selk_ref[n])
    k_ref[...] = k.astype(BF16)
    nt = (((1,), (1,)), ((), ()))
    parts_t = _split3(cs.T)
    q = lax.dot_general(wq_ref[...], h, nt, preferred_element_type=F32) * (q_scale * LOG2E) + oneq_ref[...]
    for n in range(GATE_PARTS):
        q = q + _dot(selq_ref[n], parts_t[n])
    q_ref[0] = q.astype(BF16)
    v = lax.dot_general(wv_ref[...], h, nt, preferred_element_type=F32) + onev_ref[...]
    v_ref[0] = v.astype(BF16)


def _fox_proj(x, mod, cs, w_in, *, bsz, seq, heads, tm=512):
    t, d = x.shape
    hd = d // heads
    nb = seq // tm
    wide = heads * HEAD_PAD
    pad = lambda w: jnp.pad(w.reshape(d, heads, hd), ((0, 0), (0, 0), (0, HEAD_PAD - hd))).reshape(d, wide)
    wq_t = pad(w_in[:, :d]).T.astype(BF16)
    wk = pad(w_in[:, d:2 * d]).astype(BF16)
    vrows = heads * (hd + VAL_PAD)
    wv_t = jnp.pad(w_in[:, 2 * d:3 * d].reshape(d, heads, hd),
                   ((0, 0), (0, 0), (0, VAL_PAD))).reshape(d, vrows).T.astype(BF16)
    sel_k = np.zeros((GATE_PARTS, LANES, wide), np.float32)
    sel_q = np.zeros((GATE_PARTS, wide, LANES), np.float32)
    one_k = np.zeros((1, wide), np.float32)
    one_q = np.zeros((wide, 1), np.float32)
    one_v = np.zeros((vrows, 1), np.float32)
    for hh in range(heads):
        base = hh * HEAD_PAD + hd
        one_v[hh * (hd + VAL_PAD) + hd, 0] = 1.0
        for n in range(GATE_PARTS):
            sel_k[n, hh, base + n] = -1.0
            one_q[base + n, 0] = 1.0
            one_k[0, base + GATE_PARTS + n] = 1.0
            sel_q[n, base + GATE_PARTS + n, hh] = 1.0
    return pl.pallas_call(
        functools.partial(_fox_proj_kernel, q_scale=hd ** -0.5),
        grid=(t // tm,),
        in_specs=[
            pl.BlockSpec((tm, d), lambda i: (i, 0)),
            pl.BlockSpec((1, 6, d), lambda i: (i * tm // seq, 0, 0)),
            pl.BlockSpec((tm, LANES), lambda i: (i, 0)),
            _const_spec((wide, d)),
            _const_spec((d, wide)),
            _const_spec((vrows, d)),
            _const_spec((GATE_PARTS, LANES, wide)),
            _const_spec((1, wide)),
            _const_spec((GATE_PARTS, wide, LANES)),
            _const_spec((wide, 1)),
            _const_spec((vrows, 1)),
        ],
        out_specs=[
            pl.BlockSpec((tm, wide), lambda i: (i, 0)),
            pl.BlockSpec((1, wide, tm), lambda i: (i // nb, 0, i % nb)),
            pl.BlockSpec((1, vrows, tm), lambda i: (i // nb, 0, i % nb)),
        ],
        out_shape=[
            jax.ShapeDtypeStruct((t, wide), BF16),
            jax.ShapeDtypeStruct((bsz, wide, seq), BF16),
            jax.ShapeDtypeStruct((bsz, vrows, seq), BF16),
        ],
        compiler_params=_cparams(("parallel",)),
        name="fox_proj",
    )(x, mod, cs, wq_t, wk, wv_t, jnp.asarray(sel_k, BF16), jnp.asarray(one_k), jnp.asarray(sel_q, BF16),
      jnp.asarray(one_q), jnp.asarray(one_v))


def _flash_kernel(qi_ref, kj_ref, k_ref, q_ref, v_ref, o_ref, m_ref, acc_ref, s_ref, *, heads, hd, blk):
    p_idx = pl.program_id(1)
    qi = qi_ref[p_idx]
    kj = kj_ref[p_idx]
    vp = hd + VAL_PAD

    @pl.when(kj == 0)
    def _():
        m_ref[...] = jnp.full(m_ref.shape, -jnp.inf, F32)
        acc_ref[...] = jnp.zeros_like(acc_ref)

    def step(masked):
        if masked:
            kpos = lax.broadcasted_iota(jnp.int32, (blk, blk), 0)
            qpos = lax.broadcasted_iota(jnp.int32, (blk, blk), 1)
            keep = kpos <= qpos

        def scores(h):
            hs = slice(h * HEAD_PAD, (h + 1) * HEAD_PAD)
            s = _dot(k_ref[:, hs], q_ref[0, hs, :])
            if masked:
                s = jnp.where(keep, s, -jnp.inf)
            s_ref[h % depth] = s
            return jnp.max(s, axis=0, keepdims=True)

        depth = s_ref.shape[0]
        ahead = [scores(h) for h in range(depth - 1)]
        for h in range(heads):
            if h + depth - 1 < heads:
                ahead.append(scores(h + depth - 1))
            m_cur = ahead.pop(0)
            m_prev = m_ref[h:h + 1, :]
            m_new = jnp.maximum(m_prev, m_cur)
            m_ref[h:h + 1, :] = m_new
            a = jnp.exp2(m_prev - m_new)
            p = jnp.exp2(s_ref[h % depth] - m_new).astype(BF16)
            rs = slice(h * vp, (h + 1) * vp)
            acc_ref[rs, :] = acc_ref[rs, :] * a + _dot(v_ref[0, rs, :], p)

    @pl.when(kj < qi)
    def _():
        step(False)

    @pl.when(kj == qi)
    def _():
        step(True)
        per = LANES // hd
        for g in range(heads // per):
            parts = []
            for n in range(per):
                r0 = (g * per + n) * vp
                parts.append(acc_ref[r0:r0 + hd, :] / acc_ref[r0 + hd:r0 + hd + 1, :])
            o_ref[:, g * LANES:(g + 1) * LANES] = jnp.concatenate(parts, axis=0).T.astype(BF16)


def _flash(k_aug, q_aug_t, v_t, *, bsz, seq, heads, blk=512):
    t, wide = k_aug.shape
    vrows = v_t.shape[1]
    hd = vrows // heads - VAL_PAD
    d = heads * hd
    nq = seq // blk
    pairs = [(i, j) for i in range(nq) for j in range(i + 1)]
    qi = jnp.asarray(np.array([p[0] for p in pairs], np.int32))
    kj = jnp.asarray(np.array([p[1] for p in pairs], np.int32))
    grid_spec = pltpu.PrefetchScalarGridSpec(
        num_scalar_prefetch=2,
        grid=(bsz, len(pairs)),
        in_specs=[
            pl.BlockSpec((blk, wide), lambda b, p, qi, kj: (b * nq + kj[p], 0)),
            pl.BlockSpec((1, wide, blk), lambda b, p, qi, kj: (b, 0, qi[p])),
            pl.BlockSpec((1, vrows, blk), lambda b, p, qi, kj: (b, 0, kj[p])),
        ],
        out_specs=pl.BlockSpec((blk, d), lambda b, p, qi, kj: (b * nq + qi[p], 0)),
        scratch_shapes=[
            pltpu.VMEM((heads, blk), F32),
            pltpu.VMEM((vrows, blk), F32),
            pltpu.VMEM((3, blk, blk), F32),
        ],
    )
    return pl.pallas_call(
        functools.partial(_flash_kernel, heads=heads, hd=hd, blk=blk),
        grid_spec=grid_spec,
        out_shape=jax.ShapeDtypeStruct((t, d), BF16),
        compiler_params=_cparams(("parallel", "arbitrary")),
        name="fox_attention",
    )(qi, kj, k_aug, q_aug_t, v_t)


def _mm_resln_kernel(a_ref, x_ref, mod_ref, w_ref, g_ref, b_ref, o_ref, *, alpha):
    y = _dot(a_ref[...], w_ref[...])
    o_ref[...] = _res_ln(x_ref[...], y, mod_ref[0], 0, g_ref[...], b_ref[...], alpha)


def _mm_resln(a, x, mod, w, ln_g, ln_b, *, seq, alpha, tm=512):
    t, d = x.shape
    k = a.shape[1]
    return pl.pallas_call(
        functools.partial(_mm_resln_kernel, alpha=alpha),
        grid=(t // tm,),
        in_specs=[
            pl.BlockSpec((tm, k), lambda i: (i, 0)),
            pl.BlockSpec((tm, d), lambda i: (i, 0)),
            pl.BlockSpec((1, 6, d), lambda i: (i * tm // seq, 0, 0)),
            _const_spec((k, d)),
            _const_spec((1, d)),
            _const_spec((1, d)),
        ],
        out_specs=pl.BlockSpec((tm, d), lambda i: (i, 0)),
        out_shape=jax.ShapeDtypeStruct((t, d), F32),
        compiler_params=_cparams(("parallel",)),
        name="proj_res_ln",
    )(a, x, mod, w.astype(BF16), ln_g.reshape(1, d), ln_b.reshape(1, d))


def _fox_layer(x, mod, w_in, b_f, w_out, ln_g, ln_b, *, bsz, seq, alpha):
    heads = b_f.shape[0]
    d = x.shape[1]
    cs = _forget_cumsum(x, mod, w_in[:, 3 * d:], b_f, bsz=bsz, seq=seq)
    k_aug, q_aug_t, v_t = _fox_proj(x, mod, cs, w_in, bsz=bsz, seq=seq, heads=heads)
    attn = _flash(k_aug, q_aug_t, v_t, bsz=bsz, seq=seq, heads=heads)
    return _mm_resln(attn, x, mod, w_out, ln_g, ln_b, seq=seq, alpha=alpha)


def _pool_kernel(x_ref, xh_ref, mod_ref, win_ref, wg_ref, ls_ref, wout_ref, g_ref, b_ref, o_ref, pooled_ref,
                 *, tm, seq, alpha):
    i = pl.program_id(0)
    pos0 = (i * tm) % seq
    m = mod_ref[0]
    x = x_ref[...]
    halo_ok = jnp.where(pos0 > 0, jnp.float32(1.0), jnp.float32(0.0))
    xe = jnp.concatenate([xh_ref[...], x], axis=0)
    ze = _dot(_modulate(xe, m, 0).astype(BF16), win_ref[...])
    rows = lax.broadcasted_iota(jnp.int32, (tm + POOL_HALO, 1), 0)
    ze = ze * jnp.where(rows < POOL_HALO, halo_ok, jnp.float32(1.0))
    pos = (pos0 + lax.broadcasted_iota(jnp.int32, (tm, 1), 0) + 1).astype(F32)
    gd = ze.shape[1] // len(POOL_WINDOWS)
    for g, win in enumerate(POOL_WINDOWS):
        cs = slice(g * gd, (g + 1) * gd)
        zg = ze[:, cs]
        s = zg
        span = 1
        while span < win:
            s = s + pltpu.roll(s, span, 0)
            span *= 2
        count = jnp.minimum(pos, jnp.float32(win))
        pooled = s[POOL_HALO:, :] / count - zg[POOL_HALO:, :]
        pooled_ref[:, cs] = (_dot(pooled.astype(BF16), wg_ref[g]) * ls_ref[:, cs]).astype(BF16)
    y = _dot(pooled_ref[...], wout_ref[...])
    o_ref[...] = _res_ln(x, y, m, 0, g_ref[...], b_ref[...], alpha)


def _pool_layer(x, mod, w_in, w_grp, scale, w_out, ln_g, ln_b, *, seq, alpha, tm=512):
    t, d = x.shape
    ng, gd, _ = w_grp.shape
    hb = tm // POOL_HALO
    return pl.pallas_call(
        functools.partial(_pool_kernel, tm=tm, seq=seq, alpha=alpha),
        grid=(t // tm,),
        in_specs=[
            pl.BlockSpec((tm, d), lambda i: (i, 0)),
            pl.BlockSpec((POOL_HALO, d), lambda i: (jnp.maximum(i * hb - 1, 0), 0)),
            pl.BlockSpec((1, 6, d), lambda i: (i * tm // seq, 0, 0)),
            _const_spec((d, d)),
            _const_spec((ng, gd, gd)),
            _const_spec((1, d)),
            _const_spec((d, d)),
            _const_spec((1, d)),
            _const_spec((1, d)),
        ],
        out_specs=pl.BlockSpec((tm, d), lambda i: (i, 0)),
        out_shape=jax.ShapeDtypeStruct((t, d), F32),
        scratch_shapes=[pltpu.VMEM((tm, d), BF16)],
        compiler_params=_cparams(("parallel",)),
        name="pool_mixer",
    )(x, x, mod, w_in.astype(BF16), w_grp.astype(BF16), scale.reshape(1, d), w_out.astype(BF16),
      ln_g.reshape(1, d), ln_b.reshape(1, d))


def kernel(x, c, mod_w, mod_b, ln_g, ln_b, gm_w_in, gm_ln_g, gm_ln_b, gm_w_s, gm_b_s, gm_w_out, fox_w_in,
           fox_b_f, fox_w_out, pool_w_in, pool_w_grp, pool_scale, pool_w_out, ffn_w13, ffn_w2, moe_w_router,
           moe_b_router, moe_w13, moe_w2):
    bsz, seq, d = x.shape
    depth = mod_w.shape[0]
    alpha = (2 * depth) ** 0.25
    n_mixers = 3
    mod_all = _modulation(c, mod_w, mod_b)
    xt = x.reshape(bsz * seq, d)
    for i in range(depth):
        mod = mod_all[i]
        kind, j = i % n_mixers, i // n_mixers
        if kind == 0:
            xt = _gmlp_layer(xt, mod, gm_w_in[j], gm_ln_g[j], gm_ln_b[j], gm_w_s[j], gm_b_s[j], gm_w_out[j],
                             ln_g[i, 0], ln_b[i, 0], seq=seq, alpha=alpha)
        elif kind == 1:
            xt = _fox_layer(xt, mod, fox_w_in[j], fox_b_f[j], fox_w_out[j], ln_g[i, 0], ln_b[i, 0],
                            bsz=bsz, seq=seq, alpha=alpha)
        else:
            xt = _pool_layer(xt, mod, pool_w_in[j], pool_w_grp[j], pool_scale[j], pool_w_out[j],
                             ln_g[i, 0], ln_b[i, 0], seq=seq, alpha=alpha)
        if i % 2 == 0:
            xt = _ffn_layer(xt, mod, ffn_w13[i // 2], ffn_w2[i // 2], ln_g[i, 1], ln_b[i, 1],
                            seq=seq, alpha=alpha)
        else:
            xt = _moe_layer(xt, mod, moe_w_router[i // 2], moe_b_router[i // 2], moe_w13, moe_w2, i // 2,
                            ln_g[i, 1], ln_b[i, 1], seq=seq, alpha=alpha)
    return xt.reshape(bsz, seq, d)
```

```python
import functools

import numpy as np
import jax
import jax.numpy as jnp
from jax import lax
from jax.experimental import pallas as pl
from jax.experimental.pallas import tpu as pltpu
from jax.experimental.pallas import tpu_sc as plsc

F32 = jnp.float32
BF16 = jnp.bfloat16

POOL_WINDOWS = (2, 4, 8, 16)
TOP_K = 2
EXPERT_BLOCK = 512
LN_EPS = 1e-5
LANES = 128
POOL_HALO = 16
VMEM_LIMIT = 56 * 1024 * 1024
SC_CORES = 2
SC_SUBCORES = 16
SC_ROWS = 128


def _cparams(sem):
    return pltpu.CompilerParams(dimension_semantics=sem, vmem_limit_bytes=VMEM_LIMIT)


def _const_spec(shape):
    nd = len(shape)
    return pl.BlockSpec(shape, lambda *_: (0,) * nd, pipeline_mode=pl.Buffered(1))


def _layer_norm(r, g, b):
    mu = jnp.mean(r, axis=-1, keepdims=True)
    xc = r - mu
    var = jnp.mean(xc * xc, axis=-1, keepdims=True)
    return xc * lax.rsqrt(var + LN_EPS) * g + b


def _modulate(x, m, off):
    return x * (1.0 + m[off + 1:off + 2]) + m[off:off + 1]


def _res_ln(x, y, m, off, g, b, alpha):
    return _layer_norm(alpha * x + (1.0 + m[off + 2:off + 3]) * y, g, b)


def _split3(a):
    hi = a.astype(BF16)
    r1 = a - hi.astype(F32)
    mid = r1.astype(BF16)
    lo = (r1 - mid.astype(F32)).astype(BF16)
    return hi, mid, lo


def _dot(a, b):
    return jnp.dot(a, b, preferred_element_type=F32)


def _pack_bf16_pair(x):
    n = x.shape[1] // 2
    hi = lax.bitcast_convert_type(x[:, :n].astype(BF16).astype(F32), jnp.uint32)
    lo = lax.bitcast_convert_type(x[:, n:].astype(BF16).astype(F32), jnp.uint32)
    return hi | (lo >> 16)


def _unpack_bf16_pair(w):
    hi = lax.bitcast_convert_type(w & jnp.uint32(0xFFFF0000), F32)
    lo = lax.bitcast_convert_type(w << 16, F32)
    return jnp.concatenate([hi, lo], axis=1)


def _mod_kernel(c_ref, w_ref, b_ref, o_ref):
    c = c_ref[...]
    s = c / (1.0 + jnp.exp(-c))
    s_hi, s_mid, _ = _split3(s)
    w = w_ref[0]
    w_hi = w.astype(BF16)
    w_lo = (w - w_hi.astype(F32)).astype(BF16)
    acc = _dot(s_hi, w_hi) + _dot(s_mid, w_hi) + _dot(s_hi, w_lo)
    o_ref[0] = acc + b_ref[0]


def _modulation(c, mod_w, mod_b):
    depth, d, n = mod_w.shape
    bsz = c.shape[0]
    rows = 8
    tn = n // 4
    c_pad = jnp.zeros((rows, d), F32).at[:bsz].set(c)
    out = pl.pallas_call(
        _mod_kernel,
        grid=(depth, n // tn),
        in_specs=[
            pl.BlockSpec((rows, d), lambda l, j: (0, 0)),
            pl.BlockSpec((1, d, tn), lambda l, j: (l, 0, j)),
            pl.BlockSpec((1, 1, tn), lambda l, j: (l, 0, j)),
        ],
        out_specs=pl.BlockSpec((1, rows, tn), lambda l, j: (l, 0, j)),
        out_shape=jax.ShapeDtypeStruct((depth, rows, n), F32),
        compiler_params=_cparams(("parallel", "parallel")),
        name="adaln_mod",
    )(c_pad, mod_w, mod_b.reshape(depth, 1, n))
    return out[:, :bsz, :].reshape(depth, bsz, 6, d)


def _gmlp_kernel(x_ref, mod_ref, win_ref, vg_ref, vb_ref, ws_ref, bst_ref, wout_ref, g_ref, b_ref,
                 o_ref, gated_ref, *, tm, sub, chunk, groups, alpha):
    m = mod_ref[0]
    row = lax.broadcasted_iota(jnp.int32, (chunk, chunk), 0)
    col = lax.broadcasted_iota(jnp.int32, (chunk, chunk), 1)
    causal = row >= col
    bst = bst_ref[...]
    w_mix = [jnp.where(causal, ws_ref[g], 0.0).astype(BF16) for g in range(groups)]
    n = tm // sub

    def project(s):
        x = x_ref[s * sub:(s + 1) * sub, :]
        return x, _dot(_modulate(x, m, 0).astype(BF16), win_ref[...])

    def activate(z):
        z = 0.5 * z * (1.0 + lax.erf(z * (2.0 ** -0.5)))
        width = z.shape[1] // 2
        return z[:, :width], _layer_norm(z[:, width:], vg_ref[...], vb_ref[...]).astype(BF16)

    def mix(s, u, v):
        gd = u.shape[1] // groups
        for g in range(groups):
            cs = slice(g * gd, (g + 1) * gd)
            for c in range(sub // chunk):
                rs = slice(c * chunk, (c + 1) * chunk)
                mixed = _dot(w_mix[g], v[rs, cs]) + bst[:, g:g + 1]
                gated_ref[s * sub + c * chunk:s * sub + (c + 1) * chunk, cs] = (u[rs, cs] * mixed).astype(BF16)
        return _dot(gated_ref[s * sub:(s + 1) * sub, :], wout_ref[...])

    def finish(s, x, y):
        o_ref[s * sub:(s + 1) * sub, :] = _res_ln(x, y, m, 0, g_ref[...], b_ref[...], alpha)

    nxt = project(0)
    pending = None
    for s in range(n):
        x, z = nxt
        if s + 1 < n:
            nxt = project(s + 1)
        u, v = activate(z)
        if pending is not None:
            finish(*pending)
        pending = (s, x, mix(s, u, v))
    finish(*pending)


def _gmlp_layer(x, mod, w_in, v_g, v_b, w_s, b_s, w_out, ln_g, ln_b, *, seq, alpha, tm=512, sub=256):
    t, d = x.shape
    groups, chunk, _ = w_s.shape
    width = w_out.shape[0]
    kern = functools.partial(_gmlp_kernel, tm=tm, sub=sub, chunk=chunk, groups=groups, alpha=alpha)
    return pl.pallas_call(
        kern,
        grid=(t // tm,),
        in_specs=[
            pl.BlockSpec((tm, d), lambda i: (i, 0)),
            pl.BlockSpec((1, 6, d), lambda i: (i * tm // seq, 0, 0)),
            _const_spec((d, 2 * width)),
            _const_spec((1, width)),
            _const_spec((1, width)),
            _const_spec((groups, chunk, chunk)),
            _const_spec((chunk, groups)),
            _const_spec((width, d)),
            _const_spec((1, d)),
            _const_spec((1, d)),
        ],
        out_specs=pl.BlockSpec((tm, d), lambda i: (i, 0)),
        out_shape=jax.ShapeDtypeStruct((t, d), F32),
        scratch_shapes=[pltpu.VMEM((tm, width), BF16)],
        compiler_params=_cparams(("parallel",)),
        name="gmlp_mixer",
    )(x, mod, w_in.astype(BF16), v_g.reshape(1, width), v_b.reshape(1, width), w_s, b_s.T,
      w_out.astype(BF16), ln_g.reshape(1, d), ln_b.reshape(1, d))


def _swiglu_pipelined(hs, w1, w3, w2, emit):
    up = lambda h: (_dot(h, w1), _dot(h, w3))
    nxt = up(hs[0])
    for s in range(len(hs)):
        a, b = nxt
        if s + 1 < len(hs):
            nxt = up(hs[s + 1])
        t = (a / (1.0 + jnp.exp(-a)) * b).astype(BF16)
        emit(s, _dot(t, w2))


def _ffn_kernel(x_ref, mod_ref, w1_ref, w3_ref, w2_ref, g_ref, b_ref, o_ref, h_ref, *, alpha, sub):
    j = pl.program_id(1)
    last = pl.num_programs(1) - 1
    subs = [slice(s, s + sub) for s in range(0, x_ref.shape[0], sub)]
    weights = lambda: (w1_ref[...], w3_ref[...], w2_ref[...])

    @pl.when(j == 0)
    def _():
        hs = []
        for rows in subs:
            h = _modulate(x_ref[rows, :], mod_ref[0], 3).astype(BF16)
            h_ref[rows, :] = h
            hs.append(h)

        def emit(s, part):
            o_ref[subs[s], :] = part

        _swiglu_pipelined(hs, *weights(), emit)

    @pl.when(jnp.logical_and(j > 0, j < last))
    def _():
        def emit(s, part):
            o_ref[subs[s], :] += part

        _swiglu_pipelined([h_ref[rows, :] for rows in subs], *weights(), emit)

    @pl.when(j == last)
    def _():
        def emit(s, part):
            rows = subs[s]
            y = o_ref[rows, :] + part
            o_ref[rows, :] = _res_ln(x_ref[rows, :], y, mod_ref[0], 3, g_ref[...], b_ref[...], alpha)

        _swiglu_pipelined([h_ref[rows, :] for rows in subs], *weights(), emit)


def _ffn_layer(x, mod, w13, w2, ln_g, ln_b, *, seq, alpha, tm=1024, tf=512):
    t, d = x.shape
    f = w2.shape[0]
    nf = f // tf
    assert nf >= 2
    w13b = w13.astype(BF16)
    return pl.pallas_call(
        functools.partial(_ffn_kernel, alpha=alpha, sub=tm // 2),
        grid=(t // tm, nf),
        in_specs=[
            pl.BlockSpec((tm, d), lambda i, j: (i, 0)),
            pl.BlockSpec((1, 6, d), lambda i, j: (i * tm // seq, 0, 0)),
            pl.BlockSpec((d, tf), lambda i, j: (0, j)),
            pl.BlockSpec((d, tf), lambda i, j: (0, nf + j)),
            pl.BlockSpec((tf, d), lambda i, j: (j, 0)),
            _const_spec((1, d)),
            _const_spec((1, d)),
        ],
        out_specs=pl.BlockSpec((tm, d), lambda i, j: (i, 0)),
        out_shape=jax.ShapeDtypeStruct((t, d), F32),
        scratch_shapes=[pltpu.VMEM((tm, d), BF16)],
        compiler_params=_cparams(("parallel", "arbitrary")),
        name="swiglu_dense",
    )(x, mod, w13b, w13b, w2.astype(BF16), ln_g.reshape(1, d), ln_b.reshape(1, d))


MOE_GROUP = 4


def _moe_ffn_kernel(ge_ref, gr_ref, h_ref, w1_ref, w3_ref, w2_ref, o_ref, acc_ref, *, sub):
    g = pl.program_id(0)
    j = pl.program_id(1)
    last = pl.num_programs(1) - 1
    n_rows = gr_ref[g]
    nv = (n_rows + EXPERT_BLOCK - 1) // EXPERT_BLOCK

    def run(n_blocks, phase):
        subs = [slice(s, s + sub) for s in range(0, n_blocks * EXPERT_BLOCK, sub)]
        row = lax.broadcasted_iota(jnp.int32, (sub, 1), 0)

        def rows_in(rows):
            w = jnp.where(row + rows.start < n_rows, h_ref[rows, :], jnp.uint32(0))
            return _unpack_bf16_pair(w).astype(BF16)

        def emit(s, part):
            if phase == "first":
                acc_ref[subs[s], :] = part
            elif phase == "middle":
                acc_ref[subs[s], :] += part
            else:
                o_ref[subs[s], :] = _pack_bf16_pair(acc_ref[subs[s], :] + part)

        _swiglu_pipelined([rows_in(rows) for rows in subs], w1_ref[0, 0].astype(BF16), w3_ref[0, 0].astype(BF16),
                          w2_ref[0, 0].astype(BF16), emit)

    for n_blocks in range(1, MOE_GROUP + 1):
        @pl.when(jnp.logical_and(nv == n_blocks, j == 0))
        def _():
            run(n_blocks, "first")

        @pl.when(jnp.logical_and(nv == n_blocks, jnp.logical_and(j > 0, j < last)))
        def _():
            run(n_blocks, "middle")

        @pl.when(jnp.logical_and(nv == n_blocks, j == last))
        def _():
            run(n_blocks, "last")

    for blk in range(MOE_GROUP):
        @pl.when(jnp.logical_and(blk >= nv, j == last))
        def _():
            o_ref[blk * EXPERT_BLOCK:(blk + 1) * EXPERT_BLOCK, :] = jnp.zeros((EXPERT_BLOCK, o_ref.shape[1]),
                                                                              jnp.uint32)


def _moe_ffn(h_slots, group_expert, group_rows, w13, w2, layer, *, tf=512):
    cap, half = h_slots.shape
    d = 2 * half
    f = w2.shape[2]
    nf = f // tf
    assert nf >= 2
    tm = MOE_GROUP * EXPERT_BLOCK

    def jj(g, j, gr):
        return jnp.where(gr[g] > 0, j, nf - 1)

    grid_spec = pltpu.PrefetchScalarGridSpec(
        num_scalar_prefetch=2,
        grid=(cap // tm, nf),
        in_specs=[
            pl.BlockSpec((tm, half), lambda g, j, ge, gr: (g, 0)),
            pl.BlockSpec((1, 1, d, tf), lambda g, j, ge, gr: (layer, ge[g], 0, jj(g, j, gr))),
            pl.BlockSpec((1, 1, d, tf), lambda g, j, ge, gr: (layer, ge[g], 0, nf + jj(g, j, gr))),
            pl.BlockSpec((1, 1, tf, d), lambda g, j, ge, gr: (layer, ge[g], jj(g, j, gr), 0)),
        ],
        out_specs=pl.BlockSpec((tm, half), lambda g, j, ge, gr: (g, 0)),
        scratch_shapes=[pltpu.VMEM((tm, d), F32)],
    )
    return pl.pallas_call(
        functools.partial(_moe_ffn_kernel, sub=EXPERT_BLOCK),
        grid_spec=grid_spec,
        out_shape=jax.ShapeDtypeStruct((cap, half), jnp.uint32),
        compiler_params=_cparams(("parallel", "arbitrary")),
        name="swiglu_experts",
    )(group_expert, group_rows, h_slots, w13, w13, w2)


def _router_kernel(x_ref, mod_ref, wr_ref, br_ref, h_ref, meta_ref, cnt_ref, carry_ref, *, tm, n_exp):
    i = pl.program_id(0)

    @pl.when(i == 0)
    def _():
        carry_ref[...] = jnp.zeros_like(carry_ref)

    h = _modulate(x_ref[...], mod_ref[0], 3)
    h_ref[...] = _pack_bf16_pair(h)
    h_hi, h_mid, _ = _split3(h)
    w = wr_ref[...]
    w_hi = w.astype(BF16)
    w_lo = (w - w_hi.astype(F32)).astype(BF16)
    logits = _dot(h_hi, w_hi) + _dot(h_mid, w_hi) + _dot(h_hi, w_lo) + br_ref[...]
    lane = lax.broadcasted_iota(jnp.int32, (tm, LANES), 1)
    neg = jnp.float32(-jnp.inf)
    logits = jnp.where(lane < n_exp, logits, neg)
    v0 = jnp.max(logits, axis=1, keepdims=True)
    e0 = jnp.min(jnp.where(logits == v0, lane, LANES), axis=1, keepdims=True)
    rest = jnp.where(lane == e0, neg, logits)
    v1 = jnp.max(rest, axis=1, keepdims=True)
    e1 = jnp.min(jnp.where(rest == v1, lane, LANES), axis=1, keepdims=True)
    p = jnp.exp(v1 - v0)
    g0 = 1.0 / (1.0 + p)
    g1 = p / (1.0 + p)
    oh0 = lane == e0
    oh1 = lane == e1
    onehot = jnp.where(jnp.logical_or(oh0, oh1), 1.0, 0.0).astype(BF16)
    r = lax.broadcasted_iota(jnp.int32, (tm, tm), 0)
    c = lax.broadcasted_iota(jnp.int32, (tm, tm), 1)
    strict = jnp.where(r > c, 1.0, 0.0).astype(BF16)
    before = _dot(strict, onehot) + carry_ref[0:1, :]
    rank0 = jnp.sum(jnp.where(oh0, before, 0.0), axis=1, keepdims=True)
    rank1 = jnp.sum(jnp.where(oh1, before, 0.0), axis=1, keepdims=True)
    total = carry_ref[0:1, :] + jnp.sum(onehot.astype(F32), axis=0, keepdims=True)
    carry_ref[...] = jnp.broadcast_to(total, carry_ref.shape)
    cnt_ref[...] = jnp.broadcast_to(total, cnt_ref.shape)
    meta = jnp.where(lane == 0, e0.astype(F32), 0.0)
    meta = jnp.where(lane == 1, e1.astype(F32), meta)
    meta = jnp.where(lane == 2, g0, meta)
    meta = jnp.where(lane == 3, g1, meta)
    meta = jnp.where(lane == 4, rank0, meta)
    meta = jnp.where(lane == 5, rank1, meta)
    meta_ref[...] = meta


def _router(x, mod, w_router, b_router, *, seq, tm=512):
    t, d = x.shape
    n_exp = w_router.shape[1]
    wr = jnp.zeros((d, LANES), F32).at[:, :n_exp].set(w_router)
    br = jnp.zeros((1, LANES), F32).at[0, :n_exp].set(b_router)
    return pl.pallas_call(
        functools.partial(_router_kernel, tm=tm, n_exp=n_exp),
        grid=(t // tm,),
        in_specs=[
            pl.BlockSpec((tm, d), lambda i: (i, 0)),
            pl.BlockSpec((1, 6, d), lambda i: (i * tm // seq, 0, 0)),
            _const_spec((d, LANES)),
            _const_spec((1, LANES)),
        ],
        out_specs=[
            pl.BlockSpec((tm, d // 2), lambda i: (i, 0)),
            pl.BlockSpec((tm, LANES), lambda i: (i, 0)),
            pl.BlockSpec((8, LANES), lambda i: (0, 0)),
        ],
        out_shape=[
            jax.ShapeDtypeStruct((t, d // 2), jnp.uint32),
            jax.ShapeDtypeStruct((t, LANES), F32),
            jax.ShapeDtypeStruct((8, LANES), F32),
        ],
        scratch_shapes=[pltpu.VMEM((8, LANES), F32)],
        compiler_params=_cparams(("arbitrary",)),
        name="moe_router",
    )(x, mod, wr, br)


def _combine_kernel(x_ref, ya_ref, yb_ref, meta_ref, mod_ref, g_ref, b_ref, o_ref, *, alpha):
    meta = meta_ref[...]
    y = meta[:, 2:3] * _unpack_bf16_pair(ya_ref[...]) + meta[:, 3:4] * _unpack_bf16_pair(yb_ref[...])
    o_ref[...] = _res_ln(x_ref[...], y, mod_ref[0], 3, g_ref[...], b_ref[...], alpha)


def _combine(x, ya, yb, meta, mod, ln_g, ln_b, *, seq, alpha, tm=512):
    t, d = x.shape
    row = pl.BlockSpec((tm, d), lambda i: (i, 0))
    packed = pl.BlockSpec((tm, d // 2), lambda i: (i, 0))
    return pl.pallas_call(
        functools.partial(_combine_kernel, alpha=alpha),
        grid=(t // tm,),
        in_specs=[row, packed, packed,
                  pl.BlockSpec((tm, LANES), lambda i: (i, 0)),
                  pl.BlockSpec((1, 6, d), lambda i: (i * tm // seq, 0, 0)),
                  _const_spec((1, d)), _const_spec((1, d))],
        out_specs=row,
        out_shape=jax.ShapeDtypeStruct((t, d), F32),
        compiler_params=_cparams(("parallel",)),
        name="moe_combine",
    )(x, ya, yb, meta, mod, ln_g.reshape(1, d), ln_b.reshape(1, d))


def _sc_mesh():
    return plsc.VectorSubcoreMesh(core_axis_name="c", subcore_axis_name="s", num_cores=SC_CORES,
                                  num_subcores=SC_SUBCORES)


def _sc_worker_share(n):
    workers = SC_CORES * SC_SUBCORES
    per_worker = n // workers
    steps = per_worker // SC_ROWS
    assert steps * SC_ROWS * workers == n
    return per_worker, steps


def _sc_gather_rows(table, idx):
    n = idx.shape[0]
    width = table.shape[1]
    per_worker, steps = _sc_worker_share(n)

    def body(table_hbm, idx_hbm, out_hbm, idx_v, rows_v, sem):
        wid = lax.axis_index("s") * SC_CORES + lax.axis_index("c")

        @pl.loop(0, steps)
        def _(j):
            off = pl.multiple_of(wid * per_worker + j * SC_ROWS, SC_ROWS)
            pltpu.sync_copy(idx_hbm.at[pl.ds(off, SC_ROWS)], idx_v)
            pltpu.async_copy(table_hbm.at[idx_v], rows_v, sem).wait()
            pltpu.sync_copy(rows_v, out_hbm.at[pl.ds(off, SC_ROWS)])

    return pl.kernel(
        body,
        out_type=jax.ShapeDtypeStruct((n, width), table.dtype),
        mesh=_sc_mesh(),
        scratch_types=[
            pltpu.VMEM((SC_ROWS,), jnp.int32),
            pltpu.VMEM((SC_ROWS, width), table.dtype),
            pltpu.SemaphoreType.DMA,
        ],
        name="sc_row_gather",
    )(table, idx)


def _sc_scatter_rows(rows, dest_a, dest_b, n_out):
    n, width = rows.shape
    per_worker, steps = _sc_worker_share(n)

    def body(rows_hbm, da_hbm, db_hbm, out_hbm, ia_v, ib_v, rows_v, sem):
        wid = lax.axis_index("s") * SC_CORES + lax.axis_index("c")

        @pl.loop(0, steps)
        def _(j):
            off = pl.multiple_of(wid * per_worker + j * SC_ROWS, SC_ROWS)
            pltpu.sync_copy(da_hbm.at[pl.ds(off, SC_ROWS)], ia_v)
            pltpu.sync_copy(db_hbm.at[pl.ds(off, SC_ROWS)], ib_v)
            pltpu.sync_copy(rows_hbm.at[pl.ds(off, SC_ROWS)], rows_v)
            pltpu.async_copy(rows_v, out_hbm.at[ia_v], sem).wait()
            pltpu.async_copy(rows_v, out_hbm.at[ib_v], sem).wait()

    return pl.kernel(
        body,
        out_type=jax.ShapeDtypeStruct((n_out, width), rows.dtype),
        mesh=_sc_mesh(),
        scratch_types=[
            pltpu.VMEM((SC_ROWS,), jnp.int32),
            pltpu.VMEM((SC_ROWS,), jnp.int32),
            pltpu.VMEM((SC_ROWS, width), rows.dtype),
            pltpu.SemaphoreType.DMA,
        ],
        name="sc_row_scatter",
    )(rows, dest_a, dest_b)


def _moe_layer(x, mod, w_router, b_router, w13, w2, layer, ln_g, ln_b, *, seq, alpha):
    t, d = x.shape
    n_exp = w_router.shape[1]
    h, meta, cnt = _router(x, mod, w_router, b_router, seq=seq)
    e0 = meta[:, 0].astype(jnp.int32)
    e1 = meta[:, 1].astype(jnp.int32)
    rank0 = meta[:, 4].astype(jnp.int32)
    rank1 = meta[:, 5].astype(jnp.int32)
    counts = cnt[0, :n_exp].astype(jnp.int32)
    group_rows = MOE_GROUP * EXPERT_BLOCK
    padded = (counts + group_rows - 1) // group_rows * group_rows
    pad_end = jnp.cumsum(padded)
    pad_start = pad_end - padded
    dest0 = pad_start[e0] + rank0
    dest1 = pad_start[e1] + rank1
    cap = t * TOP_K + n_exp * group_rows
    group_start = jnp.arange(cap // group_rows, dtype=jnp.int32) * group_rows
    group_expert = jnp.minimum(jnp.sum(pad_end[None, :] <= group_start[:, None], axis=1), n_exp - 1).astype(jnp.int32)
    rows_left = counts[group_expert] - (group_start - pad_start[group_expert])
    rows_in_group = jnp.clip(rows_left, 0, group_rows).astype(jnp.int32)
    h_slots = _sc_scatter_rows(h, dest0, dest1, cap)
    y_slots = _moe_ffn(h_slots, group_expert, rows_in_group, w13, w2, layer)
    ya = _sc_gather_rows(y_slots, dest0)
    yb = _sc_gather_rows(y_slots, dest1)
    return _combine(x, ya, yb, meta, mod, ln_g, ln_b, seq=seq, alpha=alpha)


HEAD_PAD = LANES
GATE_PARTS = 3
GATE_ROWS = 16
VAL_PAD = 16
LOG2E = 1.4426950408889634


def _head_lanes(h, hd):
    k0 = (h % (HEAD_PAD // hd)) * hd
    return k0, (hd if k0 == 0 else 0)


def _fox_proj_kernel(x_ref, mod_ref, wf_ref, bf_ref, wq_ref, wk_ref, wv_ref, selk_ref, onek_ref, selq_ref,
                     oneq_ref, onev_ref, k_ref, q_ref, v_ref, carry_ref, *, q_scale, heads, hd, nb):
    tm = x_ref.shape[0]

    @pl.when(pl.program_id(0) % nb == 0)
    def _():
        carry_ref[...] = jnp.zeros_like(carry_ref)

    h_hi, h_mid, _ = _split3(_modulate(x_ref[...], mod_ref[0], 0))

    wf = wf_ref[...]
    wf_hi = wf.astype(BF16)
    wf_lo = (wf - wf_hi.astype(F32)).astype(BF16)
    f = _dot(h_hi, wf_hi) + _dot(h_mid, wf_hi) + _dot(h_hi, wf_lo) + bf_ref[...]
    lf = jnp.minimum(f, 0.0) - jnp.log(1.0 + jnp.exp(-jnp.abs(f)))
    r = lax.broadcasted_iota(jnp.int32, (tm, tm), 0)
    c = lax.broadcasted_iota(jnp.int32, (tm, tm), 1)
    tri = jnp.where(r >= c, 1.0, 0.0).astype(BF16)
    cs = carry_ref[0:1, :]
    for part in _split3(lf):
        cs = cs + _dot(tri, part)
    carry_ref[...] = jnp.broadcast_to(cs[tm - 1:tm, :], carry_ref.shape)
    cs = cs * LOG2E

    gate_k = onek_ref[...]
    for n, part in enumerate(_split3(cs)):
        gate_k = gate_k + _dot(part, selk_ref[n])
    kc = _dot(h_hi, wk_ref[...])
    lane = lax.broadcasted_iota(jnp.int32, (tm, HEAD_PAD), 1)
    for h in range(heads):
        k0, _ = _head_lanes(h, hd)
        src = (h * hd // HEAD_PAD) * HEAD_PAD
        is_k = jnp.logical_and(lane >= k0, lane < k0 + hd)
        hs = slice(h * HEAD_PAD, (h + 1) * HEAD_PAD)
        k_ref[:, hs] = jnp.where(is_k, kc[:, src:src + HEAD_PAD], gate_k[:, hs]).astype(BF16)

    nt = (((1,), (1,)), ((), ()))
    qc = lax.dot_general(wq_ref[...], h_hi, nt, preferred_element_type=F32) * (q_scale * LOG2E)
    gate_q = oneq_ref[...]
    for n, part in enumerate(_split3(cs.T)):
        gate_q = gate_q + _dot(selq_ref[n], part)
    fill = HEAD_PAD - hd - GATE_ROWS
    for h in range(heads):
        k0, g0 = _head_lanes(h, hd)
        base = h * HEAD_PAD
        q_ref[0, base + k0:base + k0 + hd, :] = qc[h * hd:(h + 1) * hd, :].astype(BF16)
        q_ref[0, base + g0:base + g0 + GATE_ROWS, :] = gate_q[h * GATE_ROWS:(h + 1) * GATE_ROWS, :].astype(BF16)
        q_ref[0, base + g0 + GATE_ROWS:base + g0 + GATE_ROWS + fill, :] = jnp.zeros((fill, tm), BF16)

    v = lax.dot_general(wv_ref[...], h_hi, nt, preferred_element_type=F32) + onev_ref[...]
    v_ref[0] = v.astype(BF16)


def _fox_proj(x, mod, w_in, b_f, *, bsz, seq, heads, tm=512):
    t, d = x.shape
    hd = d // heads
    assert HEAD_PAD == 2 * hd and GATE_ROWS >= 2 * GATE_PARTS
    nb = seq // tm
    wide = heads * HEAD_PAD
    wq_t = w_in[:, :d].T.astype(BF16)
    wk = w_in[:, d:2 * d].astype(BF16)
    vrows = heads * (hd + VAL_PAD)
    wv_t = jnp.pad(w_in[:, 2 * d:3 * d].reshape(d, heads, hd),
                   ((0, 0), (0, 0), (0, VAL_PAD))).reshape(d, vrows).T.astype(BF16)
    wf = jnp.zeros((d, LANES), F32).at[:, :heads].set(w_in[:, 3 * d:])
    bf = jnp.zeros((1, LANES), F32).at[0, :heads].set(b_f)
    sel_k = np.zeros((GATE_PARTS, LANES, wide), np.float32)
    one_k = np.zeros((1, wide), np.float32)
    sel_q = np.zeros((GATE_PARTS, heads * GATE_ROWS, LANES), np.float32)
    one_q = np.zeros((heads * GATE_ROWS, 1), np.float32)
    one_v = np.zeros((vrows, 1), np.float32)
    for hh in range(heads):
        _, g0 = _head_lanes(hh, hd)
        one_v[hh * (hd + VAL_PAD) + hd, 0] = 1.0
        for n in range(GATE_PARTS):
            sel_k[n, hh, hh * HEAD_PAD + g0 + n] = -1.0
            one_q[hh * GATE_ROWS + n, 0] = 1.0
            one_k[0, hh * HEAD_PAD + g0 + GATE_PARTS + n] = 1.0
            sel_q[n, hh * GATE_ROWS + GATE_PARTS + n, hh] = 1.0
    return pl.pallas_call(
        functools.partial(_fox_proj_kernel, q_scale=hd ** -0.5, heads=heads, hd=hd, nb=nb),
        grid=(t // tm,),
        in_specs=[
            pl.BlockSpec((tm, d), lambda i: (i, 0)),
            pl.BlockSpec((1, 6, d), lambda i: (i * tm // seq, 0, 0)),
            _const_spec((d, LANES)),
            _const_spec((1, LANES)),
            _const_spec((d, d)),
            _const_spec((d, d)),
            _const_spec((vrows, d)),
            _const_spec((GATE_PARTS, LANES, wide)),
            _const_spec((1, wide)),
            _const_spec((GATE_PARTS, heads * GATE_ROWS, LANES)),
            _const_spec((heads * GATE_ROWS, 1)),
            _const_spec((vrows, 1)),
        ],
        out_specs=[
            pl.BlockSpec((tm, wide), lambda i: (i, 0)),
            pl.BlockSpec((1, wide, tm), lambda i: (i // nb, 0, i % nb)),
            pl.BlockSpec((1, vrows, tm), lambda i: (i // nb, 0, i % nb)),
        ],
        out_shape=[
            jax.ShapeDtypeStruct((t, wide), BF16),
            jax.ShapeDtypeStruct((bsz, wide, seq), BF16),
            jax.ShapeDtypeStruct((bsz, vrows, seq), BF16),
        ],
        scratch_shapes=[pltpu.VMEM((8, LANES), F32)],
        compiler_params=_cparams(("arbitrary",)),
        name="fox_proj",
    )(x, mod, wf, bf, wq_t, wk, wv_t, jnp.asarray(sel_k, BF16), jnp.asarray(one_k), jnp.asarray(sel_q, BF16),
      jnp.asarray(one_q), jnp.asarray(one_v))


def _flash_kernel(qi_ref, kj_ref, k_ref, q_ref, v_ref, o_ref, m_ref, acc_ref, s_ref, *, heads, hd, blk):
    p_idx = pl.program_id(1)
    qi = qi_ref[p_idx]
    kj = kj_ref[p_idx]
    vp = hd + VAL_PAD

    @pl.when(kj == 0)
    def _():
        m_ref[...] = jnp.full(m_ref.shape, -jnp.inf, F32)
        acc_ref[...] = jnp.zeros_like(acc_ref)

    def step(masked):
        if masked:
            kpos = lax.broadcasted_iota(jnp.int32, (blk, blk), 0)
            qpos = lax.broadcasted_iota(jnp.int32, (blk, blk), 1)
            keep = kpos <= qpos

        def scores(h):
            hs = slice(h * HEAD_PAD, (h + 1) * HEAD_PAD)
            s = _dot(k_ref[:, hs], q_ref[0, hs, :])
            if masked:
                s = jnp.where(keep, s, -jnp.inf)
            s_ref[h % depth] = s
            return jnp.max(s, axis=0, keepdims=True)

        depth = s_ref.shape[0]
        ahead = [scores(h) for h in range(depth - 1)]
        for h in range(heads):
            if h + depth - 1 < heads:
                ahead.append(scores(h + depth - 1))
            m_cur = ahead.pop(0)
            m_prev = m_ref[h:h + 1, :]
            m_new = jnp.maximum(m_prev, m_cur)
            m_ref[h:h + 1, :] = m_new
            a = jnp.exp2(m_prev - m_new)
            p = jnp.exp2(s_ref[h % depth] - m_new).astype(BF16)
            rs = slice(h * vp, (h + 1) * vp)
            acc_ref[rs, :] = acc_ref[rs, :] * a + _dot(v_ref[0, rs, :], p)

    @pl.when(kj < qi)
    def _():
        step(False)

    @pl.when(kj == qi)
    def _():
        step(True)
        per = LANES // hd
        for g in range(heads // per):
            parts = []
            for n in range(per):
                r0 = (g * per + n) * vp
                parts.append(acc_ref[r0:r0 + hd, :] / acc_ref[r0 + hd:r0 + hd + 1, :])
            o_ref[:, g * LANES:(g + 1) * LANES] = jnp.concatenate(parts, axis=0).T.astype(BF16)


def _flash(k_aug, q_aug_t, v_t, *, bsz, seq, heads, blk=512):
    t, wide = k_aug.shape
    vrows = v_t.shape[1]
    hd = vrows // heads - VAL_PAD
    d = heads * hd
    nq = seq // blk
    pairs = [(i, j) for i in range(nq) for j in range(i + 1)]
    qi = jnp.asarray(np.array([p[0] for p in pairs], np.int32))
    kj = jnp.asarray(np.array([p[1] for p in pairs], np.int32))
    grid_spec = pltpu.PrefetchScalarGridSpec(
        num_scalar_prefetch=2,
        grid=(bsz, len(pairs)),
        in_specs=[
            pl.BlockSpec((blk, wide), lambda b, p, qi, kj: (b * nq + kj[p], 0)),
            pl.BlockSpec((1, wide, blk), lambda b, p, qi, kj: (b, 0, qi[p])),
            pl.BlockSpec((1, vrows, blk), lambda b, p, qi, kj: (b, 0, kj[p])),
        ],
        out_specs=pl.BlockSpec((blk, d), lambda b, p, qi, kj: (b * nq + qi[p], 0)),
        scratch_shapes=[
            pltpu.VMEM((heads, blk), F32),
            pltpu.VMEM((vrows, blk), F32),
            pltpu.VMEM((3, blk, blk), F32),
        ],
    )
    return pl.pallas_call(
        functools.partial(_flash_kernel, heads=heads, hd=hd, blk=blk),
        grid_spec=grid_spec,
        out_shape=jax.ShapeDtypeStruct((t, d), BF16),
        compiler_params=_cparams(("parallel", "arbitrary")),
        name="fox_attention",
    )(qi, kj, k_aug, q_aug_t, v_t)


def _mm_resln_kernel(a_ref, x_ref, mod_ref, w_ref, g_ref, b_ref, o_ref, *, alpha):
    y = _dot(a_ref[...], w_ref[...])
    o_ref[...] = _res_ln(x_ref[...], y, mod_ref[0], 0, g_ref[...], b_ref[...], alpha)


def _mm_resln(a, x, mod, w, ln_g, ln_b, *, seq, alpha, tm=512):
    t, d = x.shape
    k = a.shape[1]
    return pl.pallas_call(
        functools.partial(_mm_resln_kernel, alpha=alpha),
        grid=(t // tm,),
        in_specs=[
            pl.BlockSpec((tm, k), lambda i: (i, 0)),
            pl.BlockSpec((tm, d), lambda i: (i, 0)),
            pl.BlockSpec((1, 6, d), lambda i: (i * tm // seq, 0, 0)),
            _const_spec((k, d)),
            _const_spec((1, d)),
            _const_spec((1, d)),
        ],
        out_specs=pl.BlockSpec((tm, d), lambda i: (i, 0)),
        out_shape=jax.ShapeDtypeStruct((t, d), F32),
        compiler_params=_cparams(("parallel",)),
        name="proj_res_ln",
    )(a, x, mod, w.astype(BF16), ln_g.reshape(1, d), ln_b.reshape(1, d))


def _fox_layer(x, mod, w_in, b_f, w_out, ln_g, ln_b, *, bsz, seq, alpha):
    heads = b_f.shape[0]
    k_aug, q_aug_t, v_t = _fox_proj(x, mod, w_in, b_f, bsz=bsz, seq=seq, heads=heads)
    attn = _flash(k_aug, q_aug_t, v_t, bsz=bsz, seq=seq, heads=heads)
    return _mm_resln(attn, x, mod, w_out, ln_g, ln_b, seq=seq, alpha=alpha)


def _pool_kernel(x_ref, xh_ref, mod_ref, win_ref, wg_ref, ls_ref, wout_ref, g_ref, b_ref, o_ref, pooled_ref,
                 *, tm, seq, alpha):
    i = pl.program_id(0)
    pos0 = (i * tm) % seq
    m = mod_ref[0]
    x = x_ref[...]
    halo_ok = jnp.where(pos0 > 0, jnp.float32(1.0), jnp.float32(0.0))
    xe = jnp.concatenate([xh_ref[...], x], axis=0)
    ze = _dot(_modulate(xe, m, 0).astype(BF16), win_ref[...])
    rows = lax.broadcasted_iota(jnp.int32, (tm + POOL_HALO, 1), 0)
    ze = ze * jnp.where(rows < POOL_HALO, halo_ok, jnp.float32(1.0))
    pos = (pos0 + lax.broadcasted_iota(jnp.int32, (tm, 1), 0) + 1).astype(F32)
    gd = ze.shape[1] // len(POOL_WINDOWS)
    for g, win in enumerate(POOL_WINDOWS):
        cs = slice(g * gd, (g + 1) * gd)
        zg = ze[:, cs]
        s = zg
        span = 1
        while span < win:
            s = s + pltpu.roll(s, span, 0)
            span *= 2
        count = jnp.minimum(pos, jnp.float32(win))
        pooled = s[POOL_HALO:, :] / count - zg[POOL_HALO:, :]
        pooled_ref[:, cs] = (_dot(pooled.astype(BF16), wg_ref[g]) * ls_ref[:, cs]).astype(BF16)
    y = _dot(pooled_ref[...], wout_ref[...])
    o_ref[...] = _res_ln(x, y, m, 0, g_ref[...], b_ref[...], alpha)


def _pool_layer(x, mod, w_in, w_grp, scale, w_out, ln_g, ln_b, *, seq, alpha, tm=512):
    t, d = x.shape
    ng, gd, _ = w_grp.shape
    hb = tm // POOL_HALO
    return pl.pallas_call(
        functools.partial(_pool_kernel, tm=tm, seq=seq, alpha=alpha),
        grid=(t // tm,),
        in_specs=[
            pl.BlockSpec((tm, d), lambda i: (i, 0)),
            pl.BlockSpec((POOL_HALO, d), lambda i: (jnp.maximum(i * hb - 1, 0), 0)),
            pl.BlockSpec((1, 6, d), lambda i: (i * tm // seq, 0, 0)),
            _const_spec((d, d)),
            _const_spec((ng, gd, gd)),
            _const_spec((1, d)),
            _const_spec((d, d)),
            _const_spec((1, d)),
            _const_spec((1, d)),
        ],
        out_specs=pl.BlockSpec((tm, d), lambda i: (i, 0)),
        out_shape=jax.ShapeDtypeStruct((t, d), F32),
        scratch_shapes=[pltpu.VMEM((tm, d), BF16)],
        compiler_params=_cparams(("parallel",)),
        name="pool_mixer",
    )(x, x, mod, w_in.astype(BF16), w_grp.astype(BF16), scale.reshape(1, d), w_out.astype(BF16),
      ln_g.reshape(1, d), ln_b.reshape(1, d))


def kernel(x, c, mod_w, mod_b, ln_g, ln_b, gm_w_in, gm_ln_g, gm_ln_b, gm_w_s, gm_b_s, gm_w_out, fox_w_in,
           fox_b_f, fox_w_out, pool_w_in, pool_w_grp, pool_scale, pool_w_out, ffn_w13, ffn_w2, moe_w_router,
           moe_b_router, moe_w13, moe_w2):
    bsz, seq, d = x.shape
    depth = mod_w.shape[0]
    alpha = (2 * depth) ** 0.25
    n_mixers = 3
    mod_all = _modulation(c, mod_w, mod_b)
    xt = x.reshape(bsz * seq, d)
    for i in range(depth):
        mod = mod_all[i]
        kind, j = i % n_mixers, i // n_mixers
        if kind == 0:
            xt = _gmlp_layer(xt, mod, gm_w_in[j], gm_ln_g[j], gm_ln_b[j], gm_w_s[j], gm_b_s[j], gm_w_out[j],
                             ln_g[i, 0], ln_b[i, 0], seq=seq, alpha=alpha)
        elif kind == 1:
            xt = _fox_layer(xt, mod, fox_w_in[j], fox_b_f[j], fox_w_out[j], ln_g[i, 0], ln_b[i, 0],
                            bsz=bsz, seq=seq, alpha=alpha)
        else:
            xt = _pool_layer(xt, mod, pool_w_in[j], pool_w_grp[j], pool_scale[j], pool_w_out[j],
                             ln_g[i, 0], ln_b[i, 0], seq=seq, alpha=alpha)
        if i % 2 == 0:
            xt = _ffn_layer(xt, mod, ffn_w13[i // 2], ffn_w2[i // 2], ln_g[i, 1], ln_b[i, 1],
                            seq=seq, alpha=alpha)
        else:
            xt = _moe_layer(xt, mod, moe_w_router[i // 2], moe_b_router[i // 2], moe_w13, moe_w2, i // 2,
                            ln_g[i, 1], ln_b[i, 1], seq=seq, alpha=alpha)
    return xt.reshape(bsz, seq, d)
```

```python
import functools

import numpy as np
import jax
import jax.numpy as jnp
from jax import lax
from jax.experimental import pallas as pl
from jax.experimental.pallas import tpu as pltpu
from jax.experimental.pallas import tpu_sc as plsc

F32 = jnp.float32
BF16 = jnp.bfloat16

POOL_WINDOWS = (2, 4, 8, 16)
TOP_K = 2
EXPERT_BLOCK = 512
LN_EPS = 1e-5
LANES = 128
POOL_HALO = 16
VMEM_LIMIT = 56 * 1024 * 1024
SC_CORES = 2
SC_SUBCORES = 16
SC_ROWS = 128


def _cparams(sem):
    return pltpu.CompilerParams(dimension_semantics=sem, vmem_limit_bytes=VMEM_LIMIT)


def _const_spec(shape):
    nd = len(shape)
    return pl.BlockSpec(shape, lambda *_: (0,) * nd, pipeline_mode=pl.Buffered(1))


def _layer_norm(r, g, b):
    mu = jnp.mean(r, axis=-1, keepdims=True)
    xc = r - mu
    var = jnp.mean(xc * xc, axis=-1, keepdims=True)
    return xc * lax.rsqrt(var + LN_EPS) * g + b


def _modulate(x, m, off):
    return x * (1.0 + m[off + 1:off + 2]) + m[off:off + 1]


def _res_ln(x, y, m, off, g, b, alpha):
    return _layer_norm(alpha * x + (1.0 + m[off + 2:off + 3]) * y, g, b)


def _split3(a):
    hi = a.astype(BF16)
    r1 = a - hi.astype(F32)
    mid = r1.astype(BF16)
    lo = (r1 - mid.astype(F32)).astype(BF16)
    return hi, mid, lo


def _dot(a, b):
    return jnp.dot(a, b, preferred_element_type=F32)


def _pack_bf16_pair(x):
    n = x.shape[1] // 2
    hi = lax.bitcast_convert_type(x[:, :n].astype(BF16).astype(F32), jnp.uint32)
    lo = lax.bitcast_convert_type(x[:, n:].astype(BF16).astype(F32), jnp.uint32)
    return hi | (lo >> 16)


def _unpack_bf16_pair(w):
    hi = lax.bitcast_convert_type(w & jnp.uint32(0xFFFF0000), F32)
    lo = lax.bitcast_convert_type(w << 16, F32)
    return jnp.concatenate([hi, lo], axis=1)


def _mod_kernel(c_ref, w_ref, b_ref, o_ref):
    c = c_ref[...]
    s = c / (1.0 + jnp.exp(-c))
    s_hi, s_mid, _ = _split3(s)
    w = w_ref[0]
    w_hi = w.astype(BF16)
    w_lo = (w - w_hi.astype(F32)).astype(BF16)
    acc = _dot(s_hi, w_hi) + _dot(s_mid, w_hi) + _dot(s_hi, w_lo)
    o_ref[0] = acc + b_ref[0]


def _modulation(c, mod_w, mod_b):
    depth, d, n = mod_w.shape
    bsz = c.shape[0]
    rows = 8
    tn = n // 4
    c_pad = jnp.zeros((rows, d), F32).at[:bsz].set(c)
    out = pl.pallas_call(
        _mod_kernel,
        grid=(depth, n // tn),
        in_specs=[
            pl.BlockSpec((rows, d), lambda l, j: (0, 0)),
            pl.BlockSpec((1, d, tn), lambda l, j: (l, 0, j)),
            pl.BlockSpec((1, 1, tn), lambda l, j: (l, 0, j)),
        ],
        out_specs=pl.BlockSpec((1, rows, tn), lambda l, j: (l, 0, j)),
        out_shape=jax.ShapeDtypeStruct((depth, rows, n), F32),
        compiler_params=_cparams(("parallel", "parallel")),
        name="adaln_mod",
    )(c_pad, mod_w, mod_b.reshape(depth, 1, n))
    return out[:, :bsz, :].reshape(depth, bsz, 6, d)


def _gmlp_kernel(x_ref, mod_ref, win_ref, vg_ref, vb_ref, ws_ref, bst_ref, wout_ref, g_ref, b_ref,
                 o_ref, gated_ref, *, tm, sub, chunk, groups, alpha):
    m = mod_ref[0]
    row = lax.broadcasted_iota(jnp.int32, (chunk, chunk), 0)
    col = lax.broadcasted_iota(jnp.int32, (chunk, chunk), 1)
    causal = row >= col
    bst = bst_ref[...]
    w_mix = [jnp.where(causal, ws_ref[g], 0.0).astype(BF16) for g in range(groups)]
    n = tm // sub

    def project(s):
        x = x_ref[s * sub:(s + 1) * sub, :]
        return x, _dot(_modulate(x, m, 0).astype(BF16), win_ref[...])

    def activate(z):
        z = 0.5 * z * (1.0 + lax.erf(z * (2.0 ** -0.5)))
        width = z.shape[1] // 2
        return z[:, :width], _layer_norm(z[:, width:], vg_ref[...], vb_ref[...]).astype(BF16)

    def mix(s, u, v):
        gd = u.shape[1] // groups
        for g in range(groups):
            cs = slice(g * gd, (g + 1) * gd)
            for c in range(sub // chunk):
                rs = slice(c * chunk, (c + 1) * chunk)
                mixed = _dot(w_mix[g], v[rs, cs]) + bst[:, g:g + 1]
                gated_ref[s * sub + c * chunk:s * sub + (c + 1) * chunk, cs] = (u[rs, cs] * mixed).astype(BF16)
        return _dot(gated_ref[s * sub:(s + 1) * sub, :], wout_ref[...])

    def finish(s, x, y):
        o_ref[s * sub:(s + 1) * sub, :] = _res_ln(x, y, m, 0, g_ref[...], b_ref[...], alpha)

    nxt = project(0)
    pending = None
    for s in range(n):
        x, z = nxt
        if s + 1 < n:
            nxt = project(s + 1)
        u, v = activate(z)
        if pending is not None:
            finish(*pending)
        pending = (s, x, mix(s, u, v))
    finish(*pending)


def _gmlp_layer(x, mod, w_in, v_g, v_b, w_s, b_s, w_out, ln_g, ln_b, *, seq, alpha, tm=512, sub=256):
    t, d = x.shape
    groups, chunk, _ = w_s.shape
    width = w_out.shape[0]
    kern = functools.partial(_gmlp_kernel, tm=tm, sub=sub, chunk=chunk, groups=groups, alpha=alpha)
    return pl.pallas_call(
        kern,
        grid=(t // tm,),
        in_specs=[
            pl.BlockSpec((tm, d), lambda i: (i, 0)),
            pl.BlockSpec((1, 6, d), lambda i: (i * tm // seq, 0, 0)),
            _const_spec((d, 2 * width)),
            _const_spec((1, width)),
            _const_spec((1, width)),
            _const_spec((groups, chunk, chunk)),
            _const_spec((chunk, groups)),
            _const_spec((width, d)),
            _const_spec((1, d)),
            _const_spec((1, d)),
        ],
        out_specs=pl.BlockSpec((tm, d), lambda i: (i, 0)),
        out_shape=jax.ShapeDtypeStruct((t, d), F32),
        scratch_shapes=[pltpu.VMEM((tm, width), BF16)],
        compiler_params=_cparams(("parallel",)),
        name="gmlp_mixer",
    )(x, mod, w_in.astype(BF16), v_g.reshape(1, width), v_b.reshape(1, width), w_s, b_s.T,
      w_out.astype(BF16), ln_g.reshape(1, d), ln_b.reshape(1, d))


def _swiglu_pipelined(hs, w1, w3, w2, emit):
    up = lambda h: (_dot(h, w1), _dot(h, w3))
    nxt = up(hs[0])
    for s in range(len(hs)):
        a, b = nxt
        if s + 1 < len(hs):
            nxt = up(hs[s + 1])
        t = (a / (1.0 + jnp.exp(-a)) * b).astype(BF16)
        emit(s, _dot(t, w2))


def _ffn_kernel(x_ref, mod_ref, w1_ref, w3_ref, w2_ref, g_ref, b_ref, o_ref, h_ref, *, alpha, sub):
    j = pl.program_id(1)
    last = pl.num_programs(1) - 1
    subs = [slice(s, s + sub) for s in range(0, x_ref.shape[0], sub)]
    weights = lambda: (w1_ref[...], w3_ref[...], w2_ref[...])

    @pl.when(j == 0)
    def _():
        hs = []
        for rows in subs:
            h = _modulate(x_ref[rows, :], mod_ref[0], 3).astype(BF16)
            h_ref[rows, :] = h
            hs.append(h)

        def emit(s, part):
            o_ref[subs[s], :] = part

        _swiglu_pipelined(hs, *weights(), emit)

    @pl.when(jnp.logical_and(j > 0, j < last))
    def _():
        def emit(s, part):
            o_ref[subs[s], :] += part

        _swiglu_pipelined([h_ref[rows, :] for rows in subs], *weights(), emit)

    @pl.when(j == last)
    def _():
        def emit(s, part):
            rows = subs[s]
            y = o_ref[rows, :] + part
            o_ref[rows, :] = _res_ln(x_ref[rows, :], y, mod_ref[0], 3, g_ref[...], b_ref[...], alpha)

        _swiglu_pipelined([h_ref[rows, :] for rows in subs], *weights(), emit)


def _ffn_layer(x, mod, w13, w2, ln_g, ln_b, *, seq, alpha, tm=1024, tf=512):
    t, d = x.shape
    f = w2.shape[0]
    nf = f // tf
    assert nf >= 2
    w13b = w13.astype(BF16)
    return pl.pallas_call(
        functools.partial(_ffn_kernel, alpha=alpha, sub=tm // 2),
        grid=(t // tm, nf),
        in_specs=[
            pl.BlockSpec((tm, d), lambda i, j: (i, 0)),
            pl.BlockSpec((1, 6, d), lambda i, j: (i * tm // seq, 0, 0)),
            pl.BlockSpec((d, tf), lambda i, j: (0, j)),
            pl.BlockSpec((d, tf), lambda i, j: (0, nf + j)),
            pl.BlockSpec((tf, d), lambda i, j: (j, 0)),
            _const_spec((1, d)),
            _const_spec((1, d)),
        ],
        out_specs=pl.BlockSpec((tm, d), lambda i, j: (i, 0)),
        out_shape=jax.ShapeDtypeStruct((t, d), F32),
        scratch_shapes=[pltpu.VMEM((tm, d), BF16)],
        compiler_params=_cparams(("parallel", "arbitrary")),
        name="swiglu_dense",
    )(x, mod, w13b, w13b, w2.astype(BF16), ln_g.reshape(1, d), ln_b.reshape(1, d))


MOE_GROUP = 2


def _moe_ffn_kernel(ge_ref, gr_ref, h_ref, w1_ref, w3_ref, w2_ref, o_ref, acc_ref, *, sub):
    g = pl.program_id(0)
    j = pl.program_id(1)
    last = pl.num_programs(1) - 1
    n_rows = gr_ref[g]
    nv = (n_rows + EXPERT_BLOCK - 1) // EXPERT_BLOCK

    def run(n_blocks, phase):
        subs = [slice(s, s + sub) for s in range(0, n_blocks * EXPERT_BLOCK, sub)]
        row = lax.broadcasted_iota(jnp.int32, (sub, 1), 0)

        def rows_in(rows):
            w = jnp.where(row + rows.start < n_rows, h_ref[rows, :], jnp.uint32(0))
            return _unpack_bf16_pair(w).astype(BF16)

        def emit(s, part):
            if phase == "first":
                acc_ref[subs[s], :] = part
            elif phase == "middle":
                acc_ref[subs[s], :] += part
            else:
                o_ref[subs[s], :] = _pack_bf16_pair(acc_ref[subs[s], :] + part)

        _swiglu_pipelined([rows_in(rows) for rows in subs], w1_ref[0, 0].astype(BF16), w3_ref[0, 0].astype(BF16),
                          w2_ref[0, 0].astype(BF16), emit)

    for n_blocks in range(1, MOE_GROUP + 1):
        @pl.when(jnp.logical_and(nv == n_blocks, j == 0))
        def _():
            run(n_blocks, "first")

        @pl.when(jnp.logical_and(nv == n_blocks, jnp.logical_and(j > 0, j < last)))
        def _():
            run(n_blocks, "middle")

        @pl.when(jnp.logical_and(nv == n_blocks, j == last))
        def _():
            run(n_blocks, "last")

    for blk in range(MOE_GROUP):
        @pl.when(jnp.logical_and(blk >= nv, j == last))
        def _():
            o_ref[blk * EXPERT_BLOCK:(blk + 1) * EXPERT_BLOCK, :] = jnp.zeros((EXPERT_BLOCK, o_ref.shape[1]),
                                                                              jnp.uint32)


def _moe_ffn(h_slots, group_expert, group_rows, w13, w2, layer, *, tf=512):
    cap, half = h_slots.shape
    d = 2 * half
    f = w2.shape[2]
    nf = f // tf
    assert nf >= 2
    tm = MOE_GROUP * EXPERT_BLOCK

    def jj(g, j, gr):
        return jnp.where(gr[g] > 0, j, nf - 1)

    grid_spec = pltpu.PrefetchScalarGridSpec(
        num_scalar_prefetch=2,
        grid=(cap // tm, nf),
        in_specs=[
            pl.BlockSpec((tm, half), lambda g, j, ge, gr: (g, 0)),
            pl.BlockSpec((1, 1, d, tf), lambda g, j, ge, gr: (layer, ge[g], 0, jj(g, j, gr))),
            pl.BlockSpec((1, 1, d, tf), lambda g, j, ge, gr: (layer, ge[g], 0, nf + jj(g, j, gr))),
            pl.BlockSpec((1, 1, tf, d), lambda g, j, ge, gr: (layer, ge[g], jj(g, j, gr), 0)),
        ],
        out_specs=pl.BlockSpec((tm, half), lambda g, j, ge, gr: (g, 0)),
        scratch_shapes=[pltpu.VMEM((tm, d), F32)],
    )
    return pl.pallas_call(
        functools.partial(_moe_ffn_kernel, sub=EXPERT_BLOCK),
        grid_spec=grid_spec,
        out_shape=jax.ShapeDtypeStruct((cap, half), jnp.uint32),
        compiler_params=_cparams(("parallel", "arbitrary")),
        name="swiglu_experts",
    )(group_expert, group_rows, h_slots, w13, w13, w2)


def _router_kernel(x_ref, mod_ref, wr_ref, br_ref, h_ref, meta_ref, cnt_ref, carry_ref, *, tm, n_exp):
    i = pl.program_id(0)

    @pl.when(i == 0)
    def _():
        carry_ref[...] = jnp.zeros_like(carry_ref)

    h = _modulate(x_ref[...], mod_ref[0], 3)
    h_ref[...] = _pack_bf16_pair(h)
    h_hi, h_mid, _ = _split3(h)
    w = wr_ref[...]
    w_hi = w.astype(BF16)
    w_lo = (w - w_hi.astype(F32)).astype(BF16)
    logits = _dot(h_hi, w_hi) + _dot(h_mid, w_hi) + _dot(h_hi, w_lo) + br_ref[...]
    lane = lax.broadcasted_iota(jnp.int32, (tm, LANES), 1)
    neg = jnp.float32(-jnp.inf)
    logits = jnp.where(lane < n_exp, logits, neg)
    v0 = jnp.max(logits, axis=1, keepdims=True)
    e0 = jnp.min(jnp.where(logits == v0, lane, LANES), axis=1, keepdims=True)
    rest = jnp.where(lane == e0, neg, logits)
    v1 = jnp.max(rest, axis=1, keepdims=True)
    e1 = jnp.min(jnp.where(rest == v1, lane, LANES), axis=1, keepdims=True)
    p = jnp.exp(v1 - v0)
    g0 = 1.0 / (1.0 + p)
    g1 = p / (1.0 + p)
    oh0 = lane == e0
    oh1 = lane == e1
    onehot = jnp.where(jnp.logical_or(oh0, oh1), 1.0, 0.0).astype(BF16)
    r = lax.broadcasted_iota(jnp.int32, (tm, tm), 0)
    c = lax.broadcasted_iota(jnp.int32, (tm, tm), 1)
    strict = jnp.where(r > c, 1.0, 0.0).astype(BF16)
    before = _dot(strict, onehot) + carry_ref[0:1, :]
    rank0 = jnp.sum(jnp.where(oh0, before, 0.0), axis=1, keepdims=True)
    rank1 = jnp.sum(jnp.where(oh1, before, 0.0), axis=1, keepdims=True)
    total = carry_ref[0:1, :] + jnp.sum(onehot.astype(F32), axis=0, keepdims=True)
    carry_ref[...] = jnp.broadcast_to(total, carry_ref.shape)
    cnt_ref[...] = jnp.broadcast_to(total, cnt_ref.shape)
    meta = jnp.where(lane == 0, e0.astype(F32), 0.0)
    meta = jnp.where(lane == 1, e1.astype(F32), meta)
    meta = jnp.where(lane == 2, g0, meta)
    meta = jnp.where(lane == 3, g1, meta)
    meta = jnp.where(lane == 4, rank0, meta)
    meta = jnp.where(lane == 5, rank1, meta)
    meta_ref[...] = meta


def _router(x, mod, w_router, b_router, *, seq, tm=512):
    t, d = x.shape
    n_exp = w_router.shape[1]
    wr = jnp.zeros((d, LANES), F32).at[:, :n_exp].set(w_router)
    br = jnp.zeros((1, LANES), F32).at[0, :n_exp].set(b_router)
    return pl.pallas_call(
        functools.partial(_router_kernel, tm=tm, n_exp=n_exp),
        grid=(t // tm,),
        in_specs=[
            pl.BlockSpec((tm, d), lambda i: (i, 0)),
            pl.BlockSpec((1, 6, d), lambda i: (i * tm // seq, 0, 0)),
            _const_spec((d, LANES)),
            _const_spec((1, LANES)),
        ],
        out_specs=[
            pl.BlockSpec((tm, d // 2), lambda i: (i, 0)),
            pl.BlockSpec((tm, LANES), lambda i: (i, 0)),
            pl.BlockSpec((8, LANES), lambda i: (0, 0)),
        ],
        out_shape=[
            jax.ShapeDtypeStruct((t, d // 2), jnp.uint32),
            jax.ShapeDtypeStruct((t, LANES), F32),
            jax.ShapeDtypeStruct((8, LANES), F32),
        ],
        scratch_shapes=[pltpu.VMEM((8, LANES), F32)],
        compiler_params=_cparams(("arbitrary",)),
        name="moe_router",
    )(x, mod, wr, br)


def _combine_kernel(x_ref, ya_ref, yb_ref, meta_ref, mod_ref, g_ref, b_ref, o_ref, *, alpha):
    meta = meta_ref[...]
    y = meta[:, 2:3] * _unpack_bf16_pair(ya_ref[...]) + meta[:, 3:4] * _unpack_bf16_pair(yb_ref[...])
    o_ref[...] = _res_ln(x_ref[...], y, mod_ref[0], 3, g_ref[...], b_ref[...], alpha)


def _combine(x, ya, yb, meta, mod, ln_g, ln_b, *, seq, alpha, tm=512):
    t, d = x.shape
    row = pl.BlockSpec((tm, d), lambda i: (i, 0))
    packed = pl.BlockSpec((tm, d // 2), lambda i: (i, 0))
    return pl.pallas_call(
        functools.partial(_combine_kernel, alpha=alpha),
        grid=(t // tm,),
        in_specs=[row, packed, packed,
                  pl.BlockSpec((tm, LANES), lambda i: (i, 0)),
                  pl.BlockSpec((1, 6, d), lambda i: (i * tm // seq, 0, 0)),
                  _const_spec((1, d)), _const_spec((1, d))],
        out_specs=row,
        out_shape=jax.ShapeDtypeStruct((t, d), F32),
        compiler_params=_cparams(("parallel",)),
        name="moe_combine",
    )(x, ya, yb, meta, mod, ln_g.reshape(1, d), ln_b.reshape(1, d))


def _sc_mesh():
    return plsc.VectorSubcoreMesh(core_axis_name="c", subcore_axis_name="s", num_cores=SC_CORES,
                                  num_subcores=SC_SUBCORES)


def _sc_worker_share(n):
    workers = SC_CORES * SC_SUBCORES
    per_worker = n // workers
    steps = per_worker // SC_ROWS
    assert steps * SC_ROWS * workers == n
    return per_worker, steps


def _sc_gather_rows(table, idx):
    n = idx.shape[0]
    width = table.shape[1]
    per_worker, steps = _sc_worker_share(n)

    def body(table_hbm, idx_hbm, out_hbm, idx_v, rows_v, sem):
        wid = lax.axis_index("s") * SC_CORES + lax.axis_index("c")

        @pl.loop(0, steps)
        def _(j):
            off = pl.multiple_of(wid * per_worker + j * SC_ROWS, SC_ROWS)
            pltpu.sync_copy(idx_hbm.at[pl.ds(off, SC_ROWS)], idx_v)
            pltpu.async_copy(table_hbm.at[idx_v], rows_v, sem).wait()
            pltpu.sync_copy(rows_v, out_hbm.at[pl.ds(off, SC_ROWS)])

    return pl.kernel(
        body,
        out_type=jax.ShapeDtypeStruct((n, width), table.dtype),
        mesh=_sc_mesh(),
        scratch_types=[
            pltpu.VMEM((SC_ROWS,), jnp.int32),
            pltpu.VMEM((SC_ROWS, width), table.dtype),
            pltpu.SemaphoreType.DMA,
        ],
        name="sc_row_gather",
    )(table, idx)


def _sc_scatter_rows(rows, dest_a, dest_b, n_out):
    n, width = rows.shape
    per_worker, steps = _sc_worker_share(n)

    def body(rows_hbm, da_hbm, db_hbm, out_hbm, ia_v, ib_v, rows_v, sem):
        wid = lax.axis_index("s") * SC_CORES + lax.axis_index("c")

        @pl.loop(0, steps)
        def _(j):
            off = pl.multiple_of(wid * per_worker + j * SC_ROWS, SC_ROWS)
            pltpu.sync_copy(da_hbm.at[pl.ds(off, SC_ROWS)], ia_v)
            pltpu.sync_copy(db_hbm.at[pl.ds(off, SC_ROWS)], ib_v)
            pltpu.sync_copy(rows_hbm.at[pl.ds(off, SC_ROWS)], rows_v)
            pltpu.async_copy(rows_v, out_hbm.at[ia_v], sem).wait()
            pltpu.async_copy(rows_v, out_hbm.at[ib_v], sem).wait()

    return pl.kernel(
        body,
        out_type=jax.ShapeDtypeStruct((n_out, width), rows.dtype),
        mesh=_sc_mesh(),
        scratch_types=[
            pltpu.VMEM((SC_ROWS,), jnp.int32),
            pltpu.VMEM((SC_ROWS,), jnp.int32),
            pltpu.VMEM((SC_ROWS, width), rows.dtype),
            pltpu.SemaphoreType.DMA,
        ],
        name="sc_row_scatter",
    )(rows, dest_a, dest_b)


def _moe_layer(x, mod, w_router, b_router, w13, w2, layer, ln_g, ln_b, *, seq, alpha):
    t, d = x.shape
    n_exp = w_router.shape[1]
    h, meta, cnt = _router(x, mod, w_router, b_router, seq=seq)
    e0 = meta[:, 0].astype(jnp.int32)
    e1 = meta[:, 1].astype(jnp.int32)
    rank0 = meta[:, 4].astype(jnp.int32)
    rank1 = meta[:, 5].astype(jnp.int32)
    counts = cnt[0, :n_exp].astype(jnp.int32)
    group_rows = MOE_GROUP * EXPERT_BLOCK
    padded = (counts + group_rows - 1) // group_rows * group_rows
    pad_end = jnp.cumsum(padded)
    pad_start = pad_end - padded
    dest0 = pad_start[e0] + rank0
    dest1 = pad_start[e1] + rank1
    cap = t * TOP_K + n_exp * group_rows
    group_start = jnp.arange(cap // group_rows, dtype=jnp.int32) * group_rows
    group_expert = jnp.minimum(jnp.sum(pad_end[None, :] <= group_start[:, None], axis=1), n_exp - 1).astype(jnp.int32)
    rows_left = counts[group_expert] - (group_start - pad_start[group_expert])
    rows_in_group = jnp.clip(rows_left, 0, group_rows).astype(jnp.int32)
    h_slots = _sc_scatter_rows(h, dest0, dest1, cap)
    y_slots = _moe_ffn(h_slots, group_expert, rows_in_group, w13, w2, layer)
    ya = _sc_gather_rows(y_slots, dest0)
    yb = _sc_gather_rows(y_slots, dest1)
    return _combine(x, ya, yb, meta, mod, ln_g, ln_b, seq=seq, alpha=alpha)


HEAD_PAD = LANES
GATE_PARTS = 3
GATE_ROWS = 16
VAL_PAD = 16
LOG2E = 1.4426950408889634


def _head_lanes(h, hd):
    k0 = (h % (HEAD_PAD // hd)) * hd
    return k0, (hd if k0 == 0 else 0)


def _fox_proj_kernel(x_ref, mod_ref, wf_ref, bf_ref, wq_ref, wk_ref, wv_ref, selk_ref, onek_ref, selq_ref,
                     oneq_ref, onev_ref, k_ref, q_ref, v_ref, carry_ref, *, q_scale, heads, hd, nb):
    tm = x_ref.shape[0]

    @pl.when(pl.program_id(0) % nb == 0)
    def _():
        carry_ref[...] = jnp.zeros_like(carry_ref)

    h_hi, h_mid, _ = _split3(_modulate(x_ref[...], mod_ref[0], 0))

    wf = wf_ref[...]
    wf_hi = wf.astype(BF16)
    wf_lo = (wf - wf_hi.astype(F32)).astype(BF16)
    f = _dot(h_hi, wf_hi) + _dot(h_mid, wf_hi) + _dot(h_hi, wf_lo) + bf_ref[...]
    lf = jnp.minimum(f, 0.0) - jnp.log(1.0 + jnp.exp(-jnp.abs(f)))
    r = lax.broadcasted_iota(jnp.int32, (tm, tm), 0)
    c = lax.broadcasted_iota(jnp.int32, (tm, tm), 1)
    tri = jnp.where(r >= c, 1.0, 0.0).astype(BF16)
    cs = carry_ref[0:1, :]
    for part in _split3(lf):
        cs = cs + _dot(tri, part)
    carry_ref[...] = jnp.broadcast_to(cs[tm - 1:tm, :], carry_ref.shape)
    cs = cs * LOG2E

    gate_k = onek_ref[...]
    for n, part in enumerate(_split3(cs)):
        gate_k = gate_k + _dot(part, selk_ref[n])
    kc = _dot(h_hi, wk_ref[...])
    lane = lax.broadcasted_iota(jnp.int32, (tm, HEAD_PAD), 1)
    for h in range(heads):
        k0, _ = _head_lanes(h, hd)
        src = (h * hd // HEAD_PAD) * HEAD_PAD
        is_k = jnp.logical_and(lane >= k0, lane < k0 + hd)
        hs = slice(h * HEAD_PAD, (h + 1) * HEAD_PAD)
        k_ref[:, hs] = jnp.where(is_k, kc[:, src:src + HEAD_PAD], gate_k[:, hs]).astype(BF16)

    nt = (((1,), (1,)), ((), ()))
    qc = lax.dot_general(wq_ref[...], h_hi, nt, preferred_element_type=F32) * (q_scale * LOG2E)
    gate_q = oneq_ref[...]
    for n, part in enumerate(_split3(cs.T)):
        gate_q = gate_q + _dot(selq_ref[n], part)
    fill = HEAD_PAD - hd - GATE_ROWS
    for h in range(heads):
        k0, g0 = _head_lanes(h, hd)
        base = h * HEAD_PAD
        q_ref[0, base + k0:base + k0 + hd, :] = qc[h * hd:(h + 1) * hd, :].astype(BF16)
        q_ref[0, base + g0:base + g0 + GATE_ROWS, :] = gate_q[h * GATE_ROWS:(h + 1) * GATE_ROWS, :].astype(BF16)
        q_ref[0, base + g0 + GATE_ROWS:base + g0 + GATE_ROWS + fill, :] = jnp.zeros((fill, tm), BF16)

    v = lax.dot_general(wv_ref[...], h_hi, nt, preferred_element_type=F32) + onev_ref[...]
    v_ref[0] = v.astype(BF16)


def _fox_proj(x, mod, w_in, b_f, *, bsz, seq, heads, tm=512):
    t, d = x.shape
    hd = d // heads
    assert HEAD_PAD == 2 * hd and GATE_ROWS >= 2 * GATE_PARTS
    nb = seq // tm
    wide = heads * HEAD_PAD
    wq_t = w_in[:, :d].T.astype(BF16)
    wk = w_in[:, d:2 * d].astype(BF16)
    vrows = heads * (hd + VAL_PAD)
    wv_t = jnp.pad(w_in[:, 2 * d:3 * d].reshape(d, heads, hd),
                   ((0, 0), (0, 0), (0, VAL_PAD))).reshape(d, vrows).T.astype(BF16)
    wf = jnp.zeros((d, LANES), F32).at[:, :heads].set(w_in[:, 3 * d:])
    bf = jnp.zeros((1, LANES), F32).at[0, :heads].set(b_f)
    sel_k = np.zeros((GATE_PARTS, LANES, wide), np.float32)
    one_k = np.zeros((1, wide), np.float32)
    sel_q = np.zeros((GATE_PARTS, heads * GATE_ROWS, LANES), np.float32)
    one_q = np.zeros((heads * GATE_ROWS, 1), np.float32)
    one_v = np.zeros((vrows, 1), np.float32)
    for hh in range(heads):
        _, g0 = _head_lanes(hh, hd)
        one_v[hh * (hd + VAL_PAD) + hd, 0] = 1.0
        for n in range(GATE_PARTS):
            sel_k[n, hh, hh * HEAD_PAD + g0 + n] = -1.0
            one_q[hh * GATE_ROWS + n, 0] = 1.0
            one_k[0, hh * HEAD_PAD + g0 + GATE_PARTS + n] = 1.0
            sel_q[n, hh * GATE_ROWS + GATE_PARTS + n, hh] = 1.0
    return pl.pallas_call(
        functools.partial(_fox_proj_kernel, q_scale=hd ** -0.5, heads=heads, hd=hd, nb=nb),
        grid=(t // tm,),
        in_specs=[
            pl.BlockSpec((tm, d), lambda i: (i, 0)),
            pl.BlockSpec((1, 6, d), lambda i: (i * tm // seq, 0, 0)),
            _const_spec((d, LANES)),
            _const_spec((1, LANES)),
            _const_spec((d, d)),
            _const_spec((d, d)),
            _const_spec((vrows, d)),
            _const_spec((GATE_PARTS, LANES, wide)),
            _const_spec((1, wide)),
            _const_spec((GATE_PARTS, heads * GATE_ROWS, LANES)),
            _const_spec((heads * GATE_ROWS, 1)),
            _const_spec((vrows, 1)),
        ],
        out_specs=[
            pl.BlockSpec((tm, wide), lambda i: (i, 0)),
            pl.BlockSpec((1, wide, tm), lambda i: (i // nb, 0, i % nb)),
            pl.BlockSpec((1, vrows, tm), lambda i: (i // nb, 0, i % nb)),
        ],
        out_shape=[
            jax.ShapeDtypeStruct((t, wide), BF16),
            jax.ShapeDtypeStruct((bsz, wide, seq), BF16),
            jax.ShapeDtypeStruct((bsz, vrows, seq), BF16),
        ],
        scratch_shapes=[pltpu.VMEM((8, LANES), F32)],
        compiler_params=_cparams(("arbitrary",)),
        name="fox_proj",
    )(x, mod, wf, bf, wq_t, wk, wv_t, jnp.asarray(sel_k, BF16), jnp.asarray(one_k), jnp.asarray(sel_q, BF16),
      jnp.asarray(one_q), jnp.asarray(one_v))


def _flash_kernel(qi_ref, kj_ref, k_ref, q_ref, v_ref, x_ref, mod_ref, wout_ref, g_ref, b_ref, o_ref,
                  m_ref, acc_ref, s_ref, *, heads, hd, blk, alpha):
    p_idx = pl.program_id(1)
    qi = qi_ref[p_idx]
    kj = kj_ref[p_idx]
    vp = hd + VAL_PAD

    @pl.when(kj == 0)
    def _():
        m_ref[...] = jnp.full(m_ref.shape, -jnp.inf, F32)
        acc_ref[...] = jnp.zeros_like(acc_ref)

    def step(masked):
        if masked:
            kpos = lax.broadcasted_iota(jnp.int32, (blk, blk), 0)
            qpos = lax.broadcasted_iota(jnp.int32, (blk, blk), 1)
            keep = kpos <= qpos

        def scores(h):
            hs = slice(h * HEAD_PAD, (h + 1) * HEAD_PAD)
            s = _dot(k_ref[:, hs], q_ref[0, hs, :])
            if masked:
                s = jnp.where(keep, s, -jnp.inf)
            s_ref[h % depth] = s
            return jnp.max(s, axis=0, keepdims=True)

        depth = s_ref.shape[0]
        ahead = [scores(h) for h in range(depth - 1)]
        for h in range(heads):
            if h + depth - 1 < heads:
                ahead.append(scores(h + depth - 1))
            m_cur = ahead.pop(0)
            m_prev = m_ref[h:h + 1, :]
            m_new = jnp.maximum(m_prev, m_cur)
            m_ref[h:h + 1, :] = m_new
            a = jnp.exp2(m_prev - m_new)
            p = jnp.exp2(s_ref[h % depth] - m_new).astype(BF16)
            rs = slice(h * vp, (h + 1) * vp)
            acc_ref[rs, :] = acc_ref[rs, :] * a + _dot(v_ref[0, rs, :], p)

    @pl.when(kj < qi)
    def _():
        step(False)

    @pl.when(kj == qi)
    def _():
        step(True)
        per = LANES // hd
        cols = []
        for g in range(heads // per):
            parts = []
            for n in range(per):
                r0 = (g * per + n) * vp
                parts.append(acc_ref[r0:r0 + hd, :] / acc_ref[r0 + hd:r0 + hd + 1, :])
            cols.append(jnp.concatenate(parts, axis=0).T.astype(BF16))
        y = _dot(jnp.concatenate(cols, axis=1), wout_ref[...])
        o_ref[...] = _res_ln(x_ref[...], y, mod_ref[0], 0, g_ref[...], b_ref[...], alpha)


def _flash(k_aug, q_aug_t, v_t, x, mod, w_out, ln_g, ln_b, *, bsz, seq, heads, alpha, blk=512):
    t, wide = k_aug.shape
    vrows = v_t.shape[1]
    hd = vrows // heads - VAL_PAD
    d = heads * hd
    nq = seq // blk
    pairs = [(i, j) for i in range(nq) for j in range(i + 1)]
    qi = jnp.asarray(np.array([p[0] for p in pairs], np.int32))
    kj = jnp.asarray(np.array([p[1] for p in pairs], np.int32))
    grid_spec = pltpu.PrefetchScalarGridSpec(
        num_scalar_prefetch=2,
        grid=(bsz, len(pairs)),
        in_specs=[
            pl.BlockSpec((blk, wide), lambda b, p, qi, kj: (b * nq + kj[p], 0)),
            pl.BlockSpec((1, wide, blk), lambda b, p, qi, kj: (b, 0, qi[p])),
            pl.BlockSpec((1, vrows, blk), lambda b, p, qi, kj: (b, 0, kj[p])),
            pl.BlockSpec((blk, d), lambda b, p, qi, kj: (b * nq + qi[p], 0)),
            pl.BlockSpec((1, 6, d), lambda b, p, qi, kj: (b, 0, 0)),
            pl.BlockSpec((d, d), lambda b, p, qi, kj: (0, 0), pipeline_mode=pl.Buffered(1)),
            pl.BlockSpec((1, d), lambda b, p, qi, kj: (0, 0), pipeline_mode=pl.Buffered(1)),
            pl.BlockSpec((1, d), lambda b, p, qi, kj: (0, 0), pipeline_mode=pl.Buffered(1)),
        ],
        out_specs=pl.BlockSpec((blk, d), lambda b, p, qi, kj: (b * nq + qi[p], 0)),
        scratch_shapes=[
            pltpu.VMEM((heads, blk), F32),
            pltpu.VMEM((vrows, blk), F32),
            pltpu.VMEM((3, blk, blk), F32),
        ],
    )
    return pl.pallas_call(
        functools.partial(_flash_kernel, heads=heads, hd=hd, blk=blk, alpha=alpha),
        grid_spec=grid_spec,
        out_shape=jax.ShapeDtypeStruct((t, d), F32),
        compiler_params=_cparams(("parallel", "arbitrary")),
        name="fox_attention",
    )(qi, kj, k_aug, q_aug_t, v_t, x, mod, w_out.astype(BF16), ln_g.reshape(1, d), ln_b.reshape(1, d))


def _fox_layer(x, mod, w_in, b_f, w_out, ln_g, ln_b, *, bsz, seq, alpha):
    heads = b_f.shape[0]
    k_aug, q_aug_t, v_t = _fox_proj(x, mod, w_in, b_f, bsz=bsz, seq=seq, heads=heads)
    return _flash(k_aug, q_aug_t, v_t, x, mod, w_out, ln_g, ln_b, bsz=bsz, seq=seq, heads=heads, alpha=alpha)


def _pool_kernel(x_ref, xh_ref, mod_ref, win_ref, wg_ref, ls_ref, wout_ref, g_ref, b_ref, o_ref, pooled_ref,
                 *, tm, seq, alpha):
    i = pl.program_id(0)
    pos0 = (i * tm) % seq
    m = mod_ref[0]
    x = x_ref[...]
    halo_ok = jnp.where(pos0 > 0, jnp.float32(1.0), jnp.float32(0.0))
    xe = jnp.concatenate([xh_ref[...], x], axis=0)
    ze = _dot(_modulate(xe, m, 0).astype(BF16), win_ref[...])
    rows = lax.broadcasted_iota(jnp.int32, (tm + POOL_HALO, 1), 0)
    ze = ze * jnp.where(rows < POOL_HALO, halo_ok, jnp.float32(1.0))
    pos = (pos0 + lax.broadcasted_iota(jnp.int32, (tm, 1), 0) + 1).astype(F32)
    gd = ze.shape[1] // len(POOL_WINDOWS)
    for g, win in enumerate(POOL_WINDOWS):
        cs = slice(g * gd, (g + 1) * gd)
        zg = ze[:, cs]
        s = zg
        span = 1
        while span < win:
            s = s + pltpu.roll(s, span, 0)
            span *= 2
        count = jnp.minimum(pos, jnp.float32(win))
        pooled = s[POOL_HALO:, :] / count - zg[POOL_HALO:, :]
        pooled_ref[:, cs] = (_dot(pooled.astype(BF16), wg_ref[g]) * ls_ref[:, cs]).astype(BF16)
    y = _dot(pooled_ref[...], wout_ref[...])
    o_ref[...] = _res_ln(x, y, m, 0, g_ref[...], b_ref[...], alpha)


def _pool_layer(x, mod, w_in, w_grp, scale, w_out, ln_g, ln_b, *, seq, alpha, tm=512):
    t, d = x.shape
    ng, gd, _ = w_grp.shape
    hb = tm // POOL_HALO
    return pl.pallas_call(
        functools.partial(_pool_kernel, tm=tm, seq=seq, alpha=alpha),
        grid=(t // tm,),
        in_specs=[
            pl.BlockSpec((tm, d), lambda i: (i, 0)),
            pl.BlockSpec((POOL_HALO, d), lambda i: (jnp.maximum(i * hb - 1, 0), 0)),
            pl.BlockSpec((1, 6, d), lambda i: (i * tm // seq, 0, 0)),
            _const_spec((d, d)),
            _const_spec((ng, gd, gd)),
            _const_spec((1, d)),
            _const_spec((d, d)),
            _const_spec((1, d)),
            _const_spec((1, d)),
        ],
        out_specs=pl.BlockSpec((tm, d), lambda i: (i, 0)),
        out_shape=jax.ShapeDtypeStruct((t, d), F32),
        scratch_shapes=[pltpu.VMEM((tm, d), BF16)],
        compiler_params=_cparams(("parallel",)),
        name="pool_mixer",
    )(x, x, mod, w_in.astype(BF16), w_grp.astype(BF16), scale.reshape(1, d), w_out.astype(BF16),
      ln_g.reshape(1, d), ln_b.reshape(1, d))


def kernel(x, c, mod_w, mod_b, ln_g, ln_b, gm_w_in, gm_ln_g, gm_ln_b, gm_w_s, gm_b_s, gm_w_out, fox_w_in,
           fox_b_f, fox_w_out, pool_w_in, pool_w_grp, pool_scale, pool_w_out, ffn_w13, ffn_w2, moe_w_router,
           moe_b_router, moe_w13, moe_w2):
    bsz, seq, d = x.shape
    depth = mod_w.shape[0]
    alpha = (2 * depth) ** 0.25
    n_mixers = 3
    mod_all = _modulation(c, mod_w, mod_b)
    xt = x.reshape(bsz * seq, d)
    for i in range(depth):
        mod = mod_all[i]
        kind, j = i % n_mixers, i // n_mixers
        if kind == 0:
            xt = _gmlp_layer(xt, mod, gm_w_in[j], gm_ln_g[j], gm_ln_b[j], gm_w_s[j], gm_b_s[j], gm_w_out[j],
                             ln_g[i, 0], ln_b[i, 0], seq=seq, alpha=alpha)
        elif kind == 1:
            xt = _fox_layer(xt, mod, fox_w_in[j], fox_b_f[j], fox_w_out[j], ln_g[i, 0], ln_b[i, 0],
                            bsz=bsz, seq=seq, alpha=alpha)
        else:
            xt = _pool_layer(xt, mod, pool_w_in[j], pool_w_grp[j], pool_scale[j], pool_w_out[j],
                             ln_g[i, 0], ln_b[i, 0], seq=seq, alpha=alpha)
        if i % 2 == 0:
            xt = _ffn_layer(xt, mod, ffn_w13[i // 2], ffn_w2[i // 2], ln_g[i, 1], ln_b[i, 1],
                            seq=seq, alpha=alpha)
        else:
            xt = _moe_layer(xt, mod, moe_w_router[i // 2], moe_b_router[i // 2], moe_w13, moe_w2, i // 2,
                            ln_g[i, 1], ln_b[i, 1], seq=seq, alpha=alpha)
    return xt.reshape(bsz, seq, d)
```

```python
import functools

import numpy as np
import jax
import jax.numpy as jnp
from jax import lax
from jax.experimental import pallas as pl
from jax.experimental.pallas import tpu as pltpu
from jax.experimental.pallas import tpu_sc as plsc

F32 = jnp.float32
BF16 = jnp.bfloat16

POOL_WINDOWS = (2, 4, 8, 16)
TOP_K = 2
EXPERT_BLOCK = 512
LN_EPS = 1e-5
LANES = 128
POOL_HALO = 16
VMEM_LIMIT = 56 * 1024 * 1024
SC_CORES = 2
SC_SUBCORES = 16
SC_ROWS = 128


def _cparams(sem):
    return pltpu.CompilerParams(dimension_semantics=sem, vmem_limit_bytes=VMEM_LIMIT)


def _const_spec(shape):
    nd = len(shape)
    return pl.BlockSpec(shape, lambda *_: (0,) * nd, pipeline_mode=pl.Buffered(1))


def _layer_norm(r, g, b):
    mu = jnp.mean(r, axis=-1, keepdims=True)
    xc = r - mu
    var = jnp.mean(xc * xc, axis=-1, keepdims=True)
    return xc * lax.rsqrt(var + LN_EPS) * g + b


def _modulate(x, m, off):
    return x * (1.0 + m[off + 1:off + 2]) + m[off:off + 1]


def _res_ln(x, y, m, off, g, b, alpha):
    return _layer_norm(alpha * x + (1.0 + m[off + 2:off + 3]) * y, g, b)


def _split3(a):
    hi = a.astype(BF16)
    r1 = a - hi.astype(F32)
    mid = r1.astype(BF16)
    lo = (r1 - mid.astype(F32)).astype(BF16)
    return hi, mid, lo


def _dot(a, b):
    return jnp.dot(a, b, preferred_element_type=F32)


def _pack_bf16_pair(x):
    n = x.shape[1] // 2
    hi = lax.bitcast_convert_type(x[:, :n].astype(BF16).astype(F32), jnp.uint32)
    lo = lax.bitcast_convert_type(x[:, n:].astype(BF16).astype(F32), jnp.uint32)
    return hi | (lo >> 16)


def _unpack_bf16_pair(w):
    hi = lax.bitcast_convert_type(w & jnp.uint32(0xFFFF0000), F32)
    lo = lax.bitcast_convert_type(w << 16, F32)
    return jnp.concatenate([hi, lo], axis=1)


def _mod_kernel(c_ref, w_ref, b_ref, o_ref):
    c = c_ref[...]
    s = c / (1.0 + jnp.exp(-c))
    s_hi, s_mid, _ = _split3(s)
    w = w_ref[0]
    w_hi = w.astype(BF16)
    w_lo = (w - w_hi.astype(F32)).astype(BF16)
    acc = _dot(s_hi, w_hi) + _dot(s_mid, w_hi) + _dot(s_hi, w_lo)
    o_ref[0] = acc + b_ref[0]


def _modulation(c, mod_w, mod_b):
    depth, d, n = mod_w.shape
    bsz = c.shape[0]
    rows = 8
    tn = n // 4
    c_pad = jnp.zeros((rows, d), F32).at[:bsz].set(c)
    out = pl.pallas_call(
        _mod_kernel,
        grid=(depth, n // tn),
        in_specs=[
            pl.BlockSpec((rows, d), lambda l, j: (0, 0)),
            pl.BlockSpec((1, d, tn), lambda l, j: (l, 0, j)),
            pl.BlockSpec((1, 1, tn), lambda l, j: (l, 0, j)),
        ],
        out_specs=pl.BlockSpec((1, rows, tn), lambda l, j: (l, 0, j)),
        out_shape=jax.ShapeDtypeStruct((depth, rows, n), F32),
        compiler_params=_cparams(("parallel", "parallel")),
        name="adaln_mod",
    )(c_pad, mod_w, mod_b.reshape(depth, 1, n))
    return out[:, :bsz, :].reshape(depth, bsz, 6, d)


def _gmlp_kernel(x_ref, mod_ref, win_ref, vg_ref, vb_ref, ws_ref, bst_ref, wout_ref, g_ref, b_ref,
                 *rest, tm, sub, chunk, groups, alpha, n_exp):
    if n_exp:
        wr_ref, br_ref, o_ref, hp_ref, meta_ref, cnt_ref, gated_ref, carry_ref = rest

        @pl.when(pl.program_id(0) == 0)
        def _():
            carry_ref[...] = jnp.zeros_like(carry_ref)
    else:
        o_ref, gated_ref = rest
    m = mod_ref[0]
    row = lax.broadcasted_iota(jnp.int32, (chunk, chunk), 0)
    col = lax.broadcasted_iota(jnp.int32, (chunk, chunk), 1)
    causal = row >= col
    bst = bst_ref[...]
    w_mix = [jnp.where(causal, ws_ref[g], 0.0).astype(BF16) for g in range(groups)]
    n = tm // sub

    def project(s):
        x = x_ref[s * sub:(s + 1) * sub, :]
        return x, _dot(_modulate(x, m, 0).astype(BF16), win_ref[...])

    def activate(z):
        z = 0.5 * z * (1.0 + lax.erf(z * (2.0 ** -0.5)))
        width = z.shape[1] // 2
        return z[:, :width], _layer_norm(z[:, width:], vg_ref[...], vb_ref[...]).astype(BF16)

    def mix(s, u, v):
        gd = u.shape[1] // groups
        for g in range(groups):
            cs = slice(g * gd, (g + 1) * gd)
            for c in range(sub // chunk):
                rs = slice(c * chunk, (c + 1) * chunk)
                mixed = _dot(w_mix[g], v[rs, cs]) + bst[:, g:g + 1]
                gated_ref[s * sub + c * chunk:s * sub + (c + 1) * chunk, cs] = (u[rs, cs] * mixed).astype(BF16)
        return _dot(gated_ref[s * sub:(s + 1) * sub, :], wout_ref[...])

    def finish(s, x, y):
        rows = slice(s * sub, (s + 1) * sub)
        x_new = _res_ln(x, y, m, 0, g_ref[...], b_ref[...], alpha)
        o_ref[rows, :] = x_new
        if n_exp:
            hp_ref[rows, :], meta_ref[rows, :] = _route_rows(x_new, m, wr_ref, br_ref, carry_ref, n_exp)

    nxt = project(0)
    pending = None
    for s in range(n):
        x, z = nxt
        if s + 1 < n:
            nxt = project(s + 1)
        u, v = activate(z)
        if pending is not None:
            finish(*pending)
        pending = (s, x, mix(s, u, v))
    finish(*pending)
    if n_exp:
        cnt_ref[...] = carry_ref[...]


def _gmlp_layer(x, mod, w_in, v_g, v_b, w_s, b_s, w_out, ln_g, ln_b, router=None, *, seq, alpha, tm=512,
                sub=256):
    t, d = x.shape
    groups, chunk, _ = w_s.shape
    width = w_out.shape[0]
    n_exp = router[0].shape[1] if router else 0
    kern = functools.partial(_gmlp_kernel, tm=tm, sub=sub, chunk=chunk, groups=groups, alpha=alpha, n_exp=n_exp)
    row_map = lambda i: (i, 0)
    out_specs, out_shape = [pl.BlockSpec((tm, d), row_map)], [jax.ShapeDtypeStruct((t, d), F32)]
    scratch = [pltpu.VMEM((tm, width), BF16)]
    extra_specs, extra_args = [], []
    if router:
        r_specs, r_shapes = _router_out(t, d, tm, row_map)
        out_specs, out_shape = out_specs + r_specs, out_shape + r_shapes
        scratch.append(pltpu.VMEM((8, LANES), F32))
        extra_specs = [_const_spec((d, LANES)), _const_spec((1, LANES))]
        extra_args = list(_router_operands(*router))
    outs = pl.pallas_call(
        kern,
        grid=(t // tm,),
        in_specs=[
            pl.BlockSpec((tm, d), lambda i: (i, 0)),
            pl.BlockSpec((1, 6, d), lambda i: (i * tm // seq, 0, 0)),
            _const_spec((d, 2 * width)),
            _const_spec((1, width)),
            _const_spec((1, width)),
            _const_spec((groups, chunk, chunk)),
            _const_spec((chunk, groups)),
            _const_spec((width, d)),
            _const_spec((1, d)),
            _const_spec((1, d)),
        ] + extra_specs,
        out_specs=out_specs,
        out_shape=out_shape,
        scratch_shapes=scratch,
        compiler_params=_cparams(("arbitrary",)),
        name="gmlp_mixer",
    )(x, mod, w_in.astype(BF16), v_g.reshape(1, width), v_b.reshape(1, width), w_s, b_s.T,
      w_out.astype(BF16), ln_g.reshape(1, d), ln_b.reshape(1, d), *extra_args)
    return outs if router else outs[0]


def _swiglu_pipelined(hs, w1, w3, w2, emit):
    up = lambda h: (_dot(h, w1), _dot(h, w3))
    nxt = up(hs[0])
    for s in range(len(hs)):
        a, b = nxt
        if s + 1 < len(hs):
            nxt = up(hs[s + 1])
        t = (a / (1.0 + jnp.exp(-a)) * b).astype(BF16)
        emit(s, _dot(t, w2))


def _ffn_kernel(x_ref, mod_ref, w1_ref, w3_ref, w2_ref, g_ref, b_ref, o_ref, h_ref, *, alpha, sub):
    j = pl.program_id(1)
    last = pl.num_programs(1) - 1
    subs = [slice(s, s + sub) for s in range(0, x_ref.shape[0], sub)]
    weights = lambda: (w1_ref[...], w3_ref[...], w2_ref[...])

    @pl.when(j == 0)
    def _():
        hs = []
        for rows in subs:
            h = _modulate(x_ref[rows, :], mod_ref[0], 3).astype(BF16)
            h_ref[rows, :] = h
            hs.append(h)

        def emit(s, part):
            o_ref[subs[s], :] = part

        _swiglu_pipelined(hs, *weights(), emit)

    @pl.when(jnp.logical_and(j > 0, j < last))
    def _():
        def emit(s, part):
            o_ref[subs[s], :] += part

        _swiglu_pipelined([h_ref[rows, :] for rows in subs], *weights(), emit)

    @pl.when(j == last)
    def _():
        def emit(s, part):
            rows = subs[s]
            y = o_ref[rows, :] + part
            o_ref[rows, :] = _res_ln(x_ref[rows, :], y, mod_ref[0], 3, g_ref[...], b_ref[...], alpha)

        _swiglu_pipelined([h_ref[rows, :] for rows in subs], *weights(), emit)


def _ffn_layer(x, mod, w13, w2, ln_g, ln_b, *, seq, alpha, tm=1024, tf=512):
    t, d = x.shape
    f = w2.shape[0]
    nf = f // tf
    assert nf >= 2
    w13b = w13.astype(BF16)
    return pl.pallas_call(
        functools.partial(_ffn_kernel, alpha=alpha, sub=tm // 2),
        grid=(t // tm, nf),
        in_specs=[
            pl.BlockSpec((tm, d), lambda i, j: (i, 0)),
            pl.BlockSpec((1, 6, d), lambda i, j: (i * tm // seq, 0, 0)),
            pl.BlockSpec((d, tf), lambda i, j: (0, j)),
            pl.BlockSpec((d, tf), lambda i, j: (0, nf + j)),
            pl.BlockSpec((tf, d), lambda i, j: (j, 0)),
            _const_spec((1, d)),
            _const_spec((1, d)),
        ],
        out_specs=pl.BlockSpec((tm, d), lambda i, j: (i, 0)),
        out_shape=jax.ShapeDtypeStruct((t, d), F32),
        scratch_shapes=[pltpu.VMEM((tm, d), BF16)],
        compiler_params=_cparams(("parallel", "arbitrary")),
        name="swiglu_dense",
    )(x, mod, w13b, w13b, w2.astype(BF16), ln_g.reshape(1, d), ln_b.reshape(1, d))


MOE_GROUP = 2


def _moe_ffn_kernel(ge_ref, gr_ref, h_ref, w1_ref, w3_ref, w2_ref, o_ref, acc_ref, *, sub):
    g = pl.program_id(0)
    j = pl.program_id(1)
    last = pl.num_programs(1) - 1
    n_rows = gr_ref[g]
    nv = (n_rows + EXPERT_BLOCK - 1) // EXPERT_BLOCK

    def run(n_blocks, phase):
        subs = [slice(s, s + sub) for s in range(0, n_blocks * EXPERT_BLOCK, sub)]
        row = lax.broadcasted_iota(jnp.int32, (sub, 1), 0)

        def rows_in(rows):
            w = jnp.where(row + rows.start < n_rows, h_ref[rows, :], jnp.uint32(0))
            return _unpack_bf16_pair(w).astype(BF16)

        def emit(s, part):
            if phase == "first":
                acc_ref[subs[s], :] = part
            elif phase == "middle":
                acc_ref[subs[s], :] += part
            else:
                o_ref[subs[s], :] = _pack_bf16_pair(acc_ref[subs[s], :] + part)

        _swiglu_pipelined([rows_in(rows) for rows in subs], w1_ref[0, 0].astype(BF16), w3_ref[0, 0].astype(BF16),
                          w2_ref[0, 0].astype(BF16), emit)

    for n_blocks in range(1, MOE_GROUP + 1):
        @pl.when(jnp.logical_and(nv == n_blocks, j == 0))
        def _():
            run(n_blocks, "first")

        @pl.when(jnp.logical_and(nv == n_blocks, jnp.logical_and(j > 0, j < last)))
        def _():
            run(n_blocks, "middle")

        @pl.when(jnp.logical_and(nv == n_blocks, j == last))
        def _():
            run(n_blocks, "last")

    for blk in range(MOE_GROUP):
        @pl.when(jnp.logical_and(blk >= nv, j == last))
        def _():
            o_ref[blk * EXPERT_BLOCK:(blk + 1) * EXPERT_BLOCK, :] = jnp.zeros((EXPERT_BLOCK, o_ref.shape[1]),
                                                                              jnp.uint32)


def _moe_ffn(h_slots, group_expert, group_rows, w13, w2, layer, *, tf=512):
    cap, half = h_slots.shape
    d = 2 * half
    f = w2.shape[2]
    nf = f // tf
    assert nf >= 2
    tm = MOE_GROUP * EXPERT_BLOCK

    def jj(g, j, gr):
        return jnp.where(gr[g] > 0, j, nf - 1)

    grid_spec = pltpu.PrefetchScalarGridSpec(
        num_scalar_prefetch=2,
        grid=(cap // tm, nf),
        in_specs=[
            pl.BlockSpec((tm, half), lambda g, j, ge, gr: (g, 0)),
            pl.BlockSpec((1, 1, d, tf), lambda g, j, ge, gr: (layer, ge[g], 0, jj(g, j, gr))),
            pl.BlockSpec((1, 1, d, tf), lambda g, j, ge, gr: (layer, ge[g], 0, nf + jj(g, j, gr))),
            pl.BlockSpec((1, 1, tf, d), lambda g, j, ge, gr: (layer, ge[g], jj(g, j, gr), 0)),
        ],
        out_specs=pl.BlockSpec((tm, half), lambda g, j, ge, gr: (g, 0)),
        scratch_shapes=[pltpu.VMEM((tm, d), F32)],
    )
    return pl.pallas_call(
        functools.partial(_moe_ffn_kernel, sub=EXPERT_BLOCK),
        grid_spec=grid_spec,
        out_shape=jax.ShapeDtypeStruct((cap, half), jnp.uint32),
        compiler_params=_cparams(("parallel", "arbitrary")),
        name="swiglu_experts",
    )(group_expert, group_rows, h_slots, w13, w13, w2)


def _route_rows(x_rows, m, wr_ref, br_ref, carry_ref, n_exp):
    tm = x_rows.shape[0]
    h = _modulate(x_rows, m, 3)
    h_hi, h_mid, _ = _split3(h)
    w = wr_ref[...]
    w_hi = w.astype(BF16)
    w_lo = (w - w_hi.astype(F32)).astype(BF16)
    logits = _dot(h_hi, w_hi) + _dot(h_mid, w_hi) + _dot(h_hi, w_lo) + br_ref[...]
    lane = lax.broadcasted_iota(jnp.int32, (tm, LANES), 1)
    neg = jnp.float32(-jnp.inf)
    logits = jnp.where(lane < n_exp, logits, neg)
    v0 = jnp.max(logits, axis=1, keepdims=True)
    e0 = jnp.min(jnp.where(logits == v0, lane, LANES), axis=1, keepdims=True)
    rest = jnp.where(lane == e0, neg, logits)
    v1 = jnp.max(rest, axis=1, keepdims=True)
    e1 = jnp.min(jnp.where(rest == v1, lane, LANES), axis=1, keepdims=True)
    p = jnp.exp(v1 - v0)
    g0 = 1.0 / (1.0 + p)
    g1 = p / (1.0 + p)
    oh0 = lane == e0
    oh1 = lane == e1
    onehot = jnp.where(jnp.logical_or(oh0, oh1), 1.0, 0.0).astype(BF16)
    r = lax.broadcasted_iota(jnp.int32, (tm, tm), 0)
    c = lax.broadcasted_iota(jnp.int32, (tm, tm), 1)
    strict = jnp.where(r > c, 1.0, 0.0).astype(BF16)
    before = _dot(strict, onehot) + carry_ref[0:1, :]
    rank0 = jnp.sum(jnp.where(oh0, before, 0.0), axis=1, keepdims=True)
    rank1 = jnp.sum(jnp.where(oh1, before, 0.0), axis=1, keepdims=True)
    total = carry_ref[0:1, :] + jnp.sum(onehot.astype(F32), axis=0, keepdims=True)
    carry_ref[...] = jnp.broadcast_to(total, carry_ref.shape)
    meta = jnp.where(lane == 0, e0.astype(F32), 0.0)
    meta = jnp.where(lane == 1, e1.astype(F32), meta)
    meta = jnp.where(lane == 2, g0, meta)
    meta = jnp.where(lane == 3, g1, meta)
    meta = jnp.where(lane == 4, rank0, meta)
    meta = jnp.where(lane == 5, rank1, meta)
    return _pack_bf16_pair(h), meta


def _router_operands(w_router, b_router):
    d, n_exp = w_router.shape
    wr = jnp.zeros((d, LANES), F32).at[:, :n_exp].set(w_router)
    br = jnp.zeros((1, LANES), F32).at[0, :n_exp].set(b_router)
    return wr, br


def _router_out(t, d, tm, row_map):
    specs = [pl.BlockSpec((tm, d // 2), row_map), pl.BlockSpec((tm, LANES), row_map),
             pl.BlockSpec((8, LANES), lambda *_: (0, 0))]
    shapes = [jax.ShapeDtypeStruct((t, d // 2), jnp.uint32), jax.ShapeDtypeStruct((t, LANES), F32),
              jax.ShapeDtypeStruct((8, LANES), F32)]
    return specs, shapes


def _combine_kernel(x_ref, ya_ref, yb_ref, meta_ref, mod_ref, g_ref, b_ref, o_ref, *, alpha):
    meta = meta_ref[...]
    y = meta[:, 2:3] * _unpack_bf16_pair(ya_ref[...]) + meta[:, 3:4] * _unpack_bf16_pair(yb_ref[...])
    o_ref[...] = _res_ln(x_ref[...], y, mod_ref[0], 3, g_ref[...], b_ref[...], alpha)


def _combine(x, ya, yb, meta, mod, ln_g, ln_b, *, seq, alpha, tm=512):
    t, d = x.shape
    row = pl.BlockSpec((tm, d), lambda i: (i, 0))
    packed = pl.BlockSpec((tm, d // 2), lambda i: (i, 0))
    return pl.pallas_call(
        functools.partial(_combine_kernel, alpha=alpha),
        grid=(t // tm,),
        in_specs=[row, packed, packed,
                  pl.BlockSpec((tm, LANES), lambda i: (i, 0)),
                  pl.BlockSpec((1, 6, d), lambda i: (i * tm // seq, 0, 0)),
                  _const_spec((1, d)), _const_spec((1, d))],
        out_specs=row,
        out_shape=jax.ShapeDtypeStruct((t, d), F32),
        compiler_params=_cparams(("parallel",)),
        name="moe_combine",
    )(x, ya, yb, meta, mod, ln_g.reshape(1, d), ln_b.reshape(1, d))


def _sc_mesh():
    return plsc.VectorSubcoreMesh(core_axis_name="c", subcore_axis_name="s", num_cores=SC_CORES,
                                  num_subcores=SC_SUBCORES)


def _sc_worker_share(n):
    workers = SC_CORES * SC_SUBCORES
    per_worker = n // workers
    steps = per_worker // SC_ROWS
    assert steps * SC_ROWS * workers == n
    return per_worker, steps


def _sc_gather_rows(table, idx):
    n = idx.shape[0]
    width = table.shape[1]
    per_worker, steps = _sc_worker_share(n)

    def body(table_hbm, idx_hbm, out_hbm, idx_v, rows_v, sem):
        wid = lax.axis_index("s") * SC_CORES + lax.axis_index("c")

        @pl.loop(0, steps)
        def _(j):
            off = pl.multiple_of(wid * per_worker + j * SC_ROWS, SC_ROWS)
            pltpu.sync_copy(idx_hbm.at[pl.ds(off, SC_ROWS)], idx_v)
            pltpu.async_copy(table_hbm.at[idx_v], rows_v, sem).wait()
            pltpu.sync_copy(rows_v, out_hbm.at[pl.ds(off, SC_ROWS)])

    return pl.kernel(
        body,
        out_type=jax.ShapeDtypeStruct((n, width), table.dtype),
        mesh=_sc_mesh(),
        scratch_types=[
            pltpu.VMEM((SC_ROWS,), jnp.int32),
            pltpu.VMEM((SC_ROWS, width), table.dtype),
            pltpu.SemaphoreType.DMA,
        ],
        name="sc_row_gather",
    )(table, idx)


def _sc_scatter_rows(rows, dest_a, dest_b, n_out):
    n, width = rows.shape
    per_worker, steps = _sc_worker_share(n)

    def body(rows_hbm, da_hbm, db_hbm, out_hbm, ia_v, ib_v, rows_v, sem):
        wid = lax.axis_index("s") * SC_CORES + lax.axis_index("c")

        @pl.loop(0, steps)
        def _(j):
            off = pl.multiple_of(wid * per_worker + j * SC_ROWS, SC_ROWS)
            pltpu.sync_copy(da_hbm.at[pl.ds(off, SC_ROWS)], ia_v)
            pltpu.sync_copy(db_hbm.at[pl.ds(off, SC_ROWS)], ib_v)
            pltpu.sync_copy(rows_hbm.at[pl.ds(off, SC_ROWS)], rows_v)
            pltpu.async_copy(rows_v, out_hbm.at[ia_v], sem).wait()
            pltpu.async_copy(rows_v, out_hbm.at[ib_v], sem).wait()

    return pl.kernel(
        body,
        out_type=jax.ShapeDtypeStruct((n_out, width), rows.dtype),
        mesh=_sc_mesh(),
        scratch_types=[
            pltpu.VMEM((SC_ROWS,), jnp.int32),
            pltpu.VMEM((SC_ROWS,), jnp.int32),
            pltpu.VMEM((SC_ROWS, width), rows.dtype),
            pltpu.SemaphoreType.DMA,
        ],
        name="sc_row_scatter",
    )(rows, dest_a, dest_b)


def _moe_layer(routed, mod, w13, w2, layer, ln_g, ln_b, *, seq, alpha):
    x, h, meta, cnt = routed
    t, d = x.shape
    n_exp = w13.shape[1]
    e0 = meta[:, 0].astype(jnp.int32)
    e1 = meta[:, 1].astype(jnp.int32)
    rank0 = meta[:, 4].astype(jnp.int32)
    rank1 = meta[:, 5].astype(jnp.int32)
    counts = cnt[0, :n_exp].astype(jnp.int32)
    group_rows = MOE_GROUP * EXPERT_BLOCK
    padded = (counts + group_rows - 1) // group_rows * group_rows
    pad_end = jnp.cumsum(padded)
    pad_start = pad_end - padded
    dest0 = pad_start[e0] + rank0
    dest1 = pad_start[e1] + rank1
    cap = t * TOP_K + n_exp * group_rows
    group_start = jnp.arange(cap // group_rows, dtype=jnp.int32) * group_rows
    group_expert = jnp.minimum(jnp.sum(pad_end[None, :] <= group_start[:, None], axis=1), n_exp - 1).astype(jnp.int32)
    rows_left = counts[group_expert] - (group_start - pad_start[group_expert])
    rows_in_group = jnp.clip(rows_left, 0, group_rows).astype(jnp.int32)
    h_slots = _sc_scatter_rows(h, dest0, dest1, cap)
    y_slots = _moe_ffn(h_slots, group_expert, rows_in_group, w13, w2, layer)
    ya = _sc_gather_rows(y_slots, dest0)
    yb = _sc_gather_rows(y_slots, dest1)
    return _combine(x, ya, yb, meta, mod, ln_g, ln_b, seq=seq, alpha=alpha)


HEAD_PAD = LANES
GATE_PARTS = 3
GATE_ROWS = 16
VAL_PAD = 16
LOG2E = 1.4426950408889634


def _head_lanes(h, hd):
    k0 = (h % (HEAD_PAD // hd)) * hd
    return k0, (hd if k0 == 0 else 0)


def _fox_proj_kernel(x_ref, mod_ref, wf_ref, bf_ref, wq_ref, wk_ref, wv_ref, selk_ref, onek_ref, selq_ref,
                     oneq_ref, onev_ref, k_ref, q_ref, v_ref, carry_ref, *, q_scale, heads, hd, nb):
    tm = x_ref.shape[0]

    @pl.when(pl.program_id(0) % nb == 0)
    def _():
        carry_ref[...] = jnp.zeros_like(carry_ref)

    h_hi, h_mid, _ = _split3(_modulate(x_ref[...], mod_ref[0], 0))

    wf = wf_ref[...]
    wf_hi = wf.astype(BF16)
    wf_lo = (wf - wf_hi.astype(F32)).astype(BF16)
    f = _dot(h_hi, wf_hi) + _dot(h_mid, wf_hi) + _dot(h_hi, wf_lo) + bf_ref[...]
    lf = jnp.minimum(f, 0.0) - jnp.log(1.0 + jnp.exp(-jnp.abs(f)))
    r = lax.broadcasted_iota(jnp.int32, (tm, tm), 0)
    c = lax.broadcasted_iota(jnp.int32, (tm, tm), 1)
    tri = jnp.where(r >= c, 1.0, 0.0).astype(BF16)
    cs = carry_ref[0:1, :]
    for part in _split3(lf):
        cs = cs + _dot(tri, part)
    carry_ref[...] = jnp.broadcast_to(cs[tm - 1:tm, :], carry_ref.shape)
    cs = cs * LOG2E

    gate_k = onek_ref[...]
    for n, part in enumerate(_split3(cs)):
        gate_k = gate_k + _dot(part, selk_ref[n])
    kc = _dot(h_hi, wk_ref[...])
    lane = lax.broadcasted_iota(jnp.int32, (tm, HEAD_PAD), 1)
    for h in range(heads):
        k0, _ = _head_lanes(h, hd)
        src = (h * hd // HEAD_PAD) * HEAD_PAD
        is_k = jnp.logical_and(lane >= k0, lane < k0 + hd)
        hs = slice(h * HEAD_PAD, (h + 1) * HEAD_PAD)
        k_ref[:, hs] = jnp.where(is_k, kc[:, src:src + HEAD_PAD], gate_k[:, hs]).astype(BF16)

    nt = (((1,), (1,)), ((), ()))
    qc = lax.dot_general(wq_ref[...], h_hi, nt, preferred_element_type=F32) * (q_scale * LOG2E)
    gate_q = oneq_ref[...]
    for n, part in enumerate(_split3(cs.T)):
        gate_q = gate_q + _dot(selq_ref[n], part)
    fill = HEAD_PAD - hd - GATE_ROWS
    for h in range(heads):
        k0, g0 = _head_lanes(h, hd)
        base = h * HEAD_PAD
        q_ref[0, base + k0:base + k0 + hd, :] = qc[h * hd:(h + 1) * hd, :].astype(BF16)
        q_ref[0, base + g0:base + g0 + GATE_ROWS, :] = gate_q[h * GATE_ROWS:(h + 1) * GATE_ROWS, :].astype(BF16)
        q_ref[0, base + g0 + GATE_ROWS:base + g0 + GATE_ROWS + fill, :] = jnp.zeros((fill, tm), BF16)

    v = lax.dot_general(wv_ref[...], h_hi, nt, preferred_element_type=F32) + onev_ref[...]
    v_ref[0] = v.astype(BF16)


def _fox_proj(x, mod, w_in, b_f, *, bsz, seq, heads, tm=512):
    t, d = x.shape
    hd = d // heads
    assert HEAD_PAD == 2 * hd and GATE_ROWS >= 2 * GATE_PARTS
    nb = seq // tm
    wide = heads * HEAD_PAD
    wq_t = w_in[:, :d].T.astype(BF16)
    wk = w_in[:, d:2 * d].astype(BF16)
    vrows = heads * (hd + VAL_PAD)
    wv_t = jnp.pad(w_in[:, 2 * d:3 * d].reshape(d, heads, hd),
                   ((0, 0), (0, 0), (0, VAL_PAD))).reshape(d, vrows).T.astype(BF16)
    wf = jnp.zeros((d, LANES), F32).at[:, :heads].set(w_in[:, 3 * d:])
    bf = jnp.zeros((1, LANES), F32).at[0, :heads].set(b_f)
    sel_k = np.zeros((GATE_PARTS, LANES, wide), np.float32)
    one_k = np.zeros((1, wide), np.float32)
    sel_q = np.zeros((GATE_PARTS, heads * GATE_ROWS, LANES), np.float32)
    one_q = np.zeros((heads * GATE_ROWS, 1), np.float32)
    one_v = np.zeros((vrows, 1), np.float32)
    for hh in range(heads):
        _, g0 = _head_lanes(hh, hd)
        one_v[hh * (hd + VAL_PAD) + hd, 0] = 1.0
        for n in range(GATE_PARTS):
            sel_k[n, hh, hh * HEAD_PAD + g0 + n] = -1.0
            one_q[hh * GATE_ROWS + n, 0] = 1.0
            one_k[0, hh * HEAD_PAD + g0 + GATE_PARTS + n] = 1.0
            sel_q[n, hh * GATE_ROWS + GATE_PARTS + n, hh] = 1.0
    return pl.pallas_call(
        functools.partial(_fox_proj_kernel, q_scale=hd ** -0.5, heads=heads, hd=hd, nb=nb),
        grid=(t // tm,),
        in_specs=[
            pl.BlockSpec((tm, d), lambda i: (i, 0)),
            pl.BlockSpec((1, 6, d), lambda i: (i * tm // seq, 0, 0)),
            _const_spec((d, LANES)),
            _const_spec((1, LANES)),
            _const_spec((d, d)),
            _const_spec((d, d)),
            _const_spec((vrows, d)),
            _const_spec((GATE_PARTS, LANES, wide)),
            _const_spec((1, wide)),
            _const_spec((GATE_PARTS, heads * GATE_ROWS, LANES)),
            _const_spec((heads * GATE_ROWS, 1)),
            _const_spec((vrows, 1)),
        ],
        out_specs=[
            pl.BlockSpec((tm, wide), lambda i: (i, 0)),
            pl.BlockSpec((1, wide, tm), lambda i: (i // nb, 0, i % nb)),
            pl.BlockSpec((1, vrows, tm), lambda i: (i // nb, 0, i % nb)),
        ],
        out_shape=[
            jax.ShapeDtypeStruct((t, wide), BF16),
            jax.ShapeDtypeStruct((bsz, wide, seq), BF16),
            jax.ShapeDtypeStruct((bsz, vrows, seq), BF16),
        ],
        scratch_shapes=[pltpu.VMEM((8, LANES), F32)],
        compiler_params=_cparams(("arbitrary",)),
        name="fox_proj",
    )(x, mod, wf, bf, wq_t, wk, wv_t, jnp.asarray(sel_k, BF16), jnp.asarray(one_k), jnp.asarray(sel_q, BF16),
      jnp.asarray(one_q), jnp.asarray(one_v))


def _flash_kernel(qi_ref, kj_ref, k_ref, q_ref, v_ref, x_ref, mod_ref, wout_ref, g_ref, b_ref, *rest,
                  heads, hd, blk, alpha, n_exp):
    p_idx = pl.program_id(1)
    if n_exp:
        wr_ref, br_ref, o_ref, hp_ref, meta_ref, cnt_ref, m_ref, acc_ref, s_ref, carry_ref = rest

        @pl.when(jnp.logical_and(pl.program_id(0) == 0, p_idx == 0))
        def _():
            carry_ref[...] = jnp.zeros_like(carry_ref)
    else:
        o_ref, m_ref, acc_ref, s_ref = rest
    qi = qi_ref[p_idx]
    kj = kj_ref[p_idx]
    vp = hd + VAL_PAD

    @pl.when(kj == 0)
    def _():
        m_ref[...] = jnp.full(m_ref.shape, -jnp.inf, F32)
        acc_ref[...] = jnp.zeros_like(acc_ref)

    def step(masked):
        if masked:
            kpos = lax.broadcasted_iota(jnp.int32, (blk, blk), 0)
            qpos = lax.broadcasted_iota(jnp.int32, (blk, blk), 1)
            keep = kpos <= qpos

        def scores(h):
            hs = slice(h * HEAD_PAD, (h + 1) * HEAD_PAD)
            s = _dot(k_ref[:, hs], q_ref[0, hs, :])
            if masked:
                s = jnp.where(keep, s, -jnp.inf)
            s_ref[h % depth] = s
            return jnp.max(s, axis=0, keepdims=True)

        depth = s_ref.shape[0]
        ahead = [scores(h) for h in range(depth - 1)]
        for h in range(heads):
            if h + depth - 1 < heads:
                ahead.append(scores(h + depth - 1))
            m_cur = ahead.pop(0)
            m_prev = m_ref[h:h + 1, :]
            m_new = jnp.maximum(m_prev, m_cur)
            m_ref[h:h + 1, :] = m_new
            a = jnp.exp2(m_prev - m_new)
            p = jnp.exp2(s_ref[h % depth] - m_new).astype(BF16)
            rs = slice(h * vp, (h + 1) * vp)
            acc_ref[rs, :] = acc_ref[rs, :] * a + _dot(v_ref[0, rs, :], p)

    @pl.when(kj < qi)
    def _():
        step(False)

    @pl.when(kj == qi)
    def _():
        step(True)
        per = LANES // hd
        cols = []
        for g in range(heads // per):
            parts = []
            for n in range(per):
                r0 = (g * per + n) * vp
                parts.append(acc_ref[r0:r0 + hd, :] / acc_ref[r0 + hd:r0 + hd + 1, :])
            cols.append(jnp.concatenate(parts, axis=0).T.astype(BF16))
        y = _dot(jnp.concatenate(cols, axis=1), wout_ref[...])
        x_new = _res_ln(x_ref[...], y, mod_ref[0], 0, g_ref[...], b_ref[...], alpha)
        o_ref[...] = x_new
        if n_exp:
            hp_ref[...], meta_ref[...] = _route_rows(x_new, mod_ref[0], wr_ref, br_ref, carry_ref, n_exp)
            cnt_ref[...] = carry_ref[...]


def _flash(k_aug, q_aug_t, v_t, x, mod, w_out, ln_g, ln_b, router=None, *, bsz, seq, heads, alpha, blk=512):
    t, wide = k_aug.shape
    vrows = v_t.shape[1]
    hd = vrows // heads - VAL_PAD
    d = heads * hd
    nq = seq // blk
    pairs = [(i, j) for i in range(nq) for j in range(i + 1)]
    qi = jnp.asarray(np.array([p[0] for p in pairs], np.int32))
    kj = jnp.asarray(np.array([p[1] for p in pairs], np.int32))
    n_exp = router[0].shape[1] if router else 0
    row_map = lambda b, p, qi, kj: (b * nq + qi[p], 0)
    const = lambda shape: pl.BlockSpec(shape, lambda b, p, qi, kj: (0, 0), pipeline_mode=pl.Buffered(1))
    out_specs, out_shape = [pl.BlockSpec((blk, d), row_map)], [jax.ShapeDtypeStruct((t, d), F32)]
    scratch = [pltpu.VMEM((heads, blk), F32), pltpu.VMEM((vrows, blk), F32), pltpu.VMEM((3, blk, blk), F32)]
    extra_specs, extra_args = [], []
    if router:
        r_specs, r_shapes = _router_out(t, d, blk, row_map)
        out_specs, out_shape = out_specs + r_specs, out_shape + r_shapes
        scratch.append(pltpu.VMEM((8, LANES), F32))
        extra_specs = [const((d, LANES)), const((1, LANES))]
        extra_args = list(_router_operands(*router))
    grid_spec = pltpu.PrefetchScalarGridSpec(
        num_scalar_prefetch=2,
        grid=(bsz, len(pairs)),
        in_specs=[
            pl.BlockSpec((blk, wide), lambda b, p, qi, kj: (b * nq + kj[p], 0)),
            pl.BlockSpec((1, wide, blk), lambda b, p, qi, kj: (b, 0, qi[p])),
            pl.BlockSpec((1, vrows, blk), lambda b, p, qi, kj: (b, 0, kj[p])),
            pl.BlockSpec((blk, d), row_map),
            pl.BlockSpec((1, 6, d), lambda b, p, qi, kj: (b, 0, 0)),
            const((d, d)),
            const((1, d)),
            const((1, d)),
        ] + extra_specs,
        out_specs=out_specs,
        scratch_shapes=scratch,
    )
    outs = pl.pallas_call(
        functools.partial(_flash_kernel, heads=heads, hd=hd, blk=blk, alpha=alpha, n_exp=n_exp),
        grid_spec=grid_spec,
        out_shape=out_shape,
        compiler_params=_cparams(("arbitrary", "arbitrary")),
        name="fox_attention",
    )(qi, kj, k_aug, q_aug_t, v_t, x, mod, w_out.astype(BF16), ln_g.reshape(1, d), ln_b.reshape(1, d), *extra_args)
    return outs if router else outs[0]


def _fox_layer(x, mod, w_in, b_f, w_out, ln_g, ln_b, router=None, *, bsz, seq, alpha):
    heads = b_f.shape[0]
    k_aug, q_aug_t, v_t = _fox_proj(x, mod, w_in, b_f, bsz=bsz, seq=seq, heads=heads)
    return _flash(k_aug, q_aug_t, v_t, x, mod, w_out, ln_g, ln_b, router, bsz=bsz, seq=seq, heads=heads,
                  alpha=alpha)


def _pool_kernel(x_ref, xh_ref, mod_ref, win_ref, wg_ref, ls_ref, wout_ref, g_ref, b_ref, o_ref, pooled_ref,
                 *, tm, seq, alpha):
    i = pl.program_id(0)
    pos0 = (i * tm) % seq
    m = mod_ref[0]
    x = x_ref[...]
    halo_ok = jnp.where(pos0 > 0, jnp.float32(1.0), jnp.float32(0.0))
    xe = jnp.concatenate([xh_ref[...], x], axis=0)
    ze = _dot(_modulate(xe, m, 0).astype(BF16), win_ref[...])
    rows = lax.broadcasted_iota(jnp.int32, (tm + POOL_HALO, 1), 0)
    ze = ze * jnp.where(rows < POOL_HALO, halo_ok, jnp.float32(1.0))
    pos = (pos0 + lax.broadcasted_iota(jnp.int32, (tm, 1), 0) + 1).astype(F32)
    gd = ze.shape[1] // len(POOL_WINDOWS)
    for g, win in enumerate(POOL_WINDOWS):
        cs = slice(g * gd, (g + 1) * gd)
        zg = ze[:, cs]
        s = zg
        span = 1
        while span < win:
            s = s + pltpu.roll(s, span, 0)
            span *= 2
        count = jnp.minimum(pos, jnp.float32(win))
        pooled = s[POOL_HALO:, :] / count - zg[POOL_HALO:, :]
        pooled_ref[:, cs] = (_dot(pooled.astype(BF16), wg_ref[g]) * ls_ref[:, cs]).astype(BF16)
    y = _dot(pooled_ref[...], wout_ref[...])
    o_ref[...] = _res_ln(x, y, m, 0, g_ref[...], b_ref[...], alpha)


def _pool_layer(x, mod, w_in, w_grp, scale, w_out, ln_g, ln_b, *, seq, alpha, tm=512):
    t, d = x.shape
    ng, gd, _ = w_grp.shape
    hb = tm // POOL_HALO
    return pl.pallas_call(
        functools.partial(_pool_kernel, tm=tm, seq=seq, alpha=alpha),
        grid=(t // tm,),
        in_specs=[
            pl.BlockSpec((tm, d), lambda i: (i, 0)),
            pl.BlockSpec((POOL_HALO, d), lambda i: (jnp.maximum(i * hb - 1, 0), 0)),
            pl.BlockSpec((1, 6, d), lambda i: (i * tm // seq, 0, 0)),
            _const_spec((d, d)),
            _const_spec((ng, gd, gd)),
            _const_spec((1, d)),
            _const_spec((d, d)),
            _const_spec((1, d)),
            _const_spec((1, d)),
        ],
        out_specs=pl.BlockSpec((tm, d), lambda i: (i, 0)),
        out_shape=jax.ShapeDtypeStruct((t, d), F32),
        scratch_shapes=[pltpu.VMEM((tm, d), BF16)],
        compiler_params=_cparams(("parallel",)),
        name="pool_mixer",
    )(x, x, mod, w_in.astype(BF16), w_grp.astype(BF16), scale.reshape(1, d), w_out.astype(BF16),
      ln_g.reshape(1, d), ln_b.reshape(1, d))


def kernel(x, c, mod_w, mod_b, ln_g, ln_b, gm_w_in, gm_ln_g, gm_ln_b, gm_w_s, gm_b_s, gm_w_out, fox_w_in,
           fox_b_f, fox_w_out, pool_w_in, pool_w_grp, pool_scale, pool_w_out, ffn_w13, ffn_w2, moe_w_router,
           moe_b_router, moe_w13, moe_w2):
    bsz, seq, d = x.shape
    depth = mod_w.shape[0]
    alpha = (2 * depth) ** 0.25
    n_mixers = 3
    mod_all = _modulation(c, mod_w, mod_b)
    xt = x.reshape(bsz * seq, d)
    for i in range(depth):
        mod = mod_all[i]
        kind, j = i % n_mixers, i // n_mixers
        router = (moe_w_router[i // 2], moe_b_router[i // 2]) if i % 2 else None
        if kind == 0:
            xt = _gmlp_layer(xt, mod, gm_w_in[j], gm_ln_g[j], gm_ln_b[j], gm_w_s[j], gm_b_s[j], gm_w_out[j],
                             ln_g[i, 0], ln_b[i, 0], router, seq=seq, alpha=alpha)
        elif kind == 1:
            xt = _fox_layer(xt, mod, fox_w_in[j], fox_b_f[j], fox_w_out[j], ln_g[i, 0], ln_b[i, 0], router,
                            bsz=bsz, seq=seq, alpha=alpha)
        else:
            assert router is None, "the pooling mixer has no routing epilogue"
            xt = _pool_layer(xt, mod, pool_w_in[j], pool_w_grp[j], pool_scale[j], pool_w_out[j],
                             ln_g[i, 0], ln_b[i, 0], seq=seq, alpha=alpha)
        if router is None:
            xt = _ffn_layer(xt, mod, ffn_w13[i // 2], ffn_w2[i // 2], ln_g[i, 1], ln_b[i, 1],
                            seq=seq, alpha=alpha)
        else:
            xt = _moe_layer(xt, mod, moe_w13, moe_w2, i // 2, ln_g[i, 1], ln_b[i, 1], seq=seq, alpha=alpha)
    return xt.reshape(bsz, seq, d)
```

```python
import functools

import numpy as np
import jax
import jax.numpy as jnp
from jax import lax
from jax.experimental import pallas as pl
from jax.experimental.pallas import tpu as pltpu
from jax.experimental.pallas import tpu_sc as plsc

F32 = jnp.float32
BF16 = jnp.bfloat16

POOL_WINDOWS = (2, 4, 8, 16)
TOP_K = 2
EXPERT_BLOCK = 512
LN_EPS = 1e-5
LANES = 128
POOL_HALO = 16
VMEM_LIMIT = 56 * 1024 * 1024
SC_CORES = 2
SC_SUBCORES = 16
SC_ROWS = 128


def _cparams(sem):
    return pltpu.CompilerParams(dimension_semantics=sem, vmem_limit_bytes=VMEM_LIMIT)


def _const_spec(shape):
    nd = len(shape)
    return pl.BlockSpec(shape, lambda *_: (0,) * nd, pipeline_mode=pl.Buffered(1))


def _layer_norm(r, g, b):
    mu = jnp.mean(r, axis=-1, keepdims=True)
    xc = r - mu
    var = jnp.mean(xc * xc, axis=-1, keepdims=True)
    return xc * lax.rsqrt(var + LN_EPS) * g + b


def _modulate(x, m, off):
    return x * (1.0 + m[off + 1:off + 2]) + m[off:off + 1]


def _res_ln(x, y, m, off, g, b, alpha):
    return _layer_norm(alpha * x + (1.0 + m[off + 2:off + 3]) * y, g, b)


def _split3(a):
    hi = a.astype(BF16)
    r1 = a - hi.astype(F32)
    mid = r1.astype(BF16)
    lo = (r1 - mid.astype(F32)).astype(BF16)
    return hi, mid, lo


def _dot(a, b):
    return jnp.dot(a, b, preferred_element_type=F32)


def _dot_split(a_hi, a_mid, w):
    n = w.shape[1]
    w_hi = w.astype(BF16)
    w_lo = (w - w_hi.astype(F32)).astype(BF16)
    both = _dot(a_hi, jnp.concatenate([w_hi, w_lo], axis=1))
    return both[:, :n] + both[:, n:] + _dot(a_mid, w_hi)


def _pack_bf16_pair(x):
    n = x.shape[1] // 2
    hi = lax.bitcast_convert_type(x[:, :n].astype(BF16).astype(F32), jnp.uint32)
    lo = lax.bitcast_convert_type(x[:, n:].astype(BF16).astype(F32), jnp.uint32)
    return hi | (lo >> 16)


def _unpack_bf16_pair(w):
    hi = lax.bitcast_convert_type(w & jnp.uint32(0xFFFF0000), F32)
    lo = lax.bitcast_convert_type(w << 16, F32)
    return jnp.concatenate([hi, lo], axis=1)


def _mod_kernel(c_ref, w_ref, b_ref, o_ref):
    c = c_ref[...]
    s = c / (1.0 + jnp.exp(-c))
    s_hi, s_mid, _ = _split3(s)
    w = w_ref[0]
    w_hi = w.astype(BF16)
    w_lo = (w - w_hi.astype(F32)).astype(BF16)
    acc = _dot(s_hi, w_hi) + _dot(s_mid, w_hi) + _dot(s_hi, w_lo)
    o_ref[0] = acc + b_ref[0]


def _modulation(c, mod_w, mod_b):
    depth, d, n = mod_w.shape
    bsz = c.shape[0]
    rows = 8
    tn = n // 4
    c_pad = jnp.zeros((rows, d), F32).at[:bsz].set(c)
    out = pl.pallas_call(
        _mod_kernel,
        grid=(depth, n // tn),
        in_specs=[
            pl.BlockSpec((rows, d), lambda l, j: (0, 0)),
            pl.BlockSpec((1, d, tn), lambda l, j: (l, 0, j)),
            pl.BlockSpec((1, 1, tn), lambda l, j: (l, 0, j)),
        ],
        out_specs=pl.BlockSpec((1, rows, tn), lambda l, j: (l, 0, j)),
        out_shape=jax.ShapeDtypeStruct((depth, rows, n), F32),
        compiler_params=_cparams(("parallel", "parallel")),
        name="adaln_mod",
    )(c_pad, mod_w, mod_b.reshape(depth, 1, n))
    return out[:, :bsz, :].reshape(depth, bsz, 6, d)


def _gmlp_kernel(x_ref, mod_ref, win_ref, vg_ref, vb_ref, ws_ref, bst_ref, wout_ref, g_ref, b_ref,
                 *rest, tm, sub, chunk, groups, alpha, n_exp):
    if n_exp:
        wr_ref, br_ref, o_ref, hp_ref, meta_ref, cnt_ref, gated_ref, carry_ref = rest

        @pl.when(pl.program_id(0) == 0)
        def _():
            carry_ref[...] = jnp.zeros_like(carry_ref)
    else:
        o_ref, gated_ref = rest
    m = mod_ref[0]
    row = lax.broadcasted_iota(jnp.int32, (chunk, chunk), 0)
    col = lax.broadcasted_iota(jnp.int32, (chunk, chunk), 1)
    causal = row >= col
    bst = bst_ref[...]
    w_mix = [jnp.where(causal, ws_ref[g], 0.0).astype(BF16) for g in range(groups)]
    n = tm // sub

    def project(s):
        x = x_ref[s * sub:(s + 1) * sub, :]
        return x, _dot(_modulate(x, m, 0).astype(BF16), win_ref[...])

    def activate(z):
        z = 0.5 * z * (1.0 + lax.erf(z * (2.0 ** -0.5)))
        width = z.shape[1] // 2
        return z[:, :width], _layer_norm(z[:, width:], vg_ref[...], vb_ref[...]).astype(BF16)

    def mix(s, u, v):
        gd = u.shape[1] // groups
        for g in range(groups):
            cs = slice(g * gd, (g + 1) * gd)
            for c in range(sub // chunk):
                rs = slice(c * chunk, (c + 1) * chunk)
                mixed = _dot(w_mix[g], v[rs, cs]) + bst[:, g:g + 1]
                gated_ref[s * sub + c * chunk:s * sub + (c + 1) * chunk, cs] = (u[rs, cs] * mixed).astype(BF16)
        return _dot(gated_ref[s * sub:(s + 1) * sub, :], wout_ref[...])

    def finish(s, x, y):
        rows = slice(s * sub, (s + 1) * sub)
        x_new = _res_ln(x, y, m, 0, g_ref[...], b_ref[...], alpha)
        o_ref[rows, :] = x_new
        if n_exp:
            hp_ref[rows, :], meta_ref[rows, :] = _route_rows(x_new, m, wr_ref, br_ref, carry_ref, n_exp)

    nxt = project(0)
    pending = None
    for s in range(n):
        x, z = nxt
        if s + 1 < n:
            nxt = project(s + 1)
        u, v = activate(z)
        if pending is not None:
            finish(*pending)
        pending = (s, x, mix(s, u, v))
    finish(*pending)
    if n_exp:
        cnt_ref[...] = carry_ref[...]


def _gmlp_layer(x, mod, w_in, v_g, v_b, w_s, b_s, w_out, ln_g, ln_b, router=None, *, seq, alpha, tm=512,
                sub=256):
    t, d = x.shape
    groups, chunk, _ = w_s.shape
    width = w_out.shape[0]
    n_exp = router[0].shape[1] if router else 0
    kern = functools.partial(_gmlp_kernel, tm=tm, sub=sub, chunk=chunk, groups=groups, alpha=alpha, n_exp=n_exp)
    row_map = lambda i: (i, 0)
    out_specs, out_shape = [pl.BlockSpec((tm, d), row_map)], [jax.ShapeDtypeStruct((t, d), F32)]
    scratch = [pltpu.VMEM((tm, width), BF16)]
    extra_specs, extra_args = [], []
    if router:
        r_specs, r_shapes = _router_out(t, d, tm, row_map)
        out_specs, out_shape = out_specs + r_specs, out_shape + r_shapes
        scratch.append(pltpu.VMEM((8, LANES), F32))
        extra_specs = [_const_spec((d, LANES)), _const_spec((1, LANES))]
        extra_args = list(_router_operands(*router))
    outs = pl.pallas_call(
        kern,
        grid=(t // tm,),
        in_specs=[
            pl.BlockSpec((tm, d), lambda i: (i, 0)),
            pl.BlockSpec((1, 6, d), lambda i: (i * tm // seq, 0, 0)),
            _const_spec((d, 2 * width)),
            _const_spec((1, width)),
            _const_spec((1, width)),
            _const_spec((groups, chunk, chunk)),
            _const_spec((chunk, groups)),
            _const_spec((width, d)),
            _const_spec((1, d)),
            _const_spec((1, d)),
        ] + extra_specs,
        out_specs=out_specs,
        out_shape=out_shape,
        scratch_shapes=scratch,
        compiler_params=_cparams(("arbitrary",)),
        name="gmlp_mixer",
    )(x, mod, w_in.astype(BF16), v_g.reshape(1, width), v_b.reshape(1, width), w_s, b_s.T,
      w_out.astype(BF16), ln_g.reshape(1, d), ln_b.reshape(1, d), *extra_args)
    return outs if router else outs[0]


def _swiglu_pipelined(hs, w1, w3, w2, emit):
    up = lambda h: (_dot(h, w1), _dot(h, w3))
    nxt = up(hs[0])
    for s in range(len(hs)):
        a, b = nxt
        if s + 1 < len(hs):
            nxt = up(hs[s + 1])
        t = (a / (1.0 + jnp.exp(-a)) * b).astype(BF16)
        emit(s, _dot(t, w2))


def _ffn_kernel(x_ref, mod_ref, w1_ref, w3_ref, w2_ref, g_ref, b_ref, o_ref, h_ref, *, alpha, sub):
    j = pl.program_id(1)
    last = pl.num_programs(1) - 1
    subs = [slice(s, s + sub) for s in range(0, x_ref.shape[0], sub)]
    weights = lambda: (w1_ref[...], w3_ref[...], w2_ref[...])

    @pl.when(j == 0)
    def _():
        hs = []
        for rows in subs:
            h = _modulate(x_ref[rows, :], mod_ref[0], 3).astype(BF16)
            h_ref[rows, :] = h
            hs.append(h)

        def emit(s, part):
            o_ref[subs[s], :] = part

        _swiglu_pipelined(hs, *weights(), emit)

    @pl.when(jnp.logical_and(j > 0, j < last))
    def _():
        def emit(s, part):
            o_ref[subs[s], :] += part

        _swiglu_pipelined([h_ref[rows, :] for rows in subs], *weights(), emit)

    @pl.when(j == last)
    def _():
        def emit(s, part):
            rows = subs[s]
            y = o_ref[rows, :] + part
            o_ref[rows, :] = _res_ln(x_ref[rows, :], y, mod_ref[0], 3, g_ref[...], b_ref[...], alpha)

        _swiglu_pipelined([h_ref[rows, :] for rows in subs], *weights(), emit)


def _ffn_layer(x, mod, w13, w2, ln_g, ln_b, *, seq, alpha, tm=1024, tf=512):
    t, d = x.shape
    f = w2.shape[0]
    nf = f // tf
    assert nf >= 2
    w13b = w13.astype(BF16)
    return pl.pallas_call(
        functools.partial(_ffn_kernel, alpha=alpha, sub=tm // 4),
        grid=(t // tm, nf),
        in_specs=[
            pl.BlockSpec((tm, d), lambda i, j: (i, 0)),
            pl.BlockSpec((1, 6, d), lambda i, j: (i * tm // seq, 0, 0)),
            pl.BlockSpec((d, tf), lambda i, j: (0, j)),
            pl.BlockSpec((d, tf), lambda i, j: (0, nf + j)),
            pl.BlockSpec((tf, d), lambda i, j: (j, 0)),
            _const_spec((1, d)),
            _const_spec((1, d)),
        ],
        out_specs=pl.BlockSpec((tm, d), lambda i, j: (i, 0)),
        out_shape=jax.ShapeDtypeStruct((t, d), F32),
        scratch_shapes=[pltpu.VMEM((tm, d), BF16)],
        compiler_params=_cparams(("parallel", "arbitrary")),
        name="swiglu_dense",
    )(x, mod, w13b, w13b, w2.astype(BF16), ln_g.reshape(1, d), ln_b.reshape(1, d))


MOE_GROUP = 2


def _moe_ffn_kernel(ge_ref, gr_ref, h_ref, w1_ref, w3_ref, w2_ref, o_ref, acc_ref, *, sub):
    g = pl.program_id(0)
    j = pl.program_id(1)
    last = pl.num_programs(1) - 1
    n_rows = gr_ref[g]
    nv = (n_rows + EXPERT_BLOCK - 1) // EXPERT_BLOCK

    def run(n_blocks, phase):
        subs = [slice(s, s + sub) for s in range(0, n_blocks * EXPERT_BLOCK, sub)]
        row = lax.broadcasted_iota(jnp.int32, (sub, 1), 0)

        def rows_in(rows):
            w = jnp.where(row + rows.start < n_rows, h_ref[rows, :], jnp.uint32(0))
            return _unpack_bf16_pair(w).astype(BF16)

        def emit(s, part):
            if phase == "first":
                acc_ref[subs[s], :] = part
            elif phase == "middle":
                acc_ref[subs[s], :] += part
            else:
                o_ref[subs[s], :] = _pack_bf16_pair(acc_ref[subs[s], :] + part)

        _swiglu_pipelined([rows_in(rows) for rows in subs], w1_ref[0, 0].astype(BF16), w3_ref[0, 0].astype(BF16),
                          w2_ref[0, 0].astype(BF16), emit)

    for n_blocks in range(1, MOE_GROUP + 1):
        @pl.when(jnp.logical_and(nv == n_blocks, j == 0))
        def _():
            run(n_blocks, "first")

        @pl.when(jnp.logical_and(nv == n_blocks, jnp.logical_and(j > 0, j < last)))
        def _():
            run(n_blocks, "middle")

        @pl.when(jnp.logical_and(nv == n_blocks, j == last))
        def _():
            run(n_blocks, "last")

    for blk in range(MOE_GROUP):
        @pl.when(jnp.logical_and(blk >= nv, j == last))
        def _():
            o_ref[blk * EXPERT_BLOCK:(blk + 1) * EXPERT_BLOCK, :] = jnp.zeros((EXPERT_BLOCK, o_ref.shape[1]),
                                                                              jnp.uint32)


def _moe_ffn(h_slots, group_expert, group_rows, w13, w2, layer, *, tf=512):
    cap, half = h_slots.shape
    d = 2 * half
    f = w2.shape[2]
    nf = f // tf
    assert nf >= 2
    tm = MOE_GROUP * EXPERT_BLOCK

    def jj(g, j, gr):
        return jnp.where(gr[g] > 0, j, nf - 1)

    grid_spec = pltpu.PrefetchScalarGridSpec(
        num_scalar_prefetch=2,
        grid=(cap // tm, nf),
        in_specs=[
            pl.BlockSpec((tm, half), lambda g, j, ge, gr: (g, 0)),
            pl.BlockSpec((1, 1, d, tf), lambda g, j, ge, gr: (layer, ge[g], 0, jj(g, j, gr))),
            pl.BlockSpec((1, 1, d, tf), lambda g, j, ge, gr: (layer, ge[g], 0, nf + jj(g, j, gr))),
            pl.BlockSpec((1, 1, tf, d), lambda g, j, ge, gr: (layer, ge[g], jj(g, j, gr), 0)),
        ],
        out_specs=pl.BlockSpec((tm, half), lambda g, j, ge, gr: (g, 0)),
        scratch_shapes=[pltpu.VMEM((tm, d), F32)],
    )
    return pl.pallas_call(
        functools.partial(_moe_ffn_kernel, sub=EXPERT_BLOCK // 2),
        grid_spec=grid_spec,
        out_shape=jax.ShapeDtypeStruct((cap, half), jnp.uint32),
        compiler_params=_cparams(("parallel", "arbitrary")),
        name="swiglu_experts",
    )(group_expert, group_rows, h_slots, w13, w13, w2)


def _route_rows(x_rows, m, wr_ref, br_ref, carry_ref, n_exp):
    tm = x_rows.shape[0]
    h = _modulate(x_rows, m, 3)
    h_hi, h_mid, _ = _split3(h)
    logits = _dot_split(h_hi, h_mid, wr_ref[...]) + br_ref[...]
    lane = lax.broadcasted_iota(jnp.int32, (tm, LANES), 1)
    neg = jnp.float32(-jnp.inf)
    logits = jnp.where(lane < n_exp, logits, neg)
    v0 = jnp.max(logits, axis=1, keepdims=True)
    e0 = jnp.min(jnp.where(logits == v0, lane, LANES), axis=1, keepdims=True)
    rest = jnp.where(lane == e0, neg, logits)
    v1 = jnp.max(rest, axis=1, keepdims=True)
    e1 = jnp.min(jnp.where(rest == v1, lane, LANES), axis=1, keepdims=True)
    p = jnp.exp(v1 - v0)
    g0 = 1.0 / (1.0 + p)
    g1 = p / (1.0 + p)
    oh0 = lane == e0
    oh1 = lane == e1
    onehot = jnp.where(jnp.logical_or(oh0, oh1), 1.0, 0.0).astype(BF16)
    r = lax.broadcasted_iota(jnp.int32, (tm, tm), 0)
    c = lax.broadcasted_iota(jnp.int32, (tm, tm), 1)
    strict = jnp.where(r > c, 1.0, 0.0).astype(BF16)
    before = _dot(strict, onehot) + carry_ref[0:1, :]
    rank0 = jnp.sum(jnp.where(oh0, before, 0.0), axis=1, keepdims=True)
    rank1 = jnp.sum(jnp.where(oh1, before, 0.0), axis=1, keepdims=True)
    total = carry_ref[0:1, :] + jnp.sum(onehot.astype(F32), axis=0, keepdims=True)
    carry_ref[...] = jnp.broadcast_to(total, carry_ref.shape)
    meta = jnp.where(lane == 0, e0.astype(F32), 0.0)
    meta = jnp.where(lane == 1, e1.astype(F32), meta)
    meta = jnp.where(lane == 2, g0, meta)
    meta = jnp.where(lane == 3, g1, meta)
    meta = jnp.where(lane == 4, rank0, meta)
    meta = jnp.where(lane == 5, rank1, meta)
    return _pack_bf16_pair(h), meta


def _router_operands(w_router, b_router):
    d, n_exp = w_router.shape
    wr = jnp.zeros((d, LANES), F32).at[:, :n_exp].set(w_router)
    br = jnp.zeros((1, LANES), F32).at[0, :n_exp].set(b_router)
    return wr, br


def _router_out(t, d, tm, row_map):
    specs = [pl.BlockSpec((tm, d // 2), row_map), pl.BlockSpec((tm, LANES), row_map),
             pl.BlockSpec((8, LANES), lambda *_: (0, 0))]
    shapes = [jax.ShapeDtypeStruct((t, d // 2), jnp.uint32), jax.ShapeDtypeStruct((t, LANES), F32),
              jax.ShapeDtypeStruct((8, LANES), F32)]
    return specs, shapes


def _combine_kernel(x_ref, ya_ref, yb_ref, meta_ref, mod_ref, g_ref, b_ref, o_ref, *, alpha):
    meta = meta_ref[...]
    y = meta[:, 2:3] * _unpack_bf16_pair(ya_ref[...]) + meta[:, 3:4] * _unpack_bf16_pair(yb_ref[...])
    o_ref[...] = _res_ln(x_ref[...], y, mod_ref[0], 3, g_ref[...], b_ref[...], alpha)


def _combine(x, ya, yb, meta, mod, ln_g, ln_b, *, seq, alpha, tm=512):
    t, d = x.shape
    row = pl.BlockSpec((tm, d), lambda i: (i, 0))
    packed = pl.BlockSpec((tm, d // 2), lambda i: (i, 0))
    return pl.pallas_call(
        functools.partial(_combine_kernel, alpha=alpha),
        grid=(t // tm,),
        in_specs=[row, packed, packed,
                  pl.BlockSpec((tm, LANES), lambda i: (i, 0)),
                  pl.BlockSpec((1, 6, d), lambda i: (i * tm // seq, 0, 0)),
                  _const_spec((1, d)), _const_spec((1, d))],
        out_specs=row,
        out_shape=jax.ShapeDtypeStruct((t, d), F32),
        compiler_params=_cparams(("parallel",)),
        name="moe_combine",
    )(x, ya, yb, meta, mod, ln_g.reshape(1, d), ln_b.reshape(1, d))


def _sc_mesh():
    return plsc.VectorSubcoreMesh(core_axis_name="c", subcore_axis_name="s", num_cores=SC_CORES,
                                  num_subcores=SC_SUBCORES)


def _sc_worker_share(n):
    workers = SC_CORES * SC_SUBCORES
    per_worker = n // workers
    steps = per_worker // SC_ROWS
    assert steps * SC_ROWS * workers == n
    return per_worker, steps


def _sc_gather_rows(table, idx):
    n = idx.shape[0]
    width = table.shape[1]
    per_worker, steps = _sc_worker_share(n)

    def body(table_hbm, idx_hbm, out_hbm, idx_v, rows_v, sem):
        wid = lax.axis_index("s") * SC_CORES + lax.axis_index("c")

        @pl.loop(0, steps)
        def _(j):
            off = pl.multiple_of(wid * per_worker + j * SC_ROWS, SC_ROWS)
            pltpu.sync_copy(idx_hbm.at[pl.ds(off, SC_ROWS)], idx_v)
            pltpu.async_copy(table_hbm.at[idx_v], rows_v, sem).wait()
            pltpu.sync_copy(rows_v, out_hbm.at[pl.ds(off, SC_ROWS)])

    return pl.kernel(
        body,
        out_type=jax.ShapeDtypeStruct((n, width), table.dtype),
        mesh=_sc_mesh(),
        scratch_types=[
            pltpu.VMEM((SC_ROWS,), jnp.int32),
            pltpu.VMEM((SC_ROWS, width), table.dtype),
            pltpu.SemaphoreType.DMA,
        ],
        name="sc_row_gather",
    )(table, idx)


def _sc_scatter_rows(rows, dest_a, dest_b, n_out):
    n, width = rows.shape
    per_worker, steps = _sc_worker_share(n)

    def body(rows_hbm, da_hbm, db_hbm, out_hbm, ia_v, ib_v, rows_v, sem):
        wid = lax.axis_index("s") * SC_CORES + lax.axis_index("c")

        @pl.loop(0, steps)
        def _(j):
            off = pl.multiple_of(wid * per_worker + j * SC_ROWS, SC_ROWS)
            pltpu.sync_copy(da_hbm.at[pl.ds(off, SC_ROWS)], ia_v)
            pltpu.sync_copy(db_hbm.at[pl.ds(off, SC_ROWS)], ib_v)
            pltpu.sync_copy(rows_hbm.at[pl.ds(off, SC_ROWS)], rows_v)
            pltpu.async_copy(rows_v, out_hbm.at[ia_v], sem).wait()
            pltpu.async_copy(rows_v, out_hbm.at[ib_v], sem).wait()

    return pl.kernel(
        body,
        out_type=jax.ShapeDtypeStruct((n_out, width), rows.dtype),
        mesh=_sc_mesh(),
        scratch_types=[
            pltpu.VMEM((SC_ROWS,), jnp.int32),
            pltpu.VMEM((SC_ROWS,), jnp.int32),
            pltpu.VMEM((SC_ROWS, width), rows.dtype),
            pltpu.SemaphoreType.DMA,
        ],
        name="sc_row_scatter",
    )(rows, dest_a, dest_b)


def _moe_layer(routed, mod, w13, w2, layer, ln_g, ln_b, *, seq, alpha):
    x, h, meta, cnt = routed
    t, d = x.shape
    n_exp = w13.shape[1]
    e0 = meta[:, 0].astype(jnp.int32)
    e1 = meta[:, 1].astype(jnp.int32)
    rank0 = meta[:, 4].astype(jnp.int32)
    rank1 = meta[:, 5].astype(jnp.int32)
    counts = cnt[0, :n_exp].astype(jnp.int32)
    group_rows = MOE_GROUP * EXPERT_BLOCK
    padded = (counts + group_rows - 1) // group_rows * group_rows
    pad_end = jnp.cumsum(padded)
    pad_start = pad_end - padded
    dest0 = pad_start[e0] + rank0
    dest1 = pad_start[e1] + rank1
    cap = t * TOP_K + n_exp * group_rows
    group_start = jnp.arange(cap // group_rows, dtype=jnp.int32) * group_rows
    group_expert = jnp.minimum(jnp.sum(pad_end[None, :] <= group_start[:, None], axis=1), n_exp - 1).astype(jnp.int32)
    rows_left = counts[group_expert] - (group_start - pad_start[group_expert])
    rows_in_group = jnp.clip(rows_left, 0, group_rows).astype(jnp.int32)
    h_slots = _sc_scatter_rows(h, dest0, dest1, cap)
    y_slots = _moe_ffn(h_slots, group_expert, rows_in_group, w13, w2, layer)
    ya = _sc_gather_rows(y_slots, dest0)
    yb = _sc_gather_rows(y_slots, dest1)
    return _combine(x, ya, yb, meta, mod, ln_g, ln_b, seq=seq, alpha=alpha)


HEAD_PAD = LANES
GATE_PARTS = 3
GATE_ROWS = 16
VAL_PAD = 16
LOG2E = 1.4426950408889634


def _head_lanes(h, hd):
    k0 = (h % (HEAD_PAD // hd)) * hd
    return k0, (hd if k0 == 0 else 0)


def _fox_proj_kernel(x_ref, mod_ref, wf_ref, bf_ref, wq_ref, wk_ref, wv_ref, selk_ref, onek_ref, selq_ref,
                     oneq_ref, onev_ref, k_ref, q_ref, v_ref, carry_ref, *, q_scale, heads, hd, nb):
    tm = x_ref.shape[0]

    @pl.when(pl.program_id(0) % nb == 0)
    def _():
        carry_ref[...] = jnp.zeros_like(carry_ref)

    h_hi, h_mid, _ = _split3(_modulate(x_ref[...], mod_ref[0], 0))

    f = _dot_split(h_hi, h_mid, wf_ref[...]) + bf_ref[...]
    lf = jnp.minimum(f, 0.0) - jnp.log(1.0 + jnp.exp(-jnp.abs(f)))
    r = lax.broadcasted_iota(jnp.int32, (tm, tm), 0)
    c = lax.broadcasted_iota(jnp.int32, (tm, tm), 1)
    tri = jnp.where(r >= c, 1.0, 0.0).astype(BF16)
    sums = _dot(tri, jnp.concatenate(_split3(lf), axis=1))
    cs = carry_ref[0:1, :] + sum(sums[:, n * LANES:(n + 1) * LANES] for n in range(GATE_PARTS))
    carry_ref[...] = jnp.broadcast_to(cs[tm - 1:tm, :], carry_ref.shape)
    cs = cs * LOG2E

    gate_k = onek_ref[...] + _dot(jnp.concatenate(_split3(cs), axis=1), selk_ref[...])
    kc = _dot(h_hi, wk_ref[...])
    lane = lax.broadcasted_iota(jnp.int32, (tm, HEAD_PAD), 1)
    for h in range(heads):
        k0, _ = _head_lanes(h, hd)
        src = (h * hd // HEAD_PAD) * HEAD_PAD
        is_k = jnp.logical_and(lane >= k0, lane < k0 + hd)
        hs = slice(h * HEAD_PAD, (h + 1) * HEAD_PAD)
        k_ref[:, hs] = jnp.where(is_k, kc[:, src:src + HEAD_PAD], gate_k[:, hs]).astype(BF16)

    nt = (((1,), (1,)), ((), ()))
    qc = lax.dot_general(wq_ref[...], h_hi, nt, preferred_element_type=F32) * (q_scale * LOG2E)
    gate_q = oneq_ref[...] + _dot(selq_ref[...], jnp.concatenate(_split3(cs.T), axis=0))
    fill = HEAD_PAD - hd - GATE_ROWS
    for h in range(heads):
        k0, g0 = _head_lanes(h, hd)
        base = h * HEAD_PAD
        q_ref[0, base + k0:base + k0 + hd, :] = qc[h * hd:(h + 1) * hd, :].astype(BF16)
        q_ref[0, base + g0:base + g0 + GATE_ROWS, :] = gate_q[h * GATE_ROWS:(h + 1) * GATE_ROWS, :].astype(BF16)
        q_ref[0, base + g0 + GATE_ROWS:base + g0 + GATE_ROWS + fill, :] = jnp.zeros((fill, tm), BF16)

    v = lax.dot_general(wv_ref[...], h_hi, nt, preferred_element_type=F32) + onev_ref[...]
    v_ref[0] = v.astype(BF16)


def _fox_proj(x, mod, w_in, b_f, *, bsz, seq, heads, tm=512):
    t, d = x.shape
    hd = d // heads
    assert HEAD_PAD == 2 * hd and GATE_ROWS >= 2 * GATE_PARTS
    nb = seq // tm
    wide = heads * HEAD_PAD
    wq_t = w_in[:, :d].T.astype(BF16)
    wk = w_in[:, d:2 * d].astype(BF16)
    vrows = heads * (hd + VAL_PAD)
    wv_t = jnp.pad(w_in[:, 2 * d:3 * d].reshape(d, heads, hd),
                   ((0, 0), (0, 0), (0, VAL_PAD))).reshape(d, vrows).T.astype(BF16)
    wf = jnp.zeros((d, LANES), F32).at[:, :heads].set(w_in[:, 3 * d:])
    bf = jnp.zeros((1, LANES), F32).at[0, :heads].set(b_f)
    sel_k = np.zeros((GATE_PARTS, LANES, wide), np.float32)
    one_k = np.zeros((1, wide), np.float32)
    sel_q = np.zeros((GATE_PARTS, heads * GATE_ROWS, LANES), np.float32)
    one_q = np.zeros((heads * GATE_ROWS, 1), np.float32)
    one_v = np.zeros((vrows, 1), np.float32)
    for hh in range(heads):
        _, g0 = _head_lanes(hh, hd)
        one_v[hh * (hd + VAL_PAD) + hd, 0] = 1.0
        for n in range(GATE_PARTS):
            sel_k[n, hh, hh * HEAD_PAD + g0 + n] = -1.0
            one_q[hh * GATE_ROWS + n, 0] = 1.0
            one_k[0, hh * HEAD_PAD + g0 + GATE_PARTS + n] = 1.0
            sel_q[n, hh * GATE_ROWS + GATE_PARTS + n, hh] = 1.0
    return pl.pallas_call(
        functools.partial(_fox_proj_kernel, q_scale=hd ** -0.5, heads=heads, hd=hd, nb=nb),
        grid=(t // tm,),
        in_specs=[
            pl.BlockSpec((tm, d), lambda i: (i, 0)),
            pl.BlockSpec((1, 6, d), lambda i: (i * tm // seq, 0, 0)),
            _const_spec((d, LANES)),
            _const_spec((1, LANES)),
            _const_spec((d, d)),
            _const_spec((d, d)),
            _const_spec((vrows, d)),
            _const_spec((GATE_PARTS * LANES, wide)),
            _const_spec((1, wide)),
            _const_spec((heads * GATE_ROWS, GATE_PARTS * LANES)),
            _const_spec((heads * GATE_ROWS, 1)),
            _const_spec((vrows, 1)),
        ],
        out_specs=[
            pl.BlockSpec((tm, wide), lambda i: (i, 0)),
            pl.BlockSpec((1, wide, tm), lambda i: (i // nb, 0, i % nb)),
            pl.BlockSpec((1, vrows, tm), lambda i: (i // nb, 0, i % nb)),
        ],
        out_shape=[
            jax.ShapeDtypeStruct((t, wide), BF16),
            jax.ShapeDtypeStruct((bsz, wide, seq), BF16),
            jax.ShapeDtypeStruct((bsz, vrows, seq), BF16),
        ],
        scratch_shapes=[pltpu.VMEM((8, LANES), F32)],
        compiler_params=_cparams(("arbitrary",)),
        name="fox_proj",
    )(x, mod, wf, bf, wq_t, wk, wv_t, jnp.asarray(sel_k.reshape(GATE_PARTS * LANES, wide), BF16), jnp.asarray(one_k),
      jnp.asarray(np.concatenate(list(sel_q), axis=1), BF16), jnp.asarray(one_q), jnp.asarray(one_v))


def _flash_kernel(qi_ref, kj_ref, k_ref, q_ref, v_ref, x_ref, mod_ref, wout_ref, g_ref, b_ref, *rest,
                  heads, hd, blk, alpha, n_exp):
    p_idx = pl.program_id(1)
    if n_exp:
        wr_ref, br_ref, o_ref, hp_ref, meta_ref, cnt_ref, m_ref, acc_ref, s_ref, carry_ref = rest

        @pl.when(jnp.logical_and(pl.program_id(0) == 0, p_idx == 0))
        def _():
            carry_ref[...] = jnp.zeros_like(carry_ref)
    else:
        o_ref, m_ref, acc_ref, s_ref = rest
    qi = qi_ref[p_idx]
    kj = kj_ref[p_idx]
    vp = hd + VAL_PAD

    @pl.when(kj == 0)
    def _():
        m_ref[...] = jnp.full(m_ref.shape, -jnp.inf, F32)
        acc_ref[...] = jnp.zeros_like(acc_ref)

    def step(masked):
        if masked:
            kpos = lax.broadcasted_iota(jnp.int32, (blk, blk), 0)
            qpos = lax.broadcasted_iota(jnp.int32, (blk, blk), 1)
            keep = kpos <= qpos

        def scores(h):
            hs = slice(h * HEAD_PAD, (h + 1) * HEAD_PAD)
            s = _dot(k_ref[:, hs], q_ref[0, hs, :])
            if masked:
                s = jnp.where(keep, s, -jnp.inf)
            s_ref[h % depth] = s
            return jnp.max(s, axis=0, keepdims=True)

        depth = s_ref.shape[0]
        ahead = [scores(h) for h in range(depth - 1)]
        for h in range(heads):
            if h + depth - 1 < heads:
                ahead.append(scores(h + depth - 1))
            m_cur = ahead.pop(0)
            m_prev = m_ref[h:h + 1, :]
            m_new = jnp.maximum(m_prev, m_cur)
            m_ref[h:h + 1, :] = m_new
            a = jnp.exp2(m_prev - m_new)
            p = jnp.exp2(s_ref[h % depth] - m_new).astype(BF16)
            rs = slice(h * vp, (h + 1) * vp)
            acc_ref[rs, :] = acc_ref[rs, :] * a + _dot(v_ref[0, rs, :], p)

    @pl.when(kj < qi)
    def _():
        step(False)

    @pl.when(kj == qi)
    def _():
        step(True)
        per = LANES // hd
        cols = []
        for g in range(heads // per):
            parts = []
            for n in range(per):
                r0 = (g * per + n) * vp
                parts.append(acc_ref[r0:r0 + hd, :] / acc_ref[r0 + hd:r0 + hd + 1, :])
            cols.append(jnp.concatenate(parts, axis=0).T.astype(BF16))
        y = _dot(jnp.concatenate(cols, axis=1), wout_ref[...])
        x_new = _res_ln(x_ref[...], y, mod_ref[0], 0, g_ref[...], b_ref[...], alpha)
        o_ref[...] = x_new
        if n_exp:
            hp_ref[...], meta_ref[...] = _route_rows(x_new, mod_ref[0], wr_ref, br_ref, carry_ref, n_exp)
            cnt_ref[...] = carry_ref[...]


def _flash(k_aug, q_aug_t, v_t, x, mod, w_out, ln_g, ln_b, router=None, *, bsz, seq, heads, alpha, blk=512):
    t, wide = k_aug.shape
    vrows = v_t.shape[1]
    hd = vrows // heads - VAL_PAD
    d = heads * hd
    nq = seq // blk
    pairs = [(i, j) for i in range(nq) for j in range(i + 1)]
    qi = jnp.asarray(np.array([p[0] for p in pairs], np.int32))
    kj = jnp.asarray(np.array([p[1] for p in pairs], np.int32))
    n_exp = router[0].shape[1] if router else 0
    row_map = lambda b, p, qi, kj: (b * nq + qi[p], 0)
    const = lambda shape: pl.BlockSpec(shape, lambda b, p, qi, kj: (0, 0), pipeline_mode=pl.Buffered(1))
    out_specs, out_shape = [pl.BlockSpec((blk, d), row_map)], [jax.ShapeDtypeStruct((t, d), F32)]
    scratch = [pltpu.VMEM((heads, blk), F32), pltpu.VMEM((vrows, blk), F32), pltpu.VMEM((3, blk, blk), F32)]
    extra_specs, extra_args = [], []
    if router:
        r_specs, r_shapes = _router_out(t, d, blk, row_map)
        out_specs, out_shape = out_specs + r_specs, out_shape + r_shapes
        scratch.append(pltpu.VMEM((8, LANES), F32))
        extra_specs = [const((d, LANES)), const((1, LANES))]
        extra_args = list(_router_operands(*router))
    grid_spec = pltpu.PrefetchScalarGridSpec(
        num_scalar_prefetch=2,
        grid=(bsz, len(pairs)),
        in_specs=[
            pl.BlockSpec((blk, wide), lambda b, p, qi, kj: (b * nq + kj[p], 0)),
            pl.BlockSpec((1, wide, blk), lambda b, p, qi, kj: (b, 0, qi[p])),
            pl.BlockSpec((1, vrows, blk), lambda b, p, qi, kj: (b, 0, kj[p])),
            pl.BlockSpec((blk, d), row_map),
            pl.BlockSpec((1, 6, d), lambda b, p, qi, kj: (b, 0, 0)),
            const((d, d)),
            const((1, d)),
            const((1, d)),
        ] + extra_specs,
        out_specs=out_specs,
        scratch_shapes=scratch,
    )
    outs = pl.pallas_call(
        functools.partial(_flash_kernel, heads=heads, hd=hd, blk=blk, alpha=alpha, n_exp=n_exp),
        grid_spec=grid_spec,
        out_shape=out_shape,
        compiler_params=_cparams(("arbitrary", "arbitrary")),
        name="fox_attention",
    )(qi, kj, k_aug, q_aug_t, v_t, x, mod, w_out.astype(BF16), ln_g.reshape(1, d), ln_b.reshape(1, d), *extra_args)
    return outs if router else outs[0]


def _fox_layer(x, mod, w_in, b_f, w_out, ln_g, ln_b, router=None, *, bsz, seq, alpha):
    heads = b_f.shape[0]
    k_aug, q_aug_t, v_t = _fox_proj(x, mod, w_in, b_f, bsz=bsz, seq=seq, heads=heads)
    return _flash(k_aug, q_aug_t, v_t, x, mod, w_out, ln_g, ln_b, router, bsz=bsz, seq=seq, heads=heads,
                  alpha=alpha)


def _pool_kernel(x_ref, xh_ref, mod_ref, win_ref, wg_ref, ls_ref, wout_ref, g_ref, b_ref, o_ref, pooled_ref,
                 *, tm, seq, alpha):
    i = pl.program_id(0)
    pos0 = (i * tm) % seq
    m = mod_ref[0]
    x = x_ref[...]
    halo_ok = jnp.where(pos0 > 0, jnp.float32(1.0), jnp.float32(0.0))
    xe = jnp.concatenate([xh_ref[...], x], axis=0)
    ze = _dot(_modulate(xe, m, 0).astype(BF16), win_ref[...])
    rows = lax.broadcasted_iota(jnp.int32, (tm + POOL_HALO, 1), 0)
    ze = ze * jnp.where(rows < POOL_HALO, halo_ok, jnp.float32(1.0))
    pos = (pos0 + lax.broadcasted_iota(jnp.int32, (tm, 1), 0) + 1).astype(F32)
    gd = ze.shape[1] // len(POOL_WINDOWS)
    for g, win in enumerate(POOL_WINDOWS):
        cs = slice(g * gd, (g + 1) * gd)
        zg = ze[:, cs]
        s = zg
        span = 1
        while span < win:
            s = s + pltpu.roll(s, span, 0)
            span *= 2
        count = jnp.minimum(pos, jnp.float32(win))
        pooled = s[POOL_HALO:, :] / count - zg[POOL_HALO:, :]
        pooled_ref[:, cs] = (_dot(pooled.astype(BF16), wg_ref[g]) * ls_ref[:, cs]).astype(BF16)
    y = _dot(pooled_ref[...], wout_ref[...])
    o_ref[...] = _res_ln(x, y, m, 0, g_ref[...], b_ref[...], alpha)


def _pool_layer(x, mod, w_in, w_grp, scale, w_out, ln_g, ln_b, *, seq, alpha, tm=512):
    t, d = x.shape
    ng, gd, _ = w_grp.shape
    hb = tm // POOL_HALO
    return pl.pallas_call(
        functools.partial(_pool_kernel, tm=tm, seq=seq, alpha=alpha),
        grid=(t // tm,),
        in_specs=[
            pl.BlockSpec((tm, d), lambda i: (i, 0)),
            pl.BlockSpec((POOL_HALO, d), lambda i: (jnp.maximum(i * hb - 1, 0), 0)),
            pl.BlockSpec((1, 6, d), lambda i: (i * tm // seq, 0, 0)),
            _const_spec((d, d)),
            _const_spec((ng, gd, gd)),
            _const_spec((1, d)),
            _const_spec((d, d)),
            _const_spec((1, d)),
            _const_spec((1, d)),
        ],
        out_specs=pl.BlockSpec((tm, d), lambda i: (i, 0)),
        out_shape=jax.ShapeDtypeStruct((t, d), F32),
        scratch_shapes=[pltpu.VMEM((tm, d), BF16)],
        compiler_params=_cparams(("parallel",)),
        name="pool_mixer",
    )(x, x, mod, w_in.astype(BF16), w_grp.astype(BF16), scale.reshape(1, d), w_out.astype(BF16),
      ln_g.reshape(1, d), ln_b.reshape(1, d))


def kernel(x, c, mod_w, mod_b, ln_g, ln_b, gm_w_in, gm_ln_g, gm_ln_b, gm_w_s, gm_b_s, gm_w_out, fox_w_in,
           fox_b_f, fox_w_out, pool_w_in, pool_w_grp, pool_scale, pool_w_out, ffn_w13, ffn_w2, moe_w_router,
           moe_b_router, moe_w13, moe_w2):
    bsz, seq, d = x.shape
    depth = mod_w.shape[0]
    alpha = (2 * depth) ** 0.25
    n_mixers = 3
    mod_all = _modulation(c, mod_w, mod_b)
    xt = x.reshape(bsz * seq, d)
    for i in range(depth):
        mod = mod_all[i]
        kind, j = i % n_mixers, i // n_mixers
        router = (moe_w_router[i // 2], moe_b_router[i // 2]) if i % 2 else None
        if kind == 0:
            xt = _gmlp_layer(xt, mod, gm_w_in[j], gm_ln_g[j], gm_ln_b[j], gm_w_s[j], gm_b_s[j], gm_w_out[j],
                             ln_g[i, 0], ln_b[i, 0], router, seq=seq, alpha=alpha)
        elif kind == 1:
            xt = _fox_layer(xt, mod, fox_w_in[j], fox_b_f[j], fox_w_out[j], ln_g[i, 0], ln_b[i, 0], router,
                            bsz=bsz, seq=seq, alpha=alpha)
        else:
            assert router is None, "the pooling mixer has no routing epilogue"
            xt = _pool_layer(xt, mod, pool_w_in[j], pool_w_grp[j], pool_scale[j], pool_w_out[j],
                             ln_g[i, 0], ln_b[i, 0], seq=seq, alpha=alpha)
        if router is None:
            xt = _ffn_layer(xt, mod, ffn_w13[i // 2], ffn_w2[i // 2], ln_g[i, 1], ln_b[i, 1],
                            seq=seq, alpha=alpha)
        else:
            xt = _moe_layer(xt, mod, moe_w13, moe_w2, i // 2, ln_g[i, 1], ln_b[i, 1], seq=seq, alpha=alpha)
    return xt.reshape(bsz, seq, d)
```

```python
import functools

import numpy as np
import jax
import jax.numpy as jnp
from jax import lax
from jax.experimental import pallas as pl
from jax.experimental.pallas import tpu as pltpu
from jax.experimental.pallas import tpu_sc as plsc

F32 = jnp.float32
BF16 = jnp.bfloat16

POOL_WINDOWS = (2, 4, 8, 16)
TOP_K = 2
EXPERT_BLOCK = 512
LN_EPS = 1e-5
LANES = 128
POOL_HALO = 16
VMEM_LIMIT = 56 * 1024 * 1024
SC_CORES = 2
SC_SUBCORES = 16
SC_ROWS = 128


def _cparams(sem):
    return pltpu.CompilerParams(dimension_semantics=sem, vmem_limit_bytes=VMEM_LIMIT)


def _const_spec(shape):
    nd = len(shape)
    return pl.BlockSpec(shape, lambda *_: (0,) * nd, pipeline_mode=pl.Buffered(1))


def _layer_norm(r, g, b):
    mu = jnp.mean(r, axis=-1, keepdims=True)
    xc = r - mu
    var = jnp.mean(xc * xc, axis=-1, keepdims=True)
    return xc * lax.rsqrt(var + LN_EPS) * g + b


def _modulate(x, m, off):
    return x * (1.0 + m[off + 1:off + 2]) + m[off:off + 1]


def _res_ln(x, y, m, off, g, b, alpha):
    return _layer_norm(alpha * x + (1.0 + m[off + 2:off + 3]) * y, g, b)


def _split3(a):
    hi = a.astype(BF16)
    r1 = a - hi.astype(F32)
    mid = r1.astype(BF16)
    lo = (r1 - mid.astype(F32)).astype(BF16)
    return hi, mid, lo


def _dot(a, b):
    return jnp.dot(a, b, preferred_element_type=F32)


def _dot_split(a_hi, a_mid, w):
    n = w.shape[1]
    w_hi = w.astype(BF16)
    w_lo = (w - w_hi.astype(F32)).astype(BF16)
    both = _dot(a_hi, jnp.concatenate([w_hi, w_lo], axis=1))
    return both[:, :n] + both[:, n:] + _dot(a_mid, w_hi)


def _pack_bf16_pair(x):
    n = x.shape[1] // 2
    hi = lax.bitcast_convert_type(x[:, :n].astype(BF16).astype(F32), jnp.uint32)
    lo = lax.bitcast_convert_type(x[:, n:].astype(BF16).astype(F32), jnp.uint32)
    return hi | (lo >> 16)


def _unpack_bf16_pair(w):
    hi = lax.bitcast_convert_type(w & jnp.uint32(0xFFFF0000), F32)
    lo = lax.bitcast_convert_type(w << 16, F32)
    return jnp.concatenate([hi, lo], axis=1)


def _mod_kernel(c_ref, w_ref, b_ref, o_ref):
    c = c_ref[...]
    s = c / (1.0 + jnp.exp(-c))
    s_hi, s_mid, _ = _split3(s)
    w = w_ref[0]
    w_hi = w.astype(BF16)
    w_lo = (w - w_hi.astype(F32)).astype(BF16)
    acc = _dot(s_hi, w_hi) + _dot(s_mid, w_hi) + _dot(s_hi, w_lo)
    o_ref[0] = acc + b_ref[0]


def _modulation(c, mod_w, mod_b):
    depth, d, n = mod_w.shape
    bsz = c.shape[0]
    rows = 8
    tn = n // 4
    c_pad = jnp.zeros((rows, d), F32).at[:bsz].set(c)
    out = pl.pallas_call(
        _mod_kernel,
        grid=(depth, n // tn),
        in_specs=[
            pl.BlockSpec((rows, d), lambda l, j: (0, 0)),
            pl.BlockSpec((1, d, tn), lambda l, j: (l, 0, j)),
            pl.BlockSpec((1, 1, tn), lambda l, j: (l, 0, j)),
        ],
        out_specs=pl.BlockSpec((1, rows, tn), lambda l, j: (l, 0, j)),
        out_shape=jax.ShapeDtypeStruct((depth, rows, n), F32),
        compiler_params=_cparams(("parallel", "parallel")),
        name="adaln_mod",
    )(c_pad, mod_w, mod_b.reshape(depth, 1, n))
    return out[:, :bsz, :].reshape(depth, bsz, 6, d)


def _gmlp_kernel(x_ref, mod_ref, win_ref, vg_ref, vb_ref, ws_ref, bst_ref, wout_ref, g_ref, b_ref,
                 *rest, tm, sub, chunk, groups, alpha, n_exp):
    if n_exp:
        wr_ref, br_ref, o_ref, hp_ref, meta_ref, cnt_ref, gated_ref, carry_ref = rest

        @pl.when(pl.program_id(0) == 0)
        def _():
            carry_ref[...] = jnp.zeros_like(carry_ref)
    else:
        o_ref, gated_ref = rest
    m = mod_ref[0]
    row = lax.broadcasted_iota(jnp.int32, (chunk, chunk), 0)
    col = lax.broadcasted_iota(jnp.int32, (chunk, chunk), 1)
    causal = row >= col
    bst = bst_ref[...]
    w_mix = [jnp.where(causal, ws_ref[g], 0.0).astype(BF16) for g in range(groups)]
    n = tm // sub

    def project(s):
        x = x_ref[s * sub:(s + 1) * sub, :]
        return x, _dot(_modulate(x, m, 0).astype(BF16), win_ref[...])

    def activate(z):
        z = 0.5 * z * (1.0 + lax.erf(z * (2.0 ** -0.5)))
        width = z.shape[1] // 2
        return z[:, :width], _layer_norm(z[:, width:], vg_ref[...], vb_ref[...]).astype(BF16)

    def mix(s, u, v):
        gd = u.shape[1] // groups
        for g in range(groups):
            cs = slice(g * gd, (g + 1) * gd)
            for c in range(sub // chunk):
                rs = slice(c * chunk, (c + 1) * chunk)
                mixed = _dot(w_mix[g], v[rs, cs]) + bst[:, g:g + 1]
                gated_ref[s * sub + c * chunk:s * sub + (c + 1) * chunk, cs] = (u[rs, cs] * mixed).astype(BF16)
        return _dot(gated_ref[s * sub:(s + 1) * sub, :], wout_ref[...])

    def finish(s, x, y):
        rows = slice(s * sub, (s + 1) * sub)
        x_new = _res_ln(x, y, m, 0, g_ref[...], b_ref[...], alpha)
        o_ref[rows, :] = x_new
        if n_exp:
            hp_ref[rows, :], meta_ref[rows, :] = _route_rows(x_new, m, wr_ref, br_ref, carry_ref, n_exp)

    nxt = project(0)
    pending = None
    for s in range(n):
        x, z = nxt
        if s + 1 < n:
            nxt = project(s + 1)
        u, v = activate(z)
        if pending is not None:
            finish(*pending)
        pending = (s, x, mix(s, u, v))
    finish(*pending)
    if n_exp:
        cnt_ref[...] = carry_ref[...]


def _gmlp_layer(x, mod, w_in, v_g, v_b, w_s, b_s, w_out, ln_g, ln_b, router=None, *, seq, alpha, tm=512,
                sub=256):
    t, d = x.shape
    groups, chunk, _ = w_s.shape
    width = w_out.shape[0]
    n_exp = router[0].shape[1] if router else 0
    kern = functools.partial(_gmlp_kernel, tm=tm, sub=sub, chunk=chunk, groups=groups, alpha=alpha, n_exp=n_exp)
    row_map = lambda i: (i, 0)
    out_specs, out_shape = [pl.BlockSpec((tm, d), row_map)], [jax.ShapeDtypeStruct((t, d), F32)]
    scratch = [pltpu.VMEM((tm, width), BF16)]
    extra_specs, extra_args = [], []
    if router:
        r_specs, r_shapes = _router_out(t, d, tm, row_map)
        out_specs, out_shape = out_specs + r_specs, out_shape + r_shapes
        scratch.append(pltpu.VMEM((8, LANES), F32))
        extra_specs = [_const_spec((d, LANES)), _const_spec((1, LANES))]
        extra_args = list(_router_operands(*router))
    outs = pl.pallas_call(
        kern,
        grid=(t // tm,),
        in_specs=[
            pl.BlockSpec((tm, d), lambda i: (i, 0)),
            pl.BlockSpec((1, 6, d), lambda i: (i * tm // seq, 0, 0)),
            _const_spec((d, 2 * width)),
            _const_spec((1, width)),
            _const_spec((1, width)),
            _const_spec((groups, chunk, chunk)),
            _const_spec((chunk, groups)),
            _const_spec((width, d)),
            _const_spec((1, d)),
            _const_spec((1, d)),
        ] + extra_specs,
        out_specs=out_specs,
        out_shape=out_shape,
        scratch_shapes=scratch,
        compiler_params=_cparams(("arbitrary",)),
        name="gmlp_mixer",
    )(x, mod, w_in.astype(BF16), v_g.reshape(1, width), v_b.reshape(1, width), w_s, b_s.T,
      w_out.astype(BF16), ln_g.reshape(1, d), ln_b.reshape(1, d), *extra_args)
    return outs if router else outs[0]


def _swiglu_pipelined(hs, w1, w3, w2, emit):
    up = lambda h: (_dot(h, w1), _dot(h, w3))
    nxt = up(hs[0])
    for s in range(len(hs)):
        a, b = nxt
        if s + 1 < len(hs):
            nxt = up(hs[s + 1])
        t = (a / (1.0 + jnp.exp(-a)) * b).astype(BF16)
        emit(s, _dot(t, w2))


def _ffn_kernel(x_ref, mod_ref, w1_ref, w3_ref, w2_ref, g_ref, b_ref, o_ref, h_ref, *, alpha, sub):
    j = pl.program_id(1)
    last = pl.num_programs(1) - 1
    subs = [slice(s, s + sub) for s in range(0, x_ref.shape[0], sub)]
    weights = lambda: (w1_ref[...], w3_ref[...], w2_ref[...])

    @pl.when(j == 0)
    def _():
        hs = []
        for rows in subs:
            h = _modulate(x_ref[rows, :], mod_ref[0], 3).astype(BF16)
            h_ref[rows, :] = h
            hs.append(h)

        def emit(s, part):
            o_ref[subs[s], :] = part

        _swiglu_pipelined(hs, *weights(), emit)

    @pl.when(jnp.logical_and(j > 0, j < last))
    def _():
        def emit(s, part):
            o_ref[subs[s], :] += part

        _swiglu_pipelined([h_ref[rows, :] for rows in subs], *weights(), emit)

    @pl.when(j == last)
    def _():
        def emit(s, part):
            rows = subs[s]
            y = o_ref[rows, :] + part
            o_ref[rows, :] = _res_ln(x_ref[rows, :], y, mod_ref[0], 3, g_ref[...], b_ref[...], alpha)

        _swiglu_pipelined([h_ref[rows, :] for rows in subs], *weights(), emit)


def _ffn_layer(x, mod, w13, w2, ln_g, ln_b, *, seq, alpha, tm=1024, tf=512):
    t, d = x.shape
    f = w2.shape[0]
    nf = f // tf
    assert nf >= 2
    w13b = w13.astype(BF16)
    return pl.pallas_call(
        functools.partial(_ffn_kernel, alpha=alpha, sub=tm // 4),
        grid=(t // tm, nf),
        in_specs=[
            pl.BlockSpec((tm, d), lambda i, j: (i, 0)),
            pl.BlockSpec((1, 6, d), lambda i, j: (i * tm // seq, 0, 0)),
            pl.BlockSpec((d, tf), lambda i, j: (0, j)),
            pl.BlockSpec((d, tf), lambda i, j: (0, nf + j)),
            pl.BlockSpec((tf, d), lambda i, j: (j, 0)),
            _const_spec((1, d)),
            _const_spec((1, d)),
        ],
        out_specs=pl.BlockSpec((tm, d), lambda i, j: (i, 0)),
        out_shape=jax.ShapeDtypeStruct((t, d), F32),
        scratch_shapes=[pltpu.VMEM((tm, d), BF16)],
        compiler_params=_cparams(("parallel", "arbitrary")),
        name="swiglu_dense",
    )(x, mod, w13b, w13b, w2.astype(BF16), ln_g.reshape(1, d), ln_b.reshape(1, d))


MOE_GROUP = 2


def _moe_ffn_kernel(ge_ref, gr_ref, h_ref, w1_ref, w3_ref, w2_ref, o_ref, acc_ref, *, sub):
    g = pl.program_id(0)
    j = pl.program_id(1)
    last = pl.num_programs(1) - 1
    n_rows = gr_ref[g]
    nv = (n_rows + EXPERT_BLOCK - 1) // EXPERT_BLOCK

    def run(n_blocks, phase):
        subs = [slice(s, s + sub) for s in range(0, n_blocks * EXPERT_BLOCK, sub)]
        row = lax.broadcasted_iota(jnp.int32, (sub, 1), 0)

        def rows_in(rows):
            w = jnp.where(row + rows.start < n_rows, h_ref[rows, :], jnp.uint32(0))
            return _unpack_bf16_pair(w).astype(BF16)

        def emit(s, part):
            if phase == "first":
                acc_ref[subs[s], :] = part
            elif phase == "middle":
                acc_ref[subs[s], :] += part
            else:
                o_ref[subs[s], :] = _pack_bf16_pair(acc_ref[subs[s], :] + part)

        _swiglu_pipelined([rows_in(rows) for rows in subs], w1_ref[0, 0].astype(BF16), w3_ref[0, 0].astype(BF16),
                          w2_ref[0, 0].astype(BF16), emit)

    for n_blocks in range(1, MOE_GROUP + 1):
        @pl.when(jnp.logical_and(nv == n_blocks, j == 0))
        def _():
            run(n_blocks, "first")

        @pl.when(jnp.logical_and(nv == n_blocks, jnp.logical_and(j > 0, j < last)))
        def _():
            run(n_blocks, "middle")

        @pl.when(jnp.logical_and(nv == n_blocks, j == last))
        def _():
            run(n_blocks, "last")

    for blk in range(MOE_GROUP):
        @pl.when(jnp.logical_and(blk >= nv, j == last))
        def _():
            o_ref[blk * EXPERT_BLOCK:(blk + 1) * EXPERT_BLOCK, :] = jnp.zeros((EXPERT_BLOCK, o_ref.shape[1]),
                                                                              jnp.uint32)


def _moe_ffn(h_slots, group_expert, group_rows, w13, w2, layer, *, tf=512):
    cap, half = h_slots.shape
    d = 2 * half
    f = w2.shape[2]
    nf = f // tf
    assert nf >= 2
    tm = MOE_GROUP * EXPERT_BLOCK

    def jj(g, j, gr):
        return jnp.where(gr[g] > 0, j, nf - 1)

    grid_spec = pltpu.PrefetchScalarGridSpec(
        num_scalar_prefetch=2,
        grid=(cap // tm, nf),
        in_specs=[
            pl.BlockSpec((tm, half), lambda g, j, ge, gr: (g, 0)),
            pl.BlockSpec((1, 1, d, tf), lambda g, j, ge, gr: (layer, ge[g], 0, jj(g, j, gr))),
            pl.BlockSpec((1, 1, d, tf), lambda g, j, ge, gr: (layer, ge[g], 0, nf + jj(g, j, gr))),
            pl.BlockSpec((1, 1, tf, d), lambda g, j, ge, gr: (layer, ge[g], jj(g, j, gr), 0)),
        ],
        out_specs=pl.BlockSpec((tm, half), lambda g, j, ge, gr: (g, 0)),
        scratch_shapes=[pltpu.VMEM((tm, d), F32)],
    )
    return pl.pallas_call(
        functools.partial(_moe_ffn_kernel, sub=EXPERT_BLOCK // 2),
        grid_spec=grid_spec,
        out_shape=jax.ShapeDtypeStruct((cap, half), jnp.uint32),
        compiler_params=_cparams(("parallel", "arbitrary")),
        name="swiglu_experts",
    )(group_expert, group_rows, h_slots, w13, w13, w2)


def _route_rows(x_rows, m, wr_ref, br_ref, carry_ref, n_exp):
    tm = x_rows.shape[0]
    h = _modulate(x_rows, m, 3)
    h_hi, h_mid, _ = _split3(h)
    logits = _dot_split(h_hi, h_mid, wr_ref[...]) + br_ref[...]
    lane = lax.broadcasted_iota(jnp.int32, (tm, LANES), 1)
    neg = jnp.float32(-jnp.inf)
    logits = jnp.where(lane < n_exp, logits, neg)
    v0 = jnp.max(logits, axis=1, keepdims=True)
    e0 = jnp.min(jnp.where(logits == v0, lane, LANES), axis=1, keepdims=True)
    rest = jnp.where(lane == e0, neg, logits)
    v1 = jnp.max(rest, axis=1, keepdims=True)
    e1 = jnp.min(jnp.where(rest == v1, lane, LANES), axis=1, keepdims=True)
    p = jnp.exp(v1 - v0)
    g0 = 1.0 / (1.0 + p)
    g1 = p / (1.0 + p)
    oh0 = lane == e0
    oh1 = lane == e1
    onehot = jnp.where(jnp.logical_or(oh0, oh1), 1.0, 0.0).astype(BF16)
    r = lax.broadcasted_iota(jnp.int32, (tm, tm), 0)
    c = lax.broadcasted_iota(jnp.int32, (tm, tm), 1)
    strict = jnp.where(r > c, 1.0, 0.0).astype(BF16)
    before = _dot(strict, onehot) + carry_ref[0:1, :]
    rank0 = jnp.sum(jnp.where(oh0, before, 0.0), axis=1, keepdims=True)
    rank1 = jnp.sum(jnp.where(oh1, before, 0.0), axis=1, keepdims=True)
    total = carry_ref[0:1, :] + jnp.sum(onehot.astype(F32), axis=0, keepdims=True)
    carry_ref[...] = jnp.broadcast_to(total, carry_ref.shape)
    meta = jnp.where(lane == 0, e0.astype(F32), 0.0)
    meta = jnp.where(lane == 1, e1.astype(F32), meta)
    meta = jnp.where(lane == 2, g0, meta)
    meta = jnp.where(lane == 3, g1, meta)
    meta = jnp.where(lane == 4, rank0, meta)
    meta = jnp.where(lane == 5, rank1, meta)
    return _pack_bf16_pair(h), meta


def _router_operands(w_router, b_router):
    d, n_exp = w_router.shape
    wr = jnp.zeros((d, LANES), F32).at[:, :n_exp].set(w_router)
    br = jnp.zeros((1, LANES), F32).at[0, :n_exp].set(b_router)
    return wr, br


def _router_out(t, d, tm, row_map):
    specs = [pl.BlockSpec((tm, d // 2), row_map), pl.BlockSpec((tm, LANES), row_map),
             pl.BlockSpec((8, LANES), lambda *_: (0, 0))]
    shapes = [jax.ShapeDtypeStruct((t, d // 2), jnp.uint32), jax.ShapeDtypeStruct((t, LANES), F32),
              jax.ShapeDtypeStruct((8, LANES), F32)]
    return specs, shapes


def _combine_kernel(x_ref, ya_ref, yb_ref, meta_ref, mod_ref, g_ref, b_ref, o_ref, *, alpha):
    meta = meta_ref[...]
    y = meta[:, 2:3] * _unpack_bf16_pair(ya_ref[...]) + meta[:, 3:4] * _unpack_bf16_pair(yb_ref[...])
    o_ref[...] = _res_ln(x_ref[...], y, mod_ref[0], 3, g_ref[...], b_ref[...], alpha)


def _combine(x, ya, yb, meta, mod, ln_g, ln_b, *, seq, alpha, tm=512):
    t, d = x.shape
    row = pl.BlockSpec((tm, d), lambda i: (i, 0))
    packed = pl.BlockSpec((tm, d // 2), lambda i: (i, 0))
    return pl.pallas_call(
        functools.partial(_combine_kernel, alpha=alpha),
        grid=(t // tm,),
        in_specs=[row, packed, packed,
                  pl.BlockSpec((tm, LANES), lambda i: (i, 0)),
                  pl.BlockSpec((1, 6, d), lambda i: (i * tm // seq, 0, 0)),
                  _const_spec((1, d)), _const_spec((1, d))],
        out_specs=row,
        out_shape=jax.ShapeDtypeStruct((t, d), F32),
        compiler_params=_cparams(("parallel",)),
        name="moe_combine",
    )(x, ya, yb, meta, mod, ln_g.reshape(1, d), ln_b.reshape(1, d))


def _sc_mesh():
    return plsc.VectorSubcoreMesh(core_axis_name="c", subcore_axis_name="s", num_cores=SC_CORES,
                                  num_subcores=SC_SUBCORES)


def _sc_worker_share(n):
    workers = SC_CORES * SC_SUBCORES
    per_worker = n // workers
    steps = per_worker // SC_ROWS
    assert steps * SC_ROWS * workers == n
    return per_worker, steps


def _sc_gather_rows(table, idx):
    n = idx.shape[0]
    width = table.shape[1]
    per_worker, steps = _sc_worker_share(n)

    def body(table_hbm, idx_hbm, out_hbm, idx_v, rows_v, sem):
        wid = lax.axis_index("s") * SC_CORES + lax.axis_index("c")

        @pl.loop(0, steps)
        def _(j):
            off = pl.multiple_of(wid * per_worker + j * SC_ROWS, SC_ROWS)
            pltpu.sync_copy(idx_hbm.at[pl.ds(off, SC_ROWS)], idx_v)
            pltpu.async_copy(table_hbm.at[idx_v], rows_v, sem).wait()
            pltpu.sync_copy(rows_v, out_hbm.at[pl.ds(off, SC_ROWS)])

    return pl.kernel(
        body,
        out_type=jax.ShapeDtypeStruct((n, width), table.dtype),
        mesh=_sc_mesh(),
        scratch_types=[
            pltpu.VMEM((SC_ROWS,), jnp.int32),
            pltpu.VMEM((SC_ROWS, width), table.dtype),
            pltpu.SemaphoreType.DMA,
        ],
        name="sc_row_gather",
    )(table, idx)


def _sc_scatter_rows(rows, dest_a, dest_b, n_out):
    n, width = rows.shape
    per_worker, steps = _sc_worker_share(n)

    def body(rows_hbm, da_hbm, db_hbm, out_hbm, ia_v, ib_v, rows_v, sem):
        wid = lax.axis_index("s") * SC_CORES + lax.axis_index("c")

        @pl.loop(0, steps)
        def _(j):
            off = pl.multiple_of(wid * per_worker + j * SC_ROWS, SC_ROWS)
            pltpu.sync_copy(da_hbm.at[pl.ds(off, SC_ROWS)], ia_v)
            pltpu.sync_copy(db_hbm.at[pl.ds(off, SC_ROWS)], ib_v)
            pltpu.sync_copy(rows_hbm.at[pl.ds(off, SC_ROWS)], rows_v)
            pltpu.async_copy(rows_v, out_hbm.at[ia_v], sem).wait()
            pltpu.async_copy(rows_v, out_hbm.at[ib_v], sem).wait()

    return pl.kernel(
        body,
        out_type=jax.ShapeDtypeStruct((n_out, width), rows.dtype),
        mesh=_sc_mesh(),
        scratch_types=[
            pltpu.VMEM((SC_ROWS,), jnp.int32),
            pltpu.VMEM((SC_ROWS,), jnp.int32),
            pltpu.VMEM((SC_ROWS, width), rows.dtype),
            pltpu.SemaphoreType.DMA,
        ],
        name="sc_row_scatter",
    )(rows, dest_a, dest_b)


def _moe_layer(routed, mod, w13, w2, layer, ln_g, ln_b, *, seq, alpha):
    x, h, meta, cnt = routed
    t, d = x.shape
    n_exp = w13.shape[1]
    e0 = meta[:, 0].astype(jnp.int32)
    e1 = meta[:, 1].astype(jnp.int32)
    rank0 = meta[:, 4].astype(jnp.int32)
    rank1 = meta[:, 5].astype(jnp.int32)
    counts = cnt[0, :n_exp].astype(jnp.int32)
    group_rows = MOE_GROUP * EXPERT_BLOCK
    padded = (counts + group_rows - 1) // group_rows * group_rows
    pad_end = jnp.cumsum(padded)
    pad_start = pad_end - padded
    dest0 = pad_start[e0] + rank0
    dest1 = pad_start[e1] + rank1
    cap = t * TOP_K + n_exp * group_rows
    group_start = jnp.arange(cap // group_rows, dtype=jnp.int32) * group_rows
    group_expert = jnp.minimum(jnp.sum(pad_end[None, :] <= group_start[:, None], axis=1), n_exp - 1).astype(jnp.int32)
    rows_left = counts[group_expert] - (group_start - pad_start[group_expert])
    rows_in_group = jnp.clip(rows_left, 0, group_rows).astype(jnp.int32)
    h_slots = _sc_scatter_rows(h, dest0, dest1, cap)
    y_slots = _moe_ffn(h_slots, group_expert, rows_in_group, w13, w2, layer)
    ya = _sc_gather_rows(y_slots, dest0)
    yb = _sc_gather_rows(y_slots, dest1)
    return _combine(x, ya, yb, meta, mod, ln_g, ln_b, seq=seq, alpha=alpha)


HEAD_PAD = LANES
GATE_PARTS = 3
GATE_ROWS = 16
VAL_PAD = 16
LOG2E = 1.4426950408889634


def _head_lanes(h, hd):
    k0 = (h % (HEAD_PAD // hd)) * hd
    return k0, (hd if k0 == 0 else 0)


def _fox_proj_kernel(x_ref, mod_ref, wf_ref, bf_ref, wq_ref, wk_ref, wv_ref, selk_ref, onek_ref, selq_ref,
                     oneq_ref, onev_ref, k_ref, q_ref, v_ref, carry_ref, *, q_scale, heads, hd, nb):
    tm = x_ref.shape[0]

    @pl.when(pl.program_id(0) % nb == 0)
    def _():
        carry_ref[...] = jnp.zeros_like(carry_ref)

    h_hi, h_mid, _ = _split3(_modulate(x_ref[...], mod_ref[0], 0))

    f = _dot_split(h_hi, h_mid, wf_ref[...]) + bf_ref[...]
    lf = jnp.minimum(f, 0.0) - jnp.log(1.0 + jnp.exp(-jnp.abs(f)))
    r = lax.broadcasted_iota(jnp.int32, (tm, tm), 0)
    c = lax.broadcasted_iota(jnp.int32, (tm, tm), 1)
    tri = jnp.where(r >= c, 1.0, 0.0).astype(BF16)
    sums = _dot(tri, jnp.concatenate(_split3(lf), axis=1))
    cs = carry_ref[0:1, :] + sum(sums[:, n * LANES:(n + 1) * LANES] for n in range(GATE_PARTS))
    carry_ref[...] = jnp.broadcast_to(cs[tm - 1:tm, :], carry_ref.shape)
    cs = cs * LOG2E

    gate_k = onek_ref[...] + _dot(jnp.concatenate(_split3(cs), axis=1), selk_ref[...])
    kc = _dot(h_hi, wk_ref[...])
    lane = lax.broadcasted_iota(jnp.int32, (tm, HEAD_PAD), 1)
    for h in range(heads):
        k0, _ = _head_lanes(h, hd)
        src = (h * hd // HEAD_PAD) * HEAD_PAD
        is_k = jnp.logical_and(lane >= k0, lane < k0 + hd)
        hs = slice(h * HEAD_PAD, (h + 1) * HEAD_PAD)
        k_ref[:, hs] = jnp.where(is_k, kc[:, src:src + HEAD_PAD], gate_k[:, hs]).astype(BF16)

    nt = (((1,), (1,)), ((), ()))
    qc = lax.dot_general(wq_ref[...], h_hi, nt, preferred_element_type=F32) * (q_scale * LOG2E)
    gate_q = oneq_ref[...] + _dot(selq_ref[...], jnp.concatenate(_split3(cs.T), axis=0))
    fill = HEAD_PAD - hd - GATE_ROWS
    for h in range(heads):
        k0, g0 = _head_lanes(h, hd)
        base = h * HEAD_PAD
        q_ref[0, base + k0:base + k0 + hd, :] = qc[h * hd:(h + 1) * hd, :].astype(BF16)
        q_ref[0, base + g0:base + g0 + GATE_ROWS, :] = gate_q[h * GATE_ROWS:(h + 1) * GATE_ROWS, :].astype(BF16)
        q_ref[0, base + g0 + GATE_ROWS:base + g0 + GATE_ROWS + fill, :] = jnp.zeros((fill, tm), BF16)

    v = lax.dot_general(wv_ref[...], h_hi, nt, preferred_element_type=F32) + onev_ref[...]
    v_ref[0] = v.astype(BF16)


def _fox_proj(x, mod, w_in, b_f, *, bsz, seq, heads, tm=512):
    t, d = x.shape
    hd = d // heads
    assert HEAD_PAD == 2 * hd and GATE_ROWS >= 2 * GATE_PARTS
    nb = seq // tm
    wide = heads * HEAD_PAD
    wq_t = w_in[:, :d].T.astype(BF16)
    wk = w_in[:, d:2 * d].astype(BF16)
    vrows = heads * (hd + VAL_PAD)
    wv_t = jnp.pad(w_in[:, 2 * d:3 * d].reshape(d, heads, hd),
                   ((0, 0), (0, 0), (0, VAL_PAD))).reshape(d, vrows).T.astype(BF16)
    wf = jnp.zeros((d, LANES), F32).at[:, :heads].set(w_in[:, 3 * d:])
    bf = jnp.zeros((1, LANES), F32).at[0, :heads].set(b_f)
    sel_k = np.zeros((GATE_PARTS, LANES, wide), np.float32)
    one_k = np.zeros((1, wide), np.float32)
    sel_q = np.zeros((GATE_PARTS, heads * GATE_ROWS, LANES), np.float32)
    one_q = np.zeros((heads * GATE_ROWS, 1), np.float32)
    one_v = np.zeros((vrows, 1), np.float32)
    for hh in range(heads):
        _, g0 = _head_lanes(hh, hd)
        one_v[hh * (hd + VAL_PAD) + hd, 0] = 1.0
        for n in range(GATE_PARTS):
            sel_k[n, hh, hh * HEAD_PAD + g0 + n] = -1.0
            one_q[hh * GATE_ROWS + n, 0] = 1.0
            one_k[0, hh * HEAD_PAD + g0 + GATE_PARTS + n] = 1.0
            sel_q[n, hh * GATE_ROWS + GATE_PARTS + n, hh] = 1.0
    return pl.pallas_call(
        functools.partial(_fox_proj_kernel, q_scale=hd ** -0.5, heads=heads, hd=hd, nb=nb),
        grid=(t // tm,),
        in_specs=[
            pl.BlockSpec((tm, d), lambda i: (i, 0)),
            pl.BlockSpec((1, 6, d), lambda i: (i * tm // seq, 0, 0)),
            _const_spec((d, LANES)),
            _const_spec((1, LANES)),
            _const_spec((d, d)),
            _const_spec((d, d)),
            _const_spec((vrows, d)),
            _const_spec((GATE_PARTS * LANES, wide)),
            _const_spec((1, wide)),
            _const_spec((heads * GATE_ROWS, GATE_PARTS * LANES)),
            _const_spec((heads * GATE_ROWS, 1)),
            _const_spec((vrows, 1)),
        ],
        out_specs=[
            pl.BlockSpec((tm, wide), lambda i: (i, 0)),
            pl.BlockSpec((1, wide, tm), lambda i: (i // nb, 0, i % nb)),
            pl.BlockSpec((1, vrows, tm), lambda i: (i // nb, 0, i % nb)),
        ],
        out_shape=[
            jax.ShapeDtypeStruct((t, wide), BF16),
            jax.ShapeDtypeStruct((bsz, wide, seq), BF16),
            jax.ShapeDtypeStruct((bsz, vrows, seq), BF16),
        ],
        scratch_shapes=[pltpu.VMEM((8, LANES), F32)],
        compiler_params=_cparams(("arbitrary",)),
        name="fox_proj",
    )(x, mod, wf, bf, wq_t, wk, wv_t, jnp.asarray(sel_k.reshape(GATE_PARTS * LANES, wide), BF16), jnp.asarray(one_k),
      jnp.asarray(np.concatenate(list(sel_q), axis=1), BF16), jnp.asarray(one_q), jnp.asarray(one_v))


def _flash_kernel(qi_ref, kw_ref, kn_ref, kind_ref, first_ref, kwide_ref, vwide_ref, knarrow_ref, vnarrow_ref,
                  q_ref, x_ref, mod_ref, wout_ref, g_ref, b_ref, *rest, heads, hd, blk, alpha, n_exp):
    p_idx = pl.program_id(1)
    if n_exp:
        wr_ref, br_ref, o_ref, hp_ref, meta_ref, cnt_ref, m_ref, acc_ref, s_ref, carry_ref = rest

        @pl.when(jnp.logical_and(pl.program_id(0) == 0, p_idx == 0))
        def _():
            carry_ref[...] = jnp.zeros_like(carry_ref)
    else:
        o_ref, m_ref, acc_ref, s_ref = rest
    kind = kind_ref[p_idx]
    vp = hd + VAL_PAD

    @pl.when(first_ref[p_idx] == 1)
    def _():
        m_ref[...] = jnp.full(m_ref.shape, -jnp.inf, F32)
        acc_ref[...] = jnp.zeros_like(acc_ref)

    def step(keys_ref, vals_ref, nk, masked):
        if masked:
            kpos = lax.broadcasted_iota(jnp.int32, (nk, blk), 0)
            qpos = lax.broadcasted_iota(jnp.int32, (nk, blk), 1)
            keep = kpos <= qpos

        def scores(h):
            hs = slice(h * HEAD_PAD, (h + 1) * HEAD_PAD)
            s = _dot(keys_ref[:, hs], q_ref[0, hs, :])
            if masked:
                s = jnp.where(keep, s, -jnp.inf)
            s_ref[h % depth, 0:nk, :] = s
            return jnp.max(s, axis=0, keepdims=True)

        depth = s_ref.shape[0]
        ahead = [scores(h) for h in range(depth - 1)]
        for h in range(heads):
            if h + depth - 1 < heads:
                ahead.append(scores(h + depth - 1))
            m_cur = ahead.pop(0)
            m_prev = m_ref[h:h + 1, :]
            m_new = jnp.maximum(m_prev, m_cur)
            m_ref[h:h + 1, :] = m_new
            a = jnp.exp2(m_prev - m_new)
            p = jnp.exp2(s_ref[h % depth, 0:nk, :] - m_new).astype(BF16)
            rs = slice(h * vp, (h + 1) * vp)
            acc_ref[rs, :] = acc_ref[rs, :] * a + _dot(vals_ref[0, rs, :], p)

    @pl.when(kind == 0)
    def _():
        step(kwide_ref, vwide_ref, 2 * blk, False)

    @pl.when(kind == 1)
    def _():
        step(knarrow_ref, vnarrow_ref, blk, False)

    @pl.when(kind == 2)
    def _():
        step(knarrow_ref, vnarrow_ref, blk, True)
        per = LANES // hd
        cols = []
        for g in range(heads // per):
            parts = []
            for n in range(per):
                r0 = (g * per + n) * vp
                parts.append(acc_ref[r0:r0 + hd, :] / acc_ref[r0 + hd:r0 + hd + 1, :])
            cols.append(jnp.concatenate(parts, axis=0).T.astype(BF16))
        y = _dot(jnp.concatenate(cols, axis=1), wout_ref[...])
        x_new = _res_ln(x_ref[...], y, mod_ref[0], 0, g_ref[...], b_ref[...], alpha)
        o_ref[...] = x_new
        if n_exp:
            hp_ref[...], meta_ref[...] = _route_rows(x_new, mod_ref[0], wr_ref, br_ref, carry_ref, n_exp)
            cnt_ref[...] = carry_ref[...]


def _flash_schedule(nq):
    rows, kw, kn = [], 0, 0
    for i in range(nq):
        steps = [(0, w) for w in range(i // 2)] + ([(1, i - 1)] if i % 2 else []) + [(2, i)]
        for n, (kind, idx) in enumerate(steps):
            if kind == 0:
                kw = idx
            else:
                kn = idx
            rows.append((i, kw, kn, kind, int(n == 0)))
    return [jnp.asarray(np.array(col, np.int32)) for col in zip(*rows)]


def _flash(k_aug, q_aug_t, v_t, x, mod, w_out, ln_g, ln_b, router=None, *, bsz, seq, heads, alpha, blk=512):
    t, wide = k_aug.shape
    vrows = v_t.shape[1]
    hd = vrows // heads - VAL_PAD
    d = heads * hd
    nq = seq // blk
    assert nq % 2 == 0
    sched = _flash_schedule(nq)
    n_exp = router[0].shape[1] if router else 0
    row_map = lambda b, p, qi, kw, kn, kind, first: (b * nq + qi[p], 0)
    const = lambda shape: pl.BlockSpec(shape, lambda b, p, *_: (0, 0), pipeline_mode=pl.Buffered(1))
    out_specs, out_shape = [pl.BlockSpec((blk, d), row_map)], [jax.ShapeDtypeStruct((t, d), F32)]
    scratch = [pltpu.VMEM((heads, blk), F32), pltpu.VMEM((vrows, blk), F32), pltpu.VMEM((3, 2 * blk, blk), F32)]
    extra_specs, extra_args = [], []
    if router:
        r_specs, r_shapes = _router_out(t, d, blk, row_map)
        out_specs, out_shape = out_specs + r_specs, out_shape + r_shapes
        scratch.append(pltpu.VMEM((8, LANES), F32))
        extra_specs = [const((d, LANES)), const((1, LANES))]
        extra_args = list(_router_operands(*router))
    grid_spec = pltpu.PrefetchScalarGridSpec(
        num_scalar_prefetch=len(sched),
        grid=(bsz, sched[0].shape[0]),
        in_specs=[
            pl.BlockSpec((2 * blk, wide), lambda b, p, qi, kw, kn, kind, first: (b * (nq // 2) + kw[p], 0)),
            pl.BlockSpec((1, vrows, 2 * blk), lambda b, p, qi, kw, kn, kind, first: (b, 0, kw[p])),
            pl.BlockSpec((blk, wide), lambda b, p, qi, kw, kn, kind, first: (b * nq + kn[p], 0)),
            pl.BlockSpec((1, vrows, blk), lambda b, p, qi, kw, kn, kind, first: (b, 0, kn[p])),
            pl.BlockSpec((1, wide, blk), lambda b, p, qi, kw, kn, kind, first: (b, 0, qi[p])),
            pl.BlockSpec((blk, d), row_map),
            pl.BlockSpec((1, 6, d), lambda b, p, *_: (b, 0, 0)),
            const((d, d)),
            const((1, d)),
            const((1, d)),
        ] + extra_specs,
        out_specs=out_specs,
        scratch_shapes=scratch,
    )
    outs = pl.pallas_call(
        functools.partial(_flash_kernel, heads=heads, hd=hd, blk=blk, alpha=alpha, n_exp=n_exp),
        grid_spec=grid_spec,
        out_shape=out_shape,
        compiler_params=_cparams(("arbitrary", "arbitrary")),
        name="fox_attention",
    )(*sched, k_aug, v_t, k_aug, v_t, q_aug_t, x, mod, w_out.astype(BF16), ln_g.reshape(1, d), ln_b.reshape(1, d),
      *extra_args)
    return outs if router else outs[0]


def _fox_layer(x, mod, w_in, b_f, w_out, ln_g, ln_b, router=None, *, bsz, seq, alpha):
    heads = b_f.shape[0]
    k_aug, q_aug_t, v_t = _fox_proj(x, mod, w_in, b_f, bsz=bsz, seq=seq, heads=heads)
    return _flash(k_aug, q_aug_t, v_t, x, mod, w_out, ln_g, ln_b, router, bsz=bsz, seq=seq, heads=heads,
                  alpha=alpha)


def _pool_kernel(x_ref, xh_ref, mod_ref, win_ref, wg_ref, ls_ref, wout_ref, g_ref, b_ref, o_ref, pooled_ref,
                 *, tm, seq, alpha):
    i = pl.program_id(0)
    pos0 = (i * tm) % seq
    m = mod_ref[0]
    x = x_ref[...]
    halo_ok = jnp.where(pos0 > 0, jnp.float32(1.0), jnp.float32(0.0))
    xe = jnp.concatenate([xh_ref[...], x], axis=0)
    ze = _dot(_modulate(xe, m, 0).astype(BF16), win_ref[...])
    rows = lax.broadcasted_iota(jnp.int32, (tm + POOL_HALO, 1), 0)
    ze = ze * jnp.where(rows < POOL_HALO, halo_ok, jnp.float32(1.0))
    pos = (pos0 + lax.broadcasted_iota(jnp.int32, (tm, 1), 0) + 1).astype(F32)
    gd = ze.shape[1] // len(POOL_WINDOWS)
    for g, win in enumerate(POOL_WINDOWS):
        cs = slice(g * gd, (g + 1) * gd)
        zg = ze[:, cs]
        s = zg
        span = 1
        while span < win:
            s = s + pltpu.roll(s, span, 0)
            span *= 2
        count = jnp.minimum(pos, jnp.float32(win))
        pooled = s[POOL_HALO:, :] / count - zg[POOL_HALO:, :]
        pooled_ref[:, cs] = (_dot(pooled.astype(BF16), wg_ref[g]) * ls_ref[:, cs]).astype(BF16)
    y = _dot(pooled_ref[...], wout_ref[...])
    o_ref[...] = _res_ln(x, y, m, 0, g_ref[...], b_ref[...], alpha)


def _pool_layer(x, mod, w_in, w_grp, scale, w_out, ln_g, ln_b, *, seq, alpha, tm=512):
    t, d = x.shape
    ng, gd, _ = w_grp.shape
    hb = tm // POOL_HALO
    return pl.pallas_call(
        functools.partial(_pool_kernel, tm=tm, seq=seq, alpha=alpha),
        grid=(t // tm,),
        in_specs=[
            pl.BlockSpec((tm, d), lambda i: (i, 0)),
            pl.BlockSpec((POOL_HALO, d), lambda i: (jnp.maximum(i * hb - 1, 0), 0)),
            pl.BlockSpec((1, 6, d), lambda i: (i * tm // seq, 0, 0)),
            _const_spec((d, d)),
            _const_spec((ng, gd, gd)),
            _const_spec((1, d)),
            _const_spec((d, d)),
            _const_spec((1, d)),
            _const_spec((1, d)),
        ],
        out_specs=pl.BlockSpec((tm, d), lambda i: (i, 0)),
        out_shape=jax.ShapeDtypeStruct((t, d), F32),
        scratch_shapes=[pltpu.VMEM((tm, d), BF16)],
        compiler_params=_cparams(("parallel",)),
        name="pool_mixer",
    )(x, x, mod, w_in.astype(BF16), w_grp.astype(BF16), scale.reshape(1, d), w_out.astype(BF16),
      ln_g.reshape(1, d), ln_b.reshape(1, d))


def kernel(x, c, mod_w, mod_b, ln_g, ln_b, gm_w_in, gm_ln_g, gm_ln_b, gm_w_s, gm_b_s, gm_w_out, fox_w_in,
           fox_b_f, fox_w_out, pool_w_in, pool_w_grp, pool_scale, pool_w_out, ffn_w13, ffn_w2, moe_w_router,
           moe_b_router, moe_w13, moe_w2):
    bsz, seq, d = x.shape
    depth = mod_w.shape[0]
    alpha = (2 * depth) ** 0.25
    n_mixers = 3
    mod_all = _modulation(c, mod_w, mod_b)
    xt = x.reshape(bsz * seq, d)
    for i in range(depth):
        mod = mod_all[i]
        kind, j = i % n_mixers, i // n_mixers
        router = (moe_w_router[i // 2], moe_b_router[i // 2]) if i % 2 else None
        if kind == 0:
            xt = _gmlp_layer(xt, mod, gm_w_in[j], gm_ln_g[j], gm_ln_b[j], gm_w_s[j], gm_b_s[j], gm_w_out[j],
                             ln_g[i, 0], ln_b[i, 0], router, seq=seq, alpha=alpha)
        elif kind == 1:
            xt = _fox_layer(xt, mod, fox_w_in[j], fox_b_f[j], fox_w_out[j], ln_g[i, 0], ln_b[i, 0], router,
                            bsz=bsz, seq=seq, alpha=alpha)
        else:
            assert router is None, "the pooling mixer has no routing epilogue"
            xt = _pool_layer(xt, mod, pool_w_in[j], pool_w_grp[j], pool_scale[j], pool_w_out[j],
                             ln_g[i, 0], ln_b[i, 0], seq=seq, alpha=alpha)
        if router is None:
            xt = _ffn_layer(xt, mod, ffn_w13[i // 2], ffn_w2[i // 2], ln_g[i, 1], ln_b[i, 1],
                            seq=seq, alpha=alpha)
        else:
            xt = _moe_layer(xt, mod, moe_w13, moe_w2, i // 2, ln_g[i, 1], ln_b[i, 1], seq=seq, alpha=alpha)
    return xt.reshape(bsz, seq, d)
```

```python
import functools

import numpy as np
import jax
import jax.numpy as jnp
from jax import lax
from jax.experimental import pallas as pl
from jax.experimental.pallas import tpu as pltpu
from jax.experimental.pallas import tpu_sc as plsc

F32 = jnp.float32
BF16 = jnp.bfloat16

POOL_WINDOWS = (2, 4, 8, 16)
TOP_K = 2
EXPERT_BLOCK = 512
LN_EPS = 1e-5
LANES = 128
POOL_HALO = 16
VMEM_LIMIT = 56 * 1024 * 1024
SC_CORES = 2
SC_SUBCORES = 16
SC_ROWS = 128


def _cparams(sem):
    return pltpu.CompilerParams(dimension_semantics=sem, vmem_limit_bytes=VMEM_LIMIT)


def _const_spec(shape):
    nd = len(shape)
    return pl.BlockSpec(shape, lambda *_: (0,) * nd, pipeline_mode=pl.Buffered(1))


def _layer_norm(r, g, b):
    mu = jnp.mean(r, axis=-1, keepdims=True)
    xc = r - mu
    var = jnp.mean(xc * xc, axis=-1, keepdims=True)
    return xc * lax.rsqrt(var + LN_EPS) * g + b


def _modulate(x, m, off):
    return x * (1.0 + m[off + 1:off + 2]) + m[off:off + 1]


def _res_ln(x, y, m, off, g, b, alpha):
    return _layer_norm(alpha * x + (1.0 + m[off + 2:off + 3]) * y, g, b)


def _split3(a):
    hi = a.astype(BF16)
    r1 = a - hi.astype(F32)
    mid = r1.astype(BF16)
    lo = (r1 - mid.astype(F32)).astype(BF16)
    return hi, mid, lo


def _dot(a, b):
    return jnp.dot(a, b, preferred_element_type=F32)


def _dot_split(a_hi, a_mid, w):
    n = w.shape[1]
    w_hi = w.astype(BF16)
    w_lo = (w - w_hi.astype(F32)).astype(BF16)
    both = _dot(a_hi, jnp.concatenate([w_hi, w_lo], axis=1))
    return both[:, :n] + both[:, n:] + _dot(a_mid, w_hi)


def _pack_bf16_pair(x):
    n = x.shape[1] // 2
    hi = lax.bitcast_convert_type(x[:, :n].astype(BF16).astype(F32), jnp.uint32)
    lo = lax.bitcast_convert_type(x[:, n:].astype(BF16).astype(F32), jnp.uint32)
    return hi | (lo >> 16)


def _unpack_bf16_pair(w):
    hi = lax.bitcast_convert_type(w & jnp.uint32(0xFFFF0000), F32)
    lo = lax.bitcast_convert_type(w << 16, F32)
    return jnp.concatenate([hi, lo], axis=1)


def _mod_kernel(c_ref, w_ref, b_ref, o_ref):
    c = c_ref[...]
    s = c / (1.0 + jnp.exp(-c))
    s_hi, s_mid, _ = _split3(s)
    w = w_ref[0]
    w_hi = w.astype(BF16)
    w_lo = (w - w_hi.astype(F32)).astype(BF16)
    acc = _dot(s_hi, w_hi) + _dot(s_mid, w_hi) + _dot(s_hi, w_lo)
    o_ref[0] = acc + b_ref[0]


def _modulation(c, mod_w, mod_b):
    depth, d, n = mod_w.shape
    bsz = c.shape[0]
    rows = 8
    tn = n // 4
    c_pad = jnp.zeros((rows, d), F32).at[:bsz].set(c)
    out = pl.pallas_call(
        _mod_kernel,
        grid=(depth, n // tn),
        in_specs=[
            pl.BlockSpec((rows, d), lambda l, j: (0, 0)),
            pl.BlockSpec((1, d, tn), lambda l, j: (l, 0, j)),
            pl.BlockSpec((1, 1, tn), lambda l, j: (l, 0, j)),
        ],
        out_specs=pl.BlockSpec((1, rows, tn), lambda l, j: (l, 0, j)),
        out_shape=jax.ShapeDtypeStruct((depth, rows, n), F32),
        compiler_params=_cparams(("parallel", "parallel")),
        name="adaln_mod",
    )(c_pad, mod_w, mod_b.reshape(depth, 1, n))
    return out[:, :bsz, :].reshape(depth, bsz, 6, d)


def _gmlp_kernel(x_ref, mod_ref, win_ref, vg_ref, vb_ref, ws_ref, bst_ref, wout_ref, g_ref, b_ref,
                 *rest, tm, sub, chunk, groups, alpha, n_exp, region):
    if n_exp:
        wr_ref, br_ref, o_ref, hp_ref, meta_ref, cnt_ref, gated_ref, carry_ref = rest

        @pl.when(pl.program_id(0) == 0)
        def _():
            carry_ref[...] = jnp.zeros_like(carry_ref)
    else:
        o_ref, gated_ref = rest
    m = mod_ref[0]
    row = lax.broadcasted_iota(jnp.int32, (chunk, chunk), 0)
    col = lax.broadcasted_iota(jnp.int32, (chunk, chunk), 1)
    causal = row >= col
    bst = bst_ref[...]
    w_mix = [jnp.where(causal, ws_ref[g], 0.0).astype(BF16) for g in range(groups)]
    n = tm // sub

    def project(s):
        x = x_ref[s * sub:(s + 1) * sub, :]
        return x, _dot(_modulate(x, m, 0).astype(BF16), win_ref[...])

    def activate(z):
        z = 0.5 * z * (1.0 + lax.erf(z * (2.0 ** -0.5)))
        width = z.shape[1] // 2
        return z[:, :width], _layer_norm(z[:, width:], vg_ref[...], vb_ref[...]).astype(BF16)

    def mix(s, u, v):
        gd = u.shape[1] // groups
        for g in range(groups):
            cs = slice(g * gd, (g + 1) * gd)
            for c in range(sub // chunk):
                rs = slice(c * chunk, (c + 1) * chunk)
                mixed = _dot(w_mix[g], v[rs, cs]) + bst[:, g:g + 1]
                gated_ref[s * sub + c * chunk:s * sub + (c + 1) * chunk, cs] = (u[rs, cs] * mixed).astype(BF16)
        return _dot(gated_ref[s * sub:(s + 1) * sub, :], wout_ref[...])

    def finish(s, x, y):
        rows = slice(s * sub, (s + 1) * sub)
        x_new = _res_ln(x, y, m, 0, g_ref[...], b_ref[...], alpha)
        o_ref[rows, :] = x_new
        if n_exp:
            hp_ref[rows, :], meta_ref[rows, :] = _route_rows(x_new, m, wr_ref, br_ref, carry_ref, n_exp, region)

    nxt = project(0)
    pending = None
    for s in range(n):
        x, z = nxt
        if s + 1 < n:
            nxt = project(s + 1)
        u, v = activate(z)
        if pending is not None:
            finish(*pending)
        pending = (s, x, mix(s, u, v))
    finish(*pending)
    if n_exp:
        cnt_ref[...] = carry_ref[...]


def _gmlp_layer(x, mod, w_in, v_g, v_b, w_s, b_s, w_out, ln_g, ln_b, router=None, *, seq, alpha, tm=512,
                sub=256):
    t, d = x.shape
    groups, chunk, _ = w_s.shape
    width = w_out.shape[0]
    n_exp = router[0].shape[1] if router else 0
    kern = functools.partial(_gmlp_kernel, tm=tm, sub=sub, chunk=chunk, groups=groups, alpha=alpha, n_exp=n_exp,
                             region=t)
    row_map = lambda i: (i, 0)
    out_specs, out_shape = [pl.BlockSpec((tm, d), row_map)], [jax.ShapeDtypeStruct((t, d), F32)]
    scratch = [pltpu.VMEM((tm, width), BF16)]
    extra_specs, extra_args = [], []
    if router:
        r_specs, r_shapes = _router_out(t, d, tm, row_map)
        out_specs, out_shape = out_specs + r_specs, out_shape + r_shapes
        scratch.append(pltpu.VMEM((8, LANES), F32))
        extra_specs = [_const_spec((d, LANES)), _const_spec((1, LANES))]
        extra_args = list(_router_operands(*router))
    outs = pl.pallas_call(
        kern,
        grid=(t // tm,),
        in_specs=[
            pl.BlockSpec((tm, d), lambda i: (i, 0)),
            pl.BlockSpec((1, 6, d), lambda i: (i * tm // seq, 0, 0)),
            _const_spec((d, 2 * width)),
            _const_spec((1, width)),
            _const_spec((1, width)),
            _const_spec((groups, chunk, chunk)),
            _const_spec((chunk, groups)),
            _const_spec((width, d)),
            _const_spec((1, d)),
            _const_spec((1, d)),
        ] + extra_specs,
        out_specs=out_specs,
        out_shape=out_shape,
        scratch_shapes=scratch,
        compiler_params=_cparams(("arbitrary",)),
        name="gmlp_mixer",
    )(x, mod, w_in.astype(BF16), v_g.reshape(1, width), v_b.reshape(1, width), w_s, b_s.T,
      w_out.astype(BF16), ln_g.reshape(1, d), ln_b.reshape(1, d), *extra_args)
    return outs if router else outs[0]


def _swiglu_pipelined(hs, w1, w3, w2, emit):
    up = lambda h: (_dot(h, w1), _dot(h, w3))
    nxt = up(hs[0])
    for s in range(len(hs)):
        a, b = nxt
        if s + 1 < len(hs):
            nxt = up(hs[s + 1])
        t = (a / (1.0 + jnp.exp(-a)) * b).astype(BF16)
        emit(s, _dot(t, w2))


def _ffn_kernel(x_ref, mod_ref, w1_ref, w3_ref, w2_ref, g_ref, b_ref, o_ref, h_ref, *, alpha, sub):
    j = pl.program_id(1)
    last = pl.num_programs(1) - 1
    subs = [slice(s, s + sub) for s in range(0, x_ref.shape[0], sub)]
    weights = lambda: (w1_ref[...], w3_ref[...], w2_ref[...])

    @pl.when(j == 0)
    def _():
        hs = []
        for rows in subs:
            h = _modulate(x_ref[rows, :], mod_ref[0], 3).astype(BF16)
            h_ref[rows, :] = h
            hs.append(h)

        def emit(s, part):
            o_ref[subs[s], :] = part

        _swiglu_pipelined(hs, *weights(), emit)

    @pl.when(jnp.logical_and(j > 0, j < last))
    def _():
        def emit(s, part):
            o_ref[subs[s], :] += part

        _swiglu_pipelined([h_ref[rows, :] for rows in subs], *weights(), emit)

    @pl.when(j == last)
    def _():
        def emit(s, part):
            rows = subs[s]
            y = o_ref[rows, :] + part
            o_ref[rows, :] = _res_ln(x_ref[rows, :], y, mod_ref[0], 3, g_ref[...], b_ref[...], alpha)

        _swiglu_pipelined([h_ref[rows, :] for rows in subs], *weights(), emit)


def _ffn_layer(x, mod, w13, w2, ln_g, ln_b, *, seq, alpha, tm=1024, tf=512):
    t, d = x.shape
    f = w2.shape[0]
    nf = f // tf
    assert nf >= 2
    w13b = w13.astype(BF16)
    return pl.pallas_call(
        functools.partial(_ffn_kernel, alpha=alpha, sub=tm // 4),
        grid=(t // tm, nf),
        in_specs=[
            pl.BlockSpec((tm, d), lambda i, j: (i, 0)),
            pl.BlockSpec((1, 6, d), lambda i, j: (i * tm // seq, 0, 0)),
            pl.BlockSpec((d, tf), lambda i, j: (0, j)),
            pl.BlockSpec((d, tf), lambda i, j: (0, nf + j)),
            pl.BlockSpec((tf, d), lambda i, j: (j, 0)),
            _const_spec((1, d)),
            _const_spec((1, d)),
        ],
        out_specs=pl.BlockSpec((tm, d), lambda i, j: (i, 0)),
        out_shape=jax.ShapeDtypeStruct((t, d), F32),
        scratch_shapes=[pltpu.VMEM((tm, d), BF16)],
        compiler_params=_cparams(("parallel", "arbitrary")),
        name="swiglu_dense",
    )(x, mod, w13b, w13b, w2.astype(BF16), ln_g.reshape(1, d), ln_b.reshape(1, d))


MOE_GROUP = 2


def _moe_ffn_kernel(ge_ref, gb_ref, gr_ref, h_ref, w1_ref, w3_ref, w2_ref, o_ref, acc_ref, *, sub):
    g = pl.program_id(0)
    j = pl.program_id(1)
    last = pl.num_programs(1) - 1
    n_rows = gr_ref[g]
    nv = (n_rows + EXPERT_BLOCK - 1) // EXPERT_BLOCK

    def run(n_blocks, phase):
        subs = [slice(s, s + sub) for s in range(0, n_blocks * EXPERT_BLOCK, sub)]
        row = lax.broadcasted_iota(jnp.int32, (sub, 1), 0)

        def rows_in(rows):
            w = jnp.where(row + rows.start < n_rows, h_ref[rows, :], jnp.uint32(0))
            return _unpack_bf16_pair(w).astype(BF16)

        def emit(s, part):
            if phase == "first":
                acc_ref[subs[s], :] = part
            elif phase == "middle":
                acc_ref[subs[s], :] += part
            else:
                o_ref[subs[s], :] = _pack_bf16_pair(acc_ref[subs[s], :] + part)

        _swiglu_pipelined([rows_in(rows) for rows in subs], w1_ref[0, 0].astype(BF16), w3_ref[0, 0].astype(BF16),
                          w2_ref[0, 0].astype(BF16), emit)

    for n_blocks in range(1, MOE_GROUP + 1):
        @pl.when(jnp.logical_and(nv == n_blocks, j == 0))
        def _():
            run(n_blocks, "first")

        @pl.when(jnp.logical_and(nv == n_blocks, jnp.logical_and(j > 0, j < last)))
        def _():
            run(n_blocks, "middle")

        @pl.when(jnp.logical_and(nv == n_blocks, j == last))
        def _():
            run(n_blocks, "last")

    for blk in range(MOE_GROUP):
        @pl.when(jnp.logical_and(blk >= nv, j == last))
        def _():
            o_ref[blk * EXPERT_BLOCK:(blk + 1) * EXPERT_BLOCK, :] = jnp.zeros((EXPERT_BLOCK, o_ref.shape[1]),
                                                                              jnp.uint32)


def _moe_ffn(h_slots, group_expert, group_block, group_rows, w13, w2, layer, *, tf=512):
    half = h_slots.shape[1]
    d = 2 * half
    f = w2.shape[2]
    nf = f // tf
    assert nf >= 2
    tm = MOE_GROUP * EXPERT_BLOCK

    def jj(g, j, gr):
        return jnp.where(gr[g] > 0, j, nf - 1)

    grid_spec = pltpu.PrefetchScalarGridSpec(
        num_scalar_prefetch=3,
        grid=(group_expert.shape[0], nf),
        in_specs=[
            pl.BlockSpec((tm, half), lambda g, j, ge, gb, gr: (gb[g], 0)),
            pl.BlockSpec((1, 1, d, tf), lambda g, j, ge, gb, gr: (layer, ge[g], 0, jj(g, j, gr))),
            pl.BlockSpec((1, 1, d, tf), lambda g, j, ge, gb, gr: (layer, ge[g], 0, nf + jj(g, j, gr))),
            pl.BlockSpec((1, 1, tf, d), lambda g, j, ge, gb, gr: (layer, ge[g], jj(g, j, gr), 0)),
        ],
        out_specs=pl.BlockSpec((tm, half), lambda g, j, ge, gb, gr: (gb[g], 0)),
        scratch_shapes=[pltpu.VMEM((tm, d), F32)],
    )
    return pl.pallas_call(
        functools.partial(_moe_ffn_kernel, sub=EXPERT_BLOCK // 2),
        grid_spec=grid_spec,
        out_shape=jax.ShapeDtypeStruct(h_slots.shape, jnp.uint32),
        compiler_params=_cparams(("arbitrary", "arbitrary")),
        name="swiglu_experts",
    )(group_expert, group_block, group_rows, h_slots, w13, w13, w2)


def _route_rows(x_rows, m, wr_ref, br_ref, carry_ref, n_exp, region):
    tm = x_rows.shape[0]
    h = _modulate(x_rows, m, 3)
    h_hi, h_mid, _ = _split3(h)
    logits = _dot_split(h_hi, h_mid, wr_ref[...]) + br_ref[...]
    lane = lax.broadcasted_iota(jnp.int32, (tm, LANES), 1)
    neg = jnp.float32(-jnp.inf)
    logits = jnp.where(lane < n_exp, logits, neg)
    v0 = jnp.max(logits, axis=1, keepdims=True)
    e0 = jnp.min(jnp.where(logits == v0, lane, LANES), axis=1, keepdims=True)
    rest = jnp.where(lane == e0, neg, logits)
    v1 = jnp.max(rest, axis=1, keepdims=True)
    e1 = jnp.min(jnp.where(rest == v1, lane, LANES), axis=1, keepdims=True)
    p = jnp.exp(v1 - v0)
    g0 = 1.0 / (1.0 + p)
    g1 = p / (1.0 + p)
    oh0 = lane == e0
    oh1 = lane == e1
    onehot = jnp.where(jnp.logical_or(oh0, oh1), 1.0, 0.0).astype(BF16)
    r = lax.broadcasted_iota(jnp.int32, (tm, tm), 0)
    c = lax.broadcasted_iota(jnp.int32, (tm, tm), 1)
    strict = jnp.where(r > c, 1.0, 0.0).astype(BF16)
    before = _dot(strict, onehot) + carry_ref[0:1, :]
    rank0 = jnp.sum(jnp.where(oh0, before, 0.0), axis=1, keepdims=True)
    rank1 = jnp.sum(jnp.where(oh1, before, 0.0), axis=1, keepdims=True)
    total = carry_ref[0:1, :] + jnp.sum(onehot.astype(F32), axis=0, keepdims=True)
    carry_ref[...] = jnp.broadcast_to(total, carry_ref.shape)
    meta = jnp.where(lane == 0, e0.astype(F32), 0.0)
    meta = jnp.where(lane == 1, e1.astype(F32), meta)
    meta = jnp.where(lane == 2, g0, meta)
    meta = jnp.where(lane == 3, g1, meta)
    meta = jnp.where(lane == 4, e0.astype(F32) * region + rank0, meta)
    meta = jnp.where(lane == 5, e1.astype(F32) * region + rank1, meta)
    return _pack_bf16_pair(h), meta


def _router_operands(w_router, b_router):
    d, n_exp = w_router.shape
    wr = jnp.zeros((d, LANES), F32).at[:, :n_exp].set(w_router)
    br = jnp.zeros((1, LANES), F32).at[0, :n_exp].set(b_router)
    return wr, br


def _router_out(t, d, tm, row_map):
    specs = [pl.BlockSpec((tm, d // 2), row_map), pl.BlockSpec((tm, LANES), row_map),
             pl.BlockSpec((8, LANES), lambda *_: (0, 0))]
    shapes = [jax.ShapeDtypeStruct((t, d // 2), jnp.uint32), jax.ShapeDtypeStruct((t, LANES), F32),
              jax.ShapeDtypeStruct((8, LANES), F32)]
    return specs, shapes


def _combine_kernel(x_ref, ya_ref, yb_ref, meta_ref, mod_ref, g_ref, b_ref, o_ref, *, alpha):
    meta = meta_ref[...]
    y = meta[:, 2:3] * _unpack_bf16_pair(ya_ref[...]) + meta[:, 3:4] * _unpack_bf16_pair(yb_ref[...])
    o_ref[...] = _res_ln(x_ref[...], y, mod_ref[0], 3, g_ref[...], b_ref[...], alpha)


def _combine(x, ya, yb, meta, mod, ln_g, ln_b, *, seq, alpha, tm=512):
    t, d = x.shape
    row = pl.BlockSpec((tm, d), lambda i: (i, 0))
    packed = pl.BlockSpec((tm, d // 2), lambda i: (i, 0))
    return pl.pallas_call(
        functools.partial(_combine_kernel, alpha=alpha),
        grid=(t // tm,),
        in_specs=[row, packed, packed,
                  pl.BlockSpec((tm, LANES), lambda i: (i, 0)),
                  pl.BlockSpec((1, 6, d), lambda i: (i * tm // seq, 0, 0)),
                  _const_spec((1, d)), _const_spec((1, d))],
        out_specs=row,
        out_shape=jax.ShapeDtypeStruct((t, d), F32),
        compiler_params=_cparams(("parallel",)),
        name="moe_combine",
    )(x, ya, yb, meta, mod, ln_g.reshape(1, d), ln_b.reshape(1, d))


def _sc_mesh():
    return plsc.VectorSubcoreMesh(core_axis_name="c", subcore_axis_name="s", num_cores=SC_CORES,
                                  num_subcores=SC_SUBCORES)


def _sc_worker_share(n):
    workers = SC_CORES * SC_SUBCORES
    per_worker = n // workers
    steps = per_worker // SC_ROWS
    assert steps * SC_ROWS * workers == n
    return per_worker, steps


def _sc_gather_rows(table, idx_a, idx_b):
    n = idx_a.shape[0]
    width = table.shape[1]
    per_worker, steps = _sc_worker_share(n)

    def body(table_hbm, ia_hbm, ib_hbm, oa_hbm, ob_hbm, idx_v, rows_v, sem):
        wid = lax.axis_index("s") * SC_CORES + lax.axis_index("c")

        @pl.loop(0, steps)
        def _(j):
            off = pl.multiple_of(wid * per_worker + j * SC_ROWS, SC_ROWS)
            for idx_hbm, out_hbm in ((ia_hbm, oa_hbm), (ib_hbm, ob_hbm)):
                pltpu.sync_copy(idx_hbm.at[pl.ds(off, SC_ROWS)], idx_v)
                pltpu.async_copy(table_hbm.at[idx_v], rows_v, sem).wait()
                pltpu.sync_copy(rows_v, out_hbm.at[pl.ds(off, SC_ROWS)])

    out = jax.ShapeDtypeStruct((n, width), table.dtype)
    return pl.kernel(
        body,
        out_type=(out, out),
        mesh=_sc_mesh(),
        scratch_types=[
            pltpu.VMEM((SC_ROWS,), jnp.int32),
            pltpu.VMEM((SC_ROWS, width), table.dtype),
            pltpu.SemaphoreType.DMA,
        ],
        name="sc_row_gather",
    )(table, idx_a, idx_b)


def _sc_scatter_rows(rows, dest_a, dest_b, n_out):
    n, width = rows.shape
    per_worker, steps = _sc_worker_share(n)

    def body(rows_hbm, da_hbm, db_hbm, out_hbm, ia_v, ib_v, rows_v, sem):
        wid = lax.axis_index("s") * SC_CORES + lax.axis_index("c")

        @pl.loop(0, steps)
        def _(j):
            off = pl.multiple_of(wid * per_worker + j * SC_ROWS, SC_ROWS)
            pltpu.sync_copy(da_hbm.at[pl.ds(off, SC_ROWS)], ia_v)
            pltpu.sync_copy(db_hbm.at[pl.ds(off, SC_ROWS)], ib_v)
            pltpu.sync_copy(rows_hbm.at[pl.ds(off, SC_ROWS)], rows_v)
            pltpu.async_copy(rows_v, out_hbm.at[ia_v], sem).wait()
            pltpu.async_copy(rows_v, out_hbm.at[ib_v], sem).wait()

    return pl.kernel(
        body,
        out_type=jax.ShapeDtypeStruct((n_out, width), rows.dtype),
        mesh=_sc_mesh(),
        scratch_types=[
            pltpu.VMEM((SC_ROWS,), jnp.int32),
            pltpu.VMEM((SC_ROWS,), jnp.int32),
            pltpu.VMEM((SC_ROWS, width), rows.dtype),
            pltpu.SemaphoreType.DMA,
        ],
        name="sc_row_scatter",
    )(rows, dest_a, dest_b)


def _moe_layer(routed, mod, w13, w2, layer, ln_g, ln_b, *, seq, alpha):
    x, h, meta, cnt = routed
    t, d = x.shape
    n_exp = w13.shape[1]
    dest0 = meta[:, 4].astype(jnp.int32)
    dest1 = meta[:, 5].astype(jnp.int32)
    counts = cnt[0, :n_exp].astype(jnp.int32)
    group_rows = MOE_GROUP * EXPERT_BLOCK
    assert t % group_rows == 0
    region_groups = t // group_rows
    groups_per_expert = (counts + group_rows - 1) // group_rows
    group_end = jnp.cumsum(groups_per_expert)
    g = jnp.arange(t * TOP_K // group_rows + n_exp, dtype=jnp.int32)
    expert = jnp.minimum(jnp.sum(group_end[None, :] <= g[:, None], axis=1), n_exp - 1).astype(jnp.int32)
    blk_in_expert = g - (group_end - groups_per_expert)[expert]
    used = g < group_end[-1]
    spare_block = n_exp * region_groups
    group_block = jnp.where(used, expert * region_groups + blk_in_expert, spare_block).astype(jnp.int32)
    rows_in_group = jnp.where(used, jnp.clip(counts[expert] - blk_in_expert * group_rows, 0, group_rows),
                              0).astype(jnp.int32)
    h_slots = _sc_scatter_rows(h, dest0, dest1, (spare_block + 1) * group_rows)
    y_slots = _moe_ffn(h_slots, expert, group_block, rows_in_group, w13, w2, layer)
    ya, yb = _sc_gather_rows(y_slots, dest0, dest1)
    return _combine(x, ya, yb, meta, mod, ln_g, ln_b, seq=seq, alpha=alpha)


HEAD_PAD = LANES
GATE_PARTS = 3
GATE_ROWS = 16
VAL_PAD = 16
LOG2E = 1.4426950408889634


def _head_lanes(h, hd):
    k0 = (h % (HEAD_PAD // hd)) * hd
    return k0, (hd if k0 == 0 else 0)


def _fox_proj_kernel(x_ref, mod_ref, wf_ref, bf_ref, wq_ref, wk_ref, wv_ref, selk_ref, onek_ref, selq_ref,
                     oneq_ref, onev_ref, k_ref, q_ref, v_ref, carry_ref, *, q_scale, heads, hd, nb):
    tm = x_ref.shape[0]

    @pl.when(pl.program_id(0) % nb == 0)
    def _():
        carry_ref[...] = jnp.zeros_like(carry_ref)

    h_hi, h_mid, _ = _split3(_modulate(x_ref[...], mod_ref[0], 0))

    f = _dot_split(h_hi, h_mid, wf_ref[...]) + bf_ref[...]
    lf = jnp.minimum(f, 0.0) - jnp.log(1.0 + jnp.exp(-jnp.abs(f)))
    r = lax.broadcasted_iota(jnp.int32, (tm, tm), 0)
    c = lax.broadcasted_iota(jnp.int32, (tm, tm), 1)
    tri = jnp.where(r >= c, 1.0, 0.0).astype(BF16)
    sums = _dot(tri, jnp.concatenate(_split3(lf), axis=1))
    cs = carry_ref[0:1, :] + sum(sums[:, n * LANES:(n + 1) * LANES] for n in range(GATE_PARTS))
    carry_ref[...] = jnp.broadcast_to(cs[tm - 1:tm, :], carry_ref.shape)
    cs = cs * LOG2E

    gate_k = onek_ref[...] + _dot(jnp.concatenate(_split3(cs), axis=1), selk_ref[...])
    kc = _dot(h_hi, wk_ref[...])
    lane = lax.broadcasted_iota(jnp.int32, (tm, HEAD_PAD), 1)
    for h in range(heads):
        k0, _ = _head_lanes(h, hd)
        src = (h * hd // HEAD_PAD) * HEAD_PAD
        is_k = jnp.logical_and(lane >= k0, lane < k0 + hd)
        hs = slice(h * HEAD_PAD, (h + 1) * HEAD_PAD)
        k_ref[:, hs] = jnp.where(is_k, kc[:, src:src + HEAD_PAD], gate_k[:, hs]).astype(BF16)

    nt = (((1,), (1,)), ((), ()))
    qc = lax.dot_general(wq_ref[...], h_hi, nt, preferred_element_type=F32) * (q_scale * LOG2E)
    gate_q = oneq_ref[...] + _dot(selq_ref[...], jnp.concatenate(_split3(cs.T), axis=0))
    fill = HEAD_PAD - hd - GATE_ROWS
    for h in range(heads):
        k0, g0 = _head_lanes(h, hd)
        base = h * HEAD_PAD
        q_ref[0, base + k0:base + k0 + hd, :] = qc[h * hd:(h + 1) * hd, :].astype(BF16)
        q_ref[0, base + g0:base + g0 + GATE_ROWS, :] = gate_q[h * GATE_ROWS:(h + 1) * GATE_ROWS, :].astype(BF16)
        q_ref[0, base + g0 + GATE_ROWS:base + g0 + GATE_ROWS + fill, :] = jnp.zeros((fill, tm), BF16)

    v = lax.dot_general(wv_ref[...], h_hi, nt, preferred_element_type=F32) + onev_ref[...]
    v_ref[0] = v.astype(BF16)


def _fox_proj(x, mod, w_in, b_f, *, bsz, seq, heads, tm=512):
    t, d = x.shape
    hd = d // heads
    assert HEAD_PAD == 2 * hd and GATE_ROWS >= 2 * GATE_PARTS
    nb = seq // tm
    wide = heads * HEAD_PAD
    wq_t = w_in[:, :d].T.astype(BF16)
    wk = w_in[:, d:2 * d].astype(BF16)
    vrows = heads * (hd + VAL_PAD)
    wv_t = jnp.pad(w_in[:, 2 * d:3 * d].reshape(d, heads, hd),
                   ((0, 0), (0, 0), (0, VAL_PAD))).reshape(d, vrows).T.astype(BF16)
    wf = jnp.zeros((d, LANES), F32).at[:, :heads].set(w_in[:, 3 * d:])
    bf = jnp.zeros((1, LANES), F32).at[0, :heads].set(b_f)
    sel_k = np.zeros((GATE_PARTS, LANES, wide), np.float32)
    one_k = np.zeros((1, wide), np.float32)
    sel_q = np.zeros((GATE_PARTS, heads * GATE_ROWS, LANES), np.float32)
    one_q = np.zeros((heads * GATE_ROWS, 1), np.float32)
    one_v = np.zeros((vrows, 1), np.float32)
    for hh in range(heads):
        _, g0 = _head_lanes(hh, hd)
        one_v[hh * (hd + VAL_PAD) + hd, 0] = 1.0
        for n in range(GATE_PARTS):
            sel_k[n, hh, hh * HEAD_PAD + g0 + n] = -1.0
            one_q[hh * GATE_ROWS + n, 0] = 1.0
            one_k[0, hh * HEAD_PAD + g0 + GATE_PARTS + n] = 1.0
            sel_q[n, hh * GATE_ROWS + GATE_PARTS + n, hh] = 1.0
    return pl.pallas_call(
        functools.partial(_fox_proj_kernel, q_scale=hd ** -0.5, heads=heads, hd=hd, nb=nb),
        grid=(t // tm,),
        in_specs=[
            pl.BlockSpec((tm, d), lambda i: (i, 0)),
            pl.BlockSpec((1, 6, d), lambda i: (i * tm // seq, 0, 0)),
            _const_spec((d, LANES)),
            _const_spec((1, LANES)),
            _const_spec((d, d)),
            _const_spec((d, d)),
            _const_spec((vrows, d)),
            _const_spec((GATE_PARTS * LANES, wide)),
            _const_spec((1, wide)),
            _const_spec((heads * GATE_ROWS, GATE_PARTS * LANES)),
            _const_spec((heads * GATE_ROWS, 1)),
            _const_spec((vrows, 1)),
        ],
        out_specs=[
            pl.BlockSpec((tm, wide), lambda i: (i, 0)),
            pl.BlockSpec((1, wide, tm), lambda i: (i // nb, 0, i % nb)),
            pl.BlockSpec((1, vrows, tm), lambda i: (i // nb, 0, i % nb)),
        ],
        out_shape=[
            jax.ShapeDtypeStruct((t, wide), BF16),
            jax.ShapeDtypeStruct((bsz, wide, seq), BF16),
            jax.ShapeDtypeStruct((bsz, vrows, seq), BF16),
        ],
        scratch_shapes=[pltpu.VMEM((8, LANES), F32)],
        compiler_params=_cparams(("arbitrary",)),
        name="fox_proj",
    )(x, mod, wf, bf, wq_t, wk, wv_t, jnp.asarray(sel_k.reshape(GATE_PARTS * LANES, wide), BF16), jnp.asarray(one_k),
      jnp.asarray(np.concatenate(list(sel_q), axis=1), BF16), jnp.asarray(one_q), jnp.asarray(one_v))


def _flash_kernel(qi_ref, kj_ref, k_ref, q_ref, v_ref, x_ref, mod_ref, wout_ref, g_ref, b_ref, *rest,
                  heads, hd, blk, alpha, n_exp, region):
    p_idx = pl.program_id(1)
    if n_exp:
        wr_ref, br_ref, o_ref, hp_ref, meta_ref, cnt_ref, m_ref, acc_ref, s_ref, carry_ref = rest

        @pl.when(jnp.logical_and(pl.program_id(0) == 0, p_idx == 0))
        def _():
            carry_ref[...] = jnp.zeros_like(carry_ref)
    else:
        o_ref, m_ref, acc_ref, s_ref = rest
    qi = qi_ref[p_idx]
    kj = kj_ref[p_idx]
    vp = hd + VAL_PAD

    @pl.when(kj == 0)
    def _():
        m_ref[...] = jnp.full(m_ref.shape, -jnp.inf, F32)
        acc_ref[...] = jnp.zeros_like(acc_ref)

    def step(masked):
        if masked:
            kpos = lax.broadcasted_iota(jnp.int32, (blk, blk), 0)
            qpos = lax.broadcasted_iota(jnp.int32, (blk, blk), 1)
            keep = kpos <= qpos

        def scores(h):
            hs = slice(h * HEAD_PAD, (h + 1) * HEAD_PAD)
            s = _dot(k_ref[:, hs], q_ref[0, hs, :])
            if masked:
                s = jnp.where(keep, s, -jnp.inf)
            s_ref[h % depth] = s
            return jnp.max(s, axis=0, keepdims=True)

        depth = s_ref.shape[0]
        ahead = [scores(h) for h in range(depth - 1)]
        for h in range(heads):
            if h + depth - 1 < heads:
                ahead.append(scores(h + depth - 1))
            m_cur = ahead.pop(0)
            m_prev = m_ref[h:h + 1, :]
            m_new = jnp.maximum(m_prev, m_cur)
            m_ref[h:h + 1, :] = m_new
            a = jnp.exp2(m_prev - m_new)
            p = jnp.exp2(s_ref[h % depth] - m_new).astype(BF16)
            rs = slice(h * vp, (h + 1) * vp)
            acc_ref[rs, :] = acc_ref[rs, :] * a + _dot(v_ref[0, rs, :], p)

    @pl.when(kj < qi)
    def _():
        step(False)

    @pl.when(kj == qi)
    def _():
        step(True)
        per = LANES // hd
        cols = []
        for g in range(heads // per):
            parts = []
            for n in range(per):
                r0 = (g * per + n) * vp
                parts.append(acc_ref[r0:r0 + hd, :] / acc_ref[r0 + hd:r0 + hd + 1, :])
            cols.append(jnp.concatenate(parts, axis=0).T.astype(BF16))
        y = _dot(jnp.concatenate(cols, axis=1), wout_ref[...])
        x_new = _res_ln(x_ref[...], y, mod_ref[0], 0, g_ref[...], b_ref[...], alpha)
        o_ref[...] = x_new
        if n_exp:
            hp_ref[...], meta_ref[...] = _route_rows(x_new, mod_ref[0], wr_ref, br_ref, carry_ref, n_exp, region)
            cnt_ref[...] = carry_ref[...]


def _flash(k_aug, q_aug_t, v_t, x, mod, w_out, ln_g, ln_b, router=None, *, bsz, seq, heads, alpha, blk=512):
    t, wide = k_aug.shape
    vrows = v_t.shape[1]
    hd = vrows // heads - VAL_PAD
    d = heads * hd
    nq = seq // blk
    pairs = [(i, j) for i in range(nq) for j in range(i + 1)]
    qi = jnp.asarray(np.array([p[0] for p in pairs], np.int32))
    kj = jnp.asarray(np.array([p[1] for p in pairs], np.int32))
    n_exp = router[0].shape[1] if router else 0
    row_map = lambda b, p, qi, kj: (b * nq + qi[p], 0)
    const = lambda shape: pl.BlockSpec(shape, lambda b, p, qi, kj: (0, 0), pipeline_mode=pl.Buffered(1))
    out_specs, out_shape = [pl.BlockSpec((blk, d), row_map)], [jax.ShapeDtypeStruct((t, d), F32)]
    scratch = [pltpu.VMEM((heads, blk), F32), pltpu.VMEM((vrows, blk), F32), pltpu.VMEM((3, blk, blk), F32)]
    extra_specs, extra_args = [], []
    if router:
        r_specs, r_shapes = _router_out(t, d, blk, row_map)
        out_specs, out_shape = out_specs + r_specs, out_shape + r_shapes
        scratch.append(pltpu.VMEM((8, LANES), F32))
        extra_specs = [const((d, LANES)), const((1, LANES))]
        extra_args = list(_router_operands(*router))
    grid_spec = pltpu.PrefetchScalarGridSpec(
        num_scalar_prefetch=2,
        grid=(bsz, len(pairs)),
        in_specs=[
            pl.BlockSpec((blk, wide), lambda b, p, qi, kj: (b * nq + kj[p], 0)),
            pl.BlockSpec((1, wide, blk), lambda b, p, qi, kj: (b, 0, qi[p])),
            pl.BlockSpec((1, vrows, blk), lambda b, p, qi, kj: (b, 0, kj[p])),
            pl.BlockSpec((blk, d), row_map),
            pl.BlockSpec((1, 6, d), lambda b, p, qi, kj: (b, 0, 0)),
            const((d, d)),
            const((1, d)),
            const((1, d)),
        ] + extra_specs,
        out_specs=out_specs,
        scratch_shapes=scratch,
    )
    outs = pl.pallas_call(
        functools.partial(_flash_kernel, heads=heads, hd=hd, blk=blk, alpha=alpha, n_exp=n_exp, region=t),
        grid_spec=grid_spec,
        out_shape=out_shape,
        compiler_params=_cparams(("arbitrary", "arbitrary")),
        name="fox_attention",
    )(qi, kj, k_aug, q_aug_t, v_t, x, mod, w_out.astype(BF16), ln_g.reshape(1, d), ln_b.reshape(1, d), *extra_args)
    return outs if router else outs[0]


def _fox_layer(x, mod, w_in, b_f, w_out, ln_g, ln_b, router=None, *, bsz, seq, alpha):
    heads = b_f.shape[0]
    k_aug, q_aug_t, v_t = _fox_proj(x, mod, w_in, b_f, bsz=bsz, seq=seq, heads=heads)
    return _flash(k_aug, q_aug_t, v_t, x, mod, w_out, ln_g, ln_b, router, bsz=bsz, seq=seq, heads=heads,
                  alpha=alpha)


def _pool_kernel(x_ref, xh_ref, mod_ref, win_ref, wg_ref, ls_ref, wout_ref, g_ref, b_ref, o_ref, pooled_ref,
                 *, tm, sub, seq, alpha):
    pos0 = (pl.program_id(0) * tm) % seq
    m = mod_ref[0]
    n = tm // sub
    halo_ok = jnp.where(pos0 > 0, jnp.float32(1.0), jnp.float32(0.0))
    rows = lax.broadcasted_iota(jnp.int32, (sub + POOL_HALO, 1), 0)

    def project(s):
        x = x_ref[s * sub:(s + 1) * sub, :]
        halo = xh_ref[...] if s == 0 else x_ref[s * sub - POOL_HALO:s * sub, :]
        ze = _dot(_modulate(jnp.concatenate([halo, x], axis=0), m, 0).astype(BF16), win_ref[...])
        if s == 0:
            ze = ze * jnp.where(rows < POOL_HALO, halo_ok, jnp.float32(1.0))
        return x, ze

    def pool(s, ze):
        pos = (pos0 + s * sub + lax.broadcasted_iota(jnp.int32, (sub, 1), 0) + 1).astype(F32)
        gd = ze.shape[1] // len(POOL_WINDOWS)
        out_rows = slice(s * sub, (s + 1) * sub)
        for g, win in enumerate(POOL_WINDOWS):
            cs = slice(g * gd, (g + 1) * gd)
            zg = ze[:, cs]
            acc = zg
            span = 1
            while span < win:
                acc = acc + pltpu.roll(acc, span, 0)
                span *= 2
            count = jnp.minimum(pos, jnp.float32(win))
            pooled = acc[POOL_HALO:, :] / count - zg[POOL_HALO:, :]
            pooled_ref[out_rows, cs] = (_dot(pooled.astype(BF16), wg_ref[g]) * ls_ref[:, cs]).astype(BF16)
        return _dot(pooled_ref[out_rows, :], wout_ref[...])

    nxt = project(0)
    for s in range(n):
        x, ze = nxt
        if s + 1 < n:
            nxt = project(s + 1)
        y = pool(s, ze)
        o_ref[s * sub:(s + 1) * sub, :] = _res_ln(x, y, m, 0, g_ref[...], b_ref[...], alpha)


def _pool_layer(x, mod, w_in, w_grp, scale, w_out, ln_g, ln_b, *, seq, alpha, tm=512, sub=256):
    t, d = x.shape
    ng, gd, _ = w_grp.shape
    hb = tm // POOL_HALO
    return pl.pallas_call(
        functools.partial(_pool_kernel, tm=tm, sub=sub, seq=seq, alpha=alpha),
        grid=(t // tm,),
        in_specs=[
            pl.BlockSpec((tm, d), lambda i: (i, 0)),
            pl.BlockSpec((POOL_HALO, d), lambda i: (jnp.maximum(i * hb - 1, 0), 0)),
            pl.BlockSpec((1, 6, d), lambda i: (i * tm // seq, 0, 0)),
            _const_spec((d, d)),
            _const_spec((ng, gd, gd)),
            _const_spec((1, d)),
            _const_spec((d, d)),
            _const_spec((1, d)),
            _const_spec((1, d)),
        ],
        out_specs=pl.BlockSpec((tm, d), lambda i: (i, 0)),
        out_shape=jax.ShapeDtypeStruct((t, d), F32),
        scratch_shapes=[pltpu.VMEM((tm, d), BF16)],
        compiler_params=_cparams(("parallel",)),
        name="pool_mixer",
    )(x, x, mod, w_in.astype(BF16), w_grp.astype(BF16), scale.reshape(1, d), w_out.astype(BF16),
      ln_g.reshape(1, d), ln_b.reshape(1, d))


def kernel(x, c, mod_w, mod_b, ln_g, ln_b, gm_w_in, gm_ln_g, gm_ln_b, gm_w_s, gm_b_s, gm_w_out, fox_w_in,
           fox_b_f, fox_w_out, pool_w_in, pool_w_grp, pool_scale, pool_w_out, ffn_w13, ffn_w2, moe_w_router,
           moe_b_router, moe_w13, moe_w2):
    bsz, seq, d = x.shape
    depth = mod_w.shape[0]
    alpha = (2 * depth) ** 0.25
    n_mixers = 3
    mod_all = _modulation(c, mod_w, mod_b)
    xt = x.reshape(bsz * seq, d)
    for i in range(depth):
        mod = mod_all[i]
        kind, j = i % n_mixers, i // n_mixers
        router = (moe_w_router[i // 2], moe_b_router[i // 2]) if i % 2 else None
        if kind == 0:
            xt = _gmlp_layer(xt, mod, gm_w_in[j], gm_ln_g[j], gm_ln_b[j], gm_w_s[j], gm_b_s[j], gm_w_out[j],
                             ln_g[i, 0], ln_b[i, 0], router, seq=seq, alpha=alpha)
        elif kind == 1:
            xt = _fox_layer(xt, mod, fox_w_in[j], fox_b_f[j], fox_w_out[j], ln_g[i, 0], ln_b[i, 0], router,
                            bsz=bsz, seq=seq, alpha=alpha)
        else:
            assert router is None, "the pooling mixer has no routing epilogue"
            xt = _pool_layer(xt, mod, pool_w_in[j], pool_w_grp[j], pool_scale[j], pool_w_out[j],
                             ln_g[i, 0], ln_b[i, 0], seq=seq, alpha=alpha)
        if router is None:
            xt = _ffn_layer(xt, mod, ffn_w13[i // 2], ffn_w2[i // 2], ln_g[i, 1], ln_b[i, 1],
                            seq=seq, alpha=alpha)
        else:
            xt = _moe_layer(xt, mod, moe_w13, moe_w2, i // 2, ln_g[i, 1], ln_b[i, 1], seq=seq, alpha=alpha)
    return xt.reshape(bsz, seq, d)
```

```python
import functools

import numpy as np
import jax
import jax.numpy as jnp
from jax import lax
from jax.experimental import pallas as pl
from jax.experimental.pallas import tpu as pltpu
from jax.experimental.pallas import tpu_sc as plsc

F32 = jnp.float32
BF16 = jnp.bfloat16

POOL_WINDOWS = (2, 4, 8, 16)
TOP_K = 2
EXPERT_BLOCK = 512
LN_EPS = 1e-5
LANES = 128
SUBLANES = 8
POOL_HALO = 16
VMEM_LIMIT = 56 * 1024 * 1024
SC_CORES = 2
SC_SUBCORES = 16
SC_ROWS = 128


def _cparams(sem):
    return pltpu.CompilerParams(dimension_semantics=sem, vmem_limit_bytes=VMEM_LIMIT)


def _const_spec(shape):
    nd = len(shape)
    return pl.BlockSpec(shape, lambda *_: (0,) * nd, pipeline_mode=pl.Buffered(1))


def _layer_norm(r, g, b):
    mu = jnp.mean(r, axis=-1, keepdims=True)
    xc = r - mu
    var = jnp.mean(xc * xc, axis=-1, keepdims=True)
    return xc * lax.rsqrt(var + LN_EPS) * g + b


def _modulate(x, m, off):
    return x * (1.0 + m[off + 1:off + 2]) + m[off:off + 1]


def _res_ln(x, y, m, off, g, b, alpha):
    return _layer_norm(alpha * x + (1.0 + m[off + 2:off + 3]) * y, g, b)


def _split3(a):
    hi = a.astype(BF16)
    r1 = a - hi.astype(F32)
    mid = r1.astype(BF16)
    lo = (r1 - mid.astype(F32)).astype(BF16)
    return hi, mid, lo


def _dot(a, b):
    return jnp.dot(a, b, preferred_element_type=F32)


def _dot_split(a_hi, a_mid, w):
    n = w.shape[1]
    w_hi = w.astype(BF16)
    w_lo = (w - w_hi.astype(F32)).astype(BF16)
    both = _dot(a_hi, jnp.concatenate([w_hi, w_lo], axis=1))
    return both[:, :n] + both[:, n:] + _dot(a_mid, w_hi)


def _pack_bf16_pair(x):
    n = x.shape[1] // 2
    hi = lax.bitcast_convert_type(x[:, :n].astype(BF16).astype(F32), jnp.uint32)
    lo = lax.bitcast_convert_type(x[:, n:].astype(BF16).astype(F32), jnp.uint32)
    return hi | (lo >> 16)


def _unpack_bf16_pair(w):
    hi = lax.bitcast_convert_type(w & jnp.uint32(0xFFFF0000), F32)
    lo = lax.bitcast_convert_type(w << 16, F32)
    return jnp.concatenate([hi, lo], axis=1)


def _mod_kernel(c_ref, w_ref, b_ref, o_ref):
    c = c_ref[...]
    s = c / (1.0 + jnp.exp(-c))
    s_hi, s_mid, _ = _split3(s)
    w = w_ref[0]
    w_hi = w.astype(BF16)
    w_lo = (w - w_hi.astype(F32)).astype(BF16)
    acc = _dot(s_hi, w_hi) + _dot(s_mid, w_hi) + _dot(s_hi, w_lo)
    o_ref[0] = acc + b_ref[0]


def _modulation(c, mod_w, mod_b):
    depth, d, n = mod_w.shape
    bsz = c.shape[0]
    rows = SUBLANES
    tn = n // 4
    c_pad = jnp.zeros((rows, d), F32).at[:bsz].set(c)
    out = pl.pallas_call(
        _mod_kernel,
        grid=(depth, n // tn),
        in_specs=[
            pl.BlockSpec((rows, d), lambda l, j: (0, 0)),
            pl.BlockSpec((1, d, tn), lambda l, j: (l, 0, j)),
            pl.BlockSpec((1, 1, tn), lambda l, j: (l, 0, j)),
        ],
        out_specs=pl.BlockSpec((1, rows, tn), lambda l, j: (l, 0, j)),
        out_shape=jax.ShapeDtypeStruct((depth, rows, n), F32),
        compiler_params=_cparams(("parallel", "parallel")),
        name="adaln_mod",
    )(c_pad, mod_w, mod_b.reshape(depth, 1, n))
    return out[:, :bsz, :].reshape(depth, bsz, 6, d)


def _gmlp_kernel(x_ref, mod_ref, win_ref, vg_ref, vb_ref, ws_ref, bst_ref, wout_ref, g_ref, b_ref,
                 *rest, tm, sub, chunk, groups, alpha, n_exp, region):
    if n_exp:
        wr_ref, br_ref, o_ref, hp_ref, meta_ref, cnt_ref, gated_ref, carry_ref = rest

        @pl.when(pl.program_id(0) == 0)
        def _():
            carry_ref[...] = jnp.zeros_like(carry_ref)
    else:
        o_ref, gated_ref = rest
    m = mod_ref[0]
    row = lax.broadcasted_iota(jnp.int32, (chunk, chunk), 0)
    col = lax.broadcasted_iota(jnp.int32, (chunk, chunk), 1)
    causal = row >= col
    bst = bst_ref[...]
    w_mix = [jnp.where(causal, ws_ref[g], 0.0).astype(BF16) for g in range(groups)]
    n = tm // sub

    def project(s):
        x = x_ref[s * sub:(s + 1) * sub, :]
        return x, _dot(_modulate(x, m, 0).astype(BF16), win_ref[...])

    def activate(z):
        z = 0.5 * z * (1.0 + lax.erf(z * (2.0 ** -0.5)))
        width = z.shape[1] // 2
        return z[:, :width], _layer_norm(z[:, width:], vg_ref[...], vb_ref[...]).astype(BF16)

    def mix(s, u, v):
        gd = u.shape[1] // groups
        for g in range(groups):
            cs = slice(g * gd, (g + 1) * gd)
            for c in range(sub // chunk):
                rs = slice(c * chunk, (c + 1) * chunk)
                mixed = _dot(w_mix[g], v[rs, cs]) + bst[:, g:g + 1]
                gated_ref[s * sub + c * chunk:s * sub + (c + 1) * chunk, cs] = (u[rs, cs] * mixed).astype(BF16)
        return _dot(gated_ref[s * sub:(s + 1) * sub, :], wout_ref[...])

    def finish(s, x, y):
        rows = slice(s * sub, (s + 1) * sub)
        x_new = _res_ln(x, y, m, 0, g_ref[...], b_ref[...], alpha)
        o_ref[rows, :] = x_new
        if n_exp:
            hp_ref[rows, :], meta_ref[rows, :] = _route_rows(x_new, m, wr_ref, br_ref, carry_ref, n_exp, region)

    nxt = project(0)
    pending = None
    for s in range(n):
        x, z = nxt
        if s + 1 < n:
            nxt = project(s + 1)
        u, v = activate(z)
        if pending is not None:
            finish(*pending)
        pending = (s, x, mix(s, u, v))
    finish(*pending)
    if n_exp:
        cnt_ref[...] = carry_ref[...]


def _gmlp_layer(x, mod, w_in, v_g, v_b, w_s, b_s, w_out, ln_g, ln_b, router=None, *, seq, alpha, tm=512,
                sub=256):
    t, d = x.shape
    groups, chunk, _ = w_s.shape
    width = w_out.shape[0]
    n_exp = router[0].shape[1] if router else 0
    kern = functools.partial(_gmlp_kernel, tm=tm, sub=sub, chunk=chunk, groups=groups, alpha=alpha, n_exp=n_exp,
                             region=t)
    row_map = lambda i: (i, 0)
    out_specs, out_shape = [pl.BlockSpec((tm, d), row_map)], [jax.ShapeDtypeStruct((t, d), F32)]
    scratch = [pltpu.VMEM((tm, width), BF16)]
    extra_specs, extra_args = [], []
    if router:
        r_specs, r_shapes = _router_out(t, d, tm, row_map)
        out_specs, out_shape = out_specs + r_specs, out_shape + r_shapes
        scratch.append(pltpu.VMEM((SUBLANES, LANES), F32))
        extra_specs = [_const_spec((d, LANES)), _const_spec((1, LANES))]
        extra_args = list(_router_operands(*router))
    outs = pl.pallas_call(
        kern,
        grid=(t // tm,),
        in_specs=[
            pl.BlockSpec((tm, d), lambda i: (i, 0)),
            pl.BlockSpec((1, 6, d), lambda i: (i * tm // seq, 0, 0)),
            _const_spec((d, 2 * width)),
            _const_spec((1, width)),
            _const_spec((1, width)),
            _const_spec((groups, chunk, chunk)),
            _const_spec((chunk, groups)),
            _const_spec((width, d)),
            _const_spec((1, d)),
            _const_spec((1, d)),
        ] + extra_specs,
        out_specs=out_specs,
        out_shape=out_shape,
        scratch_shapes=scratch,
        compiler_params=_cparams(("arbitrary",)),
        name="gmlp_mixer",
    )(x, mod, w_in.astype(BF16), v_g.reshape(1, width), v_b.reshape(1, width), w_s, b_s.T,
      w_out.astype(BF16), ln_g.reshape(1, d), ln_b.reshape(1, d), *extra_args)
    return outs if router else outs[0]


def _swiglu_pipelined(hs, w1, w3, w2, emit):
    up = lambda h: (_dot(h, w1), _dot(h, w3))
    nxt = up(hs[0])
    for s in range(len(hs)):
        a, b = nxt
        if s + 1 < len(hs):
            nxt = up(hs[s + 1])
        t = (a / (1.0 + jnp.exp(-a)) * b).astype(BF16)
        emit(s, _dot(t, w2))


def _ffn_kernel(x_ref, mod_ref, w1_ref, w3_ref, w2_ref, g_ref, b_ref, o_ref, h_ref, *, alpha, sub):
    j = pl.program_id(1)
    last = pl.num_programs(1) - 1
    subs = [slice(s, s + sub) for s in range(0, x_ref.shape[0], sub)]
    weights = lambda: (w1_ref[...], w3_ref[...], w2_ref[...])

    @pl.when(j == 0)
    def _():
        hs = []
        for rows in subs:
            h = _modulate(x_ref[rows, :], mod_ref[0], 3).astype(BF16)
            h_ref[rows, :] = h
            hs.append(h)

        def emit(s, part):
            o_ref[subs[s], :] = part

        _swiglu_pipelined(hs, *weights(), emit)

    @pl.when(jnp.logical_and(j > 0, j < last))
    def _():
        def emit(s, part):
            o_ref[subs[s], :] += part

        _swiglu_pipelined([h_ref[rows, :] for rows in subs], *weights(), emit)

    @pl.when(j == last)
    def _():
        def emit(s, part):
            rows = subs[s]
            y = o_ref[rows, :] + part
            o_ref[rows, :] = _res_ln(x_ref[rows, :], y, mod_ref[0], 3, g_ref[...], b_ref[...], alpha)

        _swiglu_pipelined([h_ref[rows, :] for rows in subs], *weights(), emit)


def _ffn_layer(x, mod, w13, w2, ln_g, ln_b, *, seq, alpha, tm=1024, tf=512):
    t, d = x.shape
    f = w2.shape[0]
    nf = f // tf
    assert nf >= 2
    w13b = w13.astype(BF16)
    return pl.pallas_call(
        functools.partial(_ffn_kernel, alpha=alpha, sub=tm // 4),
        grid=(t // tm, nf),
        in_specs=[
            pl.BlockSpec((tm, d), lambda i, j: (i, 0)),
            pl.BlockSpec((1, 6, d), lambda i, j: (i * tm // seq, 0, 0)),
            pl.BlockSpec((d, tf), lambda i, j: (0, j)),
            pl.BlockSpec((d, tf), lambda i, j: (0, nf + j)),
            pl.BlockSpec((tf, d), lambda i, j: (j, 0)),
            _const_spec((1, d)),
            _const_spec((1, d)),
        ],
        out_specs=pl.BlockSpec((tm, d), lambda i, j: (i, 0)),
        out_shape=jax.ShapeDtypeStruct((t, d), F32),
        scratch_shapes=[pltpu.VMEM((tm, d), BF16)],
        compiler_params=_cparams(("parallel", "arbitrary")),
        name="swiglu_dense",
    )(x, mod, w13b, w13b, w2.astype(BF16), ln_g.reshape(1, d), ln_b.reshape(1, d))


MOE_GROUP = 2


def _moe_ffn_kernel(ge_ref, gb_ref, gr_ref, h_ref, w1_ref, w3_ref, w2_ref, o_ref, acc_ref, *, sub):
    g = pl.program_id(0)
    j = pl.program_id(1)
    last = pl.num_programs(1) - 1
    n_rows = gr_ref[g]
    nv = (n_rows + EXPERT_BLOCK - 1) // EXPERT_BLOCK

    def run(n_blocks, phase):
        subs = [slice(s, s + sub) for s in range(0, n_blocks * EXPERT_BLOCK, sub)]
        row = lax.broadcasted_iota(jnp.int32, (sub, 1), 0)

        def rows_in(rows):
            w = jnp.where(row + rows.start < n_rows, h_ref[rows, :], jnp.uint32(0))
            return _unpack_bf16_pair(w).astype(BF16)

        def emit(s, part):
            if phase == "first":
                acc_ref[subs[s], :] = part
            elif phase == "middle":
                acc_ref[subs[s], :] += part
            else:
                o_ref[subs[s], :] = _pack_bf16_pair(acc_ref[subs[s], :] + part)

        _swiglu_pipelined([rows_in(rows) for rows in subs], w1_ref[0, 0].astype(BF16), w3_ref[0, 0].astype(BF16),
                          w2_ref[0, 0].astype(BF16), emit)

    for n_blocks in range(1, MOE_GROUP + 1):
        @pl.when(jnp.logical_and(nv == n_blocks, j == 0))
        def _():
            run(n_blocks, "first")

        @pl.when(jnp.logical_and(nv == n_blocks, jnp.logical_and(j > 0, j < last)))
        def _():
            run(n_blocks, "middle")

        @pl.when(jnp.logical_and(nv == n_blocks, j == last))
        def _():
            run(n_blocks, "last")

    for blk in range(MOE_GROUP):
        @pl.when(jnp.logical_and(blk >= nv, j == last))
        def _():
            o_ref[blk * EXPERT_BLOCK:(blk + 1) * EXPERT_BLOCK, :] = jnp.zeros((EXPERT_BLOCK, o_ref.shape[1]),
                                                                              jnp.uint32)


def _moe_ffn(h_slots, group_expert, group_block, group_rows, w13, w2, layer, *, tf=512):
    half = h_slots.shape[1]
    d = 2 * half
    f = w2.shape[2]
    nf = f // tf
    assert nf >= 2
    tm = MOE_GROUP * EXPERT_BLOCK

    def jj(g, j, gr):
        return jnp.where(gr[g] > 0, j, nf - 1)

    grid_spec = pltpu.PrefetchScalarGridSpec(
        num_scalar_prefetch=3,
        grid=(group_expert.shape[0], nf),
        in_specs=[
            pl.BlockSpec((tm, half), lambda g, j, ge, gb, gr: (gb[g], 0)),
            pl.BlockSpec((1, 1, d, tf), lambda g, j, ge, gb, gr: (layer, ge[g], 0, jj(g, j, gr))),
            pl.BlockSpec((1, 1, d, tf), lambda g, j, ge, gb, gr: (layer, ge[g], 0, nf + jj(g, j, gr))),
            pl.BlockSpec((1, 1, tf, d), lambda g, j, ge, gb, gr: (layer, ge[g], jj(g, j, gr), 0)),
        ],
        out_specs=pl.BlockSpec((tm, half), lambda g, j, ge, gb, gr: (gb[g], 0)),
        scratch_shapes=[pltpu.VMEM((tm, d), F32)],
    )
    return pl.pallas_call(
        functools.partial(_moe_ffn_kernel, sub=EXPERT_BLOCK // 2),
        grid_spec=grid_spec,
        out_shape=jax.ShapeDtypeStruct(h_slots.shape, jnp.uint32),
        compiler_params=_cparams(("arbitrary", "arbitrary")),
        name="swiglu_experts",
    )(group_expert, group_block, group_rows, h_slots, w13, w13, w2)


def _route_rows(x_rows, m, wr_ref, br_ref, carry_ref, n_exp, region):
    tm = x_rows.shape[0]
    h = _modulate(x_rows, m, 3)
    h_hi, h_mid, _ = _split3(h)
    logits = _dot_split(h_hi, h_mid, wr_ref[...]) + br_ref[...]
    lane = lax.broadcasted_iota(jnp.int32, (tm, LANES), 1)
    neg = jnp.float32(-jnp.inf)
    logits = jnp.where(lane < n_exp, logits, neg)
    v0 = jnp.max(logits, axis=1, keepdims=True)
    e0 = jnp.min(jnp.where(logits == v0, lane, LANES), axis=1, keepdims=True)
    rest = jnp.where(lane == e0, neg, logits)
    v1 = jnp.max(rest, axis=1, keepdims=True)
    e1 = jnp.min(jnp.where(rest == v1, lane, LANES), axis=1, keepdims=True)
    p = jnp.exp(v1 - v0)
    g0 = 1.0 / (1.0 + p)
    g1 = p / (1.0 + p)
    oh0 = lane == e0
    oh1 = lane == e1
    onehot = jnp.where(jnp.logical_or(oh0, oh1), 1.0, 0.0).astype(BF16)
    r = lax.broadcasted_iota(jnp.int32, (tm, tm), 0)
    c = lax.broadcasted_iota(jnp.int32, (tm, tm), 1)
    strict = jnp.where(r > c, 1.0, 0.0).astype(BF16)
    before = _dot(strict, onehot) + carry_ref[0:1, :]
    rank0 = jnp.sum(jnp.where(oh0, before, 0.0), axis=1, keepdims=True)
    rank1 = jnp.sum(jnp.where(oh1, before, 0.0), axis=1, keepdims=True)
    total = carry_ref[0:1, :] + jnp.sum(onehot.astype(F32), axis=0, keepdims=True)
    carry_ref[...] = jnp.broadcast_to(total, carry_ref.shape)
    meta = jnp.where(lane == 0, e0.astype(F32), 0.0)
    meta = jnp.where(lane == 1, e1.astype(F32), meta)
    meta = jnp.where(lane == 2, g0, meta)
    meta = jnp.where(lane == 3, g1, meta)
    meta = jnp.where(lane == 4, e0.astype(F32) * region + rank0, meta)
    meta = jnp.where(lane == 5, e1.astype(F32) * region + rank1, meta)
    return _pack_bf16_pair(h), meta


def _router_operands(w_router, b_router):
    d, n_exp = w_router.shape
    wr = jnp.zeros((d, LANES), F32).at[:, :n_exp].set(w_router)
    br = jnp.zeros((1, LANES), F32).at[0, :n_exp].set(b_router)
    return wr, br


def _router_out(t, d, tm, row_map):
    specs = [pl.BlockSpec((tm, d // 2), row_map), pl.BlockSpec((tm, LANES), row_map),
             pl.BlockSpec((SUBLANES, LANES), lambda *_: (0, 0))]
    shapes = [jax.ShapeDtypeStruct((t, d // 2), jnp.uint32), jax.ShapeDtypeStruct((t, LANES), F32),
              jax.ShapeDtypeStruct((SUBLANES, LANES), F32)]
    return specs, shapes


def _combine_kernel(x_ref, ya_ref, yb_ref, meta_ref, mod_ref, g_ref, b_ref, o_ref, *, alpha):
    meta = meta_ref[...]
    y = meta[:, 2:3] * _unpack_bf16_pair(ya_ref[...]) + meta[:, 3:4] * _unpack_bf16_pair(yb_ref[...])
    o_ref[...] = _res_ln(x_ref[...], y, mod_ref[0], 3, g_ref[...], b_ref[...], alpha)


def _combine(x, ya, yb, meta, mod, ln_g, ln_b, *, seq, alpha, tm=512):
    t, d = x.shape
    row = pl.BlockSpec((tm, d), lambda i: (i, 0))
    packed = pl.BlockSpec((tm, d // 2), lambda i: (i, 0))
    return pl.pallas_call(
        functools.partial(_combine_kernel, alpha=alpha),
        grid=(t // tm,),
        in_specs=[row, packed, packed,
                  pl.BlockSpec((tm, LANES), lambda i: (i, 0)),
                  pl.BlockSpec((1, 6, d), lambda i: (i * tm // seq, 0, 0)),
                  _const_spec((1, d)), _const_spec((1, d))],
        out_specs=row,
        out_shape=jax.ShapeDtypeStruct((t, d), F32),
        compiler_params=_cparams(("parallel",)),
        name="moe_combine",
    )(x, ya, yb, meta, mod, ln_g.reshape(1, d), ln_b.reshape(1, d))


def _sc_mesh():
    return plsc.VectorSubcoreMesh(core_axis_name="c", subcore_axis_name="s", num_cores=SC_CORES,
                                  num_subcores=SC_SUBCORES)


def _sc_worker_share(n):
    workers = SC_CORES * SC_SUBCORES
    per_worker = n // workers
    steps = per_worker // SC_ROWS
    assert steps * SC_ROWS * workers == n
    return per_worker, steps


def _sc_gather_rows(table, idx_a, idx_b):
    n = idx_a.shape[0]
    width = table.shape[1]
    per_worker, steps = _sc_worker_share(n)

    def body(table_hbm, ia_hbm, ib_hbm, oa_hbm, ob_hbm, ia_v, ib_v, rows_v, sem):
        wid = lax.axis_index("s") * SC_CORES + lax.axis_index("c")
        base = pl.multiple_of(wid * per_worker, SC_ROWS)
        pltpu.sync_copy(ia_hbm.at[pl.ds(base, per_worker)], ia_v)
        pltpu.sync_copy(ib_hbm.at[pl.ds(base, per_worker)], ib_v)

        @pl.loop(0, steps)
        def _(j):
            lo = pl.multiple_of(j * SC_ROWS, SC_ROWS)
            for idx_v, out_hbm in ((ia_v, oa_hbm), (ib_v, ob_hbm)):
                pltpu.async_copy(table_hbm.at[idx_v.at[pl.ds(lo, SC_ROWS)]], rows_v, sem).wait()
                pltpu.sync_copy(rows_v, out_hbm.at[pl.ds(base + lo, SC_ROWS)])

    out = jax.ShapeDtypeStruct((n, width), table.dtype)
    return pl.kernel(
        body,
        out_type=(out, out),
        mesh=_sc_mesh(),
        scratch_types=[
            pltpu.VMEM((per_worker,), jnp.int32),
            pltpu.VMEM((per_worker,), jnp.int32),
            pltpu.VMEM((SC_ROWS, width), table.dtype),
            pltpu.SemaphoreType.DMA,
        ],
        name="sc_row_gather",
    )(table, idx_a, idx_b)


def _sc_scatter_rows(rows, dest_a, dest_b, n_out):
    n, width = rows.shape
    per_worker, steps = _sc_worker_share(n)

    def body(rows_hbm, da_hbm, db_hbm, out_hbm, ia_v, ib_v, rows_v, sem):
        wid = lax.axis_index("s") * SC_CORES + lax.axis_index("c")

        @pl.loop(0, steps)
        def _(j):
            off = pl.multiple_of(wid * per_worker + j * SC_ROWS, SC_ROWS)
            pltpu.sync_copy(da_hbm.at[pl.ds(off, SC_ROWS)], ia_v)
            pltpu.sync_copy(db_hbm.at[pl.ds(off, SC_ROWS)], ib_v)
            pltpu.sync_copy(rows_hbm.at[pl.ds(off, SC_ROWS)], rows_v)
            pltpu.async_copy(rows_v, out_hbm.at[ia_v], sem).wait()
            pltpu.async_copy(rows_v, out_hbm.at[ib_v], sem).wait()

    return pl.kernel(
        body,
        out_type=jax.ShapeDtypeStruct((n_out, width), rows.dtype),
        mesh=_sc_mesh(),
        scratch_types=[
            pltpu.VMEM((SC_ROWS,), jnp.int32),
            pltpu.VMEM((SC_ROWS,), jnp.int32),
            pltpu.VMEM((SC_ROWS, width), rows.dtype),
            pltpu.SemaphoreType.DMA,
        ],
        name="sc_row_scatter",
    )(rows, dest_a, dest_b)


def _moe_layer(routed, mod, w13, w2, layer, ln_g, ln_b, *, seq, alpha):
    x, h, meta, cnt = routed
    t, d = x.shape
    n_exp = w13.shape[1]
    dest0 = meta[:, 4].astype(jnp.int32)
    dest1 = meta[:, 5].astype(jnp.int32)
    counts = cnt[0, :n_exp].astype(jnp.int32)
    group_rows = MOE_GROUP * EXPERT_BLOCK
    assert t % group_rows == 0
    region_groups = t // group_rows
    groups_per_expert = (counts + group_rows - 1) // group_rows
    group_end = jnp.cumsum(groups_per_expert)
    g = jnp.arange(t * TOP_K // group_rows + n_exp, dtype=jnp.int32)
    expert = jnp.minimum(jnp.sum(group_end[None, :] <= g[:, None], axis=1), n_exp - 1).astype(jnp.int32)
    blk_in_expert = g - (group_end - groups_per_expert)[expert]
    used = g < group_end[-1]
    spare_block = n_exp * region_groups
    group_block = jnp.where(used, expert * region_groups + blk_in_expert, spare_block).astype(jnp.int32)
    rows_in_group = jnp.where(used, jnp.clip(counts[expert] - blk_in_expert * group_rows, 0, group_rows),
                              0).astype(jnp.int32)
    h_slots = _sc_scatter_rows(h, dest0, dest1, (spare_block + 1) * group_rows)
    y_slots = _moe_ffn(h_slots, expert, group_block, rows_in_group, w13, w2, layer)
    ya, yb = _sc_gather_rows(y_slots, dest0, dest1)
    return _combine(x, ya, yb, meta, mod, ln_g, ln_b, seq=seq, alpha=alpha)


HEAD_PAD = LANES
GATE_PARTS = 3
GATE_ROWS = 16
VAL_PAD = 16
LOG2E = 1.4426950408889634


def _head_lanes(h, hd):
    k0 = (h % (HEAD_PAD // hd)) * hd
    return k0, (hd if k0 == 0 else 0)


def _fox_proj_kernel(x_ref, mod_ref, wf_ref, bf_ref, wq_ref, wk_ref, wv_ref, selk_ref, onek_ref, selq_ref,
                     oneq_ref, onev_ref, k_ref, q_ref, v_ref, carry_ref, *, q_scale, heads, hd, nb):
    tm = x_ref.shape[0]

    @pl.when(pl.program_id(0) % nb == 0)
    def _():
        carry_ref[...] = jnp.zeros_like(carry_ref)

    h_hi, h_mid, _ = _split3(_modulate(x_ref[...], mod_ref[0], 0))

    f = _dot_split(h_hi, h_mid, wf_ref[...]) + bf_ref[...]
    lf = jnp.minimum(f, 0.0) - jnp.log(1.0 + jnp.exp(-jnp.abs(f)))
    r = lax.broadcasted_iota(jnp.int32, (tm, tm), 0)
    c = lax.broadcasted_iota(jnp.int32, (tm, tm), 1)
    tri = jnp.where(r >= c, 1.0, 0.0).astype(BF16)
    sums = _dot(tri, jnp.concatenate(_split3(lf), axis=1))
    cs = carry_ref[0:1, :] + sum(sums[:, n * LANES:(n + 1) * LANES] for n in range(GATE_PARTS))
    carry_ref[...] = jnp.broadcast_to(cs[tm - 1:tm, :], carry_ref.shape)
    cs = cs * LOG2E

    gate_k = onek_ref[...] + _dot(jnp.concatenate(_split3(cs), axis=1), selk_ref[...])
    kc = _dot(h_hi, wk_ref[...])
    lane = lax.broadcasted_iota(jnp.int32, (tm, HEAD_PAD), 1)
    for h in range(heads):
        k0, _ = _head_lanes(h, hd)
        src = (h * hd // HEAD_PAD) * HEAD_PAD
        is_k = jnp.logical_and(lane >= k0, lane < k0 + hd)
        hs = slice(h * HEAD_PAD, (h + 1) * HEAD_PAD)
        k_ref[:, hs] = jnp.where(is_k, kc[:, src:src + HEAD_PAD], gate_k[:, hs]).astype(BF16)

    nt = (((1,), (1,)), ((), ()))
    qc = lax.dot_general(wq_ref[...], h_hi, nt, preferred_element_type=F32) * (q_scale * LOG2E)
    gate_q = oneq_ref[...] + _dot(selq_ref[...], jnp.concatenate(_split3(cs.T), axis=0))
    fill = HEAD_PAD - hd - GATE_ROWS
    for h in range(heads):
        k0, g0 = _head_lanes(h, hd)
        base = h * HEAD_PAD
        q_ref[0, base + k0:base + k0 + hd, :] = qc[h * hd:(h + 1) * hd, :].astype(BF16)
        q_ref[0, base + g0:base + g0 + GATE_ROWS, :] = gate_q[h * GATE_ROWS:(h + 1) * GATE_ROWS, :].astype(BF16)
        q_ref[0, base + g0 + GATE_ROWS:base + g0 + GATE_ROWS + fill, :] = jnp.zeros((fill, tm), BF16)

    v = lax.dot_general(wv_ref[...], h_hi, nt, preferred_element_type=F32) + onev_ref[...]
    v_ref[0] = v.astype(BF16)


def _fox_proj(x, mod, w_in, b_f, *, bsz, seq, heads, tm=512):
    t, d = x.shape
    hd = d // heads
    assert HEAD_PAD == 2 * hd and GATE_ROWS >= 2 * GATE_PARTS
    nb = seq // tm
    wide = heads * HEAD_PAD
    wq_t = w_in[:, :d].T.astype(BF16)
    wk = w_in[:, d:2 * d].astype(BF16)
    vrows = heads * (hd + VAL_PAD)
    wv_t = jnp.pad(w_in[:, 2 * d:3 * d].reshape(d, heads, hd),
                   ((0, 0), (0, 0), (0, VAL_PAD))).reshape(d, vrows).T.astype(BF16)
    wf = jnp.zeros((d, LANES), F32).at[:, :heads].set(w_in[:, 3 * d:])
    bf = jnp.zeros((1, LANES), F32).at[0, :heads].set(b_f)
    sel_k = np.zeros((GATE_PARTS, LANES, wide), np.float32)
    one_k = np.zeros((1, wide), np.float32)
    sel_q = np.zeros((GATE_PARTS, heads * GATE_ROWS, LANES), np.float32)
    one_q = np.zeros((heads * GATE_ROWS, 1), np.float32)
    one_v = np.zeros((vrows, 1), np.float32)
    for hh in range(heads):
        _, g0 = _head_lanes(hh, hd)
        one_v[hh * (hd + VAL_PAD) + hd, 0] = 1.0
        for n in range(GATE_PARTS):
            sel_k[n, hh, hh * HEAD_PAD + g0 + n] = -1.0
            one_q[hh * GATE_ROWS + n, 0] = 1.0
            one_k[0, hh * HEAD_PAD + g0 + GATE_PARTS + n] = 1.0
            sel_q[n, hh * GATE_ROWS + GATE_PARTS + n, hh] = 1.0
    return pl.pallas_call(
        functools.partial(_fox_proj_kernel, q_scale=hd ** -0.5, heads=heads, hd=hd, nb=nb),
        grid=(t // tm,),
        in_specs=[
            pl.BlockSpec((tm, d), lambda i: (i, 0)),
            pl.BlockSpec((1, 6, d), lambda i: (i * tm // seq, 0, 0)),
            _const_spec((d, LANES)),
            _const_spec((1, LANES)),
            _const_spec((d, d)),
            _const_spec((d, d)),
            _const_spec((vrows, d)),
            _const_spec((GATE_PARTS * LANES, wide)),
            _const_spec((1, wide)),
            _const_spec((heads * GATE_ROWS, GATE_PARTS * LANES)),
            _const_spec((heads * GATE_ROWS, 1)),
            _const_spec((vrows, 1)),
        ],
        out_specs=[
            pl.BlockSpec((tm, wide), lambda i: (i, 0)),
            pl.BlockSpec((1, wide, tm), lambda i: (i // nb, 0, i % nb)),
            pl.BlockSpec((1, vrows, tm), lambda i: (i // nb, 0, i % nb)),
        ],
        out_shape=[
            jax.ShapeDtypeStruct((t, wide), BF16),
            jax.ShapeDtypeStruct((bsz, wide, seq), BF16),
            jax.ShapeDtypeStruct((bsz, vrows, seq), BF16),
        ],
        scratch_shapes=[pltpu.VMEM((SUBLANES, LANES), F32)],
        compiler_params=_cparams(("arbitrary",)),
        name="fox_proj",
    )(x, mod, wf, bf, wq_t, wk, wv_t, jnp.asarray(sel_k.reshape(GATE_PARTS * LANES, wide), BF16), jnp.asarray(one_k),
      jnp.asarray(np.concatenate(list(sel_q), axis=1), BF16), jnp.asarray(one_q), jnp.asarray(one_v))


def _flash_kernel(qi_ref, kj_ref, k_ref, q_ref, v_ref, x_ref, mod_ref, wout_ref, g_ref, b_ref, *rest,
                  heads, hd, blk, alpha, n_exp, region):
    p_idx = pl.program_id(1)
    if n_exp:
        wr_ref, br_ref, o_ref, hp_ref, meta_ref, cnt_ref, m_ref, acc_ref, s_ref, carry_ref = rest

        @pl.when(jnp.logical_and(pl.program_id(0) == 0, p_idx == 0))
        def _():
            carry_ref[...] = jnp.zeros_like(carry_ref)
    else:
        o_ref, m_ref, acc_ref, s_ref = rest
    qi = qi_ref[p_idx]
    kj = kj_ref[p_idx]
    vp = hd + VAL_PAD

    @pl.when(kj == 0)
    def _():
        m_ref[...] = jnp.full(m_ref.shape, -jnp.inf, F32)
        acc_ref[...] = jnp.zeros_like(acc_ref)

    def step(diagonal):
        half = blk // 2
        tiles = [(slice(0, half), slice(0, half)), (slice(half, blk), slice(0, blk))] if diagonal else \
                [(slice(0, blk), slice(0, blk))]
        if diagonal:
            causal = lambda rows, first_q: (lax.broadcasted_iota(jnp.int32, (rows, half), 0)
                                            <= lax.broadcasted_iota(jnp.int32, (rows, half), 1) + first_q)
            keep = [causal(half, 0), causal(blk, half)]

        def scores(h):
            hs = slice(h * HEAD_PAD, (h + 1) * HEAD_PAD)
            tops = []
            for n, (qs, ks) in enumerate(tiles):
                s = _dot(k_ref[ks, hs], q_ref[0, hs, qs])
                if diagonal:
                    s = jnp.where(keep[n], s, -jnp.inf)
                s_ref[h % depth, ks, qs] = s
                tops.append(jnp.max(s, axis=0, keepdims=True))
            return tops

        depth = s_ref.shape[0]
        ahead = [scores(h) for h in range(depth - 1)]
        for h in range(heads):
            if h + depth - 1 < heads:
                ahead.append(scores(h + depth - 1))
            rs = slice(h * vp, (h + 1) * vp)
            for (qs, ks), m_cur in zip(tiles, ahead.pop(0)):
                m_prev = m_ref[h:h + 1, qs]
                m_new = jnp.maximum(m_prev, m_cur)
                m_ref[h:h + 1, qs] = m_new
                p = jnp.exp2(s_ref[h % depth, ks, qs] - m_new).astype(BF16)
                acc_ref[rs, qs] = acc_ref[rs, qs] * jnp.exp2(m_prev - m_new) + _dot(v_ref[0, rs, ks], p)

    @pl.when(kj < qi)
    def _():
        step(False)

    @pl.when(kj == qi)
    def _():
        step(True)
        per = LANES // hd
        cols = []
        for g in range(heads // per):
            parts = []
            for n in range(per):
                r0 = (g * per + n) * vp
                parts.append(acc_ref[r0:r0 + hd, :] / acc_ref[r0 + hd:r0 + hd + 1, :])
            cols.append(jnp.concatenate(parts, axis=0).T.astype(BF16))
        y = _dot(jnp.concatenate(cols, axis=1), wout_ref[...])
        x_new = _res_ln(x_ref[...], y, mod_ref[0], 0, g_ref[...], b_ref[...], alpha)
        o_ref[...] = x_new
        if n_exp:
            hp_ref[...], meta_ref[...] = _route_rows(x_new, mod_ref[0], wr_ref, br_ref, carry_ref, n_exp, region)
            cnt_ref[...] = carry_ref[...]


def _flash(k_aug, q_aug_t, v_t, x, mod, w_out, ln_g, ln_b, router=None, *, bsz, seq, heads, alpha, blk=512):
    t, wide = k_aug.shape
    vrows = v_t.shape[1]
    hd = vrows // heads - VAL_PAD
    d = heads * hd
    nq = seq // blk
    pairs = [(i, j) for i in range(nq) for j in range(i + 1)]
    qi = jnp.asarray(np.array([p[0] for p in pairs], np.int32))
    kj = jnp.asarray(np.array([p[1] for p in pairs], np.int32))
    n_exp = router[0].shape[1] if router else 0
    row_map = lambda b, p, qi, kj: (b * nq + qi[p], 0)
    const = lambda shape: pl.BlockSpec(shape, lambda b, p, qi, kj: (0, 0), pipeline_mode=pl.Buffered(1))
    out_specs, out_shape = [pl.BlockSpec((blk, d), row_map)], [jax.ShapeDtypeStruct((t, d), F32)]
    scratch = [pltpu.VMEM((heads, blk), F32), pltpu.VMEM((vrows, blk), F32), pltpu.VMEM((3, blk, blk), F32)]
    extra_specs, extra_args = [], []
    if router:
        r_specs, r_shapes = _router_out(t, d, blk, row_map)
        out_specs, out_shape = out_specs + r_specs, out_shape + r_shapes
        scratch.append(pltpu.VMEM((SUBLANES, LANES), F32))
        extra_specs = [const((d, LANES)), const((1, LANES))]
        extra_args = list(_router_operands(*router))
    grid_spec = pltpu.PrefetchScalarGridSpec(
        num_scalar_prefetch=2,
        grid=(bsz, len(pairs)),
        in_specs=[
            pl.BlockSpec((blk, wide), lambda b, p, qi, kj: (b * nq + kj[p], 0)),
            pl.BlockSpec((1, wide, blk), lambda b, p, qi, kj: (b, 0, qi[p])),
            pl.BlockSpec((1, vrows, blk), lambda b, p, qi, kj: (b, 0, kj[p])),
            pl.BlockSpec((blk, d), row_map),
            pl.BlockSpec((1, 6, d), lambda b, p, qi, kj: (b, 0, 0)),
            const((d, d)),
            const((1, d)),
            const((1, d)),
        ] + extra_specs,
        out_specs=out_specs,
        scratch_shapes=scratch,
    )
    outs = pl.pallas_call(
        functools.partial(_flash_kernel, heads=heads, hd=hd, blk=blk, alpha=alpha, n_exp=n_exp, region=t),
        grid_spec=grid_spec,
        out_shape=out_shape,
        compiler_params=_cparams(("arbitrary", "arbitrary")),
        name="fox_attention",
    )(qi, kj, k_aug, q_aug_t, v_t, x, mod, w_out.astype(BF16), ln_g.reshape(1, d), ln_b.reshape(1, d), *extra_args)
    return outs if router else outs[0]


def _fox_layer(x, mod, w_in, b_f, w_out, ln_g, ln_b, router=None, *, bsz, seq, alpha):
    heads = b_f.shape[0]
    k_aug, q_aug_t, v_t = _fox_proj(x, mod, w_in, b_f, bsz=bsz, seq=seq, heads=heads)
    return _flash(k_aug, q_aug_t, v_t, x, mod, w_out, ln_g, ln_b, router, bsz=bsz, seq=seq, heads=heads,
                  alpha=alpha)


def _pool_kernel(x_ref, xh_ref, mod_ref, win_ref, wg_ref, ls_ref, wout_ref, g_ref, b_ref, o_ref, pooled_ref,
                 *, tm, sub, seq, alpha):
    pos0 = (pl.program_id(0) * tm) % seq
    m = mod_ref[0]
    n = tm // sub
    halo_ok = jnp.where(pos0 > 0, jnp.float32(1.0), jnp.float32(0.0))
    rows = lax.broadcasted_iota(jnp.int32, (sub + POOL_HALO, 1), 0)

    def project(s):
        x = x_ref[s * sub:(s + 1) * sub, :]
        halo = xh_ref[...] if s == 0 else x_ref[s * sub - POOL_HALO:s * sub, :]
        ze = _dot(_modulate(jnp.concatenate([halo, x], axis=0), m, 0).astype(BF16), win_ref[...])
        if s == 0:
            ze = ze * jnp.where(rows < POOL_HALO, halo_ok, jnp.float32(1.0))
        return x, ze

    def pool(s, ze):
        pos = (pos0 + s * sub + lax.broadcasted_iota(jnp.int32, (sub, 1), 0) + 1).astype(F32)
        gd = ze.shape[1] // len(POOL_WINDOWS)
        out_rows = slice(s * sub, (s + 1) * sub)
        for g, win in enumerate(POOL_WINDOWS):
            cs = slice(g * gd, (g + 1) * gd)
            zg = ze[:, cs]
            acc = zg
            span = 1
            while span < win:
                acc = acc + pltpu.roll(acc, span, 0)
                span *= 2
            count = jnp.minimum(pos, jnp.float32(win))
            pooled = acc[POOL_HALO:, :] / count - zg[POOL_HALO:, :]
            pooled_ref[out_rows, cs] = (_dot(pooled.astype(BF16), wg_ref[g]) * ls_ref[:, cs]).astype(BF16)
        return _dot(pooled_ref[out_rows, :], wout_ref[...])

    nxt = project(0)
    for s in range(n):
        x, ze = nxt
        if s + 1 < n:
            nxt = project(s + 1)
        y = pool(s, ze)
        o_ref[s * sub:(s + 1) * sub, :] = _res_ln(x, y, m, 0, g_ref[...], b_ref[...], alpha)


def _pool_layer(x, mod, w_in, w_grp, scale, w_out, ln_g, ln_b, *, seq, alpha, tm=512, sub=256):
    t, d = x.shape
    ng, gd, _ = w_grp.shape
    hb = tm // POOL_HALO
    return pl.pallas_call(
        functools.partial(_pool_kernel, tm=tm, sub=sub, seq=seq, alpha=alpha),
        grid=(t // tm,),
        in_specs=[
            pl.BlockSpec((tm, d), lambda i: (i, 0)),
            pl.BlockSpec((POOL_HALO, d), lambda i: (jnp.maximum(i * hb - 1, 0), 0)),
            pl.BlockSpec((1, 6, d), lambda i: (i * tm // seq, 0, 0)),
            _const_spec((d, d)),
            _const_spec((ng, gd, gd)),
            _const_spec((1, d)),
            _const_spec((d, d)),
            _const_spec((1, d)),
            _const_spec((1, d)),
        ],
        out_specs=pl.BlockSpec((tm, d), lambda i: (i, 0)),
        out_shape=jax.ShapeDtypeStruct((t, d), F32),
        scratch_shapes=[pltpu.VMEM((tm, d), BF16)],
        compiler_params=_cparams(("parallel",)),
        name="pool_mixer",
    )(x, x, mod, w_in.astype(BF16), w_grp.astype(BF16), scale.reshape(1, d), w_out.astype(BF16),
      ln_g.reshape(1, d), ln_b.reshape(1, d))


def kernel(x, c, mod_w, mod_b, ln_g, ln_b, gm_w_in, gm_ln_g, gm_ln_b, gm_w_s, gm_b_s, gm_w_out, fox_w_in,
           fox_b_f, fox_w_out, pool_w_in, pool_w_grp, pool_scale, pool_w_out, ffn_w13, ffn_w2, moe_w_router,
           moe_b_router, moe_w13, moe_w2):
    bsz, seq, d = x.shape
    depth = mod_w.shape[0]
    alpha = (2 * depth) ** 0.25
    n_mixers = 3
    mod_all = _modulation(c, mod_w, mod_b)
    xt = x.reshape(bsz * seq, d)
    for i in range(depth):
        mod = mod_all[i]
        kind, j = i % n_mixers, i // n_mixers
        router = (moe_w_router[i // 2], moe_b_router[i // 2]) if i % 2 else None
        if kind == 0:
            xt = _gmlp_layer(xt, mod, gm_w_in[j], gm_ln_g[j], gm_ln_b[j], gm_w_s[j], gm_b_s[j], gm_w_out[j],
                             ln_g[i, 0], ln_b[i, 0], router, seq=seq, alpha=alpha)
        elif kind == 1:
            xt = _fox_layer(xt, mod, fox_w_in[j], fox_b_f[j], fox_w_out[j], ln_g[i, 0], ln_b[i, 0], router,
                            bsz=bsz, seq=seq, alpha=alpha)
        else:
            assert router is None, "the pooling mixer has no routing epilogue"
            xt = _pool_layer(xt, mod, pool_w_in[j], pool_w_grp[j], pool_scale[j], pool_w_out[j],
                             ln_g[i, 0], ln_b[i, 0], seq=seq, alpha=alpha)
        if router is None:
            xt = _ffn_layer(xt, mod, ffn_w13[i // 2], ffn_w2[i // 2], ln_g[i, 1], ln_b[i, 1],
                            seq=seq, alpha=alpha)
        else:
            xt = _moe_layer(xt, mod, moe_w13, moe_w2, i // 2, ln_g[i, 1], ln_b[i, 1], seq=seq, alpha=alpha)
    return xt.reshape(bsz, seq, d)
```

```python
import functools

import numpy as np
import jax
import jax.numpy as jnp
from jax import lax
from jax.experimental import pallas as pl
from jax.experimental.pallas import tpu as pltpu
from jax.experimental.pallas import tpu_sc as plsc

F32 = jnp.float32
BF16 = jnp.bfloat16

POOL_WINDOWS = (2, 4, 8, 16)
TOP_K = 2
EXPERT_BLOCK = 512
LN_EPS = 1e-5
LANES = 128
SUBLANES = 8
POOL_HALO = 16
VMEM_LIMIT = 56 * 1024 * 1024
SC_CORES = 2
SC_SUBCORES = 16
SC_ROWS = 128


def _cparams(sem):
    return pltpu.CompilerParams(dimension_semantics=sem, vmem_limit_bytes=VMEM_LIMIT)


def _const_spec(shape):
    nd = len(shape)
    return pl.BlockSpec(shape, lambda *_: (0,) * nd, pipeline_mode=pl.Buffered(1))


def _layer_norm(r, g, b):
    mu = jnp.mean(r, axis=-1, keepdims=True)
    xc = r - mu
    var = jnp.mean(xc * xc, axis=-1, keepdims=True)
    return xc * lax.rsqrt(var + LN_EPS) * g + b


def _modulate(x, m, off):
    return x * (1.0 + m[off + 1:off + 2]) + m[off:off + 1]


def _res_ln(x, y, m, off, g, b, alpha):
    return _layer_norm(alpha * x + (1.0 + m[off + 2:off + 3]) * y, g, b)


def _split3(a):
    hi = a.astype(BF16)
    r1 = a - hi.astype(F32)
    mid = r1.astype(BF16)
    lo = (r1 - mid.astype(F32)).astype(BF16)
    return hi, mid, lo


def _dot(a, b):
    return jnp.dot(a, b, preferred_element_type=F32)


def _dot_split(a_hi, a_mid, w):
    n = w.shape[1]
    w_hi = w.astype(BF16)
    w_lo = (w - w_hi.astype(F32)).astype(BF16)
    both = _dot(a_hi, jnp.concatenate([w_hi, w_lo], axis=1))
    return both[:, :n] + both[:, n:] + _dot(a_mid, w_hi)


def _pack_bf16_pair(x):
    n = x.shape[1] // 2
    hi = lax.bitcast_convert_type(x[:, :n].astype(BF16).astype(F32), jnp.uint32)
    lo = lax.bitcast_convert_type(x[:, n:].astype(BF16).astype(F32), jnp.uint32)
    return hi | (lo >> 16)


def _unpack_bf16_pair(w):
    hi = lax.bitcast_convert_type(w & jnp.uint32(0xFFFF0000), F32)
    lo = lax.bitcast_convert_type(w << 16, F32)
    return jnp.concatenate([hi, lo], axis=1)


def _mod_kernel(c_ref, w_ref, b_ref, o_ref):
    c = c_ref[...]
    s = c / (1.0 + jnp.exp(-c))
    s_hi, s_mid, _ = _split3(s)
    w = w_ref[0]
    w_hi = w.astype(BF16)
    w_lo = (w - w_hi.astype(F32)).astype(BF16)
    acc = _dot(s_hi, w_hi) + _dot(s_mid, w_hi) + _dot(s_hi, w_lo)
    o_ref[0] = acc + b_ref[0]


def _modulation(c, mod_w, mod_b):
    depth, d, n = mod_w.shape
    bsz = c.shape[0]
    rows = SUBLANES
    tn = n // 4
    c_pad = jnp.zeros((rows, d), F32).at[:bsz].set(c)
    out = pl.pallas_call(
        _mod_kernel,
        grid=(depth, n // tn),
        in_specs=[
            pl.BlockSpec((rows, d), lambda l, j: (0, 0)),
            pl.BlockSpec((1, d, tn), lambda l, j: (l, 0, j)),
            pl.BlockSpec((1, 1, tn), lambda l, j: (l, 0, j)),
        ],
        out_specs=pl.BlockSpec((1, rows, tn), lambda l, j: (l, 0, j)),
        out_shape=jax.ShapeDtypeStruct((depth, rows, n), F32),
        compiler_params=_cparams(("parallel", "parallel")),
        name="adaln_mod",
    )(c_pad, mod_w, mod_b.reshape(depth, 1, n))
    return out[:, :bsz, :].reshape(depth, bsz, 6, d)


def _gmlp_kernel(x_ref, mod_ref, win_ref, vg_ref, vb_ref, ws_ref, bst_ref, wout_ref, g_ref, b_ref,
                 *rest, tm, sub, chunk, groups, alpha, n_exp, region):
    if n_exp:
        wr_ref, br_ref, o_ref, hp_ref, meta_ref, cnt_ref, gated_ref, carry_ref = rest

        @pl.when(pl.program_id(0) == 0)
        def _():
            carry_ref[...] = jnp.zeros_like(carry_ref)
    else:
        o_ref, gated_ref = rest
    m = mod_ref[0]
    row = lax.broadcasted_iota(jnp.int32, (chunk, chunk), 0)
    col = lax.broadcasted_iota(jnp.int32, (chunk, chunk), 1)
    causal = row >= col
    bst = bst_ref[...]
    w_mix = [jnp.where(causal, ws_ref[g], 0.0).astype(BF16) for g in range(groups)]
    n = tm // sub

    def project(s):
        x = x_ref[s * sub:(s + 1) * sub, :]
        return x, _dot(_modulate(x, m, 0).astype(BF16), win_ref[...])

    def activate(z):
        z = 0.5 * z * (1.0 + lax.erf(z * (2.0 ** -0.5)))
        width = z.shape[1] // 2
        return z[:, :width], _layer_norm(z[:, width:], vg_ref[...], vb_ref[...]).astype(BF16)

    def mix(s, u, v):
        gd = u.shape[1] // groups
        for g in range(groups):
            cs = slice(g * gd, (g + 1) * gd)
            for c in range(sub // chunk):
                rs = slice(c * chunk, (c + 1) * chunk)
                mixed = _dot(w_mix[g], v[rs, cs]) + bst[:, g:g + 1]
                gated_ref[s * sub + c * chunk:s * sub + (c + 1) * chunk, cs] = (u[rs, cs] * mixed).astype(BF16)
        return _dot(gated_ref[s * sub:(s + 1) * sub, :], wout_ref[...])

    def finish(s, x, y):
        rows = slice(s * sub, (s + 1) * sub)
        x_new = _res_ln(x, y, m, 0, g_ref[...], b_ref[...], alpha)
        o_ref[rows, :] = x_new
        if n_exp:
            hp_ref[rows, :], meta_ref[rows, :] = _route_rows(x_new, m, wr_ref, br_ref, carry_ref, n_exp, region)

    nxt = project(0)
    pending = None
    for s in range(n):
        x, z = nxt
        if s + 1 < n:
            nxt = project(s + 1)
        u, v = activate(z)
        if pending is not None:
            finish(*pending)
        pending = (s, x, mix(s, u, v))
    finish(*pending)
    if n_exp:
        cnt_ref[...] = carry_ref[...]


def _gmlp_layer(x, mod, w_in, v_g, v_b, w_s, b_s, w_out, ln_g, ln_b, router=None, *, seq, alpha, tm=512,
                sub=256):
    t, d = x.shape
    groups, chunk, _ = w_s.shape
    width = w_out.shape[0]
    n_exp = router[0].shape[1] if router else 0
    kern = functools.partial(_gmlp_kernel, tm=tm, sub=sub, chunk=chunk, groups=groups, alpha=alpha, n_exp=n_exp,
                             region=t)
    row_map = lambda i: (i, 0)
    out_specs, out_shape = [pl.BlockSpec((tm, d), row_map)], [jax.ShapeDtypeStruct((t, d), F32)]
    scratch = [pltpu.VMEM((tm, width), BF16)]
    extra_specs, extra_args = [], []
    if router:
        r_specs, r_shapes = _router_out(t, d, tm, row_map)
        out_specs, out_shape = out_specs + r_specs, out_shape + r_shapes
        scratch.append(pltpu.VMEM((SUBLANES, LANES), F32))
        extra_specs = [_const_spec((d, LANES)), _const_spec((1, LANES))]
        extra_args = list(_router_operands(*router))
    outs = pl.pallas_call(
        kern,
        grid=(t // tm,),
        in_specs=[
            pl.BlockSpec((tm, d), lambda i: (i, 0)),
            pl.BlockSpec((1, 6, d), lambda i: (i * tm // seq, 0, 0)),
            _const_spec((d, 2 * width)),
            _const_spec((1, width)),
            _const_spec((1, width)),
            _const_spec((groups, chunk, chunk)),
            _const_spec((chunk, groups)),
            _const_spec((width, d)),
            _const_spec((1, d)),
            _const_spec((1, d)),
        ] + extra_specs,
        out_specs=out_specs,
        out_shape=out_shape,
        scratch_shapes=scratch,
        compiler_params=_cparams(("arbitrary",)),
        name="gmlp_mixer",
    )(x, mod, w_in.astype(BF16), v_g.reshape(1, width), v_b.reshape(1, width), w_s, b_s.T,
      w_out.astype(BF16), ln_g.reshape(1, d), ln_b.reshape(1, d), *extra_args)
    return outs if router else outs[0]


def _swiglu_pipelined(hs, w1, w3, w2, emit):
    up = lambda h: (_dot(h, w1), _dot(h, w3))
    nxt = up(hs[0])
    for s in range(len(hs)):
        a, b = nxt
        if s + 1 < len(hs):
            nxt = up(hs[s + 1])
        t = (a / (1.0 + jnp.exp(-a)) * b).astype(BF16)
        emit(s, _dot(t, w2))


def _ffn_kernel(x_ref, mod_ref, w1_ref, w3_ref, w2_ref, g_ref, b_ref, o_ref, h_ref, *, alpha, sub):
    j = pl.program_id(1)
    last = pl.num_programs(1) - 1
    subs = [slice(s, s + sub) for s in range(0, x_ref.shape[0], sub)]
    weights = lambda: (w1_ref[...], w3_ref[...], w2_ref[...])

    @pl.when(j == 0)
    def _():
        hs = []
        for rows in subs:
            h = _modulate(x_ref[rows, :], mod_ref[0], 3).astype(BF16)
            h_ref[rows, :] = h
            hs.append(h)

        def emit(s, part):
            o_ref[subs[s], :] = part

        _swiglu_pipelined(hs, *weights(), emit)

    @pl.when(jnp.logical_and(j > 0, j < last))
    def _():
        def emit(s, part):
            o_ref[subs[s], :] += part

        _swiglu_pipelined([h_ref[rows, :] for rows in subs], *weights(), emit)

    @pl.when(j == last)
    def _():
        def emit(s, part):
            rows = subs[s]
            y = o_ref[rows, :] + part
            o_ref[rows, :] = _res_ln(x_ref[rows, :], y, mod_ref[0], 3, g_ref[...], b_ref[...], alpha)

        _swiglu_pipelined([h_ref[rows, :] for rows in subs], *weights(), emit)


def _ffn_layer(x, mod, w13, w2, ln_g, ln_b, *, seq, alpha, tm=1024, tf=512):
    t, d = x.shape
    f = w2.shape[0]
    nf = f // tf
    assert nf >= 2
    w13b = w13.astype(BF16)
    return pl.pallas_call(
        functools.partial(_ffn_kernel, alpha=alpha, sub=tm // 4),
        grid=(t // tm, nf),
        in_specs=[
            pl.BlockSpec((tm, d), lambda i, j: (i, 0)),
            pl.BlockSpec((1, 6, d), lambda i, j: (i * tm // seq, 0, 0)),
            pl.BlockSpec((d, tf), lambda i, j: (0, j)),
            pl.BlockSpec((d, tf), lambda i, j: (0, nf + j)),
            pl.BlockSpec((tf, d), lambda i, j: (j, 0)),
            _const_spec((1, d)),
            _const_spec((1, d)),
        ],
        out_specs=pl.BlockSpec((tm, d), lambda i, j: (i, 0)),
        out_shape=jax.ShapeDtypeStruct((t, d), F32),
        scratch_shapes=[pltpu.VMEM((tm, d), BF16)],
        compiler_params=_cparams(("parallel", "arbitrary")),
        name="swiglu_dense",
    )(x, mod, w13b, w13b, w2.astype(BF16), ln_g.reshape(1, d), ln_b.reshape(1, d))


MOE_GROUP = 2


def _moe_ffn_kernel(ge_ref, gb_ref, gr_ref, h_ref, w1_ref, w3_ref, w2_ref, o_ref, acc_ref, *, sub):
    g = pl.program_id(0)
    j = pl.program_id(1)
    last = pl.num_programs(1) - 1
    n_rows = gr_ref[g]
    n_live = (n_rows + sub - 1) // sub
    n_sub = o_ref.shape[0] // sub

    @pl.when(jnp.logical_and(n_rows > 0, j == 0))
    def _():
        acc_ref[...] = jnp.zeros_like(acc_ref)

    def run(count, final):
        subs = [slice(s * sub, (s + 1) * sub) for s in range(count)]
        row = lax.broadcasted_iota(jnp.int32, (sub, 1), 0)

        def rows_in(rows):
            w = jnp.where(row + rows.start < n_rows, h_ref[rows, :], jnp.uint32(0))
            return _unpack_bf16_pair(w).astype(BF16)

        def emit(s, part):
            if final:
                o_ref[subs[s], :] = _pack_bf16_pair(acc_ref[subs[s], :] + part)
            else:
                acc_ref[subs[s], :] += part

        _swiglu_pipelined([rows_in(rows) for rows in subs], w1_ref[0, 0].astype(BF16), w3_ref[0, 0].astype(BF16),
                          w2_ref[0, 0].astype(BF16), emit)

    for count in range(1, n_sub + 1):
        @pl.when(jnp.logical_and(n_live == count, j < last))
        def _():
            run(count, False)

        @pl.when(jnp.logical_and(n_live == count, j == last))
        def _():
            run(count, True)

    for s in range(n_sub):
        @pl.when(jnp.logical_and(s >= n_live, j == last))
        def _():
            o_ref[s * sub:(s + 1) * sub, :] = jnp.zeros((sub, o_ref.shape[1]), jnp.uint32)


def _moe_ffn(h_slots, group_expert, group_block, group_rows, w13, w2, layer, *, tf=512):
    half = h_slots.shape[1]
    d = 2 * half
    f = w2.shape[2]
    nf = f // tf
    assert nf >= 2
    tm = MOE_GROUP * EXPERT_BLOCK

    def jj(g, j, gr):
        return jnp.where(gr[g] > 0, j, nf - 1)

    grid_spec = pltpu.PrefetchScalarGridSpec(
        num_scalar_prefetch=3,
        grid=(group_expert.shape[0], nf),
        in_specs=[
            pl.BlockSpec((tm, half), lambda g, j, ge, gb, gr: (gb[g], 0)),
            pl.BlockSpec((1, 1, d, tf), lambda g, j, ge, gb, gr: (layer, ge[g], 0, jj(g, j, gr))),
            pl.BlockSpec((1, 1, d, tf), lambda g, j, ge, gb, gr: (layer, ge[g], 0, nf + jj(g, j, gr))),
            pl.BlockSpec((1, 1, tf, d), lambda g, j, ge, gb, gr: (layer, ge[g], jj(g, j, gr), 0)),
        ],
        out_specs=pl.BlockSpec((tm, half), lambda g, j, ge, gb, gr: (gb[g], 0)),
        scratch_shapes=[pltpu.VMEM((tm, d), F32)],
    )
    return pl.pallas_call(
        functools.partial(_moe_ffn_kernel, sub=EXPERT_BLOCK // 2),
        grid_spec=grid_spec,
        out_shape=jax.ShapeDtypeStruct(h_slots.shape, jnp.uint32),
        compiler_params=_cparams(("arbitrary", "arbitrary")),
        name="swiglu_experts",
    )(group_expert, group_block, group_rows, h_slots, w13, w13, w2)


def _route_rows(x_rows, m, wr_ref, br_ref, carry_ref, n_exp, region):
    tm = x_rows.shape[0]
    h = _modulate(x_rows, m, 3)
    h_hi, h_mid, _ = _split3(h)
    logits = _dot_split(h_hi, h_mid, wr_ref[...]) + br_ref[...]
    lane = lax.broadcasted_iota(jnp.int32, (tm, LANES), 1)
    neg = jnp.float32(-jnp.inf)
    logits = jnp.where(lane < n_exp, logits, neg)
    v0 = jnp.max(logits, axis=1, keepdims=True)
    e0 = jnp.min(jnp.where(logits == v0, lane, LANES), axis=1, keepdims=True)
    rest = jnp.where(lane == e0, neg, logits)
    v1 = jnp.max(rest, axis=1, keepdims=True)
    e1 = jnp.min(jnp.where(rest == v1, lane, LANES), axis=1, keepdims=True)
    p = jnp.exp(v1 - v0)
    g0 = 1.0 / (1.0 + p)
    g1 = p / (1.0 + p)
    oh0 = lane == e0
    oh1 = lane == e1
    onehot = jnp.where(jnp.logical_or(oh0, oh1), 1.0, 0.0).astype(BF16)
    r = lax.broadcasted_iota(jnp.int32, (tm, tm), 0)
    c = lax.broadcasted_iota(jnp.int32, (tm, tm), 1)
    strict = jnp.where(r > c, 1.0, 0.0).astype(BF16)
    before = _dot(strict, onehot) + carry_ref[0:1, :]
    rank0 = jnp.sum(jnp.where(oh0, before, 0.0), axis=1, keepdims=True)
    rank1 = jnp.sum(jnp.where(oh1, before, 0.0), axis=1, keepdims=True)
    total = carry_ref[0:1, :] + jnp.sum(onehot.astype(F32), axis=0, keepdims=True)
    carry_ref[...] = jnp.broadcast_to(total, carry_ref.shape)
    meta = jnp.where(lane == 0, e0.astype(F32), 0.0)
    meta = jnp.where(lane == 1, e1.astype(F32), meta)
    meta = jnp.where(lane == 2, g0, meta)
    meta = jnp.where(lane == 3, g1, meta)
    meta = jnp.where(lane == 4, e0.astype(F32) * region + rank0, meta)
    meta = jnp.where(lane == 5, e1.astype(F32) * region + rank1, meta)
    return _pack_bf16_pair(h), meta


def _router_operands(w_router, b_router):
    d, n_exp = w_router.shape
    wr = jnp.zeros((d, LANES), F32).at[:, :n_exp].set(w_router)
    br = jnp.zeros((1, LANES), F32).at[0, :n_exp].set(b_router)
    return wr, br


def _router_out(t, d, tm, row_map):
    specs = [pl.BlockSpec((tm, d // 2), row_map), pl.BlockSpec((tm, LANES), row_map),
             pl.BlockSpec((SUBLANES, LANES), lambda *_: (0, 0))]
    shapes = [jax.ShapeDtypeStruct((t, d // 2), jnp.uint32), jax.ShapeDtypeStruct((t, LANES), F32),
              jax.ShapeDtypeStruct((SUBLANES, LANES), F32)]
    return specs, shapes


def _combine_kernel(x_ref, ya_ref, yb_ref, meta_ref, mod_ref, g_ref, b_ref, o_ref, *, alpha):
    meta = meta_ref[...]
    y = meta[:, 2:3] * _unpack_bf16_pair(ya_ref[...]) + meta[:, 3:4] * _unpack_bf16_pair(yb_ref[...])
    o_ref[...] = _res_ln(x_ref[...], y, mod_ref[0], 3, g_ref[...], b_ref[...], alpha)


def _combine(x, ya, yb, meta, mod, ln_g, ln_b, *, seq, alpha, tm=512):
    t, d = x.shape
    row = pl.BlockSpec((tm, d), lambda i: (i, 0))
    packed = pl.BlockSpec((tm, d // 2), lambda i: (i, 0))
    return pl.pallas_call(
        functools.partial(_combine_kernel, alpha=alpha),
        grid=(t // tm,),
        in_specs=[row, packed, packed,
                  pl.BlockSpec((tm, LANES), lambda i: (i, 0)),
                  pl.BlockSpec((1, 6, d), lambda i: (i * tm // seq, 0, 0)),
                  _const_spec((1, d)), _const_spec((1, d))],
        out_specs=row,
        out_shape=jax.ShapeDtypeStruct((t, d), F32),
        compiler_params=_cparams(("parallel",)),
        name="moe_combine",
    )(x, ya, yb, meta, mod, ln_g.reshape(1, d), ln_b.reshape(1, d))


def _sc_mesh():
    return plsc.VectorSubcoreMesh(core_axis_name="c", subcore_axis_name="s", num_cores=SC_CORES,
                                  num_subcores=SC_SUBCORES)


def _sc_worker_share(n):
    workers = SC_CORES * SC_SUBCORES
    per_worker = n // workers
    steps = per_worker // SC_ROWS
    assert steps * SC_ROWS * workers == n
    return per_worker, steps


def _sc_gather_rows(table, idx_a, idx_b):
    n = idx_a.shape[0]
    width = table.shape[1]
    per_worker, steps = _sc_worker_share(n)

    def body(table_hbm, ia_hbm, ib_hbm, oa_hbm, ob_hbm, ia_v, ib_v, rows_v, sem):
        wid = lax.axis_index("s") * SC_CORES + lax.axis_index("c")
        base = pl.multiple_of(wid * per_worker, SC_ROWS)
        pltpu.sync_copy(ia_hbm.at[pl.ds(base, per_worker)], ia_v)
        pltpu.sync_copy(ib_hbm.at[pl.ds(base, per_worker)], ib_v)

        @pl.loop(0, steps)
        def _(j):
            lo = pl.multiple_of(j * SC_ROWS, SC_ROWS)
            for idx_v, out_hbm in ((ia_v, oa_hbm), (ib_v, ob_hbm)):
                pltpu.async_copy(table_hbm.at[idx_v.at[pl.ds(lo, SC_ROWS)]], rows_v, sem).wait()
                pltpu.sync_copy(rows_v, out_hbm.at[pl.ds(base + lo, SC_ROWS)])

    out = jax.ShapeDtypeStruct((n, width), table.dtype)
    return pl.kernel(
        body,
        out_type=(out, out),
        mesh=_sc_mesh(),
        scratch_types=[
            pltpu.VMEM((per_worker,), jnp.int32),
            pltpu.VMEM((per_worker,), jnp.int32),
            pltpu.VMEM((SC_ROWS, width), table.dtype),
            pltpu.SemaphoreType.DMA,
        ],
        name="sc_row_gather",
    )(table, idx_a, idx_b)


def _sc_scatter_rows(rows, dest_a, dest_b, n_out):
    n, width = rows.shape
    per_worker, steps = _sc_worker_share(n)

    def body(rows_hbm, da_hbm, db_hbm, out_hbm, ia_v, ib_v, rows_v, sem):
        wid = lax.axis_index("s") * SC_CORES + lax.axis_index("c")

        @pl.loop(0, steps)
        def _(j):
            off = pl.multiple_of(wid * per_worker + j * SC_ROWS, SC_ROWS)
            pltpu.sync_copy(da_hbm.at[pl.ds(off, SC_ROWS)], ia_v)
            pltpu.sync_copy(db_hbm.at[pl.ds(off, SC_ROWS)], ib_v)
            pltpu.sync_copy(rows_hbm.at[pl.ds(off, SC_ROWS)], rows_v)
            pltpu.async_copy(rows_v, out_hbm.at[ia_v], sem).wait()
            pltpu.async_copy(rows_v, out_hbm.at[ib_v], sem).wait()

    return pl.kernel(
        body,
        out_type=jax.ShapeDtypeStruct((n_out, width), rows.dtype),
        mesh=_sc_mesh(),
        scratch_types=[
            pltpu.VMEM((SC_ROWS,), jnp.int32),
            pltpu.VMEM((SC_ROWS,), jnp.int32),
            pltpu.VMEM((SC_ROWS, width), rows.dtype),
            pltpu.SemaphoreType.DMA,
        ],
        name="sc_row_scatter",
    )(rows, dest_a, dest_b)


def _moe_layer(routed, mod, w13, w2, layer, ln_g, ln_b, *, seq, alpha):
    x, h, meta, cnt = routed
    t, d = x.shape
    n_exp = w13.shape[1]
    dest0 = meta[:, 4].astype(jnp.int32)
    dest1 = meta[:, 5].astype(jnp.int32)
    counts = cnt[0, :n_exp].astype(jnp.int32)
    group_rows = MOE_GROUP * EXPERT_BLOCK
    assert t % group_rows == 0
    region_groups = t // group_rows
    groups_per_expert = (counts + group_rows - 1) // group_rows
    group_end = jnp.cumsum(groups_per_expert)
    g = jnp.arange(t * TOP_K // group_rows + n_exp, dtype=jnp.int32)
    expert = jnp.minimum(jnp.sum(group_end[None, :] <= g[:, None], axis=1), n_exp - 1).astype(jnp.int32)
    blk_in_expert = g - (group_end - groups_per_expert)[expert]
    used = g < group_end[-1]
    spare_block = n_exp * region_groups
    group_block = jnp.where(used, expert * region_groups + blk_in_expert, spare_block).astype(jnp.int32)
    rows_in_group = jnp.where(used, jnp.clip(counts[expert] - blk_in_expert * group_rows, 0, group_rows),
                              0).astype(jnp.int32)
    h_slots = _sc_scatter_rows(h, dest0, dest1, (spare_block + 1) * group_rows)
    y_slots = _moe_ffn(h_slots, expert, group_block, rows_in_group, w13, w2, layer)
    ya, yb = _sc_gather_rows(y_slots, dest0, dest1)
    return _combine(x, ya, yb, meta, mod, ln_g, ln_b, seq=seq, alpha=alpha)


HEAD_PAD = LANES
GATE_PARTS = 3
GATE_ROWS = 16
VAL_PAD = 16
LOG2E = 1.4426950408889634


def _head_lanes(h, hd):
    k0 = (h % (HEAD_PAD // hd)) * hd
    return k0, (hd if k0 == 0 else 0)


def _fox_proj_kernel(x_ref, mod_ref, wf_ref, bf_ref, wq_ref, wk_ref, wv_ref, selk_ref, onek_ref, selq_ref,
                     oneq_ref, onev_ref, k_ref, q_ref, v_ref, carry_ref, *, q_scale, heads, hd, nb):
    tm = x_ref.shape[0]

    @pl.when(pl.program_id(0) % nb == 0)
    def _():
        carry_ref[...] = jnp.zeros_like(carry_ref)

    h_hi, h_mid, _ = _split3(_modulate(x_ref[...], mod_ref[0], 0))

    f = _dot_split(h_hi, h_mid, wf_ref[...]) + bf_ref[...]
    lf = jnp.minimum(f, 0.0) - jnp.log(1.0 + jnp.exp(-jnp.abs(f)))
    r = lax.broadcasted_iota(jnp.int32, (tm, tm), 0)
    c = lax.broadcasted_iota(jnp.int32, (tm, tm), 1)
    tri = jnp.where(r >= c, 1.0, 0.0).astype(BF16)
    sums = _dot(tri, jnp.concatenate(_split3(lf), axis=1))
    cs = carry_ref[0:1, :] + sum(sums[:, n * LANES:(n + 1) * LANES] for n in range(GATE_PARTS))
    carry_ref[...] = jnp.broadcast_to(cs[tm - 1:tm, :], carry_ref.shape)
    cs = cs * LOG2E

    gate_k = onek_ref[...] + _dot(jnp.concatenate(_split3(cs), axis=1), selk_ref[...])
    kc = _dot(h_hi, wk_ref[...])
    lane = lax.broadcasted_iota(jnp.int32, (tm, HEAD_PAD), 1)
    for h in range(heads):
        k0, _ = _head_lanes(h, hd)
        src = (h * hd // HEAD_PAD) * HEAD_PAD
        is_k = jnp.logical_and(lane >= k0, lane < k0 + hd)
        hs = slice(h * HEAD_PAD, (h + 1) * HEAD_PAD)
        k_ref[:, hs] = jnp.where(is_k, kc[:, src:src + HEAD_PAD], gate_k[:, hs]).astype(BF16)

    nt = (((1,), (1,)), ((), ()))
    qc = lax.dot_general(wq_ref[...], h_hi, nt, preferred_element_type=F32) * (q_scale * LOG2E)
    gate_q = oneq_ref[...] + _dot(selq_ref[...], jnp.concatenate(_split3(cs.T), axis=0))
    fill = HEAD_PAD - hd - GATE_ROWS
    for h in range(heads):
        k0, g0 = _head_lanes(h, hd)
        base = h * HEAD_PAD
        q_ref[0, base + k0:base + k0 + hd, :] = qc[h * hd:(h + 1) * hd, :].astype(BF16)
        q_ref[0, base + g0:base + g0 + GATE_ROWS, :] = gate_q[h * GATE_ROWS:(h + 1) * GATE_ROWS, :].astype(BF16)
        q_ref[0, base + g0 + GATE_ROWS:base + g0 + GATE_ROWS + fill, :] = jnp.zeros((fill, tm), BF16)

    v = lax.dot_general(wv_ref[...], h_hi, nt, preferred_element_type=F32) + onev_ref[...]
    v_ref[0] = v.astype(BF16)


def _fox_proj(x, mod, w_in, b_f, *, bsz, seq, heads, tm=512):
    t, d = x.shape
    hd = d // heads
    assert HEAD_PAD == 2 * hd and GATE_ROWS >= 2 * GATE_PARTS
    nb = seq // tm
    wide = heads * HEAD_PAD
    wq_t = w_in[:, :d].T.astype(BF16)
    wk = w_in[:, d:2 * d].astype(BF16)
    vrows = heads * (hd + VAL_PAD)
    wv_t = jnp.pad(w_in[:, 2 * d:3 * d].reshape(d, heads, hd),
                   ((0, 0), (0, 0), (0, VAL_PAD))).reshape(d, vrows).T.astype(BF16)
    wf = jnp.zeros((d, LANES), F32).at[:, :heads].set(w_in[:, 3 * d:])
    bf = jnp.zeros((1, LANES), F32).at[0, :heads].set(b_f)
    sel_k = np.zeros((GATE_PARTS, LANES, wide), np.float32)
    one_k = np.zeros((1, wide), np.float32)
    sel_q = np.zeros((GATE_PARTS, heads * GATE_ROWS, LANES), np.float32)
    one_q = np.zeros((heads * GATE_ROWS, 1), np.float32)
    one_v = np.zeros((vrows, 1), np.float32)
    for hh in range(heads):
        _, g0 = _head_lanes(hh, hd)
        one_v[hh * (hd + VAL_PAD) + hd, 0] = 1.0
        for n in range(GATE_PARTS):
            sel_k[n, hh, hh * HEAD_PAD + g0 + n] = -1.0
            one_q[hh * GATE_ROWS + n, 0] = 1.0
            one_k[0, hh * HEAD_PAD + g0 + GATE_PARTS + n] = 1.0
            sel_q[n, hh * GATE_ROWS + GATE_PARTS + n, hh] = 1.0
    return pl.pallas_call(
        functools.partial(_fox_proj_kernel, q_scale=hd ** -0.5, heads=heads, hd=hd, nb=nb),
        grid=(t // tm,),
        in_specs=[
            pl.BlockSpec((tm, d), lambda i: (i, 0)),
            pl.BlockSpec((1, 6, d), lambda i: (i * tm // seq, 0, 0)),
            _const_spec((d, LANES)),
            _const_spec((1, LANES)),
            _const_spec((d, d)),
            _const_spec((d, d)),
            _const_spec((vrows, d)),
            _const_spec((GATE_PARTS * LANES, wide)),
            _const_spec((1, wide)),
            _const_spec((heads * GATE_ROWS, GATE_PARTS * LANES)),
            _const_spec((heads * GATE_ROWS, 1)),
            _const_spec((vrows, 1)),
        ],
        out_specs=[
            pl.BlockSpec((tm, wide), lambda i: (i, 0)),
            pl.BlockSpec((1, wide, tm), lambda i: (i // nb, 0, i % nb)),
            pl.BlockSpec((1, vrows, tm), lambda i: (i // nb, 0, i % nb)),
        ],
        out_shape=[
            jax.ShapeDtypeStruct((t, wide), BF16),
            jax.ShapeDtypeStruct((bsz, wide, seq), BF16),
            jax.ShapeDtypeStruct((bsz, vrows, seq), BF16),
        ],
        scratch_shapes=[pltpu.VMEM((SUBLANES, LANES), F32)],
        compiler_params=_cparams(("arbitrary",)),
        name="fox_proj",
    )(x, mod, wf, bf, wq_t, wk, wv_t, jnp.asarray(sel_k.reshape(GATE_PARTS * LANES, wide), BF16), jnp.asarray(one_k),
      jnp.asarray(np.concatenate(list(sel_q), axis=1), BF16), jnp.asarray(one_q), jnp.asarray(one_v))


def _flash_kernel(qi_ref, kj_ref, k_ref, q_ref, v_ref, x_ref, mod_ref, wout_ref, g_ref, b_ref, *rest,
                  heads, hd, blk, alpha, n_exp, region):
    p_idx = pl.program_id(1)
    if n_exp:
        wr_ref, br_ref, o_ref, hp_ref, meta_ref, cnt_ref, m_ref, acc_ref, s_ref, carry_ref = rest

        @pl.when(jnp.logical_and(pl.program_id(0) == 0, p_idx == 0))
        def _():
            carry_ref[...] = jnp.zeros_like(carry_ref)
    else:
        o_ref, m_ref, acc_ref, s_ref = rest
    qi = qi_ref[p_idx]
    kj = kj_ref[p_idx]
    vp = hd + VAL_PAD

    @pl.when(kj == 0)
    def _():
        m_ref[...] = jnp.full(m_ref.shape, -jnp.inf, F32)
        acc_ref[...] = jnp.zeros_like(acc_ref)

    def step(diagonal):
        half = blk // 2
        tiles = [(slice(0, half), slice(0, half)), (slice(half, blk), slice(0, blk))] if diagonal else \
                [(slice(0, blk), slice(0, blk))]
        if diagonal:
            causal = lambda rows, first_q: (lax.broadcasted_iota(jnp.int32, (rows, half), 0)
                                            <= lax.broadcasted_iota(jnp.int32, (rows, half), 1) + first_q)
            keep = [causal(half, 0), causal(blk, half)]

        def scores(h):
            hs = slice(h * HEAD_PAD, (h + 1) * HEAD_PAD)
            tops = []
            for n, (qs, ks) in enumerate(tiles):
                s = _dot(k_ref[ks, hs], q_ref[0, hs, qs])
                if diagonal:
                    s = jnp.where(keep[n], s, -jnp.inf)
                s_ref[h % depth, ks, qs] = s
                tops.append(jnp.max(s, axis=0, keepdims=True))
            return tops

        depth = s_ref.shape[0]
        ahead = [scores(h) for h in range(depth - 1)]
        for h in range(heads):
            if h + depth - 1 < heads:
                ahead.append(scores(h + depth - 1))
            rs = slice(h * vp, (h + 1) * vp)
            for (qs, ks), m_cur in zip(tiles, ahead.pop(0)):
                m_prev = m_ref[h:h + 1, qs]
                m_new = jnp.maximum(m_prev, m_cur)
                m_ref[h:h + 1, qs] = m_new
                p = jnp.exp2(s_ref[h % depth, ks, qs] - m_new).astype(BF16)
                acc_ref[rs, qs] = acc_ref[rs, qs] * jnp.exp2(m_prev - m_new) + _dot(v_ref[0, rs, ks], p)

    @pl.when(kj < qi)
    def _():
        step(False)

    @pl.when(kj == qi)
    def _():
        step(True)
        per = LANES // hd
        cols = []
        for g in range(heads // per):
            parts = []
            for n in range(per):
                r0 = (g * per + n) * vp
                parts.append(acc_ref[r0:r0 + hd, :] / acc_ref[r0 + hd:r0 + hd + 1, :])
            cols.append(jnp.concatenate(parts, axis=0).T.astype(BF16))
        y = _dot(jnp.concatenate(cols, axis=1), wout_ref[...])
        x_new = _res_ln(x_ref[...], y, mod_ref[0], 0, g_ref[...], b_ref[...], alpha)
        o_ref[...] = x_new
        if n_exp:
            hp_ref[...], meta_ref[...] = _route_rows(x_new, mod_ref[0], wr_ref, br_ref, carry_ref, n_exp, region)
            cnt_ref[...] = carry_ref[...]


def _flash(k_aug, q_aug_t, v_t, x, mod, w_out, ln_g, ln_b, router=None, *, bsz, seq, heads, alpha, blk=512):
    t, wide = k_aug.shape
    vrows = v_t.shape[1]
    hd = vrows // heads - VAL_PAD
    d = heads * hd
    nq = seq // blk
    pairs = [(i, j) for i in range(nq) for j in range(i + 1)]
    qi = jnp.asarray(np.array([p[0] for p in pairs], np.int32))
    kj = jnp.asarray(np.array([p[1] for p in pairs], np.int32))
    n_exp = router[0].shape[1] if router else 0
    row_map = lambda b, p, qi, kj: (b * nq + qi[p], 0)
    const = lambda shape: pl.BlockSpec(shape, lambda b, p, qi, kj: (0, 0), pipeline_mode=pl.Buffered(1))
    out_specs, out_shape = [pl.BlockSpec((blk, d), row_map)], [jax.ShapeDtypeStruct((t, d), F32)]
    scratch = [pltpu.VMEM((heads, blk), F32), pltpu.VMEM((vrows, blk), F32), pltpu.VMEM((3, blk, blk), F32)]
    extra_specs, extra_args = [], []
    if router:
        r_specs, r_shapes = _router_out(t, d, blk, row_map)
        out_specs, out_shape = out_specs + r_specs, out_shape + r_shapes
        scratch.append(pltpu.VMEM((SUBLANES, LANES), F32))
        extra_specs = [const((d, LANES)), const((1, LANES))]
        extra_args = list(_router_operands(*router))
    grid_spec = pltpu.PrefetchScalarGridSpec(
        num_scalar_prefetch=2,
        grid=(bsz, len(pairs)),
        in_specs=[
            pl.BlockSpec((blk, wide), lambda b, p, qi, kj: (b * nq + kj[p], 0)),
            pl.BlockSpec((1, wide, blk), lambda b, p, qi, kj: (b, 0, qi[p])),
            pl.BlockSpec((1, vrows, blk), lambda b, p, qi, kj: (b, 0, kj[p])),
            pl.BlockSpec((blk, d), row_map),
            pl.BlockSpec((1, 6, d), lambda b, p, qi, kj: (b, 0, 0)),
            const((d, d)),
            const((1, d)),
            const((1, d)),
        ] + extra_specs,
        out_specs=out_specs,
        scratch_shapes=scratch,
    )
    outs = pl.pallas_call(
        functools.partial(_flash_kernel, heads=heads, hd=hd, blk=blk, alpha=alpha, n_exp=n_exp, region=t),
        grid_spec=grid_spec,
        out_shape=out_shape,
        compiler_params=_cparams(("arbitrary", "arbitrary")),
        name="fox_attention",
    )(qi, kj, k_aug, q_aug_t, v_t, x, mod, w_out.astype(BF16), ln_g.reshape(1, d), ln_b.reshape(1, d), *extra_args)
    return outs if router else outs[0]


def _fox_layer(x, mod, w_in, b_f, w_out, ln_g, ln_b, router=None, *, bsz, seq, alpha):
    heads = b_f.shape[0]
    k_aug, q_aug_t, v_t = _fox_proj(x, mod, w_in, b_f, bsz=bsz, seq=seq, heads=heads)
    return _flash(k_aug, q_aug_t, v_t, x, mod, w_out, ln_g, ln_b, router, bsz=bsz, seq=seq, heads=heads,
                  alpha=alpha)


def _pool_kernel(x_ref, xh_ref, mod_ref, win_ref, wg_ref, ls_ref, wout_ref, g_ref, b_ref, o_ref, pooled_ref,
                 *, tm, sub, seq, alpha):
    pos0 = (pl.program_id(0) * tm) % seq
    m = mod_ref[0]
    n = tm // sub
    halo_ok = jnp.where(pos0 > 0, jnp.float32(1.0), jnp.float32(0.0))
    rows = lax.broadcasted_iota(jnp.int32, (sub + POOL_HALO, 1), 0)

    def project(s):
        x = x_ref[s * sub:(s + 1) * sub, :]
        halo = xh_ref[...] if s == 0 else x_ref[s * sub - POOL_HALO:s * sub, :]
        ze = _dot(_modulate(jnp.concatenate([halo, x], axis=0), m, 0).astype(BF16), win_ref[...])
        if s == 0:
            ze = ze * jnp.where(rows < POOL_HALO, halo_ok, jnp.float32(1.0))
        return x, ze

    def pool(s, ze):
        pos = (pos0 + s * sub + lax.broadcasted_iota(jnp.int32, (sub, 1), 0) + 1).astype(F32)
        gd = ze.shape[1] // len(POOL_WINDOWS)
        out_rows = slice(s * sub, (s + 1) * sub)
        for g, win in enumerate(POOL_WINDOWS):
            cs = slice(g * gd, (g + 1) * gd)
            zg = ze[:, cs]
            acc = zg
            span = 1
            while span < win:
                acc = acc + pltpu.roll(acc, span, 0)
                span *= 2
            count = jnp.minimum(pos, jnp.float32(win))
            pooled = acc[POOL_HALO:, :] / count - zg[POOL_HALO:, :]
            pooled_ref[out_rows, cs] = (_dot(pooled.astype(BF16), wg_ref[g]) * ls_ref[:, cs]).astype(BF16)
        return _dot(pooled_ref[out_rows, :], wout_ref[...])

    nxt = project(0)
    for s in range(n):
        x, ze = nxt
        if s + 1 < n:
            nxt = project(s + 1)
        y = pool(s, ze)
        o_ref[s * sub:(s + 1) * sub, :] = _res_ln(x, y, m, 0, g_ref[...], b_ref[...], alpha)


def _pool_layer(x, mod, w_in, w_grp, scale, w_out, ln_g, ln_b, *, seq, alpha, tm=512, sub=256):
    t, d = x.shape
    ng, gd, _ = w_grp.shape
    hb = tm // POOL_HALO
    return pl.pallas_call(
        functools.partial(_pool_kernel, tm=tm, sub=sub, seq=seq, alpha=alpha),
        grid=(t // tm,),
        in_specs=[
            pl.BlockSpec((tm, d), lambda i: (i, 0)),
            pl.BlockSpec((POOL_HALO, d), lambda i: (jnp.maximum(i * hb - 1, 0), 0)),
            pl.BlockSpec((1, 6, d), lambda i: (i * tm // seq, 0, 0)),
            _const_spec((d, d)),
            _const_spec((ng, gd, gd)),
            _const_spec((1, d)),
            _const_spec((d, d)),
            _const_spec((1, d)),
            _const_spec((1, d)),
        ],
        out_specs=pl.BlockSpec((tm, d), lambda i: (i, 0)),
        out_shape=jax.ShapeDtypeStruct((t, d), F32),
        scratch_shapes=[pltpu.VMEM((tm, d), BF16)],
        compiler_params=_cparams(("parallel",)),
        name="pool_mixer",
    )(x, x, mod, w_in.astype(BF16), w_grp.astype(BF16), scale.reshape(1, d), w_out.astype(BF16),
      ln_g.reshape(1, d), ln_b.reshape(1, d))


def kernel(x, c, mod_w, mod_b, ln_g, ln_b, gm_w_in, gm_ln_g, gm_ln_b, gm_w_s, gm_b_s, gm_w_out, fox_w_in,
           fox_b_f, fox_w_out, pool_w_in, pool_w_grp, pool_scale, pool_w_out, ffn_w13, ffn_w2, moe_w_router,
           moe_b_router, moe_w13, moe_w2):
    bsz, seq, d = x.shape
    depth = mod_w.shape[0]
    alpha = (2 * depth) ** 0.25
    n_mixers = 3
    mod_all = _modulation(c, mod_w, mod_b)
    xt = x.reshape(bsz * seq, d)
    for i in range(depth):
        mod = mod_all[i]
        kind, j = i % n_mixers, i // n_mixers
        router = (moe_w_router[i // 2], moe_b_router[i // 2]) if i % 2 else None
        if kind == 0:
            xt = _gmlp_layer(xt, mod, gm_w_in[j], gm_ln_g[j], gm_ln_b[j], gm_w_s[j], gm_b_s[j], gm_w_out[j],
                             ln_g[i, 0], ln_b[i, 0], router, seq=seq, alpha=alpha)
        elif kind == 1:
            xt = _fox_layer(xt, mod, fox_w_in[j], fox_b_f[j], fox_w_out[j], ln_g[i, 0], ln_b[i, 0], router,
                            bsz=bsz, seq=seq, alpha=alpha)
        else:
            assert router is None, "the pooling mixer has no routing epilogue"
            xt = _pool_layer(xt, mod, pool_w_in[j], pool_w_grp[j], pool_scale[j], pool_w_out[j],
                             ln_g[i, 0], ln_b[i, 0], seq=seq, alpha=alpha)
        if router is None:
            xt = _ffn_layer(xt, mod, ffn_w13[i // 2], ffn_w2[i // 2], ln_g[i, 1], ln_b[i, 1],
                            seq=seq, alpha=alpha)
        else:
            xt = _moe_layer(xt, mod, moe_w13, moe_w2, i // 2, ln_g[i, 1], ln_b[i, 1], seq=seq, alpha=alpha)
    return xt.reshape(bsz, seq, d)
```

```python
import functools

import numpy as np
import jax
import jax.numpy as jnp
from jax import lax
from jax.experimental import pallas as pl
from jax.experimental.pallas import tpu as pltpu
from jax.experimental.pallas import tpu_sc as plsc

F32 = jnp.float32
BF16 = jnp.bfloat16

POOL_WINDOWS = (2, 4, 8, 16)
TOP_K = 2
EXPERT_BLOCK = 512
LN_EPS = 1e-5
LANES = 128
SUBLANES = 8
POOL_HALO = 16
VMEM_LIMIT = 56 * 1024 * 1024
SC_CORES = 2
SC_SUBCORES = 16
SC_ROWS = 128


def _cparams(sem):
    return pltpu.CompilerParams(dimension_semantics=sem, vmem_limit_bytes=VMEM_LIMIT)


def _const_spec(shape):
    nd = len(shape)
    return pl.BlockSpec(shape, lambda *_: (0,) * nd, pipeline_mode=pl.Buffered(1))


def _layer_norm(r, g, b):
    mu = jnp.mean(r, axis=-1, keepdims=True)
    xc = r - mu
    var = jnp.mean(xc * xc, axis=-1, keepdims=True)
    return xc * lax.rsqrt(var + LN_EPS) * g + b


def _modulate(x, m, off):
    return x * (1.0 + m[off + 1:off + 2]) + m[off:off + 1]


def _res_ln(x, y, m, off, g, b, alpha):
    return _layer_norm(alpha * x + (1.0 + m[off + 2:off + 3]) * y, g, b)


def _split3(a):
    hi = a.astype(BF16)
    r1 = a - hi.astype(F32)
    mid = r1.astype(BF16)
    lo = (r1 - mid.astype(F32)).astype(BF16)
    return hi, mid, lo


def _dot(a, b):
    return jnp.dot(a, b, preferred_element_type=F32)


def _dot_split(a_hi, a_mid, w):
    n = w.shape[1]
    w_hi = w.astype(BF16)
    w_lo = (w - w_hi.astype(F32)).astype(BF16)
    both = _dot(a_hi, jnp.concatenate([w_hi, w_lo], axis=1))
    return both[:, :n] + both[:, n:] + _dot(a_mid, w_hi)


def _pack_bf16_pair(x):
    n = x.shape[1] // 2
    hi = lax.bitcast_convert_type(x[:, :n].astype(BF16).astype(F32), jnp.uint32)
    lo = lax.bitcast_convert_type(x[:, n:].astype(BF16).astype(F32), jnp.uint32)
    return hi | (lo >> 16)


def _unpack_bf16_pair(w):
    hi = lax.bitcast_convert_type(w & jnp.uint32(0xFFFF0000), F32)
    lo = lax.bitcast_convert_type(w << 16, F32)
    return jnp.concatenate([hi, lo], axis=1)


def _mod_kernel(c_ref, w_ref, b_ref, o_ref):
    c = c_ref[...]
    s = c / (1.0 + jnp.exp(-c))
    s_hi, s_mid, _ = _split3(s)
    w = w_ref[0]
    w_hi = w.astype(BF16)
    w_lo = (w - w_hi.astype(F32)).astype(BF16)
    acc = _dot(s_hi, w_hi) + _dot(s_mid, w_hi) + _dot(s_hi, w_lo)
    o_ref[0] = acc + b_ref[0]


def _modulation(c, mod_w, mod_b):
    depth, d, n = mod_w.shape
    bsz = c.shape[0]
    rows = SUBLANES
    tn = n // 4
    c_pad = jnp.zeros((rows, d), F32).at[:bsz].set(c)
    out = pl.pallas_call(
        _mod_kernel,
        grid=(depth, n // tn),
        in_specs=[
            pl.BlockSpec((rows, d), lambda l, j: (0, 0)),
            pl.BlockSpec((1, d, tn), lambda l, j: (l, 0, j)),
            pl.BlockSpec((1, 1, tn), lambda l, j: (l, 0, j)),
        ],
        out_specs=pl.BlockSpec((1, rows, tn), lambda l, j: (l, 0, j)),
        out_shape=jax.ShapeDtypeStruct((depth, rows, n), F32),
        compiler_params=_cparams(("parallel", "parallel")),
        name="adaln_mod",
    )(c_pad, mod_w, mod_b.reshape(depth, 1, n))
    return out[:, :bsz, :].reshape(depth, bsz, 6, d)


def _gmlp_kernel(x_ref, mod_ref, win_ref, vg_ref, vb_ref, ws_ref, bst_ref, wout_ref, g_ref, b_ref,
                 *rest, tm, sub, chunk, groups, alpha, n_exp, region):
    if n_exp:
        wr_ref, br_ref, o_ref, hp_ref, meta_ref, cnt_ref, gated_ref, carry_ref = rest

        @pl.when(pl.program_id(0) == 0)
        def _():
            carry_ref[...] = jnp.zeros_like(carry_ref)
    else:
        o_ref, gated_ref = rest
    m = mod_ref[0]
    row = lax.broadcasted_iota(jnp.int32, (chunk, chunk), 0)
    col = lax.broadcasted_iota(jnp.int32, (chunk, chunk), 1)
    causal = row >= col
    bst = bst_ref[...]
    w_mix = [jnp.where(causal, ws_ref[g], 0.0).astype(BF16) for g in range(groups)]
    n = tm // sub

    def project(s):
        x = x_ref[s * sub:(s + 1) * sub, :]
        return x, _dot(_modulate(x, m, 0).astype(BF16), win_ref[...])

    def activate(z):
        z = 0.5 * z * (1.0 + lax.erf(z * (2.0 ** -0.5)))
        width = z.shape[1] // 2
        return z[:, :width], _layer_norm(z[:, width:], vg_ref[...], vb_ref[...]).astype(BF16)

    def mix(s, u, v):
        gd = u.shape[1] // groups
        for g in range(groups):
            cs = slice(g * gd, (g + 1) * gd)
            for c in range(sub // chunk):
                rs = slice(c * chunk, (c + 1) * chunk)
                mixed = _dot(w_mix[g], v[rs, cs]) + bst[:, g:g + 1]
                gated_ref[s * sub + c * chunk:s * sub + (c + 1) * chunk, cs] = (u[rs, cs] * mixed).astype(BF16)
        return _dot(gated_ref[s * sub:(s + 1) * sub, :], wout_ref[...])

    def finish(s, x, y):
        rows = slice(s * sub, (s + 1) * sub)
        x_new = _res_ln(x, y, m, 0, g_ref[...], b_ref[...], alpha)
        o_ref[rows, :] = x_new
        if n_exp:
            hp_ref[rows, :], meta_ref[:, rows] = _route_rows(x_new, m, wr_ref, br_ref, carry_ref, n_exp, region)

    nxt = project(0)
    pending = None
    for s in range(n):
        x, z = nxt
        if s + 1 < n:
            nxt = project(s + 1)
        u, v = activate(z)
        if pending is not None:
            finish(*pending)
        pending = (s, x, mix(s, u, v))
    finish(*pending)
    if n_exp:
        cnt_ref[...] = carry_ref[...]


def _gmlp_layer(x, mod, w_in, v_g, v_b, w_s, b_s, w_out, ln_g, ln_b, router=None, *, seq, alpha, tm=512,
                sub=256):
    t, d = x.shape
    groups, chunk, _ = w_s.shape
    width = w_out.shape[0]
    n_exp = router[0].shape[1] if router else 0
    kern = functools.partial(_gmlp_kernel, tm=tm, sub=sub, chunk=chunk, groups=groups, alpha=alpha, n_exp=n_exp,
                             region=t)
    row_map = lambda i: (i, 0)
    out_specs, out_shape = [pl.BlockSpec((tm, d), row_map)], [jax.ShapeDtypeStruct((t, d), F32)]
    scratch = [pltpu.VMEM((tm, width), BF16)]
    extra_specs, extra_args = [], []
    if router:
        r_specs, r_shapes = _router_out(t, d, tm, row_map)
        out_specs, out_shape = out_specs + r_specs, out_shape + r_shapes
        scratch.append(pltpu.VMEM((ROUTE_ROWS, LANES), F32))
        extra_specs = _router_specs(d, _const_spec)
        extra_args = list(_router_operands(*router))
    outs = pl.pallas_call(
        kern,
        grid=(t // tm,),
        in_specs=[
            pl.BlockSpec((tm, d), lambda i: (i, 0)),
            pl.BlockSpec((1, 6, d), lambda i: (i * tm // seq, 0, 0)),
            _const_spec((d, 2 * width)),
            _const_spec((1, width)),
            _const_spec((1, width)),
            _const_spec((groups, chunk, chunk)),
            _const_spec((chunk, groups)),
            _const_spec((width, d)),
            _const_spec((1, d)),
            _const_spec((1, d)),
        ] + extra_specs,
        out_specs=out_specs,
        out_shape=out_shape,
        scratch_shapes=scratch,
        compiler_params=_cparams(("arbitrary",)),
        name="gmlp_mixer",
    )(x, mod, w_in.astype(BF16), v_g.reshape(1, width), v_b.reshape(1, width), w_s, b_s.T,
      w_out.astype(BF16), ln_g.reshape(1, d), ln_b.reshape(1, d), *extra_args)
    return outs if router else outs[0]


def _swiglu_pipelined(hs, w1, w3, w2, emit):
    up = lambda h: (_dot(h, w1), _dot(h, w3))
    nxt = up(hs[0])
    for s in range(len(hs)):
        a, b = nxt
        if s + 1 < len(hs):
            nxt = up(hs[s + 1])
        t = (a / (1.0 + jnp.exp(-a)) * b).astype(BF16)
        emit(s, _dot(t, w2))


def _ffn_kernel(x_ref, mod_ref, w1_ref, w3_ref, w2_ref, g_ref, b_ref, o_ref, h_ref, *, alpha, sub):
    j = pl.program_id(1)
    last = pl.num_programs(1) - 1
    subs = [slice(s, s + sub) for s in range(0, x_ref.shape[0], sub)]
    weights = lambda: (w1_ref[...], w3_ref[...], w2_ref[...])

    @pl.when(j == 0)
    def _():
        hs = []
        for rows in subs:
            h = _modulate(x_ref[rows, :], mod_ref[0], 3).astype(BF16)
            h_ref[rows, :] = h
            hs.append(h)

        def emit(s, part):
            o_ref[subs[s], :] = part

        _swiglu_pipelined(hs, *weights(), emit)

    @pl.when(jnp.logical_and(j > 0, j < last))
    def _():
        def emit(s, part):
            o_ref[subs[s], :] += part

        _swiglu_pipelined([h_ref[rows, :] for rows in subs], *weights(), emit)

    @pl.when(j == last)
    def _():
        def emit(s, part):
            rows = subs[s]
            y = o_ref[rows, :] + part
            o_ref[rows, :] = _res_ln(x_ref[rows, :], y, mod_ref[0], 3, g_ref[...], b_ref[...], alpha)

        _swiglu_pipelined([h_ref[rows, :] for rows in subs], *weights(), emit)


def _ffn_layer(x, mod, w13, w2, ln_g, ln_b, *, seq, alpha, tm=1024, tf=512):
    t, d = x.shape
    f = w2.shape[0]
    nf = f // tf
    assert nf >= 2
    w13b = w13.astype(BF16)
    return pl.pallas_call(
        functools.partial(_ffn_kernel, alpha=alpha, sub=tm // 4),
        grid=(t // tm, nf),
        in_specs=[
            pl.BlockSpec((tm, d), lambda i, j: (i, 0)),
            pl.BlockSpec((1, 6, d), lambda i, j: (i * tm // seq, 0, 0)),
            pl.BlockSpec((d, tf), lambda i, j: (0, j)),
            pl.BlockSpec((d, tf), lambda i, j: (0, nf + j)),
            pl.BlockSpec((tf, d), lambda i, j: (j, 0)),
            _const_spec((1, d)),
            _const_spec((1, d)),
        ],
        out_specs=pl.BlockSpec((tm, d), lambda i, j: (i, 0)),
        out_shape=jax.ShapeDtypeStruct((t, d), F32),
        scratch_shapes=[pltpu.VMEM((tm, d), BF16)],
        compiler_params=_cparams(("parallel", "arbitrary")),
        name="swiglu_dense",
    )(x, mod, w13b, w13b, w2.astype(BF16), ln_g.reshape(1, d), ln_b.reshape(1, d))


MOE_GROUP = 2


def _moe_ffn_kernel(ge_ref, gb_ref, gr_ref, h_ref, w1_ref, w3_ref, w2_ref, o_ref, acc_ref, *, sub):
    g = pl.program_id(0)
    j = pl.program_id(1)
    last = pl.num_programs(1) - 1
    n_rows = gr_ref[g]
    nv = (n_rows + EXPERT_BLOCK - 1) // EXPERT_BLOCK

    def run(n_blocks, phase):
        subs = [slice(s, s + sub) for s in range(0, n_blocks * EXPERT_BLOCK, sub)]
        row = lax.broadcasted_iota(jnp.int32, (sub, 1), 0)

        def rows_in(rows):
            w = jnp.where(row + rows.start < n_rows, h_ref[rows, :], jnp.uint32(0))
            return _unpack_bf16_pair(w).astype(BF16)

        def emit(s, part):
            if phase == "first":
                acc_ref[subs[s], :] = part
            elif phase == "middle":
                acc_ref[subs[s], :] += part
            else:
                o_ref[subs[s], :] = _pack_bf16_pair(acc_ref[subs[s], :] + part)

        _swiglu_pipelined([rows_in(rows) for rows in subs], w1_ref[0, 0].astype(BF16), w3_ref[0, 0].astype(BF16),
                          w2_ref[0, 0].astype(BF16), emit)

    for n_blocks in range(1, MOE_GROUP + 1):
        @pl.when(jnp.logical_and(nv == n_blocks, j == 0))
        def _():
            run(n_blocks, "first")

        @pl.when(jnp.logical_and(nv == n_blocks, jnp.logical_and(j > 0, j < last)))
        def _():
            run(n_blocks, "middle")

        @pl.when(jnp.logical_and(nv == n_blocks, j == last))
        def _():
            run(n_blocks, "last")

    for blk in range(MOE_GROUP):
        @pl.when(jnp.logical_and(blk >= nv, j == last))
        def _():
            o_ref[blk * EXPERT_BLOCK:(blk + 1) * EXPERT_BLOCK, :] = jnp.zeros((EXPERT_BLOCK, o_ref.shape[1]),
                                                                              jnp.uint32)


def _moe_ffn(h_slots, group_expert, group_block, group_rows, w13, w2, layer, *, tf=512):
    half = h_slots.shape[1]
    d = 2 * half
    f = w2.shape[2]
    nf = f // tf
    assert nf >= 2
    tm = MOE_GROUP * EXPERT_BLOCK

    def jj(g, j, gr):
        return jnp.where(gr[g] > 0, j, nf - 1)

    grid_spec = pltpu.PrefetchScalarGridSpec(
        num_scalar_prefetch=3,
        grid=(group_expert.shape[0], nf),
        in_specs=[
            pl.BlockSpec((tm, half), lambda g, j, ge, gb, gr: (gb[g], 0)),
            pl.BlockSpec((1, 1, d, tf), lambda g, j, ge, gb, gr: (layer, ge[g], 0, jj(g, j, gr))),
            pl.BlockSpec((1, 1, d, tf), lambda g, j, ge, gb, gr: (layer, ge[g], 0, nf + jj(g, j, gr))),
            pl.BlockSpec((1, 1, tf, d), lambda g, j, ge, gb, gr: (layer, ge[g], jj(g, j, gr), 0)),
        ],
        out_specs=pl.BlockSpec((tm, half), lambda g, j, ge, gb, gr: (gb[g], 0)),
        scratch_shapes=[pltpu.VMEM((tm, d), F32)],
    )
    return pl.pallas_call(
        functools.partial(_moe_ffn_kernel, sub=EXPERT_BLOCK // 2),
        grid_spec=grid_spec,
        out_shape=jax.ShapeDtypeStruct(h_slots.shape, jnp.uint32),
        compiler_params=_cparams(("arbitrary", "arbitrary")),
        name="swiglu_experts",
    )(group_expert, group_block, group_rows, h_slots, w13, w13, w2)


ROUTE_ROWS = 16


def _route_rows(x_rows, m, wrt_ref, brt_ref, carry_ref, n_exp, region):
    tm = x_rows.shape[0]
    h = _modulate(x_rows, m, 3)
    h_hi, h_mid, _ = _split3(h)
    w = wrt_ref[...]
    w_hi = w.astype(BF16)
    w_lo = (w - w_hi.astype(F32)).astype(BF16)
    nt = (((1,), (1,)), ((), ()))
    logits = (lax.dot_general(w_hi, h_hi, nt, preferred_element_type=F32)
              + lax.dot_general(w_lo, h_hi, nt, preferred_element_type=F32)
              + lax.dot_general(w_hi, h_mid, nt, preferred_element_type=F32) + brt_ref[...])
    erow = lax.broadcasted_iota(jnp.int32, (ROUTE_ROWS, tm), 0)
    neg = jnp.float32(-jnp.inf)
    logits = jnp.where(erow < n_exp, logits, neg)
    v0 = jnp.max(logits, axis=0, keepdims=True)
    e0 = jnp.min(jnp.where(logits == v0, erow, ROUTE_ROWS), axis=0, keepdims=True)
    rest = jnp.where(erow == e0, neg, logits)
    v1 = jnp.max(rest, axis=0, keepdims=True)
    e1 = jnp.min(jnp.where(rest == v1, erow, ROUTE_ROWS), axis=0, keepdims=True)
    p = jnp.exp(v1 - v0)
    g0 = 1.0 / (1.0 + p)
    g1 = p / (1.0 + p)
    oh0 = erow == e0
    oh1 = erow == e1
    onehot = jnp.where(jnp.logical_or(oh0, oh1), 1.0, 0.0).astype(BF16)
    r = lax.broadcasted_iota(jnp.int32, (tm, tm), 0)
    c = lax.broadcasted_iota(jnp.int32, (tm, tm), 1)
    earlier = jnp.where(r < c, 1.0, 0.0).astype(BF16)
    carry = carry_ref[:, 0:1]
    before = _dot(onehot, earlier) + carry
    rank0 = jnp.sum(jnp.where(oh0, before, 0.0), axis=0, keepdims=True)
    rank1 = jnp.sum(jnp.where(oh1, before, 0.0), axis=0, keepdims=True)
    total = carry + jnp.sum(onehot.astype(F32), axis=1, keepdims=True)
    carry_ref[...] = jnp.broadcast_to(total, carry_ref.shape)
    mrow = lax.broadcasted_iota(jnp.int32, (SUBLANES, tm), 0)
    e0f = e0.astype(F32)
    e1f = e1.astype(F32)
    meta = jnp.where(mrow == 0, e0f, 0.0)
    meta = jnp.where(mrow == 1, e1f, meta)
    meta = jnp.where(mrow == 2, g0, meta)
    meta = jnp.where(mrow == 3, g1, meta)
    meta = jnp.where(mrow == 4, e0f * region + rank0, meta)
    meta = jnp.where(mrow == 5, e1f * region + rank1, meta)
    return _pack_bf16_pair(h), meta


def _router_operands(w_router, b_router):
    d, n_exp = w_router.shape
    wrt = jnp.zeros((ROUTE_ROWS, d), F32).at[:n_exp].set(w_router.T)
    brt = jnp.zeros((ROUTE_ROWS, 1), F32).at[:n_exp, 0].set(b_router)
    return wrt, brt


def _router_specs(d, const):
    return [const((ROUTE_ROWS, d)), const((ROUTE_ROWS, 1))]


def _router_out(t, d, tm, row_map):
    specs = [pl.BlockSpec((tm, d // 2), row_map),
             pl.BlockSpec((SUBLANES, tm), lambda *a: (0, row_map(*a)[0])),
             pl.BlockSpec((ROUTE_ROWS, LANES), lambda *_: (0, 0))]
    shapes = [jax.ShapeDtypeStruct((t, d // 2), jnp.uint32), jax.ShapeDtypeStruct((SUBLANES, t), F32),
              jax.ShapeDtypeStruct((ROUTE_ROWS, LANES), F32)]
    return specs, shapes


def _combine_kernel(x_ref, ya_ref, yb_ref, meta_ref, mod_ref, g_ref, b_ref, o_ref, *, alpha):
    record = meta_ref[...]
    cols = jnp.concatenate([record, jnp.zeros((LANES - record.shape[0], record.shape[1]), F32)], axis=0).T
    y = cols[:, 2:3] * _unpack_bf16_pair(ya_ref[...]) + cols[:, 3:4] * _unpack_bf16_pair(yb_ref[...])
    o_ref[...] = _res_ln(x_ref[...], y, mod_ref[0], 3, g_ref[...], b_ref[...], alpha)


def _combine(x, ya, yb, meta, mod, ln_g, ln_b, *, seq, alpha, tm=512):
    t, d = x.shape
    row = pl.BlockSpec((tm, d), lambda i: (i, 0))
    packed = pl.BlockSpec((tm, d // 2), lambda i: (i, 0))
    return pl.pallas_call(
        functools.partial(_combine_kernel, alpha=alpha),
        grid=(t // tm,),
        in_specs=[row, packed, packed,
                  pl.BlockSpec((SUBLANES, tm), lambda i: (0, i)),
                  pl.BlockSpec((1, 6, d), lambda i: (i * tm // seq, 0, 0)),
                  _const_spec((1, d)), _const_spec((1, d))],
        out_specs=row,
        out_shape=jax.ShapeDtypeStruct((t, d), F32),
        compiler_params=_cparams(("parallel",)),
        name="moe_combine",
    )(x, ya, yb, meta, mod, ln_g.reshape(1, d), ln_b.reshape(1, d))


def _sc_mesh():
    return plsc.VectorSubcoreMesh(core_axis_name="c", subcore_axis_name="s", num_cores=SC_CORES,
                                  num_subcores=SC_SUBCORES)


def _sc_worker_share(n):
    workers = SC_CORES * SC_SUBCORES
    per_worker = n // workers
    steps = per_worker // SC_ROWS
    assert steps * SC_ROWS * workers == n
    return per_worker, steps


def _sc_gather_rows(table, idx_a, idx_b):
    n = idx_a.shape[0]
    width = table.shape[1]
    per_worker, steps = _sc_worker_share(n)

    def body(table_hbm, ia_hbm, ib_hbm, oa_hbm, ob_hbm, ia_v, ib_v, rows_v, sem):
        wid = lax.axis_index("s") * SC_CORES + lax.axis_index("c")
        base = pl.multiple_of(wid * per_worker, SC_ROWS)
        pltpu.sync_copy(ia_hbm.at[pl.ds(base, per_worker)], ia_v)
        pltpu.sync_copy(ib_hbm.at[pl.ds(base, per_worker)], ib_v)

        @pl.loop(0, steps)
        def _(j):
            lo = pl.multiple_of(j * SC_ROWS, SC_ROWS)
            for idx_v, out_hbm in ((ia_v, oa_hbm), (ib_v, ob_hbm)):
                pltpu.async_copy(table_hbm.at[idx_v.at[pl.ds(lo, SC_ROWS)]], rows_v, sem).wait()
                pltpu.sync_copy(rows_v, out_hbm.at[pl.ds(base + lo, SC_ROWS)])

    out = jax.ShapeDtypeStruct((n, width), table.dtype)
    return pl.kernel(
        body,
        out_type=(out, out),
        mesh=_sc_mesh(),
        scratch_types=[
            pltpu.VMEM((per_worker,), jnp.int32),
            pltpu.VMEM((per_worker,), jnp.int32),
            pltpu.VMEM((SC_ROWS, width), table.dtype),
            pltpu.SemaphoreType.DMA,
        ],
        name="sc_row_gather",
    )(table, idx_a, idx_b)


def _sc_scatter_rows(rows, dest_a, dest_b, n_out):
    n, width = rows.shape
    per_worker, steps = _sc_worker_share(n)

    def body(rows_hbm, da_hbm, db_hbm, out_hbm, ia_v, ib_v, rows_v, sem):
        wid = lax.axis_index("s") * SC_CORES + lax.axis_index("c")

        @pl.loop(0, steps)
        def _(j):
            off = pl.multiple_of(wid * per_worker + j * SC_ROWS, SC_ROWS)
            pltpu.sync_copy(da_hbm.at[pl.ds(off, SC_ROWS)], ia_v)
            pltpu.sync_copy(db_hbm.at[pl.ds(off, SC_ROWS)], ib_v)
            pltpu.sync_copy(rows_hbm.at[pl.ds(off, SC_ROWS)], rows_v)
            pltpu.async_copy(rows_v, out_hbm.at[ia_v], sem).wait()
            pltpu.async_copy(rows_v, out_hbm.at[ib_v], sem).wait()

    return pl.kernel(
        body,
        out_type=jax.ShapeDtypeStruct((n_out, width), rows.dtype),
        mesh=_sc_mesh(),
        scratch_types=[
            pltpu.VMEM((SC_ROWS,), jnp.int32),
            pltpu.VMEM((SC_ROWS,), jnp.int32),
            pltpu.VMEM((SC_ROWS, width), rows.dtype),
            pltpu.SemaphoreType.DMA,
        ],
        name="sc_row_scatter",
    )(rows, dest_a, dest_b)


def _moe_layer(routed, mod, w13, w2, layer, ln_g, ln_b, *, seq, alpha):
    x, h, meta, cnt = routed
    t, d = x.shape
    n_exp = w13.shape[1]
    dest0 = meta[4].astype(jnp.int32)
    dest1 = meta[5].astype(jnp.int32)
    counts = cnt[:n_exp, 0].astype(jnp.int32)
    group_rows = MOE_GROUP * EXPERT_BLOCK
    assert t % group_rows == 0
    region_groups = t // group_rows
    groups_per_expert = (counts + group_rows - 1) // group_rows
    group_end = jnp.cumsum(groups_per_expert)
    g = jnp.arange(t * TOP_K // group_rows + n_exp, dtype=jnp.int32)
    expert = jnp.minimum(jnp.sum(group_end[None, :] <= g[:, None], axis=1), n_exp - 1).astype(jnp.int32)
    blk_in_expert = g - (group_end - groups_per_expert)[expert]
    used = g < group_end[-1]
    spare_block = n_exp * region_groups
    group_block = jnp.where(used, expert * region_groups + blk_in_expert, spare_block).astype(jnp.int32)
    rows_in_group = jnp.where(used, jnp.clip(counts[expert] - blk_in_expert * group_rows, 0, group_rows),
                              0).astype(jnp.int32)
    h_slots = _sc_scatter_rows(h, dest0, dest1, (spare_block + 1) * group_rows)
    y_slots = _moe_ffn(h_slots, expert, group_block, rows_in_group, w13, w2, layer)
    ya, yb = _sc_gather_rows(y_slots, dest0, dest1)
    return _combine(x, ya, yb, meta, mod, ln_g, ln_b, seq=seq, alpha=alpha)


HEAD_PAD = LANES
GATE_PARTS = 3
GATE_ROWS = 16
VAL_PAD = 16
LOG2E = 1.4426950408889634


def _head_lanes(h, hd):
    k0 = (h % (HEAD_PAD // hd)) * hd
    return k0, (hd if k0 == 0 else 0)


def _fox_proj_kernel(x_ref, mod_ref, wf_ref, bf_ref, wq_ref, wk_ref, wv_ref, selk_ref, onek_ref, selq_ref,
                     oneq_ref, onev_ref, k_ref, q_ref, v_ref, carry_ref, *, q_scale, heads, hd, nb):
    tm = x_ref.shape[0]

    @pl.when(pl.program_id(0) % nb == 0)
    def _():
        carry_ref[...] = jnp.zeros_like(carry_ref)

    h_hi, h_mid, _ = _split3(_modulate(x_ref[...], mod_ref[0], 0))

    f = _dot_split(h_hi, h_mid, wf_ref[...]) + bf_ref[...]
    lf = jnp.minimum(f, 0.0) - jnp.log(1.0 + jnp.exp(-jnp.abs(f)))
    r = lax.broadcasted_iota(jnp.int32, (tm, tm), 0)
    c = lax.broadcasted_iota(jnp.int32, (tm, tm), 1)
    tri = jnp.where(r >= c, 1.0, 0.0).astype(BF16)
    sums = _dot(tri, jnp.concatenate(_split3(lf), axis=1))
    cs = carry_ref[0:1, :] + sum(sums[:, n * LANES:(n + 1) * LANES] for n in range(GATE_PARTS))
    carry_ref[...] = jnp.broadcast_to(cs[tm - 1:tm, :], carry_ref.shape)
    cs = cs * LOG2E

    gate_k = onek_ref[...] + _dot(jnp.concatenate(_split3(cs), axis=1), selk_ref[...])
    kc = _dot(h_hi, wk_ref[...])
    lane = lax.broadcasted_iota(jnp.int32, (tm, HEAD_PAD), 1)
    for h in range(heads):
        k0, _ = _head_lanes(h, hd)
        src = (h * hd // HEAD_PAD) * HEAD_PAD
        is_k = jnp.logical_and(lane >= k0, lane < k0 + hd)
        hs = slice(h * HEAD_PAD, (h + 1) * HEAD_PAD)
        k_ref[:, hs] = jnp.where(is_k, kc[:, src:src + HEAD_PAD], gate_k[:, hs]).astype(BF16)

    nt = (((1,), (1,)), ((), ()))
    qc = lax.dot_general(wq_ref[...], h_hi, nt, preferred_element_type=F32) * (q_scale * LOG2E)
    gate_q = oneq_ref[...] + _dot(selq_ref[...], jnp.concatenate(_split3(cs.T), axis=0))
    fill = HEAD_PAD - hd - GATE_ROWS
    for h in range(heads):
        k0, g0 = _head_lanes(h, hd)
        base = h * HEAD_PAD
        q_ref[0, base + k0:base + k0 + hd, :] = qc[h * hd:(h + 1) * hd, :].astype(BF16)
        q_ref[0, base + g0:base + g0 + GATE_ROWS, :] = gate_q[h * GATE_ROWS:(h + 1) * GATE_ROWS, :].astype(BF16)
        q_ref[0, base + g0 + GATE_ROWS:base + g0 + GATE_ROWS + fill, :] = jnp.zeros((fill, tm), BF16)

    v = lax.dot_general(wv_ref[...], h_hi, nt, preferred_element_type=F32) + onev_ref[...]
    v_ref[0] = v.astype(BF16)


def _fox_proj(x, mod, w_in, b_f, *, bsz, seq, heads, tm=512):
    t, d = x.shape
    hd = d // heads
    assert HEAD_PAD == 2 * hd and GATE_ROWS >= 2 * GATE_PARTS
    nb = seq // tm
    wide = heads * HEAD_PAD
    wq_t = w_in[:, :d].T.astype(BF16)
    wk = w_in[:, d:2 * d].astype(BF16)
    vrows = heads * (hd + VAL_PAD)
    wv_t = jnp.pad(w_in[:, 2 * d:3 * d].reshape(d, heads, hd),
                   ((0, 0), (0, 0), (0, VAL_PAD))).reshape(d, vrows).T.astype(BF16)
    wf = jnp.zeros((d, LANES), F32).at[:, :heads].set(w_in[:, 3 * d:])
    bf = jnp.zeros((1, LANES), F32).at[0, :heads].set(b_f)
    sel_k = np.zeros((GATE_PARTS, LANES, wide), np.float32)
    one_k = np.zeros((1, wide), np.float32)
    sel_q = np.zeros((GATE_PARTS, heads * GATE_ROWS, LANES), np.float32)
    one_q = np.zeros((heads * GATE_ROWS, 1), np.float32)
    one_v = np.zeros((vrows, 1), np.float32)
    for hh in range(heads):
        _, g0 = _head_lanes(hh, hd)
        one_v[hh * (hd + VAL_PAD) + hd, 0] = 1.0
        for n in range(GATE_PARTS):
            sel_k[n, hh, hh * HEAD_PAD + g0 + n] = -1.0
            one_q[hh * GATE_ROWS + n, 0] = 1.0
            one_k[0, hh * HEAD_PAD + g0 + GATE_PARTS + n] = 1.0
            sel_q[n, hh * GATE_ROWS + GATE_PARTS + n, hh] = 1.0
    return pl.pallas_call(
        functools.partial(_fox_proj_kernel, q_scale=hd ** -0.5, heads=heads, hd=hd, nb=nb),
        grid=(t // tm,),
        in_specs=[
            pl.BlockSpec((tm, d), lambda i: (i, 0)),
            pl.BlockSpec((1, 6, d), lambda i: (i * tm // seq, 0, 0)),
            _const_spec((d, LANES)),
            _const_spec((1, LANES)),
            _const_spec((d, d)),
            _const_spec((d, d)),
            _const_spec((vrows, d)),
            _const_spec((GATE_PARTS * LANES, wide)),
            _const_spec((1, wide)),
            _const_spec((heads * GATE_ROWS, GATE_PARTS * LANES)),
            _const_spec((heads * GATE_ROWS, 1)),
            _const_spec((vrows, 1)),
        ],
        out_specs=[
            pl.BlockSpec((tm, wide), lambda i: (i, 0)),
            pl.BlockSpec((1, wide, tm), lambda i: (i // nb, 0, i % nb)),
            pl.BlockSpec((1, vrows, tm), lambda i: (i // nb, 0, i % nb)),
        ],
        out_shape=[
            jax.ShapeDtypeStruct((t, wide), BF16),
            jax.ShapeDtypeStruct((bsz, wide, seq), BF16),
            jax.ShapeDtypeStruct((bsz, vrows, seq), BF16),
        ],
        scratch_shapes=[pltpu.VMEM((SUBLANES, LANES), F32)],
        compiler_params=_cparams(("arbitrary",)),
        name="fox_proj",
    )(x, mod, wf, bf, wq_t, wk, wv_t, jnp.asarray(sel_k.reshape(GATE_PARTS * LANES, wide), BF16), jnp.asarray(one_k),
      jnp.asarray(np.concatenate(list(sel_q), axis=1), BF16), jnp.asarray(one_q), jnp.asarray(one_v))


def _flash_kernel(qi_ref, kj_ref, k_ref, q_ref, v_ref, x_ref, mod_ref, wout_ref, g_ref, b_ref, *rest,
                  heads, hd, blk, alpha, n_exp, region):
    p_idx = pl.program_id(1)
    if n_exp:
        wr_ref, br_ref, o_ref, hp_ref, meta_ref, cnt_ref, m_ref, acc_ref, s_ref, carry_ref = rest

        @pl.when(jnp.logical_and(pl.program_id(0) == 0, p_idx == 0))
        def _():
            carry_ref[...] = jnp.zeros_like(carry_ref)
    else:
        o_ref, m_ref, acc_ref, s_ref = rest
    qi = qi_ref[p_idx]
    kj = kj_ref[p_idx]
    vp = hd + VAL_PAD

    @pl.when(kj == 0)
    def _():
        m_ref[...] = jnp.full(m_ref.shape, -jnp.inf, F32)
        acc_ref[...] = jnp.zeros_like(acc_ref)

    def step(diagonal):
        half = blk // 2
        tiles = [(slice(0, half), slice(0, half)), (slice(half, blk), slice(0, blk))] if diagonal else \
                [(slice(0, blk), slice(0, blk))]
        if diagonal:
            causal = lambda rows, first_q: (lax.broadcasted_iota(jnp.int32, (rows, half), 0)
                                            <= lax.broadcasted_iota(jnp.int32, (rows, half), 1) + first_q)
            keep = [causal(half, 0), causal(blk, half)]

        def scores(h):
            hs = slice(h * HEAD_PAD, (h + 1) * HEAD_PAD)
            tops = []
            for n, (qs, ks) in enumerate(tiles):
                s = _dot(k_ref[ks, hs], q_ref[0, hs, qs])
                if diagonal:
                    s = jnp.where(keep[n], s, -jnp.inf)
                s_ref[h % depth, ks, qs] = s
                tops.append(jnp.max(s, axis=0, keepdims=True))
            return tops

        depth = s_ref.shape[0]
        ahead = [scores(h) for h in range(depth - 1)]
        for h in range(heads):
            if h + depth - 1 < heads:
                ahead.append(scores(h + depth - 1))
            rs = slice(h * vp, (h + 1) * vp)
            for (qs, ks), m_cur in zip(tiles, ahead.pop(0)):
                m_prev = m_ref[h:h + 1, qs]
                m_new = jnp.maximum(m_prev, m_cur)
                m_ref[h:h + 1, qs] = m_new
                p = jnp.exp2(s_ref[h % depth, ks, qs] - m_new).astype(BF16)
                acc_ref[rs, qs] = acc_ref[rs, qs] * jnp.exp2(m_prev - m_new) + _dot(v_ref[0, rs, ks], p)

    @pl.when(kj < qi)
    def _():
        step(False)

    @pl.when(kj == qi)
    def _():
        step(True)
        per = LANES // hd
        cols = []
        for g in range(heads // per):
            parts = []
            for n in range(per):
                r0 = (g * per + n) * vp
                parts.append(acc_ref[r0:r0 + hd, :] / acc_ref[r0 + hd:r0 + hd + 1, :])
            cols.append(jnp.concatenate(parts, axis=0).T.astype(BF16))
        y = _dot(jnp.concatenate(cols, axis=1), wout_ref[...])
        x_new = _res_ln(x_ref[...], y, mod_ref[0], 0, g_ref[...], b_ref[...], alpha)
        o_ref[...] = x_new
        if n_exp:
            hp_ref[...], meta_ref[...] = _route_rows(x_new, mod_ref[0], wr_ref, br_ref, carry_ref, n_exp, region)
            cnt_ref[...] = carry_ref[...]


def _flash(k_aug, q_aug_t, v_t, x, mod, w_out, ln_g, ln_b, router=None, *, bsz, seq, heads, alpha, blk=512):
    t, wide = k_aug.shape
    vrows = v_t.shape[1]
    hd = vrows // heads - VAL_PAD
    d = heads * hd
    nq = seq // blk
    pairs = [(i, j) for i in range(nq) for j in range(i + 1)]
    qi = jnp.asarray(np.array([p[0] for p in pairs], np.int32))
    kj = jnp.asarray(np.array([p[1] for p in pairs], np.int32))
    n_exp = router[0].shape[1] if router else 0
    row_map = lambda b, p, qi, kj: (b * nq + qi[p], 0)
    const = lambda shape: pl.BlockSpec(shape, lambda b, p, qi, kj: (0, 0), pipeline_mode=pl.Buffered(1))
    out_specs, out_shape = [pl.BlockSpec((blk, d), row_map)], [jax.ShapeDtypeStruct((t, d), F32)]
    scratch = [pltpu.VMEM((heads, blk), F32), pltpu.VMEM((vrows, blk), F32), pltpu.VMEM((3, blk, blk), F32)]
    extra_specs, extra_args = [], []
    if router:
        r_specs, r_shapes = _router_out(t, d, blk, row_map)
        out_specs, out_shape = out_specs + r_specs, out_shape + r_shapes
        scratch.append(pltpu.VMEM((ROUTE_ROWS, LANES), F32))
        extra_specs = _router_specs(d, const)
        extra_args = list(_router_operands(*router))
    grid_spec = pltpu.PrefetchScalarGridSpec(
        num_scalar_prefetch=2,
        grid=(bsz, len(pairs)),
        in_specs=[
            pl.BlockSpec((blk, wide), lambda b, p, qi, kj: (b * nq + kj[p], 0)),
            pl.BlockSpec((1, wide, blk), lambda b, p, qi, kj: (b, 0, qi[p])),
            pl.BlockSpec((1, vrows, blk), lambda b, p, qi, kj: (b, 0, kj[p])),
            pl.BlockSpec((blk, d), row_map),
            pl.BlockSpec((1, 6, d), lambda b, p, qi, kj: (b, 0, 0)),
            const((d, d)),
            const((1, d)),
            const((1, d)),
        ] + extra_specs,
        out_specs=out_specs,
        scratch_shapes=scratch,
    )
    outs = pl.pallas_call(
        functools.partial(_flash_kernel, heads=heads, hd=hd, blk=blk, alpha=alpha, n_exp=n_exp, region=t),
        grid_spec=grid_spec,
        out_shape=out_shape,
        compiler_params=_cparams(("arbitrary", "arbitrary")),
        name="fox_attention",
    )(qi, kj, k_aug, q_aug_t, v_t, x, mod, w_out.astype(BF16), ln_g.reshape(1, d), ln_b.reshape(1, d), *extra_args)
    return outs if router else outs[0]


def _fox_layer(x, mod, w_in, b_f, w_out, ln_g, ln_b, router=None, *, bsz, seq, alpha):
    heads = b_f.shape[0]
    k_aug, q_aug_t, v_t = _fox_proj(x, mod, w_in, b_f, bsz=bsz, seq=seq, heads=heads)
    return _flash(k_aug, q_aug_t, v_t, x, mod, w_out, ln_g, ln_b, router, bsz=bsz, seq=seq, heads=heads,
                  alpha=alpha)


def _pool_kernel(x_ref, xh_ref, mod_ref, win_ref, wg_ref, ls_ref, wout_ref, g_ref, b_ref, o_ref, pooled_ref,
                 *, tm, sub, seq, alpha):
    pos0 = (pl.program_id(0) * tm) % seq
    m = mod_ref[0]
    n = tm // sub
    halo_ok = jnp.where(pos0 > 0, jnp.float32(1.0), jnp.float32(0.0))
    rows = lax.broadcasted_iota(jnp.int32, (sub + POOL_HALO, 1), 0)

    def project(s):
        x = x_ref[s * sub:(s + 1) * sub, :]
        halo = xh_ref[...] if s == 0 else x_ref[s * sub - POOL_HALO:s * sub, :]
        ze = _dot(_modulate(jnp.concatenate([halo, x], axis=0), m, 0).astype(BF16), win_ref[...])
        if s == 0:
            ze = ze * jnp.where(rows < POOL_HALO, halo_ok, jnp.float32(1.0))
        return x, ze

    def pool(s, ze):
        pos = (pos0 + s * sub + lax.broadcasted_iota(jnp.int32, (sub, 1), 0) + 1).astype(F32)
        gd = ze.shape[1] // len(POOL_WINDOWS)
        out_rows = slice(s * sub, (s + 1) * sub)
        for g, win in enumerate(POOL_WINDOWS):
            cs = slice(g * gd, (g + 1) * gd)
            zg = ze[:, cs]
            acc = zg
            span = 1
            while span < win:
                acc = acc + pltpu.roll(acc, span, 0)
                span *= 2
            count = jnp.minimum(pos, jnp.float32(win))
            pooled = acc[POOL_HALO:, :] / count - zg[POOL_HALO:, :]
            pooled_ref[out_rows, cs] = (_dot(pooled.astype(BF16), wg_ref[g]) * ls_ref[:, cs]).astype(BF16)
        return _dot(pooled_ref[out_rows, :], wout_ref[...])

    nxt = project(0)
    for s in range(n):
        x, ze = nxt
        if s + 1 < n:
            nxt = project(s + 1)
        y = pool(s, ze)
        o_ref[s * sub:(s + 1) * sub, :] = _res_ln(x, y, m, 0, g_ref[...], b_ref[...], alpha)


def _pool_layer(x, mod, w_in, w_grp, scale, w_out, ln_g, ln_b, *, seq, alpha, tm=512, sub=256):
    t, d = x.shape
    ng, gd, _ = w_grp.shape
    hb = tm // POOL_HALO
    return pl.pallas_call(
        functools.partial(_pool_kernel, tm=tm, sub=sub, seq=seq, alpha=alpha),
        grid=(t // tm,),
        in_specs=[
            pl.BlockSpec((tm, d), lambda i: (i, 0)),
            pl.BlockSpec((POOL_HALO, d), lambda i: (jnp.maximum(i * hb - 1, 0), 0)),
            pl.BlockSpec((1, 6, d), lambda i: (i * tm // seq, 0, 0)),
            _const_spec((d, d)),
            _const_spec((ng, gd, gd)),
            _const_spec((1, d)),
            _const_spec((d, d)),
            _const_spec((1, d)),
            _const_spec((1, d)),
        ],
        out_specs=pl.BlockSpec((tm, d), lambda i: (i, 0)),
        out_shape=jax.ShapeDtypeStruct((t, d), F32),
        scratch_shapes=[pltpu.VMEM((tm, d), BF16)],
        compiler_params=_cparams(("parallel",)),
        name="pool_mixer",
    )(x, x, mod, w_in.astype(BF16), w_grp.astype(BF16), scale.reshape(1, d), w_out.astype(BF16),
      ln_g.reshape(1, d), ln_b.reshape(1, d))


def kernel(x, c, mod_w, mod_b, ln_g, ln_b, gm_w_in, gm_ln_g, gm_ln_b, gm_w_s, gm_b_s, gm_w_out, fox_w_in,
           fox_b_f, fox_w_out, pool_w_in, pool_w_grp, pool_scale, pool_w_out, ffn_w13, ffn_w2, moe_w_router,
           moe_b_router, moe_w13, moe_w2):
    bsz, seq, d = x.shape
    depth = mod_w.shape[0]
    alpha = (2 * depth) ** 0.25
    n_mixers = 3
    mod_all = _modulation(c, mod_w, mod_b)
    xt = x.reshape(bsz * seq, d)
    for i in range(depth):
        mod = mod_all[i]
        kind, j = i % n_mixers, i // n_mixers
        router = (moe_w_router[i // 2], moe_b_router[i // 2]) if i % 2 else None
        if kind == 0:
            xt = _gmlp_layer(xt, mod, gm_w_in[j], gm_ln_g[j], gm_ln_b[j], gm_w_s[j], gm_b_s[j], gm_w_out[j],
                             ln_g[i, 0], ln_b[i, 0], router, seq=seq, alpha=alpha)
        elif kind == 1:
            xt = _fox_layer(xt, mod, fox_w_in[j], fox_b_f[j], fox_w_out[j], ln_g[i, 0], ln_b[i, 0], router,
                            bsz=bsz, seq=seq, alpha=alpha)
        else:
            assert router is None, "the pooling mixer has no routing epilogue"
            xt = _pool_layer(xt, mod, pool_w_in[j], pool_w_grp[j], pool_scale[j], pool_w_out[j],
                             ln_g[i, 0], ln_b[i, 0], seq=seq, alpha=alpha)
        if router is None:
            xt = _ffn_layer(xt, mod, ffn_w13[i // 2], ffn_w2[i // 2], ln_g[i, 1], ln_b[i, 1],
                            seq=seq, alpha=alpha)
        else:
            xt = _moe_layer(xt, mod, moe_w13, moe_w2, i // 2, ln_g[i, 1], ln_b[i, 1], seq=seq, alpha=alpha)
    return xt.reshape(bsz, seq, d)
```

```python
import functools

import numpy as np
import jax
import jax.numpy as jnp
from jax import lax
from jax.experimental import pallas as pl
from jax.experimental.pallas import tpu as pltpu
from jax.experimental.pallas import tpu_sc as plsc

F32 = jnp.float32
BF16 = jnp.bfloat16

POOL_WINDOWS = (2, 4, 8, 16)
TOP_K = 2
EXPERT_BLOCK = 512
LN_EPS = 1e-5
LANES = 128
SUBLANES = 8
POOL_HALO = 16
VMEM_LIMIT = 56 * 1024 * 1024
SC_CORES = 2
SC_SUBCORES = 16
SC_ROWS = 128


def _cparams(sem):
    return pltpu.CompilerParams(dimension_semantics=sem, vmem_limit_bytes=VMEM_LIMIT)


def _const_spec(shape):
    nd = len(shape)
    return pl.BlockSpec(shape, lambda *_: (0,) * nd, pipeline_mode=pl.Buffered(1))


def _layer_norm(r, g, b):
    mu = jnp.mean(r, axis=-1, keepdims=True)
    xc = r - mu
    var = jnp.mean(xc * xc, axis=-1, keepdims=True)
    return xc * lax.rsqrt(var + LN_EPS) * g + b


def _modulate(x, m, off):
    return x * (1.0 + m[off + 1:off + 2]) + m[off:off + 1]


def _res_ln(x, y, m, off, g, b, alpha):
    return _layer_norm(alpha * x + (1.0 + m[off + 2:off + 3]) * y, g, b)


def _split3(a):
    hi = a.astype(BF16)
    r1 = a - hi.astype(F32)
    mid = r1.astype(BF16)
    lo = (r1 - mid.astype(F32)).astype(BF16)
    return hi, mid, lo


def _dot(a, b):
    return jnp.dot(a, b, preferred_element_type=F32)


def _dot_split(a_hi, a_mid, w):
    n = w.shape[1]
    w_hi = w.astype(BF16)
    w_lo = (w - w_hi.astype(F32)).astype(BF16)
    both = _dot(a_hi, jnp.concatenate([w_hi, w_lo], axis=1))
    return both[:, :n] + both[:, n:] + _dot(a_mid, w_hi)


def _pack_bf16_pair(x):
    n = x.shape[1] // 2
    hi = lax.bitcast_convert_type(x[:, :n].astype(BF16).astype(F32), jnp.uint32)
    lo = lax.bitcast_convert_type(x[:, n:].astype(BF16).astype(F32), jnp.uint32)
    return hi | (lo >> 16)


def _unpack_bf16_pair(w):
    hi = lax.bitcast_convert_type(w & jnp.uint32(0xFFFF0000), F32)
    lo = lax.bitcast_convert_type(w << 16, F32)
    return jnp.concatenate([hi, lo], axis=1)


def _mod_kernel(c_ref, w_ref, b_ref, o_ref):
    c = c_ref[...]
    s = c / (1.0 + jnp.exp(-c))
    s_hi, s_mid, _ = _split3(s)
    w = w_ref[0]
    w_hi = w.astype(BF16)
    w_lo = (w - w_hi.astype(F32)).astype(BF16)
    acc = _dot(s_hi, w_hi) + _dot(s_mid, w_hi) + _dot(s_hi, w_lo)
    o_ref[0] = acc + b_ref[0]


def _modulation(c, mod_w, mod_b):
    depth, d, n = mod_w.shape
    bsz = c.shape[0]
    rows = SUBLANES
    tn = n // 2
    c_pad = jnp.zeros((rows, d), F32).at[:bsz].set(c)
    out = pl.pallas_call(
        _mod_kernel,
        grid=(depth, n // tn),
        in_specs=[
            pl.BlockSpec((rows, d), lambda l, j: (0, 0)),
            pl.BlockSpec((1, d, tn), lambda l, j: (l, 0, j)),
            pl.BlockSpec((1, 1, tn), lambda l, j: (l, 0, j)),
        ],
        out_specs=pl.BlockSpec((1, rows, tn), lambda l, j: (l, 0, j)),
        out_shape=jax.ShapeDtypeStruct((depth, rows, n), F32),
        compiler_params=_cparams(("parallel", "parallel")),
        name="adaln_mod",
    )(c_pad, mod_w, mod_b.reshape(depth, 1, n))
    return out[:, :bsz, :].reshape(depth, bsz, 6, d)


def _gmlp_kernel(x_ref, mod_ref, win_ref, vg_ref, vb_ref, ws_ref, bst_ref, wout_ref, g_ref, b_ref,
                 *rest, tm, sub, chunk, groups, alpha, n_exp, region):
    if n_exp:
        wr_ref, br_ref, o_ref, hp_ref, meta_ref, cnt_ref, gated_ref, carry_ref = rest

        @pl.when(pl.program_id(0) == 0)
        def _():
            carry_ref[...] = jnp.zeros_like(carry_ref)
    else:
        o_ref, gated_ref = rest
    m = mod_ref[0]
    row = lax.broadcasted_iota(jnp.int32, (chunk, chunk), 0)
    col = lax.broadcasted_iota(jnp.int32, (chunk, chunk), 1)
    causal = row >= col
    bst = bst_ref[...]
    w_mix = [jnp.where(causal, ws_ref[g], 0.0).astype(BF16) for g in range(groups)]
    n = tm // sub

    def project(s):
        x = x_ref[s * sub:(s + 1) * sub, :]
        return x, _dot(_modulate(x, m, 0).astype(BF16), win_ref[...])

    def activate(z):
        z = 0.5 * z * (1.0 + lax.erf(z * (2.0 ** -0.5)))
        width = z.shape[1] // 2
        return z[:, :width], _layer_norm(z[:, width:], vg_ref[...], vb_ref[...]).astype(BF16)

    def mix(s, u, v):
        gd = u.shape[1] // groups
        for g in range(groups):
            cs = slice(g * gd, (g + 1) * gd)
            for c in range(sub // chunk):
                rs = slice(c * chunk, (c + 1) * chunk)
                mixed = _dot(w_mix[g], v[rs, cs]) + bst[:, g:g + 1]
                gated_ref[s * sub + c * chunk:s * sub + (c + 1) * chunk, cs] = (u[rs, cs] * mixed).astype(BF16)
        return _dot(gated_ref[s * sub:(s + 1) * sub, :], wout_ref[...])

    def finish(s, x, y):
        rows = slice(s * sub, (s + 1) * sub)
        x_new = _res_ln(x, y, m, 0, g_ref[...], b_ref[...], alpha)
        o_ref[rows, :] = x_new
        if n_exp:
            hp_ref[rows, :], meta_ref[:, rows] = _route_rows(x_new, m, wr_ref, br_ref, carry_ref, n_exp, region)

    nxt = project(0)
    pending = None
    for s in range(n):
        x, z = nxt
        if s + 1 < n:
            nxt = project(s + 1)
        u, v = activate(z)
        if pending is not None:
            finish(*pending)
        pending = (s, x, mix(s, u, v))
    finish(*pending)
    if n_exp:
        cnt_ref[...] = carry_ref[...]


def _gmlp_layer(x, mod, w_in, v_g, v_b, w_s, b_s, w_out, ln_g, ln_b, router=None, *, seq, alpha, tm=512,
                sub=256):
    t, d = x.shape
    groups, chunk, _ = w_s.shape
    width = w_out.shape[0]
    n_exp = router[0].shape[1] if router else 0
    kern = functools.partial(_gmlp_kernel, tm=tm, sub=sub, chunk=chunk, groups=groups, alpha=alpha, n_exp=n_exp,
                             region=t)
    row_map = lambda i: (i, 0)
    out_specs, out_shape = [pl.BlockSpec((tm, d), row_map)], [jax.ShapeDtypeStruct((t, d), F32)]
    scratch = [pltpu.VMEM((tm, width), BF16)]
    extra_specs, extra_args = [], []
    if router:
        r_specs, r_shapes = _router_out(t, d, tm, row_map)
        out_specs, out_shape = out_specs + r_specs, out_shape + r_shapes
        scratch.append(pltpu.VMEM((ROUTE_ROWS, LANES), F32))
        extra_specs = _router_specs(d, _const_spec)
        extra_args = list(_router_operands(*router))
    outs = pl.pallas_call(
        kern,
        grid=(t // tm,),
        in_specs=[
            pl.BlockSpec((tm, d), lambda i: (i, 0)),
            pl.BlockSpec((1, 6, d), lambda i: (i * tm // seq, 0, 0)),
            _const_spec((d, 2 * width)),
            _const_spec((1, width)),
            _const_spec((1, width)),
            _const_spec((groups, chunk, chunk)),
            _const_spec((chunk, groups)),
            _const_spec((width, d)),
            _const_spec((1, d)),
            _const_spec((1, d)),
        ] + extra_specs,
        out_specs=out_specs,
        out_shape=out_shape,
        scratch_shapes=scratch,
        compiler_params=_cparams(("arbitrary",)),
        name="gmlp_mixer",
    )(x, mod, w_in.astype(BF16), v_g.reshape(1, width), v_b.reshape(1, width), w_s, b_s.T,
      w_out.astype(BF16), ln_g.reshape(1, d), ln_b.reshape(1, d), *extra_args)
    return outs if router else outs[0]


def _swiglu_pipelined(hs, w1, w3, w2, emit):
    up = lambda h: (_dot(h, w1), _dot(h, w3))
    nxt = up(hs[0])
    for s in range(len(hs)):
        a, b = nxt
        if s + 1 < len(hs):
            nxt = up(hs[s + 1])
        t = (a / (1.0 + jnp.exp(-a)) * b).astype(BF16)
        emit(s, _dot(t, w2))


def _ffn_kernel(x_ref, mod_ref, w1_ref, w3_ref, w2_ref, g_ref, b_ref, o_ref, h_ref, *, alpha, sub):
    j = pl.program_id(1)
    last = pl.num_programs(1) - 1
    subs = [slice(s, s + sub) for s in range(0, x_ref.shape[0], sub)]
    weights = lambda: (w1_ref[...], w3_ref[...], w2_ref[...])

    @pl.when(j == 0)
    def _():
        hs = []
        for rows in subs:
            h = _modulate(x_ref[rows, :], mod_ref[0], 3).astype(BF16)
            h_ref[rows, :] = h
            hs.append(h)

        def emit(s, part):
            o_ref[subs[s], :] = part

        _swiglu_pipelined(hs, *weights(), emit)

    @pl.when(jnp.logical_and(j > 0, j < last))
    def _():
        def emit(s, part):
            o_ref[subs[s], :] += part

        _swiglu_pipelined([h_ref[rows, :] for rows in subs], *weights(), emit)

    @pl.when(j == last)
    def _():
        def emit(s, part):
            rows = subs[s]
            y = o_ref[rows, :] + part
            o_ref[rows, :] = _res_ln(x_ref[rows, :], y, mod_ref[0], 3, g_ref[...], b_ref[...], alpha)

        _swiglu_pipelined([h_ref[rows, :] for rows in subs], *weights(), emit)


def _ffn_layer(x, mod, w13, w2, ln_g, ln_b, *, seq, alpha, tm=1024, tf=512):
    t, d = x.shape
    f = w2.shape[0]
    nf = f // tf
    assert nf >= 2
    w13b = w13.astype(BF16)
    return pl.pallas_call(
        functools.partial(_ffn_kernel, alpha=alpha, sub=tm // 4),
        grid=(t // tm, nf),
        in_specs=[
            pl.BlockSpec((tm, d), lambda i, j: (i, 0)),
            pl.BlockSpec((1, 6, d), lambda i, j: (i * tm // seq, 0, 0)),
            pl.BlockSpec((d, tf), lambda i, j: (0, j)),
            pl.BlockSpec((d, tf), lambda i, j: (0, nf + j)),
            pl.BlockSpec((tf, d), lambda i, j: (j, 0)),
            _const_spec((1, d)),
            _const_spec((1, d)),
        ],
        out_specs=pl.BlockSpec((tm, d), lambda i, j: (i, 0)),
        out_shape=jax.ShapeDtypeStruct((t, d), F32),
        scratch_shapes=[pltpu.VMEM((tm, d), BF16)],
        compiler_params=_cparams(("parallel", "arbitrary")),
        name="swiglu_dense",
    )(x, mod, w13b, w13b, w2.astype(BF16), ln_g.reshape(1, d), ln_b.reshape(1, d))


MOE_GROUP = 2


def _moe_ffn_kernel(ge_ref, gb_ref, gr_ref, h_ref, w1_ref, w3_ref, w2_ref, o_ref, acc_ref, *, sub):
    g = pl.program_id(0)
    j = pl.program_id(1)
    last = pl.num_programs(1) - 1
    n_rows = gr_ref[g]
    nv = (n_rows + EXPERT_BLOCK - 1) // EXPERT_BLOCK

    def run(n_blocks, phase):
        subs = [slice(s, s + sub) for s in range(0, n_blocks * EXPERT_BLOCK, sub)]
        row = lax.broadcasted_iota(jnp.int32, (sub, 1), 0)

        def rows_in(rows):
            w = jnp.where(row + rows.start < n_rows, h_ref[rows, :], jnp.uint32(0))
            return _unpack_bf16_pair(w).astype(BF16)

        def emit(s, part):
            if phase == "first":
                acc_ref[subs[s], :] = part
            elif phase == "middle":
                acc_ref[subs[s], :] += part
            else:
                o_ref[subs[s], :] = _pack_bf16_pair(acc_ref[subs[s], :] + part)

        _swiglu_pipelined([rows_in(rows) for rows in subs], w1_ref[0, 0].astype(BF16), w3_ref[0, 0].astype(BF16),
                          w2_ref[0, 0].astype(BF16), emit)

    for n_blocks in range(1, MOE_GROUP + 1):
        @pl.when(jnp.logical_and(nv == n_blocks, j == 0))
        def _():
            run(n_blocks, "first")

        @pl.when(jnp.logical_and(nv == n_blocks, jnp.logical_and(j > 0, j < last)))
        def _():
            run(n_blocks, "middle")

        @pl.when(jnp.logical_and(nv == n_blocks, j == last))
        def _():
            run(n_blocks, "last")

    for blk in range(MOE_GROUP):
        @pl.when(jnp.logical_and(blk >= nv, j == last))
        def _():
            o_ref[blk * EXPERT_BLOCK:(blk + 1) * EXPERT_BLOCK, :] = jnp.zeros((EXPERT_BLOCK, o_ref.shape[1]),
                                                                              jnp.uint32)


def _moe_ffn(h_slots, group_expert, group_block, group_rows, w13, w2, layer, *, tf=512):
    half = h_slots.shape[1]
    d = 2 * half
    f = w2.shape[2]
    nf = f // tf
    assert nf >= 2
    tm = MOE_GROUP * EXPERT_BLOCK

    def jj(g, j, gr):
        return jnp.where(gr[g] > 0, j, nf - 1)

    grid_spec = pltpu.PrefetchScalarGridSpec(
        num_scalar_prefetch=3,
        grid=(group_expert.shape[0], nf),
        in_specs=[
            pl.BlockSpec((tm, half), lambda g, j, ge, gb, gr: (gb[g], 0)),
            pl.BlockSpec((1, 1, d, tf), lambda g, j, ge, gb, gr: (layer, ge[g], 0, jj(g, j, gr))),
            pl.BlockSpec((1, 1, d, tf), lambda g, j, ge, gb, gr: (layer, ge[g], 0, nf + jj(g, j, gr))),
            pl.BlockSpec((1, 1, tf, d), lambda g, j, ge, gb, gr: (layer, ge[g], jj(g, j, gr), 0)),
        ],
        out_specs=pl.BlockSpec((tm, half), lambda g, j, ge, gb, gr: (gb[g], 0)),
        scratch_shapes=[pltpu.VMEM((tm, d), F32)],
    )
    return pl.pallas_call(
        functools.partial(_moe_ffn_kernel, sub=EXPERT_BLOCK // 2),
        grid_spec=grid_spec,
        out_shape=jax.ShapeDtypeStruct(h_slots.shape, jnp.uint32),
        compiler_params=_cparams(("arbitrary", "arbitrary")),
        name="swiglu_experts",
    )(group_expert, group_block, group_rows, h_slots, w13, w13, w2)


ROUTE_ROWS = 16


def _route_rows(x_rows, m, wrt_ref, brt_ref, carry_ref, n_exp, region):
    tm = x_rows.shape[0]
    h = _modulate(x_rows, m, 3)
    h_hi, h_mid, _ = _split3(h)
    w = wrt_ref[...]
    w_hi = w.astype(BF16)
    w_lo = (w - w_hi.astype(F32)).astype(BF16)
    nt = (((1,), (1,)), ((), ()))
    logits = (lax.dot_general(w_hi, h_hi, nt, preferred_element_type=F32)
              + lax.dot_general(w_lo, h_hi, nt, preferred_element_type=F32)
              + lax.dot_general(w_hi, h_mid, nt, preferred_element_type=F32) + brt_ref[...])
    erow = lax.broadcasted_iota(jnp.int32, (ROUTE_ROWS, tm), 0)
    neg = jnp.float32(-jnp.inf)
    logits = jnp.where(erow < n_exp, logits, neg)
    v0 = jnp.max(logits, axis=0, keepdims=True)
    e0 = jnp.min(jnp.where(logits == v0, erow, ROUTE_ROWS), axis=0, keepdims=True)
    rest = jnp.where(erow == e0, neg, logits)
    v1 = jnp.max(rest, axis=0, keepdims=True)
    e1 = jnp.min(jnp.where(rest == v1, erow, ROUTE_ROWS), axis=0, keepdims=True)
    p = jnp.exp(v1 - v0)
    g0 = 1.0 / (1.0 + p)
    g1 = p / (1.0 + p)
    oh0 = erow == e0
    oh1 = erow == e1
    onehot = jnp.where(jnp.logical_or(oh0, oh1), 1.0, 0.0).astype(BF16)
    r = lax.broadcasted_iota(jnp.int32, (tm, tm), 0)
    c = lax.broadcasted_iota(jnp.int32, (tm, tm), 1)
    earlier = jnp.where(r < c, 1.0, 0.0).astype(BF16)
    carry = carry_ref[:, 0:1]
    before = _dot(onehot, earlier) + carry
    rank0 = jnp.sum(jnp.where(oh0, before, 0.0), axis=0, keepdims=True)
    rank1 = jnp.sum(jnp.where(oh1, before, 0.0), axis=0, keepdims=True)
    total = carry + jnp.sum(onehot.astype(F32), axis=1, keepdims=True)
    carry_ref[...] = jnp.broadcast_to(total, carry_ref.shape)
    mrow = lax.broadcasted_iota(jnp.int32, (SUBLANES, tm), 0)
    e0f = e0.astype(F32)
    e1f = e1.astype(F32)
    meta = jnp.where(mrow == 0, e0f, 0.0)
    meta = jnp.where(mrow == 1, e1f, meta)
    meta = jnp.where(mrow == 2, g0, meta)
    meta = jnp.where(mrow == 3, g1, meta)
    meta = jnp.where(mrow == 4, e0f * region + rank0, meta)
    meta = jnp.where(mrow == 5, e1f * region + rank1, meta)
    return _pack_bf16_pair(h), meta


def _router_operands(w_router, b_router):
    d, n_exp = w_router.shape
    wrt = jnp.zeros((ROUTE_ROWS, d), F32).at[:n_exp].set(w_router.T)
    brt = jnp.zeros((ROUTE_ROWS, 1), F32).at[:n_exp, 0].set(b_router)
    return wrt, brt


def _router_specs(d, const):
    return [const((ROUTE_ROWS, d)), const((ROUTE_ROWS, 1))]


def _router_out(t, d, tm, row_map):
    specs = [pl.BlockSpec((tm, d // 2), row_map),
             pl.BlockSpec((SUBLANES, tm), lambda *a: (0, row_map(*a)[0])),
             pl.BlockSpec((ROUTE_ROWS, LANES), lambda *_: (0, 0))]
    shapes = [jax.ShapeDtypeStruct((t, d // 2), jnp.uint32), jax.ShapeDtypeStruct((SUBLANES, t), F32),
              jax.ShapeDtypeStruct((ROUTE_ROWS, LANES), F32)]
    return specs, shapes


def _combine_kernel(x_ref, ya_ref, yb_ref, meta_ref, mod_ref, g_ref, b_ref, o_ref, *, alpha):
    record = meta_ref[...]
    cols = jnp.concatenate([record, jnp.zeros((LANES - record.shape[0], record.shape[1]), F32)], axis=0).T
    y = cols[:, 2:3] * _unpack_bf16_pair(ya_ref[...]) + cols[:, 3:4] * _unpack_bf16_pair(yb_ref[...])
    o_ref[...] = _res_ln(x_ref[...], y, mod_ref[0], 3, g_ref[...], b_ref[...], alpha)


def _combine(x, ya, yb, meta, mod, ln_g, ln_b, *, seq, alpha, tm=1024):
    t, d = x.shape
    row = pl.BlockSpec((tm, d), lambda i: (i, 0))
    packed = pl.BlockSpec((tm, d // 2), lambda i: (i, 0))
    return pl.pallas_call(
        functools.partial(_combine_kernel, alpha=alpha),
        grid=(t // tm,),
        in_specs=[row, packed, packed,
                  pl.BlockSpec((SUBLANES, tm), lambda i: (0, i)),
                  pl.BlockSpec((1, 6, d), lambda i: (i * tm // seq, 0, 0)),
                  _const_spec((1, d)), _const_spec((1, d))],
        out_specs=row,
        out_shape=jax.ShapeDtypeStruct((t, d), F32),
        compiler_params=_cparams(("parallel",)),
        name="moe_combine",
    )(x, ya, yb, meta, mod, ln_g.reshape(1, d), ln_b.reshape(1, d))


def _sc_mesh():
    return plsc.VectorSubcoreMesh(core_axis_name="c", subcore_axis_name="s", num_cores=SC_CORES,
                                  num_subcores=SC_SUBCORES)


def _sc_worker_share(n):
    workers = SC_CORES * SC_SUBCORES
    per_worker = n // workers
    steps = per_worker // SC_ROWS
    assert steps * SC_ROWS * workers == n
    return per_worker, steps


def _sc_gather_rows(table, idx_a, idx_b):
    n = idx_a.shape[0]
    width = table.shape[1]
    per_worker, steps = _sc_worker_share(n)

    def body(table_hbm, ia_hbm, ib_hbm, oa_hbm, ob_hbm, ia_v, ib_v, rows_v, sem):
        wid = lax.axis_index("s") * SC_CORES + lax.axis_index("c")
        base = pl.multiple_of(wid * per_worker, SC_ROWS)
        pltpu.sync_copy(ia_hbm.at[pl.ds(base, per_worker)], ia_v)
        pltpu.sync_copy(ib_hbm.at[pl.ds(base, per_worker)], ib_v)

        @pl.loop(0, steps)
        def _(j):
            lo = pl.multiple_of(j * SC_ROWS, SC_ROWS)
            for idx_v, out_hbm in ((ia_v, oa_hbm), (ib_v, ob_hbm)):
                pltpu.async_copy(table_hbm.at[idx_v.at[pl.ds(lo, SC_ROWS)]], rows_v, sem).wait()
                pltpu.sync_copy(rows_v, out_hbm.at[pl.ds(base + lo, SC_ROWS)])

    out = jax.ShapeDtypeStruct((n, width), table.dtype)
    return pl.kernel(
        body,
        out_type=(out, out),
        mesh=_sc_mesh(),
        scratch_types=[
            pltpu.VMEM((per_worker,), jnp.int32),
            pltpu.VMEM((per_worker,), jnp.int32),
            pltpu.VMEM((SC_ROWS, width), table.dtype),
            pltpu.SemaphoreType.DMA,
        ],
        name="sc_row_gather",
    )(table, idx_a, idx_b)


def _sc_scatter_rows(rows, dest_a, dest_b, n_out):
    n, width = rows.shape
    per_worker, steps = _sc_worker_share(n)

    def body(rows_hbm, da_hbm, db_hbm, out_hbm, ia_v, ib_v, rows_v, sem):
        wid = lax.axis_index("s") * SC_CORES + lax.axis_index("c")

        @pl.loop(0, steps)
        def _(j):
            off = pl.multiple_of(wid * per_worker + j * SC_ROWS, SC_ROWS)
            pltpu.sync_copy(da_hbm.at[pl.ds(off, SC_ROWS)], ia_v)
            pltpu.sync_copy(db_hbm.at[pl.ds(off, SC_ROWS)], ib_v)
            pltpu.sync_copy(rows_hbm.at[pl.ds(off, SC_ROWS)], rows_v)
            pltpu.async_copy(rows_v, out_hbm.at[ia_v], sem).wait()
            pltpu.async_copy(rows_v, out_hbm.at[ib_v], sem).wait()

    return pl.kernel(
        body,
        out_type=jax.ShapeDtypeStruct((n_out, width), rows.dtype),
        mesh=_sc_mesh(),
        scratch_types=[
            pltpu.VMEM((SC_ROWS,), jnp.int32),
            pltpu.VMEM((SC_ROWS,), jnp.int32),
            pltpu.VMEM((SC_ROWS, width), rows.dtype),
            pltpu.SemaphoreType.DMA,
        ],
        name="sc_row_scatter",
    )(rows, dest_a, dest_b)


def _moe_layer(routed, mod, w13, w2, layer, ln_g, ln_b, *, seq, alpha):
    x, h, meta, cnt = routed
    t, d = x.shape
    n_exp = w13.shape[1]
    dest0 = meta[4].astype(jnp.int32)
    dest1 = meta[5].astype(jnp.int32)
    counts = cnt[:n_exp, 0].astype(jnp.int32)
    group_rows = MOE_GROUP * EXPERT_BLOCK
    assert t % group_rows == 0
    region_groups = t // group_rows
    groups_per_expert = (counts + group_rows - 1) // group_rows
    group_end = jnp.cumsum(groups_per_expert)
    g = jnp.arange(t * TOP_K // group_rows + n_exp, dtype=jnp.int32)
    expert = jnp.minimum(jnp.sum(group_end[None, :] <= g[:, None], axis=1), n_exp - 1).astype(jnp.int32)
    blk_in_expert = g - (group_end - groups_per_expert)[expert]
    used = g < group_end[-1]
    spare_block = n_exp * region_groups
    group_block = jnp.where(used, expert * region_groups + blk_in_expert, spare_block).astype(jnp.int32)
    rows_in_group = jnp.where(used, jnp.clip(counts[expert] - blk_in_expert * group_rows, 0, group_rows),
                              0).astype(jnp.int32)
    h_slots = _sc_scatter_rows(h, dest0, dest1, (spare_block + 1) * group_rows)
    y_slots = _moe_ffn(h_slots, expert, group_block, rows_in_group, w13, w2, layer)
    ya, yb = _sc_gather_rows(y_slots, dest0, dest1)
    return _combine(x, ya, yb, meta, mod, ln_g, ln_b, seq=seq, alpha=alpha)


HEAD_PAD = LANES
GATE_PARTS = 3
GATE_ROWS = 16
VAL_PAD = 16
LOG2E = 1.4426950408889634


def _head_lanes(h, hd):
    k0 = (h % (HEAD_PAD // hd)) * hd
    return k0, (hd if k0 == 0 else 0)


def _fox_proj_kernel(x_ref, mod_ref, wf_ref, bf_ref, wq_ref, wk_ref, wv_ref, selk_ref, onek_ref, selq_ref,
                     oneq_ref, onev_ref, k_ref, q_ref, v_ref, carry_ref, *, q_scale, heads, hd, nb):
    tm = x_ref.shape[0]

    @pl.when(pl.program_id(0) % nb == 0)
    def _():
        carry_ref[...] = jnp.zeros_like(carry_ref)

    h_hi, h_mid, _ = _split3(_modulate(x_ref[...], mod_ref[0], 0))

    f = _dot_split(h_hi, h_mid, wf_ref[...]) + bf_ref[...]
    lf = jnp.minimum(f, 0.0) - jnp.log(1.0 + jnp.exp(-jnp.abs(f)))
    r = lax.broadcasted_iota(jnp.int32, (tm, tm), 0)
    c = lax.broadcasted_iota(jnp.int32, (tm, tm), 1)
    tri = jnp.where(r >= c, 1.0, 0.0).astype(BF16)
    sums = _dot(tri, jnp.concatenate(_split3(lf), axis=1))
    cs = carry_ref[0:1, :] + sum(sums[:, n * LANES:(n + 1) * LANES] for n in range(GATE_PARTS))
    carry_ref[...] = jnp.broadcast_to(cs[tm - 1:tm, :], carry_ref.shape)
    cs = cs * LOG2E

    gate_k = onek_ref[...] + _dot(jnp.concatenate(_split3(cs), axis=1), selk_ref[...])
    kc = _dot(h_hi, wk_ref[...])
    lane = lax.broadcasted_iota(jnp.int32, (tm, HEAD_PAD), 1)
    for h in range(heads):
        k0, _ = _head_lanes(h, hd)
        src = (h * hd // HEAD_PAD) * HEAD_PAD
        is_k = jnp.logical_and(lane >= k0, lane < k0 + hd)
        hs = slice(h * HEAD_PAD, (h + 1) * HEAD_PAD)
        k_ref[:, hs] = jnp.where(is_k, kc[:, src:src + HEAD_PAD], gate_k[:, hs]).astype(BF16)

    nt = (((1,), (1,)), ((), ()))
    qc = lax.dot_general(wq_ref[...], h_hi, nt, preferred_element_type=F32) * (q_scale * LOG2E)
    gate_q = oneq_ref[...] + _dot(selq_ref[...], jnp.concatenate(_split3(cs.T), axis=0))
    fill = HEAD_PAD - hd - GATE_ROWS
    for h in range(heads):
        k0, g0 = _head_lanes(h, hd)
        base = h * HEAD_PAD
        q_ref[0, base + k0:base + k0 + hd, :] = qc[h * hd:(h + 1) * hd, :].astype(BF16)
        q_ref[0, base + g0:base + g0 + GATE_ROWS, :] = gate_q[h * GATE_ROWS:(h + 1) * GATE_ROWS, :].astype(BF16)
        q_ref[0, base + g0 + GATE_ROWS:base + g0 + GATE_ROWS + fill, :] = jnp.zeros((fill, tm), BF16)

    v = lax.dot_general(wv_ref[...], h_hi, nt, preferred_element_type=F32) + onev_ref[...]
    v_ref[0] = v.astype(BF16)


def _fox_proj(x, mod, w_in, b_f, *, bsz, seq, heads, tm=512):
    t, d = x.shape
    hd = d // heads
    assert HEAD_PAD == 2 * hd and GATE_ROWS >= 2 * GATE_PARTS
    nb = seq // tm
    wide = heads * HEAD_PAD
    wq_t = w_in[:, :d].T.astype(BF16)
    wk = w_in[:, d:2 * d].astype(BF16)
    vrows = heads * (hd + VAL_PAD)
    wv_t = jnp.pad(w_in[:, 2 * d:3 * d].reshape(d, heads, hd),
                   ((0, 0), (0, 0), (0, VAL_PAD))).reshape(d, vrows).T.astype(BF16)
    wf = jnp.zeros((d, LANES), F32).at[:, :heads].set(w_in[:, 3 * d:])
    bf = jnp.zeros((1, LANES), F32).at[0, :heads].set(b_f)
    sel_k = np.zeros((GATE_PARTS, LANES, wide), np.float32)
    one_k = np.zeros((1, wide), np.float32)
    sel_q = np.zeros((GATE_PARTS, heads * GATE_ROWS, LANES), np.float32)
    one_q = np.zeros((heads * GATE_ROWS, 1), np.float32)
    one_v = np.zeros((vrows, 1), np.float32)
    for hh in range(heads):
        _, g0 = _head_lanes(hh, hd)
        one_v[hh * (hd + VAL_PAD) + hd, 0] = 1.0
        for n in range(GATE_PARTS):
            sel_k[n, hh, hh * HEAD_PAD + g0 + n] = -1.0
            one_q[hh * GATE_ROWS + n, 0] = 1.0
            one_k[0, hh * HEAD_PAD + g0 + GATE_PARTS + n] = 1.0
            sel_q[n, hh * GATE_ROWS + GATE_PARTS + n, hh] = 1.0
    return pl.pallas_call(
        functools.partial(_fox_proj_kernel, q_scale=hd ** -0.5, heads=heads, hd=hd, nb=nb),
        grid=(t // tm,),
        in_specs=[
            pl.BlockSpec((tm, d), lambda i: (i, 0)),
            pl.BlockSpec((1, 6, d), lambda i: (i * tm // seq, 0, 0)),
            _const_spec((d, LANES)),
            _const_spec((1, LANES)),
            _const_spec((d, d)),
            _const_spec((d, d)),
            _const_spec((vrows, d)),
            _const_spec((GATE_PARTS * LANES, wide)),
            _const_spec((1, wide)),
            _const_spec((heads * GATE_ROWS, GATE_PARTS * LANES)),
            _const_spec((heads * GATE_ROWS, 1)),
            _const_spec((vrows, 1)),
        ],
        out_specs=[
            pl.BlockSpec((tm, wide), lambda i: (i, 0)),
            pl.BlockSpec((1, wide, tm), lambda i: (i // nb, 0, i % nb)),
            pl.BlockSpec((1, vrows, tm), lambda i: (i // nb, 0, i % nb)),
        ],
        out_shape=[
            jax.ShapeDtypeStruct((t, wide), BF16),
            jax.ShapeDtypeStruct((bsz, wide, seq), BF16),
            jax.ShapeDtypeStruct((bsz, vrows, seq), BF16),
        ],
        scratch_shapes=[pltpu.VMEM((SUBLANES, LANES), F32)],
        compiler_params=_cparams(("arbitrary",)),
        name="fox_proj",
    )(x, mod, wf, bf, wq_t, wk, wv_t, jnp.asarray(sel_k.reshape(GATE_PARTS * LANES, wide), BF16), jnp.asarray(one_k),
      jnp.asarray(np.concatenate(list(sel_q), axis=1), BF16), jnp.asarray(one_q), jnp.asarray(one_v))


def _flash_kernel(qi_ref, kj_ref, k_ref, q_ref, v_ref, x_ref, mod_ref, wout_ref, g_ref, b_ref, *rest,
                  heads, hd, blk, alpha, n_exp, region):
    p_idx = pl.program_id(1)
    if n_exp:
        wr_ref, br_ref, o_ref, hp_ref, meta_ref, cnt_ref, m_ref, acc_ref, s_ref, carry_ref = rest

        @pl.when(jnp.logical_and(pl.program_id(0) == 0, p_idx == 0))
        def _():
            carry_ref[...] = jnp.zeros_like(carry_ref)
    else:
        o_ref, m_ref, acc_ref, s_ref = rest
    qi = qi_ref[p_idx]
    kj = kj_ref[p_idx]
    vp = hd + VAL_PAD

    @pl.when(kj == 0)
    def _():
        m_ref[...] = jnp.full(m_ref.shape, -jnp.inf, F32)
        acc_ref[...] = jnp.zeros_like(acc_ref)

    def step(diagonal):
        half = blk // 2
        tiles = [(slice(0, half), slice(0, half)), (slice(half, blk), slice(0, blk))] if diagonal else \
                [(slice(0, blk), slice(0, blk))]
        if diagonal:
            causal = lambda rows, first_q: (lax.broadcasted_iota(jnp.int32, (rows, half), 0)
                                            <= lax.broadcasted_iota(jnp.int32, (rows, half), 1) + first_q)
            keep = [causal(half, 0), causal(blk, half)]

        def scores(h):
            hs = slice(h * HEAD_PAD, (h + 1) * HEAD_PAD)
            tops = []
            for n, (qs, ks) in enumerate(tiles):
                s = _dot(k_ref[ks, hs], q_ref[0, hs, qs])
                if diagonal:
                    s = jnp.where(keep[n], s, -jnp.inf)
                s_ref[h % depth, ks, qs] = s
                tops.append(jnp.max(s, axis=0, keepdims=True))
            return tops

        depth = s_ref.shape[0]
        ahead = [scores(h) for h in range(depth - 1)]
        for h in range(heads):
            if h + depth - 1 < heads:
                ahead.append(scores(h + depth - 1))
            rs = slice(h * vp, (h + 1) * vp)
            for (qs, ks), m_cur in zip(tiles, ahead.pop(0)):
                m_prev = m_ref[h:h + 1, qs]
                m_new = jnp.maximum(m_prev, m_cur)
                m_ref[h:h + 1, qs] = m_new
                p = jnp.exp2(s_ref[h % depth, ks, qs] - m_new).astype(BF16)
                acc_ref[rs, qs] = acc_ref[rs, qs] * jnp.exp2(m_prev - m_new) + _dot(v_ref[0, rs, ks], p)

    @pl.when(kj < qi)
    def _():
        step(False)

    @pl.when(kj == qi)
    def _():
        step(True)
        per = LANES // hd
        cols = []
        for g in range(heads // per):
            parts = []
            for n in range(per):
                r0 = (g * per + n) * vp
                parts.append(acc_ref[r0:r0 + hd, :] / acc_ref[r0 + hd:r0 + hd + 1, :])
            cols.append(jnp.concatenate(parts, axis=0).T.astype(BF16))
        y = _dot(jnp.concatenate(cols, axis=1), wout_ref[...])
        x_new = _res_ln(x_ref[...], y, mod_ref[0], 0, g_ref[...], b_ref[...], alpha)
        o_ref[...] = x_new
        if n_exp:
            hp_ref[...], meta_ref[...] = _route_rows(x_new, mod_ref[0], wr_ref, br_ref, carry_ref, n_exp, region)
            cnt_ref[...] = carry_ref[...]


def _flash(k_aug, q_aug_t, v_t, x, mod, w_out, ln_g, ln_b, router=None, *, bsz, seq, heads, alpha, blk=512):
    t, wide = k_aug.shape
    vrows = v_t.shape[1]
    hd = vrows // heads - VAL_PAD
    d = heads * hd
    nq = seq // blk
    pairs = [(i, j) for i in range(nq) for j in range(i + 1)]
    qi = jnp.asarray(np.array([p[0] for p in pairs], np.int32))
    kj = jnp.asarray(np.array([p[1] for p in pairs], np.int32))
    n_exp = router[0].shape[1] if router else 0
    row_map = lambda b, p, qi, kj: (b * nq + qi[p], 0)
    const = lambda shape: pl.BlockSpec(shape, lambda b, p, qi, kj: (0, 0), pipeline_mode=pl.Buffered(1))
    out_specs, out_shape = [pl.BlockSpec((blk, d), row_map)], [jax.ShapeDtypeStruct((t, d), F32)]
    scratch = [pltpu.VMEM((heads, blk), F32), pltpu.VMEM((vrows, blk), F32), pltpu.VMEM((3, blk, blk), F32)]
    extra_specs, extra_args = [], []
    if router:
        r_specs, r_shapes = _router_out(t, d, blk, row_map)
        out_specs, out_shape = out_specs + r_specs, out_shape + r_shapes
        scratch.append(pltpu.VMEM((ROUTE_ROWS, LANES), F32))
        extra_specs = _router_specs(d, const)
        extra_args = list(_router_operands(*router))
    grid_spec = pltpu.PrefetchScalarGridSpec(
        num_scalar_prefetch=2,
        grid=(bsz, len(pairs)),
        in_specs=[
            pl.BlockSpec((blk, wide), lambda b, p, qi, kj: (b * nq + kj[p], 0)),
            pl.BlockSpec((1, wide, blk), lambda b, p, qi, kj: (b, 0, qi[p])),
            pl.BlockSpec((1, vrows, blk), lambda b, p, qi, kj: (b, 0, kj[p])),
            pl.BlockSpec((blk, d), row_map),
            pl.BlockSpec((1, 6, d), lambda b, p, qi, kj: (b, 0, 0)),
            const((d, d)),
            const((1, d)),
            const((1, d)),
        ] + extra_specs,
        out_specs=out_specs,
        scratch_shapes=scratch,
    )
    outs = pl.pallas_call(
        functools.partial(_flash_kernel, heads=heads, hd=hd, blk=blk, alpha=alpha, n_exp=n_exp, region=t),
        grid_spec=grid_spec,
        out_shape=out_shape,
        compiler_params=_cparams(("arbitrary", "arbitrary")),
        name="fox_attention",
    )(qi, kj, k_aug, q_aug_t, v_t, x, mod, w_out.astype(BF16), ln_g.reshape(1, d), ln_b.reshape(1, d), *extra_args)
    return outs if router else outs[0]


def _fox_layer(x, mod, w_in, b_f, w_out, ln_g, ln_b, router=None, *, bsz, seq, alpha):
    heads = b_f.shape[0]
    k_aug, q_aug_t, v_t = _fox_proj(x, mod, w_in, b_f, bsz=bsz, seq=seq, heads=heads)
    return _flash(k_aug, q_aug_t, v_t, x, mod, w_out, ln_g, ln_b, router, bsz=bsz, seq=seq, heads=heads,
                  alpha=alpha)


def _pool_kernel(x_ref, xh_ref, mod_ref, win_ref, wg_ref, ls_ref, wout_ref, g_ref, b_ref, o_ref, pooled_ref,
                 *, tm, sub, seq, alpha):
    pos0 = (pl.program_id(0) * tm) % seq
    m = mod_ref[0]
    n = tm // sub
    halo_ok = jnp.where(pos0 > 0, jnp.float32(1.0), jnp.float32(0.0))
    rows = lax.broadcasted_iota(jnp.int32, (sub + POOL_HALO, 1), 0)

    def project(s):
        x = x_ref[s * sub:(s + 1) * sub, :]
        halo = xh_ref[...] if s == 0 else x_ref[s * sub - POOL_HALO:s * sub, :]
        ze = _dot(_modulate(jnp.concatenate([halo, x], axis=0), m, 0).astype(BF16), win_ref[...])
        if s == 0:
            ze = ze * jnp.where(rows < POOL_HALO, halo_ok, jnp.float32(1.0))
        return x, ze

    def pool(s, ze):
        pos = (pos0 + s * sub + lax.broadcasted_iota(jnp.int32, (sub, 1), 0) + 1).astype(F32)
        gd = ze.shape[1] // len(POOL_WINDOWS)
        out_rows = slice(s * sub, (s + 1) * sub)
        for g, win in enumerate(POOL_WINDOWS):
            cs = slice(g * gd, (g + 1) * gd)
            zg = ze[:, cs]
            acc = zg
            span = 1
            while span < win:
                acc = acc + pltpu.roll(acc, span, 0)
                span *= 2
            count = jnp.minimum(pos, jnp.float32(win))
            pooled = acc[POOL_HALO:, :] / count - zg[POOL_HALO:, :]
            pooled_ref[out_rows, cs] = (_dot(pooled.astype(BF16), wg_ref[g]) * ls_ref[:, cs]).astype(BF16)
        return _dot(pooled_ref[out_rows, :], wout_ref[...])

    nxt = project(0)
    for s in range(n):
        x, ze = nxt
        if s + 1 < n:
            nxt = project(s + 1)
        y = pool(s, ze)
        o_ref[s * sub:(s + 1) * sub, :] = _res_ln(x, y, m, 0, g_ref[...], b_ref[...], alpha)


def _pool_layer(x, mod, w_in, w_grp, scale, w_out, ln_g, ln_b, *, seq, alpha, tm=1024, sub=256):
    t, d = x.shape
    ng, gd, _ = w_grp.shape
    hb = tm // POOL_HALO
    return pl.pallas_call(
        functools.partial(_pool_kernel, tm=tm, sub=sub, seq=seq, alpha=alpha),
        grid=(t // tm,),
        in_specs=[
            pl.BlockSpec((tm, d), lambda i: (i, 0)),
            pl.BlockSpec((POOL_HALO, d), lambda i: (jnp.maximum(i * hb - 1, 0), 0)),
            pl.BlockSpec((1, 6, d), lambda i: (i * tm // seq, 0, 0)),
            _const_spec((d, d)),
            _const_spec((ng, gd, gd)),
            _const_spec((1, d)),
            _const_spec((d, d)),
            _const_spec((1, d)),
            _const_spec((1, d)),
        ],
        out_specs=pl.BlockSpec((tm, d), lambda i: (i, 0)),
        out_shape=jax.ShapeDtypeStruct((t, d), F32),
        scratch_shapes=[pltpu.VMEM((tm, d), BF16)],
        compiler_params=_cparams(("parallel",)),
        name="pool_mixer",
    )(x, x, mod, w_in.astype(BF16), w_grp.astype(BF16), scale.reshape(1, d), w_out.astype(BF16),
      ln_g.reshape(1, d), ln_b.reshape(1, d))


def kernel(x, c, mod_w, mod_b, ln_g, ln_b, gm_w_in, gm_ln_g, gm_ln_b, gm_w_s, gm_b_s, gm_w_out, fox_w_in,
           fox_b_f, fox_w_out, pool_w_in, pool_w_grp, pool_scale, pool_w_out, ffn_w13, ffn_w2, moe_w_router,
           moe_b_router, moe_w13, moe_w2):
    bsz, seq, d = x.shape
    depth = mod_w.shape[0]
    alpha = (2 * depth) ** 0.25
    n_mixers = 3
    mod_all = _modulation(c, mod_w, mod_b)
    xt = x.reshape(bsz * seq, d)
    for i in range(depth):
        mod = mod_all[i]
        kind, j = i % n_mixers, i // n_mixers
        router = (moe_w_router[i // 2], moe_b_router[i // 2]) if i % 2 else None
        if kind == 0:
            xt = _gmlp_layer(xt, mod, gm_w_in[j], gm_ln_g[j], gm_ln_b[j], gm_w_s[j], gm_b_s[j], gm_w_out[j],
                             ln_g[i, 0], ln_b[i, 0], router, seq=seq, alpha=alpha)
        elif kind == 1:
            xt = _fox_layer(xt, mod, fox_w_in[j], fox_b_f[j], fox_w_out[j], ln_g[i, 0], ln_b[i, 0], router,
                            bsz=bsz, seq=seq, alpha=alpha)
        else:
            assert router is None, "the pooling mixer has no routing epilogue"
            xt = _pool_layer(xt, mod, pool_w_in[j], pool_w_grp[j], pool_scale[j], pool_w_out[j],
                             ln_g[i, 0], ln_b[i, 0], seq=seq, alpha=alpha)
        if router is None:
            xt = _ffn_layer(xt, mod, ffn_w13[i // 2], ffn_w2[i // 2], ln_g[i, 1], ln_b[i, 1],
                            seq=seq, alpha=alpha)
        else:
            xt = _moe_layer(xt, mod, moe_w13, moe_w2, i // 2, ln_g[i, 1], ln_b[i, 1], seq=seq, alpha=alpha)
    return xt.reshape(bsz, seq, d)
```

```python
import functools

import numpy as np
import jax
import jax.numpy as jnp
from jax import lax
from jax.experimental import pallas as pl
from jax.experimental.pallas import tpu as pltpu
from jax.experimental.pallas import tpu_sc as plsc

F32 = jnp.float32
BF16 = jnp.bfloat16

POOL_WINDOWS = (2, 4, 8, 16)
TOP_K = 2
EXPERT_BLOCK = 512
LN_EPS = 1e-5
LANES = 128
SUBLANES = 8
POOL_HALO = 16
VMEM_LIMIT = 56 * 1024 * 1024
SC_CORES = 2
SC_SUBCORES = 16
SC_ROWS = 128


def _cparams(sem):
    return pltpu.CompilerParams(dimension_semantics=sem, vmem_limit_bytes=VMEM_LIMIT)


def _const_spec(shape):
    nd = len(shape)
    return pl.BlockSpec(shape, lambda *_: (0,) * nd, pipeline_mode=pl.Buffered(1))


def _layer_norm(r, g, b):
    mu = jnp.mean(r, axis=-1, keepdims=True)
    xc = r - mu
    var = jnp.mean(xc * xc, axis=-1, keepdims=True)
    return xc * lax.rsqrt(var + LN_EPS) * g + b


def _modulate(x, m, off):
    return x * (1.0 + m[off + 1:off + 2]) + m[off:off + 1]


def _res_ln(x, y, m, off, g, b, alpha):
    return _layer_norm(alpha * x + (1.0 + m[off + 2:off + 3]) * y, g, b)


def _split3(a):
    hi = a.astype(BF16)
    r1 = a - hi.astype(F32)
    mid = r1.astype(BF16)
    lo = (r1 - mid.astype(F32)).astype(BF16)
    return hi, mid, lo


def _dot(a, b):
    return jnp.dot(a, b, preferred_element_type=F32)


def _dot_split(a_hi, a_mid, w):
    n = w.shape[1]
    w_hi = w.astype(BF16)
    w_lo = (w - w_hi.astype(F32)).astype(BF16)
    both = _dot(a_hi, jnp.concatenate([w_hi, w_lo], axis=1))
    return both[:, :n] + both[:, n:] + _dot(a_mid, w_hi)


def _pack_bf16_pair(x):
    n = x.shape[1] // 2
    hi = lax.bitcast_convert_type(x[:, :n].astype(BF16).astype(F32), jnp.uint32)
    lo = lax.bitcast_convert_type(x[:, n:].astype(BF16).astype(F32), jnp.uint32)
    return hi | (lo >> 16)


def _unpack_bf16_pair(w):
    hi = lax.bitcast_convert_type(w & jnp.uint32(0xFFFF0000), F32)
    lo = lax.bitcast_convert_type(w << 16, F32)
    return jnp.concatenate([hi, lo], axis=1)


def _mod_kernel(c_ref, w_ref, b_ref, o_ref):
    c = c_ref[...]
    s = c / (1.0 + jnp.exp(-c))
    s_hi, s_mid, _ = _split3(s)
    w = w_ref[0]
    w_hi = w.astype(BF16)
    w_lo = (w - w_hi.astype(F32)).astype(BF16)
    acc = _dot(s_hi, w_hi) + _dot(s_mid, w_hi) + _dot(s_hi, w_lo)
    o_ref[0] = acc + b_ref[0]


def _modulation(c, mod_w, mod_b):
    depth, d, n = mod_w.shape
    bsz = c.shape[0]
    rows = SUBLANES
    tn = n // 2
    c_pad = jnp.zeros((rows, d), F32).at[:bsz].set(c)
    out = pl.pallas_call(
        _mod_kernel,
        grid=(depth, n // tn),
        in_specs=[
            pl.BlockSpec((rows, d), lambda l, j: (0, 0)),
            pl.BlockSpec((1, d, tn), lambda l, j: (l, 0, j)),
            pl.BlockSpec((1, 1, tn), lambda l, j: (l, 0, j)),
        ],
        out_specs=pl.BlockSpec((1, rows, tn), lambda l, j: (l, 0, j)),
        out_shape=jax.ShapeDtypeStruct((depth, rows, n), F32),
        compiler_params=_cparams(("parallel", "parallel")),
        name="adaln_mod",
    )(c_pad, mod_w, mod_b.reshape(depth, 1, n))
    return out[:, :bsz, :].reshape(depth, bsz, 6, d)


def _gmlp_kernel(x_ref, mod_ref, win_ref, vg_ref, vb_ref, ws_ref, bst_ref, wout_ref, g_ref, b_ref,
                 *rest, tm, sub, chunk, groups, alpha, n_exp, region):
    if n_exp:
        wr_ref, br_ref, o_ref, hp_ref, meta_ref, cnt_ref, gated_ref, carry_ref = rest

        @pl.when(pl.program_id(0) == 0)
        def _():
            carry_ref[...] = jnp.zeros_like(carry_ref)
    else:
        o_ref, gated_ref = rest
    m = mod_ref[0]
    row = lax.broadcasted_iota(jnp.int32, (chunk, chunk), 0)
    col = lax.broadcasted_iota(jnp.int32, (chunk, chunk), 1)
    causal = row >= col
    bst = bst_ref[...]
    w_mix = [jnp.where(causal, ws_ref[g], 0.0).astype(BF16) for g in range(groups)]
    n = tm // sub

    def project(s):
        x = x_ref[s * sub:(s + 1) * sub, :]
        return x, _dot(_modulate(x, m, 0).astype(BF16), win_ref[...])

    def activate(z):
        z = 0.5 * z * (1.0 + lax.erf(z * (2.0 ** -0.5)))
        width = z.shape[1] // 2
        return z[:, :width], _layer_norm(z[:, width:], vg_ref[...], vb_ref[...]).astype(BF16)

    def mix(s, u, v):
        gd = u.shape[1] // groups
        for g in range(groups):
            cs = slice(g * gd, (g + 1) * gd)
            for c in range(sub // chunk):
                rs = slice(c * chunk, (c + 1) * chunk)
                mixed = _dot(w_mix[g], v[rs, cs]) + bst[:, g:g + 1]
                gated_ref[s * sub + c * chunk:s * sub + (c + 1) * chunk, cs] = (u[rs, cs] * mixed).astype(BF16)
        return _dot(gated_ref[s * sub:(s + 1) * sub, :], wout_ref[...])

    def finish(s, x, y):
        rows = slice(s * sub, (s + 1) * sub)
        x_new = _res_ln(x, y, m, 0, g_ref[...], b_ref[...], alpha)
        o_ref[rows, :] = x_new
        if n_exp:
            hp_ref[rows, :], meta_ref[:, rows] = _route_rows(x_new, m, wr_ref, br_ref, carry_ref, n_exp, region)

    nxt = project(0)
    pending = None
    for s in range(n):
        x, z = nxt
        if s + 1 < n:
            nxt = project(s + 1)
        u, v = activate(z)
        if pending is not None:
            finish(*pending)
        pending = (s, x, mix(s, u, v))
    finish(*pending)
    if n_exp:
        cnt_ref[...] = carry_ref[...]


def _gmlp_layer(x, mod, w_in, v_g, v_b, w_s, b_s, w_out, ln_g, ln_b, router=None, *, seq, alpha, tm=512,
                sub=256):
    t, d = x.shape
    groups, chunk, _ = w_s.shape
    width = w_out.shape[0]
    n_exp = router[0].shape[1] if router else 0
    kern = functools.partial(_gmlp_kernel, tm=tm, sub=sub, chunk=chunk, groups=groups, alpha=alpha, n_exp=n_exp,
                             region=t)
    row_map = lambda i: (i, 0)
    out_specs, out_shape = [pl.BlockSpec((tm, d), row_map)], [jax.ShapeDtypeStruct((t, d), F32)]
    scratch = [pltpu.VMEM((tm, width), BF16)]
    extra_specs, extra_args = [], []
    if router:
        r_specs, r_shapes = _router_out(t, d, tm, row_map)
        out_specs, out_shape = out_specs + r_specs, out_shape + r_shapes
        scratch.append(pltpu.VMEM((ROUTE_ROWS, LANES), F32))
        extra_specs = _router_specs(d, _const_spec)
        extra_args = list(_router_operands(*router))
    outs = pl.pallas_call(
        kern,
        grid=(t // tm,),
        in_specs=[
            pl.BlockSpec((tm, d), lambda i: (i, 0)),
            pl.BlockSpec((1, 6, d), lambda i: (i * tm // seq, 0, 0)),
            _const_spec((d, 2 * width)),
            _const_spec((1, width)),
            _const_spec((1, width)),
            _const_spec((groups, chunk, chunk)),
            _const_spec((chunk, groups)),
            _const_spec((width, d)),
            _const_spec((1, d)),
            _const_spec((1, d)),
        ] + extra_specs,
        out_specs=out_specs,
        out_shape=out_shape,
        scratch_shapes=scratch,
        compiler_params=_cparams(("arbitrary",)),
        name="gmlp_mixer",
    )(x, mod, w_in.astype(BF16), v_g.reshape(1, width), v_b.reshape(1, width), w_s, b_s.T,
      w_out.astype(BF16), ln_g.reshape(1, d), ln_b.reshape(1, d), *extra_args)
    return outs if router else outs[0]


def _swiglu_pipelined(hs, w1, w3, w2, emit):
    up = lambda h: (_dot(h, w1), _dot(h, w3))
    nxt = up(hs[0])
    for s in range(len(hs)):
        a, b = nxt
        if s + 1 < len(hs):
            nxt = up(hs[s + 1])
        t = (a / (1.0 + jnp.exp(-a)) * b).astype(BF16)
        emit(s, _dot(t, w2))


def _ffn_kernel(x_ref, mod_ref, w1_ref, w3_ref, w2_ref, g_ref, b_ref, o_ref, h_ref, *, alpha, sub):
    j = pl.program_id(1)
    last = pl.num_programs(1) - 1
    subs = [slice(s, s + sub) for s in range(0, x_ref.shape[0], sub)]
    weights = lambda: (w1_ref[...].astype(BF16), w3_ref[...].astype(BF16), w2_ref[...].astype(BF16))

    @pl.when(j == 0)
    def _():
        hs = []
        for rows in subs:
            h = _modulate(x_ref[rows, :], mod_ref[0], 3).astype(BF16)
            h_ref[rows, :] = h
            hs.append(h)

        def emit(s, part):
            o_ref[subs[s], :] = part

        _swiglu_pipelined(hs, *weights(), emit)

    @pl.when(jnp.logical_and(j > 0, j < last))
    def _():
        def emit(s, part):
            o_ref[subs[s], :] += part

        _swiglu_pipelined([h_ref[rows, :] for rows in subs], *weights(), emit)

    @pl.when(j == last)
    def _():
        def emit(s, part):
            rows = subs[s]
            y = o_ref[rows, :] + part
            o_ref[rows, :] = _res_ln(x_ref[rows, :], y, mod_ref[0], 3, g_ref[...], b_ref[...], alpha)

        _swiglu_pipelined([h_ref[rows, :] for rows in subs], *weights(), emit)


def _ffn_layer(x, mod, w13, w2, ln_g, ln_b, *, seq, alpha, tm=1024, tf=512):
    t, d = x.shape
    f = w2.shape[0]
    nf = f // tf
    assert nf >= 2
    return pl.pallas_call(
        functools.partial(_ffn_kernel, alpha=alpha, sub=tm // 4),
        grid=(t // tm, nf),
        in_specs=[
            pl.BlockSpec((tm, d), lambda i, j: (i, 0)),
            pl.BlockSpec((1, 6, d), lambda i, j: (i * tm // seq, 0, 0)),
            pl.BlockSpec((d, tf), lambda i, j: (0, j)),
            pl.BlockSpec((d, tf), lambda i, j: (0, nf + j)),
            pl.BlockSpec((tf, d), lambda i, j: (j, 0)),
            _const_spec((1, d)),
            _const_spec((1, d)),
        ],
        out_specs=pl.BlockSpec((tm, d), lambda i, j: (i, 0)),
        out_shape=jax.ShapeDtypeStruct((t, d), F32),
        scratch_shapes=[pltpu.VMEM((tm, d), BF16)],
        compiler_params=_cparams(("parallel", "arbitrary")),
        name="swiglu_dense",
    )(x, mod, w13, w13, w2, ln_g.reshape(1, d), ln_b.reshape(1, d))


MOE_GROUP = 2


def _moe_ffn_kernel(ge_ref, gb_ref, gr_ref, h_ref, w1_ref, w3_ref, w2_ref, o_ref, acc_ref, *, sub):
    g = pl.program_id(0)
    j = pl.program_id(1)
    last = pl.num_programs(1) - 1
    n_rows = gr_ref[g]
    nv = (n_rows + EXPERT_BLOCK - 1) // EXPERT_BLOCK

    def run(n_blocks, phase):
        subs = [slice(s, s + sub) for s in range(0, n_blocks * EXPERT_BLOCK, sub)]
        row = lax.broadcasted_iota(jnp.int32, (sub, 1), 0)

        def rows_in(rows):
            w = jnp.where(row + rows.start < n_rows, h_ref[rows, :], jnp.uint32(0))
            return _unpack_bf16_pair(w).astype(BF16)

        def emit(s, part):
            if phase == "first":
                acc_ref[subs[s], :] = part
            elif phase == "middle":
                acc_ref[subs[s], :] += part
            else:
                o_ref[subs[s], :] = _pack_bf16_pair(acc_ref[subs[s], :] + part)

        _swiglu_pipelined([rows_in(rows) for rows in subs], w1_ref[0, 0].astype(BF16), w3_ref[0, 0].astype(BF16),
                          w2_ref[0, 0].astype(BF16), emit)

    for n_blocks in range(1, MOE_GROUP + 1):
        @pl.when(jnp.logical_and(nv == n_blocks, j == 0))
        def _():
            run(n_blocks, "first")

        @pl.when(jnp.logical_and(nv == n_blocks, jnp.logical_and(j > 0, j < last)))
        def _():
            run(n_blocks, "middle")

        @pl.when(jnp.logical_and(nv == n_blocks, j == last))
        def _():
            run(n_blocks, "last")

    for blk in range(MOE_GROUP):
        @pl.when(jnp.logical_and(blk >= nv, j == last))
        def _():
            o_ref[blk * EXPERT_BLOCK:(blk + 1) * EXPERT_BLOCK, :] = jnp.zeros((EXPERT_BLOCK, o_ref.shape[1]),
                                                                              jnp.uint32)


def _moe_ffn(h_slots, group_expert, group_block, group_rows, w13, w2, layer, *, tf=512):
    half = h_slots.shape[1]
    d = 2 * half
    f = w2.shape[2]
    nf = f // tf
    assert nf >= 2
    tm = MOE_GROUP * EXPERT_BLOCK

    def jj(g, j, gr):
        return jnp.where(gr[g] > 0, j, nf - 1)

    grid_spec = pltpu.PrefetchScalarGridSpec(
        num_scalar_prefetch=3,
        grid=(group_expert.shape[0], nf),
        in_specs=[
            pl.BlockSpec((tm, half), lambda g, j, ge, gb, gr: (gb[g], 0)),
            pl.BlockSpec((1, 1, d, tf), lambda g, j, ge, gb, gr: (layer, ge[g], 0, jj(g, j, gr))),
            pl.BlockSpec((1, 1, d, tf), lambda g, j, ge, gb, gr: (layer, ge[g], 0, nf + jj(g, j, gr))),
            pl.BlockSpec((1, 1, tf, d), lambda g, j, ge, gb, gr: (layer, ge[g], jj(g, j, gr), 0)),
        ],
        out_specs=pl.BlockSpec((tm, half), lambda g, j, ge, gb, gr: (gb[g], 0)),
        scratch_shapes=[pltpu.VMEM((tm, d), F32)],
    )
    return pl.pallas_call(
        functools.partial(_moe_ffn_kernel, sub=EXPERT_BLOCK // 2),
        grid_spec=grid_spec,
        out_shape=jax.ShapeDtypeStruct(h_slots.shape, jnp.uint32),
        compiler_params=_cparams(("arbitrary", "arbitrary")),
        name="swiglu_experts",
    )(group_expert, group_block, group_rows, h_slots, w13, w13, w2)


ROUTE_ROWS = 16


def _route_rows(x_rows, m, wrt_ref, brt_ref, carry_ref, n_exp, region):
    tm = x_rows.shape[0]
    h = _modulate(x_rows, m, 3)
    h_hi, h_mid, _ = _split3(h)
    w = wrt_ref[...]
    w_hi = w.astype(BF16)
    w_lo = (w - w_hi.astype(F32)).astype(BF16)
    nt = (((1,), (1,)), ((), ()))
    logits = (lax.dot_general(w_hi, h_hi, nt, preferred_element_type=F32)
              + lax.dot_general(w_lo, h_hi, nt, preferred_element_type=F32)
              + lax.dot_general(w_hi, h_mid, nt, preferred_element_type=F32) + brt_ref[...])
    erow = lax.broadcasted_iota(jnp.int32, (ROUTE_ROWS, tm), 0)
    neg = jnp.float32(-jnp.inf)
    logits = jnp.where(erow < n_exp, logits, neg)
    v0 = jnp.max(logits, axis=0, keepdims=True)
    e0 = jnp.min(jnp.where(logits == v0, erow, ROUTE_ROWS), axis=0, keepdims=True)
    rest = jnp.where(erow == e0, neg, logits)
    v1 = jnp.max(rest, axis=0, keepdims=True)
    e1 = jnp.min(jnp.where(rest == v1, erow, ROUTE_ROWS), axis=0, keepdims=True)
    p = jnp.exp(v1 - v0)
    g0 = 1.0 / (1.0 + p)
    g1 = p / (1.0 + p)
    oh0 = erow == e0
    oh1 = erow == e1
    onehot = jnp.where(jnp.logical_or(oh0, oh1), 1.0, 0.0).astype(BF16)
    r = lax.broadcasted_iota(jnp.int32, (tm, tm), 0)
    c = lax.broadcasted_iota(jnp.int32, (tm, tm), 1)
    earlier = jnp.where(r < c, 1.0, 0.0).astype(BF16)
    carry = carry_ref[:, 0:1]
    before = _dot(onehot, earlier) + carry
    rank0 = jnp.sum(jnp.where(oh0, before, 0.0), axis=0, keepdims=True)
    rank1 = jnp.sum(jnp.where(oh1, before, 0.0), axis=0, keepdims=True)
    total = carry + jnp.sum(onehot.astype(F32), axis=1, keepdims=True)
    carry_ref[...] = jnp.broadcast_to(total, carry_ref.shape)
    mrow = lax.broadcasted_iota(jnp.int32, (SUBLANES, tm), 0)
    e0f = e0.astype(F32)
    e1f = e1.astype(F32)
    meta = jnp.where(mrow == 0, e0f, 0.0)
    meta = jnp.where(mrow == 1, e1f, meta)
    meta = jnp.where(mrow == 2, g0, meta)
    meta = jnp.where(mrow == 3, g1, meta)
    meta = jnp.where(mrow == 4, e0f * region + rank0, meta)
    meta = jnp.where(mrow == 5, e1f * region + rank1, meta)
    return _pack_bf16_pair(h), meta


def _router_operands(w_router, b_router):
    d, n_exp = w_router.shape
    wrt = jnp.zeros((ROUTE_ROWS, d), F32).at[:n_exp].set(w_router.T)
    brt = jnp.zeros((ROUTE_ROWS, 1), F32).at[:n_exp, 0].set(b_router)
    return wrt, brt


def _router_specs(d, const):
    return [const((ROUTE_ROWS, d)), const((ROUTE_ROWS, 1))]


def _router_out(t, d, tm, row_map):
    specs = [pl.BlockSpec((tm, d // 2), row_map),
             pl.BlockSpec((SUBLANES, tm), lambda *a: (0, row_map(*a)[0])),
             pl.BlockSpec((ROUTE_ROWS, LANES), lambda *_: (0, 0))]
    shapes = [jax.ShapeDtypeStruct((t, d // 2), jnp.uint32), jax.ShapeDtypeStruct((SUBLANES, t), F32),
              jax.ShapeDtypeStruct((ROUTE_ROWS, LANES), F32)]
    return specs, shapes


def _combine_kernel(x_ref, ya_ref, yb_ref, meta_ref, mod_ref, g_ref, b_ref, o_ref, *, alpha):
    record = meta_ref[...]
    cols = jnp.concatenate([record, jnp.zeros((LANES - record.shape[0], record.shape[1]), F32)], axis=0).T
    y = cols[:, 2:3] * _unpack_bf16_pair(ya_ref[...]) + cols[:, 3:4] * _unpack_bf16_pair(yb_ref[...])
    o_ref[...] = _res_ln(x_ref[...], y, mod_ref[0], 3, g_ref[...], b_ref[...], alpha)


def _combine(x, ya, yb, meta, mod, ln_g, ln_b, *, seq, alpha, tm=1024):
    t, d = x.shape
    row = pl.BlockSpec((tm, d), lambda i: (i, 0))
    packed = pl.BlockSpec((tm, d // 2), lambda i: (i, 0))
    return pl.pallas_call(
        functools.partial(_combine_kernel, alpha=alpha),
        grid=(t // tm,),
        in_specs=[row, packed, packed,
                  pl.BlockSpec((SUBLANES, tm), lambda i: (0, i)),
                  pl.BlockSpec((1, 6, d), lambda i: (i * tm // seq, 0, 0)),
                  _const_spec((1, d)), _const_spec((1, d))],
        out_specs=row,
        out_shape=jax.ShapeDtypeStruct((t, d), F32),
        compiler_params=_cparams(("parallel",)),
        name="moe_combine",
    )(x, ya, yb, meta, mod, ln_g.reshape(1, d), ln_b.reshape(1, d))


def _sc_mesh():
    return plsc.VectorSubcoreMesh(core_axis_name="c", subcore_axis_name="s", num_cores=SC_CORES,
                                  num_subcores=SC_SUBCORES)


def _sc_worker_share(n):
    workers = SC_CORES * SC_SUBCORES
    per_worker = n // workers
    steps = per_worker // SC_ROWS
    assert steps * SC_ROWS * workers == n
    return per_worker, steps


def _sc_gather_rows(table, idx_a, idx_b):
    n = idx_a.shape[0]
    width = table.shape[1]
    per_worker, steps = _sc_worker_share(n)

    def body(table_hbm, ia_hbm, ib_hbm, oa_hbm, ob_hbm, ia_v, ib_v, rows_v, sem):
        wid = lax.axis_index("s") * SC_CORES + lax.axis_index("c")
        base = pl.multiple_of(wid * per_worker, SC_ROWS)
        pltpu.sync_copy(ia_hbm.at[pl.ds(base, per_worker)], ia_v)
        pltpu.sync_copy(ib_hbm.at[pl.ds(base, per_worker)], ib_v)

        @pl.loop(0, steps)
        def _(j):
            lo = pl.multiple_of(j * SC_ROWS, SC_ROWS)
            for idx_v, out_hbm in ((ia_v, oa_hbm), (ib_v, ob_hbm)):
                pltpu.async_copy(table_hbm.at[idx_v.at[pl.ds(lo, SC_ROWS)]], rows_v, sem).wait()
                pltpu.sync_copy(rows_v, out_hbm.at[pl.ds(base + lo, SC_ROWS)])

    out = jax.ShapeDtypeStruct((n, width), table.dtype)
    return pl.kernel(
        body,
        out_type=(out, out),
        mesh=_sc_mesh(),
        scratch_types=[
            pltpu.VMEM((per_worker,), jnp.int32),
            pltpu.VMEM((per_worker,), jnp.int32),
            pltpu.VMEM((SC_ROWS, width), table.dtype),
            pltpu.SemaphoreType.DMA,
        ],
        name="sc_row_gather",
    )(table, idx_a, idx_b)


def _sc_scatter_rows(rows, dest_a, dest_b, n_out):
    n, width = rows.shape
    per_worker, steps = _sc_worker_share(n)

    def body(rows_hbm, da_hbm, db_hbm, out_hbm, ia_v, ib_v, rows_v, sem):
        wid = lax.axis_index("s") * SC_CORES + lax.axis_index("c")

        @pl.loop(0, steps)
        def _(j):
            off = pl.multiple_of(wid * per_worker + j * SC_ROWS, SC_ROWS)
            pltpu.sync_copy(da_hbm.at[pl.ds(off, SC_ROWS)], ia_v)
            pltpu.sync_copy(db_hbm.at[pl.ds(off, SC_ROWS)], ib_v)
            pltpu.sync_copy(rows_hbm.at[pl.ds(off, SC_ROWS)], rows_v)
            pltpu.async_copy(rows_v, out_hbm.at[ia_v], sem).wait()
            pltpu.async_copy(rows_v, out_hbm.at[ib_v], sem).wait()

    return pl.kernel(
        body,
        out_type=jax.ShapeDtypeStruct((n_out, width), rows.dtype),
        mesh=_sc_mesh(),
        scratch_types=[
            pltpu.VMEM((SC_ROWS,), jnp.int32),
            pltpu.VMEM((SC_ROWS,), jnp.int32),
            pltpu.VMEM((SC_ROWS, width), rows.dtype),
            pltpu.SemaphoreType.DMA,
        ],
        name="sc_row_scatter",
    )(rows, dest_a, dest_b)


def _moe_layer(routed, mod, w13, w2, layer, ln_g, ln_b, *, seq, alpha):
    x, h, meta, cnt = routed
    t, d = x.shape
    n_exp = w13.shape[1]
    dest0 = meta[4].astype(jnp.int32)
    dest1 = meta[5].astype(jnp.int32)
    counts = cnt[:n_exp, 0].astype(jnp.int32)
    group_rows = MOE_GROUP * EXPERT_BLOCK
    assert t % group_rows == 0
    region_groups = t // group_rows
    groups_per_expert = (counts + group_rows - 1) // group_rows
    group_end = jnp.cumsum(groups_per_expert)
    g = jnp.arange(t * TOP_K // group_rows + n_exp, dtype=jnp.int32)
    expert = jnp.minimum(jnp.sum(group_end[None, :] <= g[:, None], axis=1), n_exp - 1).astype(jnp.int32)
    blk_in_expert = g - (group_end - groups_per_expert)[expert]
    used = g < group_end[-1]
    spare_block = n_exp * region_groups
    group_block = jnp.where(used, expert * region_groups + blk_in_expert, spare_block).astype(jnp.int32)
    rows_in_group = jnp.where(used, jnp.clip(counts[expert] - blk_in_expert * group_rows, 0, group_rows),
                              0).astype(jnp.int32)
    h_slots = _sc_scatter_rows(h, dest0, dest1, (spare_block + 1) * group_rows)
    y_slots = _moe_ffn(h_slots, expert, group_block, rows_in_group, w13, w2, layer)
    ya, yb = _sc_gather_rows(y_slots, dest0, dest1)
    return _combine(x, ya, yb, meta, mod, ln_g, ln_b, seq=seq, alpha=alpha)


HEAD_PAD = LANES
GATE_PARTS = 3
GATE_ROWS = 16
VAL_PAD = 16
LOG2E = 1.4426950408889634


def _head_lanes(h, hd):
    k0 = (h % (HEAD_PAD // hd)) * hd
    return k0, (hd if k0 == 0 else 0)


def _fox_proj_kernel(x_ref, mod_ref, wf_ref, bf_ref, wq_ref, wk_ref, wv_ref, selk_ref, onek_ref, selq_ref,
                     oneq_ref, onev_ref, k_ref, q_ref, v_ref, carry_ref, *, q_scale, heads, hd, nb):
    tm = x_ref.shape[0]

    @pl.when(pl.program_id(0) % nb == 0)
    def _():
        carry_ref[...] = jnp.zeros_like(carry_ref)

    h_hi, h_mid, _ = _split3(_modulate(x_ref[...], mod_ref[0], 0))

    f = _dot_split(h_hi, h_mid, wf_ref[...]) + bf_ref[...]
    lf = jnp.minimum(f, 0.0) - jnp.log(1.0 + jnp.exp(-jnp.abs(f)))
    r = lax.broadcasted_iota(jnp.int32, (tm, tm), 0)
    c = lax.broadcasted_iota(jnp.int32, (tm, tm), 1)
    tri = jnp.where(r >= c, 1.0, 0.0).astype(BF16)
    sums = _dot(tri, jnp.concatenate(_split3(lf), axis=1))
    cs = carry_ref[0:1, :] + sum(sums[:, n * LANES:(n + 1) * LANES] for n in range(GATE_PARTS))
    carry_ref[...] = jnp.broadcast_to(cs[tm - 1:tm, :], carry_ref.shape)
    cs = cs * LOG2E

    gate_k = onek_ref[...] + _dot(jnp.concatenate(_split3(cs), axis=1), selk_ref[...])
    kc = _dot(h_hi, wk_ref[...])
    lane = lax.broadcasted_iota(jnp.int32, (tm, HEAD_PAD), 1)
    for h in range(heads):
        k0, _ = _head_lanes(h, hd)
        src = (h * hd // HEAD_PAD) * HEAD_PAD
        is_k = jnp.logical_and(lane >= k0, lane < k0 + hd)
        hs = slice(h * HEAD_PAD, (h + 1) * HEAD_PAD)
        k_ref[:, hs] = jnp.where(is_k, kc[:, src:src + HEAD_PAD], gate_k[:, hs]).astype(BF16)

    nt = (((1,), (1,)), ((), ()))
    qc = lax.dot_general(wq_ref[...], h_hi, nt, preferred_element_type=F32) * (q_scale * LOG2E)
    gate_q = oneq_ref[...] + _dot(selq_ref[...], jnp.concatenate(_split3(cs.T), axis=0))
    fill = HEAD_PAD - hd - GATE_ROWS
    for h in range(heads):
        k0, g0 = _head_lanes(h, hd)
        base = h * HEAD_PAD
        q_ref[0, base + k0:base + k0 + hd, :] = qc[h * hd:(h + 1) * hd, :].astype(BF16)
        q_ref[0, base + g0:base + g0 + GATE_ROWS, :] = gate_q[h * GATE_ROWS:(h + 1) * GATE_ROWS, :].astype(BF16)
        q_ref[0, base + g0 + GATE_ROWS:base + g0 + GATE_ROWS + fill, :] = jnp.zeros((fill, tm), BF16)

    v = lax.dot_general(wv_ref[...], h_hi, nt, preferred_element_type=F32) + onev_ref[...]
    v_ref[0] = v.astype(BF16)


def _fox_proj(x, mod, w_in, b_f, *, bsz, seq, heads, tm=512):
    t, d = x.shape
    hd = d // heads
    assert HEAD_PAD == 2 * hd and GATE_ROWS >= 2 * GATE_PARTS
    nb = seq // tm
    wide = heads * HEAD_PAD
    wq_t = w_in[:, :d].T.astype(BF16)
    wk = w_in[:, d:2 * d].astype(BF16)
    vrows = heads * (hd + VAL_PAD)
    wv_t = jnp.pad(w_in[:, 2 * d:3 * d].reshape(d, heads, hd),
                   ((0, 0), (0, 0), (0, VAL_PAD))).reshape(d, vrows).T.astype(BF16)
    wf = jnp.zeros((d, LANES), F32).at[:, :heads].set(w_in[:, 3 * d:])
    bf = jnp.zeros((1, LANES), F32).at[0, :heads].set(b_f)
    sel_k = np.zeros((GATE_PARTS, LANES, wide), np.float32)
    one_k = np.zeros((1, wide), np.float32)
    sel_q = np.zeros((GATE_PARTS, heads * GATE_ROWS, LANES), np.float32)
    one_q = np.zeros((heads * GATE_ROWS, 1), np.float32)
    one_v = np.zeros((vrows, 1), np.float32)
    for hh in range(heads):
        _, g0 = _head_lanes(hh, hd)
        one_v[hh * (hd + VAL_PAD) + hd, 0] = 1.0
        for n in range(GATE_PARTS):
            sel_k[n, hh, hh * HEAD_PAD + g0 + n] = -1.0
            one_q[hh * GATE_ROWS + n, 0] = 1.0
            one_k[0, hh * HEAD_PAD + g0 + GATE_PARTS + n] = 1.0
            sel_q[n, hh * GATE_ROWS + GATE_PARTS + n, hh] = 1.0
    return pl.pallas_call(
        functools.partial(_fox_proj_kernel, q_scale=hd ** -0.5, heads=heads, hd=hd, nb=nb),
        grid=(t // tm,),
        in_specs=[
            pl.BlockSpec((tm, d), lambda i: (i, 0)),
            pl.BlockSpec((1, 6, d), lambda i: (i * tm // seq, 0, 0)),
            _const_spec((d, LANES)),
            _const_spec((1, LANES)),
            _const_spec((d, d)),
            _const_spec((d, d)),
            _const_spec((vrows, d)),
            _const_spec((GATE_PARTS * LANES, wide)),
            _const_spec((1, wide)),
            _const_spec((heads * GATE_ROWS, GATE_PARTS * LANES)),
            _const_spec((heads * GATE_ROWS, 1)),
            _const_spec((vrows, 1)),
        ],
        out_specs=[
            pl.BlockSpec((tm, wide), lambda i: (i, 0)),
            pl.BlockSpec((1, wide, tm), lambda i: (i // nb, 0, i % nb)),
            pl.BlockSpec((1, vrows, tm), lambda i: (i // nb, 0, i % nb)),
        ],
        out_shape=[
            jax.ShapeDtypeStruct((t, wide), BF16),
            jax.ShapeDtypeStruct((bsz, wide, seq), BF16),
            jax.ShapeDtypeStruct((bsz, vrows, seq), BF16),
        ],
        scratch_shapes=[pltpu.VMEM((SUBLANES, LANES), F32)],
        compiler_params=_cparams(("arbitrary",)),
        name="fox_proj",
    )(x, mod, wf, bf, wq_t, wk, wv_t, jnp.asarray(sel_k.reshape(GATE_PARTS * LANES, wide), BF16), jnp.asarray(one_k),
      jnp.asarray(np.concatenate(list(sel_q), axis=1), BF16), jnp.asarray(one_q), jnp.asarray(one_v))


def _flash_kernel(qi_ref, kj_ref, k_ref, q_ref, v_ref, x_ref, mod_ref, wout_ref, g_ref, b_ref, *rest,
                  heads, hd, blk, alpha, n_exp, region):
    p_idx = pl.program_id(1)
    if n_exp:
        wr_ref, br_ref, o_ref, hp_ref, meta_ref, cnt_ref, m_ref, acc_ref, s_ref, carry_ref = rest

        @pl.when(jnp.logical_and(pl.program_id(0) == 0, p_idx == 0))
        def _():
            carry_ref[...] = jnp.zeros_like(carry_ref)
    else:
        o_ref, m_ref, acc_ref, s_ref = rest
    qi = qi_ref[p_idx]
    kj = kj_ref[p_idx]
    vp = hd + VAL_PAD

    @pl.when(kj == 0)
    def _():
        m_ref[...] = jnp.full(m_ref.shape, -jnp.inf, F32)
        acc_ref[...] = jnp.zeros_like(acc_ref)

    def step(diagonal):
        half = blk // 2
        tiles = [(slice(0, half), slice(0, half)), (slice(half, blk), slice(0, blk))] if diagonal else \
                [(slice(0, blk), slice(0, blk))]
        if diagonal:
            causal = lambda rows, first_q: (lax.broadcasted_iota(jnp.int32, (rows, half), 0)
                                            <= lax.broadcasted_iota(jnp.int32, (rows, half), 1) + first_q)
            keep = [causal(half, 0), causal(blk, half)]

        def scores(h):
            hs = slice(h * HEAD_PAD, (h + 1) * HEAD_PAD)
            tops = []
            for n, (qs, ks) in enumerate(tiles):
                s = _dot(k_ref[ks, hs], q_ref[0, hs, qs])
                if diagonal:
                    s = jnp.where(keep[n], s, -jnp.inf)
                s_ref[h % depth, ks, qs] = s
                tops.append(jnp.max(s, axis=0, keepdims=True))
            return tops

        depth = s_ref.shape[0]
        ahead = [scores(h) for h in range(depth - 1)]
        for h in range(heads):
            if h + depth - 1 < heads:
                ahead.append(scores(h + depth - 1))
            rs = slice(h * vp, (h + 1) * vp)
            for (qs, ks), m_cur in zip(tiles, ahead.pop(0)):
                m_prev = m_ref[h:h + 1, qs]
                m_new = jnp.maximum(m_prev, m_cur)
                m_ref[h:h + 1, qs] = m_new
                p = jnp.exp2(s_ref[h % depth, ks, qs] - m_new).astype(BF16)
                acc_ref[rs, qs] = acc_ref[rs, qs] * jnp.exp2(m_prev - m_new) + _dot(v_ref[0, rs, ks], p)

    @pl.when(kj < qi)
    def _():
        step(False)

    @pl.when(kj == qi)
    def _():
        step(True)
        per = LANES // hd
        cols = []
        for g in range(heads // per):
            parts = []
            for n in range(per):
                r0 = (g * per + n) * vp
                parts.append(acc_ref[r0:r0 + hd, :] / acc_ref[r0 + hd:r0 + hd + 1, :])
            cols.append(jnp.concatenate(parts, axis=0).T.astype(BF16))
        y = _dot(jnp.concatenate(cols, axis=1), wout_ref[...])
        x_new = _res_ln(x_ref[...], y, mod_ref[0], 0, g_ref[...], b_ref[...], alpha)
        o_ref[...] = x_new
        if n_exp:
            hp_ref[...], meta_ref[...] = _route_rows(x_new, mod_ref[0], wr_ref, br_ref, carry_ref, n_exp, region)
            cnt_ref[...] = carry_ref[...]


def _flash(k_aug, q_aug_t, v_t, x, mod, w_out, ln_g, ln_b, router=None, *, bsz, seq, heads, alpha, blk=512):
    t, wide = k_aug.shape
    vrows = v_t.shape[1]
    hd = vrows // heads - VAL_PAD
    d = heads * hd
    nq = seq // blk
    pairs = [(i, j) for i in range(nq) for j in range(i + 1)]
    qi = jnp.asarray(np.array([p[0] for p in pairs], np.int32))
    kj = jnp.asarray(np.array([p[1] for p in pairs], np.int32))
    n_exp = router[0].shape[1] if router else 0
    row_map = lambda b, p, qi, kj: (b * nq + qi[p], 0)
    const = lambda shape: pl.BlockSpec(shape, lambda b, p, qi, kj: (0, 0), pipeline_mode=pl.Buffered(1))
    out_specs, out_shape = [pl.BlockSpec((blk, d), row_map)], [jax.ShapeDtypeStruct((t, d), F32)]
    scratch = [pltpu.VMEM((heads, blk), F32), pltpu.VMEM((vrows, blk), F32), pltpu.VMEM((3, blk, blk), F32)]
    extra_specs, extra_args = [], []
    if router:
        r_specs, r_shapes = _router_out(t, d, blk, row_map)
        out_specs, out_shape = out_specs + r_specs, out_shape + r_shapes
        scratch.append(pltpu.VMEM((ROUTE_ROWS, LANES), F32))
        extra_specs = _router_specs(d, const)
        extra_args = list(_router_operands(*router))
    grid_spec = pltpu.PrefetchScalarGridSpec(
        num_scalar_prefetch=2,
        grid=(bsz, len(pairs)),
        in_specs=[
            pl.BlockSpec((blk, wide), lambda b, p, qi, kj: (b * nq + kj[p], 0)),
            pl.BlockSpec((1, wide, blk), lambda b, p, qi, kj: (b, 0, qi[p])),
            pl.BlockSpec((1, vrows, blk), lambda b, p, qi, kj: (b, 0, kj[p])),
            pl.BlockSpec((blk, d), row_map),
            pl.BlockSpec((1, 6, d), lambda b, p, qi, kj: (b, 0, 0)),
            const((d, d)),
            const((1, d)),
            const((1, d)),
        ] + extra_specs,
        out_specs=out_specs,
        scratch_shapes=scratch,
    )
    outs = pl.pallas_call(
        functools.partial(_flash_kernel, heads=heads, hd=hd, blk=blk, alpha=alpha, n_exp=n_exp, region=t),
        grid_spec=grid_spec,
        out_shape=out_shape,
        compiler_params=_cparams(("arbitrary", "arbitrary")),
        name="fox_attention",
    )(qi, kj, k_aug, q_aug_t, v_t, x, mod, w_out.astype(BF16), ln_g.reshape(1, d), ln_b.reshape(1, d), *extra_args)
    return outs if router else outs[0]


def _fox_layer(x, mod, w_in, b_f, w_out, ln_g, ln_b, router=None, *, bsz, seq, alpha):
    heads = b_f.shape[0]
    k_aug, q_aug_t, v_t = _fox_proj(x, mod, w_in, b_f, bsz=bsz, seq=seq, heads=heads)
    return _flash(k_aug, q_aug_t, v_t, x, mod, w_out, ln_g, ln_b, router, bsz=bsz, seq=seq, heads=heads,
                  alpha=alpha)


def _pool_kernel(x_ref, xh_ref, mod_ref, win_ref, wg_ref, ls_ref, wout_ref, g_ref, b_ref, o_ref, pooled_ref,
                 *, tm, sub, seq, alpha):
    pos0 = (pl.program_id(0) * tm) % seq
    m = mod_ref[0]
    n = tm // sub
    halo_ok = jnp.where(pos0 > 0, jnp.float32(1.0), jnp.float32(0.0))
    rows = lax.broadcasted_iota(jnp.int32, (sub + POOL_HALO, 1), 0)

    def project(s):
        x = x_ref[s * sub:(s + 1) * sub, :]
        halo = xh_ref[...] if s == 0 else x_ref[s * sub - POOL_HALO:s * sub, :]
        ze = _dot(_modulate(jnp.concatenate([halo, x], axis=0), m, 0).astype(BF16), win_ref[...])
        if s == 0:
            ze = ze * jnp.where(rows < POOL_HALO, halo_ok, jnp.float32(1.0))
        return x, ze

    def pool(s, ze):
        pos = (pos0 + s * sub + lax.broadcasted_iota(jnp.int32, (sub, 1), 0) + 1).astype(F32)
        gd = ze.shape[1] // len(POOL_WINDOWS)
        out_rows = slice(s * sub, (s + 1) * sub)
        for g, win in enumerate(POOL_WINDOWS):
            cs = slice(g * gd, (g + 1) * gd)
            zg = ze[:, cs]
            acc = zg
            span = 1
            while span < win:
                acc = acc + pltpu.roll(acc, span, 0)
                span *= 2
            count = jnp.minimum(pos, jnp.float32(win))
            pooled = acc[POOL_HALO:, :] / count - zg[POOL_HALO:, :]
            pooled_ref[out_rows, cs] = (_dot(pooled.astype(BF16), wg_ref[g]) * ls_ref[:, cs]).astype(BF16)
        return _dot(pooled_ref[out_rows, :], wout_ref[...])

    nxt = project(0)
    for s in range(n):
        x, ze = nxt
        if s + 1 < n:
            nxt = project(s + 1)
        y = pool(s, ze)
        o_ref[s * sub:(s + 1) * sub, :] = _res_ln(x, y, m, 0, g_ref[...], b_ref[...], alpha)


def _pool_layer(x, mod, w_in, w_grp, scale, w_out, ln_g, ln_b, *, seq, alpha, tm=1024, sub=256):
    t, d = x.shape
    ng, gd, _ = w_grp.shape
    hb = tm // POOL_HALO
    return pl.pallas_call(
        functools.partial(_pool_kernel, tm=tm, sub=sub, seq=seq, alpha=alpha),
        grid=(t // tm,),
        in_specs=[
            pl.BlockSpec((tm, d), lambda i: (i, 0)),
            pl.BlockSpec((POOL_HALO, d), lambda i: (jnp.maximum(i * hb - 1, 0), 0)),
            pl.BlockSpec((1, 6, d), lambda i: (i * tm // seq, 0, 0)),
            _const_spec((d, d)),
            _const_spec((ng, gd, gd)),
            _const_spec((1, d)),
            _const_spec((d, d)),
            _const_spec((1, d)),
            _const_spec((1, d)),
        ],
        out_specs=pl.BlockSpec((tm, d), lambda i: (i, 0)),
        out_shape=jax.ShapeDtypeStruct((t, d), F32),
        scratch_shapes=[pltpu.VMEM((tm, d), BF16)],
        compiler_params=_cparams(("parallel",)),
        name="pool_mixer",
    )(x, x, mod, w_in.astype(BF16), w_grp.astype(BF16), scale.reshape(1, d), w_out.astype(BF16),
      ln_g.reshape(1, d), ln_b.reshape(1, d))


def kernel(x, c, mod_w, mod_b, ln_g, ln_b, gm_w_in, gm_ln_g, gm_ln_b, gm_w_s, gm_b_s, gm_w_out, fox_w_in,
           fox_b_f, fox_w_out, pool_w_in, pool_w_grp, pool_scale, pool_w_out, ffn_w13, ffn_w2, moe_w_router,
           moe_b_router, moe_w13, moe_w2):
    bsz, seq, d = x.shape
    depth = mod_w.shape[0]
    alpha = (2 * depth) ** 0.25
    n_mixers = 3
    mod_all = _modulation(c, mod_w, mod_b)
    xt = x.reshape(bsz * seq, d)
    for i in range(depth):
        mod = mod_all[i]
        kind, j = i % n_mixers, i // n_mixers
        router = (moe_w_router[i // 2], moe_b_router[i // 2]) if i % 2 else None
        if kind == 0:
            xt = _gmlp_layer(xt, mod, gm_w_in[j], gm_ln_g[j], gm_ln_b[j], gm_w_s[j], gm_b_s[j], gm_w_out[j],
                             ln_g[i, 0], ln_b[i, 0], router, seq=seq, alpha=alpha)
        elif kind == 1:
            xt = _fox_layer(xt, mod, fox_w_in[j], fox_b_f[j], fox_w_out[j], ln_g[i, 0], ln_b[i, 0], router,
                            bsz=bsz, seq=seq, alpha=alpha)
        else:
            assert router is None, "the pooling mixer has no routing epilogue"
            xt = _pool_layer(xt, mod, pool_w_in[j], pool_w_grp[j], pool_scale[j], pool_w_out[j],
                             ln_g[i, 0], ln_b[i, 0], seq=seq, alpha=alpha)
        if router is None:
            xt = _ffn_layer(xt, mod, ffn_w13[i // 2], ffn_w2[i // 2], ln_g[i, 1], ln_b[i, 1],
                            seq=seq, alpha=alpha)
        else:
            xt = _moe_layer(xt, mod, moe_w13, moe_w2, i // 2, ln_g[i, 1], ln_b[i, 1], seq=seq, alpha=alpha)
    return xt.reshape(bsz, seq, d)
```

```python
import functools

import numpy as np
import jax
import jax.numpy as jnp
from jax import lax
from jax.experimental import pallas as pl
from jax.experimental.pallas import tpu as pltpu
from jax.experimental.pallas import tpu_sc as plsc

F32 = jnp.float32
BF16 = jnp.bfloat16

POOL_WINDOWS = (2, 4, 8, 16)
TOP_K = 2
EXPERT_BLOCK = 512
LN_EPS = 1e-5
LANES = 128
SUBLANES = 8
POOL_HALO = 16
VMEM_LIMIT = 56 * 1024 * 1024
SC_CORES = 2
SC_SUBCORES = 16
SC_ROWS = 128


def _cparams(sem):
    return pltpu.CompilerParams(dimension_semantics=sem, vmem_limit_bytes=VMEM_LIMIT)


def _const_spec(shape):
    nd = len(shape)
    return pl.BlockSpec(shape, lambda *_: (0,) * nd, pipeline_mode=pl.Buffered(1))


def _layer_norm(r, g, b):
    mu = jnp.mean(r, axis=-1, keepdims=True)
    xc = r - mu
    var = jnp.mean(xc * xc, axis=-1, keepdims=True)
    return xc * lax.rsqrt(var + LN_EPS) * g + b


def _modulate(x, m, off):
    return x * (1.0 + m[off + 1:off + 2]) + m[off:off + 1]


def _res_ln(x, y, m, off, g, b, alpha):
    return _layer_norm(alpha * x + (1.0 + m[off + 2:off + 3]) * y, g, b)


def _split3(a):
    hi = a.astype(BF16)
    r1 = a - hi.astype(F32)
    mid = r1.astype(BF16)
    lo = (r1 - mid.astype(F32)).astype(BF16)
    return hi, mid, lo


def _dot(a, b):
    return jnp.dot(a, b, preferred_element_type=F32)


def _dot_split(a_hi, a_mid, w):
    n = w.shape[1]
    w_hi = w.astype(BF16)
    w_lo = (w - w_hi.astype(F32)).astype(BF16)
    both = _dot(a_hi, jnp.concatenate([w_hi, w_lo], axis=1))
    return both[:, :n] + both[:, n:] + _dot(a_mid, w_hi)


def _pack_bf16_pair(x):
    n = x.shape[1] // 2
    hi = lax.bitcast_convert_type(x[:, :n].astype(BF16).astype(F32), jnp.uint32)
    lo = lax.bitcast_convert_type(x[:, n:].astype(BF16).astype(F32), jnp.uint32)
    return hi | (lo >> 16)


def _unpack_bf16_pair(w):
    hi = lax.bitcast_convert_type(w & jnp.uint32(0xFFFF0000), F32)
    lo = lax.bitcast_convert_type(w << 16, F32)
    return jnp.concatenate([hi, lo], axis=1)


def _mod_kernel(c_ref, w_ref, b_ref, o_ref):
    c = c_ref[...]
    s = c / (1.0 + jnp.exp(-c))
    s_hi, s_mid, _ = _split3(s)
    w = w_ref[0]
    w_hi = w.astype(BF16)
    w_lo = (w - w_hi.astype(F32)).astype(BF16)
    acc = _dot(s_hi, w_hi) + _dot(s_mid, w_hi) + _dot(s_hi, w_lo)
    o_ref[0] = acc + b_ref[0]


def _modulation(c, mod_w, mod_b):
    depth, d, n = mod_w.shape
    bsz = c.shape[0]
    rows = SUBLANES
    tn = n // 2
    c_pad = jnp.zeros((rows, d), F32).at[:bsz].set(c)
    out = pl.pallas_call(
        _mod_kernel,
        grid=(depth, n // tn),
        in_specs=[
            pl.BlockSpec((rows, d), lambda l, j: (0, 0)),
            pl.BlockSpec((1, d, tn), lambda l, j: (l, 0, j)),
            pl.BlockSpec((1, 1, tn), lambda l, j: (l, 0, j)),
        ],
        out_specs=pl.BlockSpec((1, rows, tn), lambda l, j: (l, 0, j)),
        out_shape=jax.ShapeDtypeStruct((depth, rows, n), F32),
        compiler_params=_cparams(("parallel", "parallel")),
        name="adaln_mod",
    )(c_pad, mod_w, mod_b.reshape(depth, 1, n))
    return out[:, :bsz, :].reshape(depth, bsz, 6, d)


def _gmlp_kernel(x_ref, mod_ref, win_ref, vg_ref, vb_ref, ws_ref, bst_ref, wout_ref, g_ref, b_ref,
                 *rest, tm, sub, chunk, groups, alpha, n_exp, region):
    if n_exp:
        wr_ref, br_ref, o_ref, hp_ref, meta_ref, cnt_ref, gated_ref, carry_ref = rest

        @pl.when(pl.program_id(0) == 0)
        def _():
            carry_ref[...] = jnp.zeros_like(carry_ref)
    else:
        o_ref, gated_ref = rest
    m = mod_ref[0]
    row = lax.broadcasted_iota(jnp.int32, (chunk, chunk), 0)
    col = lax.broadcasted_iota(jnp.int32, (chunk, chunk), 1)
    causal = row >= col
    bst = bst_ref[...]
    w_mix = [jnp.where(causal, ws_ref[g], 0.0).astype(BF16) for g in range(groups)]
    n = tm // sub

    def project(s):
        x = x_ref[s * sub:(s + 1) * sub, :]
        return x, _dot(_modulate(x, m, 0).astype(BF16), win_ref[...])

    def activate(z):
        z = 0.5 * z * (1.0 + lax.erf(z * (2.0 ** -0.5)))
        width = z.shape[1] // 2
        return z[:, :width], _layer_norm(z[:, width:], vg_ref[...], vb_ref[...]).astype(BF16)

    def mix(s, u, v):
        gd = u.shape[1] // groups
        for g in range(groups):
            cs = slice(g * gd, (g + 1) * gd)
            for c in range(sub // chunk):
                rs = slice(c * chunk, (c + 1) * chunk)
                mixed = _dot(w_mix[g], v[rs, cs]) + bst[:, g:g + 1]
                gated_ref[s * sub + c * chunk:s * sub + (c + 1) * chunk, cs] = (u[rs, cs] * mixed).astype(BF16)
        return _dot(gated_ref[s * sub:(s + 1) * sub, :], wout_ref[...])

    def finish(s, x, y):
        rows = slice(s * sub, (s + 1) * sub)
        x_new = _res_ln(x, y, m, 0, g_ref[...], b_ref[...], alpha)
        o_ref[rows, :] = x_new
        if n_exp:
            hp_ref[rows, :], meta_ref[:, rows] = _route_rows(x_new, m, wr_ref, br_ref, carry_ref, n_exp, region)

    nxt = project(0)
    pending = None
    for s in range(n):
        x, z = nxt
        if s + 1 < n:
            nxt = project(s + 1)
        u, v = activate(z)
        if pending is not None:
            finish(*pending)
        pending = (s, x, mix(s, u, v))
    finish(*pending)
    if n_exp:
        cnt_ref[...] = carry_ref[...]


def _gmlp_layer(x, mod, w_in, v_g, v_b, w_s, b_s, w_out, ln_g, ln_b, router=None, *, seq, alpha, tm=512,
                sub=256):
    t, d = x.shape
    groups, chunk, _ = w_s.shape
    width = w_out.shape[0]
    n_exp = router[0].shape[1] if router else 0
    kern = functools.partial(_gmlp_kernel, tm=tm, sub=sub, chunk=chunk, groups=groups, alpha=alpha, n_exp=n_exp,
                             region=t)
    row_map = lambda i: (i, 0)
    out_specs, out_shape = [pl.BlockSpec((tm, d), row_map)], [jax.ShapeDtypeStruct((t, d), F32)]
    scratch = [pltpu.VMEM((tm, width), BF16)]
    extra_specs, extra_args = [], []
    if router:
        r_specs, r_shapes = _router_out(t, d, tm, row_map)
        out_specs, out_shape = out_specs + r_specs, out_shape + r_shapes
        scratch.append(pltpu.VMEM((ROUTE_ROWS, LANES), F32))
        extra_specs = _router_specs(d, _const_spec)
        extra_args = list(_router_operands(*router))
    outs = pl.pallas_call(
        kern,
        grid=(t // tm,),
        in_specs=[
            pl.BlockSpec((tm, d), lambda i: (i, 0)),
            pl.BlockSpec((1, 6, d), lambda i: (i * tm // seq, 0, 0)),
            _const_spec((d, 2 * width)),
            _const_spec((1, width)),
            _const_spec((1, width)),
            _const_spec((groups, chunk, chunk)),
            _const_spec((chunk, groups)),
            _const_spec((width, d)),
            _const_spec((1, d)),
            _const_spec((1, d)),
        ] + extra_specs,
        out_specs=out_specs,
        out_shape=out_shape,
        scratch_shapes=scratch,
        compiler_params=_cparams(("arbitrary",)),
        name="gmlp_mixer",
    )(x, mod, w_in.astype(BF16), v_g.reshape(1, width), v_b.reshape(1, width), w_s, b_s.T,
      w_out.astype(BF16), ln_g.reshape(1, d), ln_b.reshape(1, d), *extra_args)
    return outs if router else outs[0]


def _swiglu_pipelined(hs, w1, w3, w2, emit):
    up = lambda h: (_dot(h, w1), _dot(h, w3))
    nxt = up(hs[0])
    for s in range(len(hs)):
        a, b = nxt
        if s + 1 < len(hs):
            nxt = up(hs[s + 1])
        t = (a / (1.0 + jnp.exp(-a)) * b).astype(BF16)
        emit(s, _dot(t, w2))


def _ffn_kernel(x_ref, mod_ref, w1_ref, w3_ref, w2_ref, g_ref, b_ref, o_ref, h_ref, wb1_ref, wb3_ref, wb2_ref, *,
                alpha, sub):
    j = pl.program_id(1)
    last = pl.num_programs(1) - 1
    subs = [slice(s, s + sub) for s in range(0, x_ref.shape[0], sub)]

    @pl.when(pl.program_id(0) == 0)
    def _():
        wb1_ref[j] = w1_ref[...].astype(BF16)
        wb3_ref[j] = w3_ref[...].astype(BF16)
        wb2_ref[j] = w2_ref[...].astype(BF16)

    weights = lambda: (wb1_ref[j], wb3_ref[j], wb2_ref[j])

    @pl.when(j == 0)
    def _():
        hs = []
        for rows in subs:
            h = _modulate(x_ref[rows, :], mod_ref[0], 3).astype(BF16)
            h_ref[rows, :] = h
            hs.append(h)

        def emit(s, part):
            o_ref[subs[s], :] = part

        _swiglu_pipelined(hs, *weights(), emit)

    @pl.when(jnp.logical_and(j > 0, j < last))
    def _():
        def emit(s, part):
            o_ref[subs[s], :] += part

        _swiglu_pipelined([h_ref[rows, :] for rows in subs], *weights(), emit)

    @pl.when(j == last)
    def _():
        def emit(s, part):
            rows = subs[s]
            y = o_ref[rows, :] + part
            o_ref[rows, :] = _res_ln(x_ref[rows, :], y, mod_ref[0], 3, g_ref[...], b_ref[...], alpha)

        _swiglu_pipelined([h_ref[rows, :] for rows in subs], *weights(), emit)


def _ffn_layer(x, mod, w13, w2, layer, ln_g, ln_b, *, seq, alpha, tm=1024, tf=512):
    t, d = x.shape
    f = w2.shape[1]
    nf = f // tf
    assert nf >= 2
    tile = lambda i, j: jnp.where(i == 0, j, nf - 1)
    return pl.pallas_call(
        functools.partial(_ffn_kernel, alpha=alpha, sub=tm // 4),
        grid=(t // tm, nf),
        in_specs=[
            pl.BlockSpec((tm, d), lambda i, j: (i, 0)),
            pl.BlockSpec((1, 6, d), lambda i, j: (i * tm // seq, 0, 0)),
            pl.BlockSpec((None, d, tf), lambda i, j: (layer, 0, tile(i, j))),
            pl.BlockSpec((None, d, tf), lambda i, j: (layer, 0, nf + tile(i, j))),
            pl.BlockSpec((None, tf, d), lambda i, j: (layer, tile(i, j), 0)),
            _const_spec((1, d)),
            _const_spec((1, d)),
        ],
        out_specs=pl.BlockSpec((tm, d), lambda i, j: (i, 0)),
        out_shape=jax.ShapeDtypeStruct((t, d), F32),
        scratch_shapes=[pltpu.VMEM((tm, d), BF16), pltpu.VMEM((nf, d, tf), BF16), pltpu.VMEM((nf, d, tf), BF16),
                        pltpu.VMEM((nf, tf, d), BF16)],
        compiler_params=_cparams(("arbitrary", "arbitrary")),
        name="swiglu_dense",
    )(x, mod, w13, w13, w2, ln_g.reshape(1, d), ln_b.reshape(1, d))


MOE_GROUP = 2


def _moe_ffn_kernel(ge_ref, gb_ref, gr_ref, h_ref, w1_ref, w3_ref, w2_ref, o_ref, acc_ref, *, sub):
    g = pl.program_id(0)
    j = pl.program_id(1)
    last = pl.num_programs(1) - 1
    n_rows = gr_ref[g]
    nv = (n_rows + EXPERT_BLOCK - 1) // EXPERT_BLOCK

    def run(n_blocks, phase):
        subs = [slice(s, s + sub) for s in range(0, n_blocks * EXPERT_BLOCK, sub)]
        row = lax.broadcasted_iota(jnp.int32, (sub, 1), 0)

        def rows_in(rows):
            w = jnp.where(row + rows.start < n_rows, h_ref[rows, :], jnp.uint32(0))
            return _unpack_bf16_pair(w).astype(BF16)

        def emit(s, part):
            if phase == "first":
                acc_ref[subs[s], :] = part
            elif phase == "middle":
                acc_ref[subs[s], :] += part
            else:
                o_ref[subs[s], :] = _pack_bf16_pair(acc_ref[subs[s], :] + part)

        _swiglu_pipelined([rows_in(rows) for rows in subs], w1_ref[0, 0].astype(BF16), w3_ref[0, 0].astype(BF16),
                          w2_ref[0, 0].astype(BF16), emit)

    for n_blocks in range(1, MOE_GROUP + 1):
        @pl.when(jnp.logical_and(nv == n_blocks, j == 0))
        def _():
            run(n_blocks, "first")

        @pl.when(jnp.logical_and(nv == n_blocks, jnp.logical_and(j > 0, j < last)))
        def _():
            run(n_blocks, "middle")

        @pl.when(jnp.logical_and(nv == n_blocks, j == last))
        def _():
            run(n_blocks, "last")

    for blk in range(MOE_GROUP):
        @pl.when(jnp.logical_and(blk >= nv, j == last))
        def _():
            o_ref[blk * EXPERT_BLOCK:(blk + 1) * EXPERT_BLOCK, :] = jnp.zeros((EXPERT_BLOCK, o_ref.shape[1]),
                                                                              jnp.uint32)


def _moe_ffn(h_slots, group_expert, group_block, group_rows, w13, w2, layer, *, tf=512):
    half = h_slots.shape[1]
    d = 2 * half
    f = w2.shape[2]
    nf = f // tf
    assert nf >= 2
    tm = MOE_GROUP * EXPERT_BLOCK

    def jj(g, j, gr):
        return jnp.where(gr[g] > 0, j, nf - 1)

    grid_spec = pltpu.PrefetchScalarGridSpec(
        num_scalar_prefetch=3,
        grid=(group_expert.shape[0], nf),
        in_specs=[
            pl.BlockSpec((tm, half), lambda g, j, ge, gb, gr: (gb[g], 0)),
            pl.BlockSpec((1, 1, d, tf), lambda g, j, ge, gb, gr: (layer, ge[g], 0, jj(g, j, gr))),
            pl.BlockSpec((1, 1, d, tf), lambda g, j, ge, gb, gr: (layer, ge[g], 0, nf + jj(g, j, gr))),
            pl.BlockSpec((1, 1, tf, d), lambda g, j, ge, gb, gr: (layer, ge[g], jj(g, j, gr), 0)),
        ],
        out_specs=pl.BlockSpec((tm, half), lambda g, j, ge, gb, gr: (gb[g], 0)),
        scratch_shapes=[pltpu.VMEM((tm, d), F32)],
    )
    return pl.pallas_call(
        functools.partial(_moe_ffn_kernel, sub=EXPERT_BLOCK // 2),
        grid_spec=grid_spec,
        out_shape=jax.ShapeDtypeStruct(h_slots.shape, jnp.uint32),
        compiler_params=_cparams(("arbitrary", "arbitrary")),
        name="swiglu_experts",
    )(group_expert, group_block, group_rows, h_slots, w13, w13, w2)


ROUTE_ROWS = 16


def _route_rows(x_rows, m, wrt_ref, brt_ref, carry_ref, n_exp, region):
    tm = x_rows.shape[0]
    h = _modulate(x_rows, m, 3)
    h_hi, h_mid, _ = _split3(h)
    w = wrt_ref[...]
    w_hi = w.astype(BF16)
    w_lo = (w - w_hi.astype(F32)).astype(BF16)
    nt = (((1,), (1,)), ((), ()))
    logits = (lax.dot_general(w_hi, h_hi, nt, preferred_element_type=F32)
              + lax.dot_general(w_lo, h_hi, nt, preferred_element_type=F32)
              + lax.dot_general(w_hi, h_mid, nt, preferred_element_type=F32) + brt_ref[...])
    erow = lax.broadcasted_iota(jnp.int32, (ROUTE_ROWS, tm), 0)
    neg = jnp.float32(-jnp.inf)
    logits = jnp.where(erow < n_exp, logits, neg)
    v0 = jnp.max(logits, axis=0, keepdims=True)
    e0 = jnp.min(jnp.where(logits == v0, erow, ROUTE_ROWS), axis=0, keepdims=True)
    rest = jnp.where(erow == e0, neg, logits)
    v1 = jnp.max(rest, axis=0, keepdims=True)
    e1 = jnp.min(jnp.where(rest == v1, erow, ROUTE_ROWS), axis=0, keepdims=True)
    p = jnp.exp(v1 - v0)
    g0 = 1.0 / (1.0 + p)
    g1 = p / (1.0 + p)
    oh0 = erow == e0
    oh1 = erow == e1
    onehot = jnp.where(jnp.logical_or(oh0, oh1), 1.0, 0.0).astype(BF16)
    r = lax.broadcasted_iota(jnp.int32, (tm, tm), 0)
    c = lax.broadcasted_iota(jnp.int32, (tm, tm), 1)
    earlier = jnp.where(r < c, 1.0, 0.0).astype(BF16)
    carry = carry_ref[:, 0:1]
    before = _dot(onehot, earlier) + carry
    rank0 = jnp.sum(jnp.where(oh0, before, 0.0), axis=0, keepdims=True)
    rank1 = jnp.sum(jnp.where(oh1, before, 0.0), axis=0, keepdims=True)
    total = carry + jnp.sum(onehot.astype(F32), axis=1, keepdims=True)
    carry_ref[...] = jnp.broadcast_to(total, carry_ref.shape)
    mrow = lax.broadcasted_iota(jnp.int32, (SUBLANES, tm), 0)
    e0f = e0.astype(F32)
    e1f = e1.astype(F32)
    meta = jnp.where(mrow == 0, e0f, 0.0)
    meta = jnp.where(mrow == 1, e1f, meta)
    meta = jnp.where(mrow == 2, g0, meta)
    meta = jnp.where(mrow == 3, g1, meta)
    meta = jnp.where(mrow == 4, e0f * region + rank0, meta)
    meta = jnp.where(mrow == 5, e1f * region + rank1, meta)
    return _pack_bf16_pair(h), meta


def _router_operands(w_router, b_router):
    d, n_exp = w_router.shape
    wrt = jnp.zeros((ROUTE_ROWS, d), F32).at[:n_exp].set(w_router.T)
    brt = jnp.zeros((ROUTE_ROWS, 1), F32).at[:n_exp, 0].set(b_router)
    return wrt, brt


def _router_specs(d, const):
    return [const((ROUTE_ROWS, d)), const((ROUTE_ROWS, 1))]


def _router_out(t, d, tm, row_map):
    specs = [pl.BlockSpec((tm, d // 2), row_map),
             pl.BlockSpec((SUBLANES, tm), lambda *a: (0, row_map(*a)[0])),
             pl.BlockSpec((ROUTE_ROWS, LANES), lambda *_: (0, 0))]
    shapes = [jax.ShapeDtypeStruct((t, d // 2), jnp.uint32), jax.ShapeDtypeStruct((SUBLANES, t), F32),
              jax.ShapeDtypeStruct((ROUTE_ROWS, LANES), F32)]
    return specs, shapes


def _combine_kernel(x_ref, ya_ref, yb_ref, meta_ref, mod_ref, g_ref, b_ref, o_ref, *, alpha):
    record = meta_ref[...]
    cols = jnp.concatenate([record, jnp.zeros((LANES - record.shape[0], record.shape[1]), F32)], axis=0).T
    y = cols[:, 2:3] * _unpack_bf16_pair(ya_ref[...]) + cols[:, 3:4] * _unpack_bf16_pair(yb_ref[...])
    o_ref[...] = _res_ln(x_ref[...], y, mod_ref[0], 3, g_ref[...], b_ref[...], alpha)


def _combine(x, ya, yb, meta, mod, ln_g, ln_b, *, seq, alpha, tm=1024):
    t, d = x.shape
    row = pl.BlockSpec((tm, d), lambda i: (i, 0))
    packed = pl.BlockSpec((tm, d // 2), lambda i: (i, 0))
    return pl.pallas_call(
        functools.partial(_combine_kernel, alpha=alpha),
        grid=(t // tm,),
        in_specs=[row, packed, packed,
                  pl.BlockSpec((SUBLANES, tm), lambda i: (0, i)),
                  pl.BlockSpec((1, 6, d), lambda i: (i * tm // seq, 0, 0)),
                  _const_spec((1, d)), _const_spec((1, d))],
        out_specs=row,
        out_shape=jax.ShapeDtypeStruct((t, d), F32),
        compiler_params=_cparams(("parallel",)),
        name="moe_combine",
    )(x, ya, yb, meta, mod, ln_g.reshape(1, d), ln_b.reshape(1, d))


def _sc_mesh():
    return plsc.VectorSubcoreMesh(core_axis_name="c", subcore_axis_name="s", num_cores=SC_CORES,
                                  num_subcores=SC_SUBCORES)


def _sc_worker_share(n):
    workers = SC_CORES * SC_SUBCORES
    per_worker = n // workers
    steps = per_worker // SC_ROWS
    assert steps * SC_ROWS * workers == n
    return per_worker, steps


def _sc_gather_rows(table, idx_a, idx_b):
    n = idx_a.shape[0]
    width = table.shape[1]
    per_worker, steps = _sc_worker_share(n)

    def body(table_hbm, ia_hbm, ib_hbm, oa_hbm, ob_hbm, ia_v, ib_v, rows_v, sem):
        wid = lax.axis_index("s") * SC_CORES + lax.axis_index("c")
        base = pl.multiple_of(wid * per_worker, SC_ROWS)
        pltpu.sync_copy(ia_hbm.at[pl.ds(base, per_worker)], ia_v)
        pltpu.sync_copy(ib_hbm.at[pl.ds(base, per_worker)], ib_v)

        @pl.loop(0, steps)
        def _(j):
            lo = pl.multiple_of(j * SC_ROWS, SC_ROWS)
            for idx_v, out_hbm in ((ia_v, oa_hbm), (ib_v, ob_hbm)):
                pltpu.async_copy(table_hbm.at[idx_v.at[pl.ds(lo, SC_ROWS)]], rows_v, sem).wait()
                pltpu.sync_copy(rows_v, out_hbm.at[pl.ds(base + lo, SC_ROWS)])

    out = jax.ShapeDtypeStruct((n, width), table.dtype)
    return pl.kernel(
        body,
        out_type=(out, out),
        mesh=_sc_mesh(),
        scratch_types=[
            pltpu.VMEM((per_worker,), jnp.int32),
            pltpu.VMEM((per_worker,), jnp.int32),
            pltpu.VMEM((SC_ROWS, width), table.dtype),
            pltpu.SemaphoreType.DMA,
        ],
        name="sc_row_gather",
    )(table, idx_a, idx_b)


def _sc_scatter_rows(rows, dest_a, dest_b, n_out):
    n, width = rows.shape
    per_worker, steps = _sc_worker_share(n)

    def body(rows_hbm, da_hbm, db_hbm, out_hbm, ia_v, ib_v, rows_v, sem):
        wid = lax.axis_index("s") * SC_CORES + lax.axis_index("c")

        @pl.loop(0, steps)
        def _(j):
            off = pl.multiple_of(wid * per_worker + j * SC_ROWS, SC_ROWS)
            pltpu.sync_copy(da_hbm.at[pl.ds(off, SC_ROWS)], ia_v)
            pltpu.sync_copy(db_hbm.at[pl.ds(off, SC_ROWS)], ib_v)
            pltpu.sync_copy(rows_hbm.at[pl.ds(off, SC_ROWS)], rows_v)
            pltpu.async_copy(rows_v, out_hbm.at[ia_v], sem).wait()
            pltpu.async_copy(rows_v, out_hbm.at[ib_v], sem).wait()

    return pl.kernel(
        body,
        out_type=jax.ShapeDtypeStruct((n_out, width), rows.dtype),
        mesh=_sc_mesh(),
        scratch_types=[
            pltpu.VMEM((SC_ROWS,), jnp.int32),
            pltpu.VMEM((SC_ROWS,), jnp.int32),
            pltpu.VMEM((SC_ROWS, width), rows.dtype),
            pltpu.SemaphoreType.DMA,
        ],
        name="sc_row_scatter",
    )(rows, dest_a, dest_b)


def _moe_layer(routed, mod, w13, w2, layer, ln_g, ln_b, *, seq, alpha):
    x, h, meta, cnt = routed
    t, d = x.shape
    n_exp = w13.shape[1]
    dest0 = meta[4].astype(jnp.int32)
    dest1 = meta[5].astype(jnp.int32)
    counts = cnt[:n_exp, 0].astype(jnp.int32)
    group_rows = MOE_GROUP * EXPERT_BLOCK
    assert t % group_rows == 0
    region_groups = t // group_rows
    groups_per_expert = (counts + group_rows - 1) // group_rows
    group_end = jnp.cumsum(groups_per_expert)
    g = jnp.arange(t * TOP_K // group_rows + n_exp, dtype=jnp.int32)
    expert = jnp.minimum(jnp.sum(group_end[None, :] <= g[:, None], axis=1), n_exp - 1).astype(jnp.int32)
    blk_in_expert = g - (group_end - groups_per_expert)[expert]
    used = g < group_end[-1]
    spare_block = n_exp * region_groups
    group_block = jnp.where(used, expert * region_groups + blk_in_expert, spare_block).astype(jnp.int32)
    rows_in_group = jnp.where(used, jnp.clip(counts[expert] - blk_in_expert * group_rows, 0, group_rows),
                              0).astype(jnp.int32)
    h_slots = _sc_scatter_rows(h, dest0, dest1, (spare_block + 1) * group_rows)
    y_slots = _moe_ffn(h_slots, expert, group_block, rows_in_group, w13, w2, layer)
    ya, yb = _sc_gather_rows(y_slots, dest0, dest1)
    return _combine(x, ya, yb, meta, mod, ln_g, ln_b, seq=seq, alpha=alpha)


HEAD_PAD = LANES
GATE_PARTS = 3
GATE_ROWS = 16
VAL_PAD = 16
LOG2E = 1.4426950408889634


def _head_lanes(h, hd):
    k0 = (h % (HEAD_PAD // hd)) * hd
    return k0, (hd if k0 == 0 else 0)


def _fox_proj_kernel(x_ref, mod_ref, wf_ref, bf_ref, wq_ref, wk_ref, wv_ref, selk_ref, onek_ref, selq_ref,
                     oneq_ref, onev_ref, k_ref, q_ref, v_ref, carry_ref, *, q_scale, heads, hd, nb):
    tm = x_ref.shape[0]

    @pl.when(pl.program_id(0) % nb == 0)
    def _():
        carry_ref[...] = jnp.zeros_like(carry_ref)

    h_hi, h_mid, _ = _split3(_modulate(x_ref[...], mod_ref[0], 0))

    f = _dot_split(h_hi, h_mid, wf_ref[...]) + bf_ref[...]
    lf = jnp.minimum(f, 0.0) - jnp.log(1.0 + jnp.exp(-jnp.abs(f)))
    r = lax.broadcasted_iota(jnp.int32, (tm, tm), 0)
    c = lax.broadcasted_iota(jnp.int32, (tm, tm), 1)
    tri = jnp.where(r >= c, 1.0, 0.0).astype(BF16)
    sums = _dot(tri, jnp.concatenate(_split3(lf), axis=1))
    cs = carry_ref[0:1, :] + sum(sums[:, n * LANES:(n + 1) * LANES] for n in range(GATE_PARTS))
    carry_ref[...] = jnp.broadcast_to(cs[tm - 1:tm, :], carry_ref.shape)
    cs = cs * LOG2E

    gate_k = onek_ref[...] + _dot(jnp.concatenate(_split3(cs), axis=1), selk_ref[...])
    kc = _dot(h_hi, wk_ref[...])
    lane = lax.broadcasted_iota(jnp.int32, (tm, HEAD_PAD), 1)
    for h in range(heads):
        k0, _ = _head_lanes(h, hd)
        src = (h * hd // HEAD_PAD) * HEAD_PAD
        is_k = jnp.logical_and(lane >= k0, lane < k0 + hd)
        hs = slice(h * HEAD_PAD, (h + 1) * HEAD_PAD)
        k_ref[:, hs] = jnp.where(is_k, kc[:, src:src + HEAD_PAD], gate_k[:, hs]).astype(BF16)

    nt = (((1,), (1,)), ((), ()))
    qc = lax.dot_general(wq_ref[...], h_hi, nt, preferred_element_type=F32) * (q_scale * LOG2E)
    gate_q = oneq_ref[...] + _dot(selq_ref[...], jnp.concatenate(_split3(cs.T), axis=0))
    fill = HEAD_PAD - hd - GATE_ROWS
    for h in range(heads):
        k0, g0 = _head_lanes(h, hd)
        base = h * HEAD_PAD
        q_ref[0, base + k0:base + k0 + hd, :] = qc[h * hd:(h + 1) * hd, :].astype(BF16)
        q_ref[0, base + g0:base + g0 + GATE_ROWS, :] = gate_q[h * GATE_ROWS:(h + 1) * GATE_ROWS, :].astype(BF16)
        q_ref[0, base + g0 + GATE_ROWS:base + g0 + GATE_ROWS + fill, :] = jnp.zeros((fill, tm), BF16)

    v = lax.dot_general(wv_ref[...], h_hi, nt, preferred_element_type=F32) + onev_ref[...]
    v_ref[0] = v.astype(BF16)


def _fox_proj(x, mod, w_in, b_f, *, bsz, seq, heads, tm=512):
    t, d = x.shape
    hd = d // heads
    assert HEAD_PAD == 2 * hd and GATE_ROWS >= 2 * GATE_PARTS
    nb = seq // tm
    wide = heads * HEAD_PAD
    wq_t = w_in[:, :d].T.astype(BF16)
    wk = w_in[:, d:2 * d].astype(BF16)
    vrows = heads * (hd + VAL_PAD)
    wv_t = jnp.pad(w_in[:, 2 * d:3 * d].reshape(d, heads, hd),
                   ((0, 0), (0, 0), (0, VAL_PAD))).reshape(d, vrows).T.astype(BF16)
    wf = jnp.zeros((d, LANES), F32).at[:, :heads].set(w_in[:, 3 * d:])
    bf = jnp.zeros((1, LANES), F32).at[0, :heads].set(b_f)
    sel_k = np.zeros((GATE_PARTS, LANES, wide), np.float32)
    one_k = np.zeros((1, wide), np.float32)
    sel_q = np.zeros((GATE_PARTS, heads * GATE_ROWS, LANES), np.float32)
    one_q = np.zeros((heads * GATE_ROWS, 1), np.float32)
    one_v = np.zeros((vrows, 1), np.float32)
    for hh in range(heads):
        _, g0 = _head_lanes(hh, hd)
        one_v[hh * (hd + VAL_PAD) + hd, 0] = 1.0
        for n in range(GATE_PARTS):
            sel_k[n, hh, hh * HEAD_PAD + g0 + n] = -1.0
            one_q[hh * GATE_ROWS + n, 0] = 1.0
            one_k[0, hh * HEAD_PAD + g0 + GATE_PARTS + n] = 1.0
            sel_q[n, hh * GATE_ROWS + GATE_PARTS + n, hh] = 1.0
    return pl.pallas_call(
        functools.partial(_fox_proj_kernel, q_scale=hd ** -0.5, heads=heads, hd=hd, nb=nb),
        grid=(t // tm,),
        in_specs=[
            pl.BlockSpec((tm, d), lambda i: (i, 0)),
            pl.BlockSpec((1, 6, d), lambda i: (i * tm // seq, 0, 0)),
            _const_spec((d, LANES)),
            _const_spec((1, LANES)),
            _const_spec((d, d)),
            _const_spec((d, d)),
            _const_spec((vrows, d)),
            _const_spec((GATE_PARTS * LANES, wide)),
            _const_spec((1, wide)),
            _const_spec((heads * GATE_ROWS, GATE_PARTS * LANES)),
            _const_spec((heads * GATE_ROWS, 1)),
            _const_spec((vrows, 1)),
        ],
        out_specs=[
            pl.BlockSpec((tm, wide), lambda i: (i, 0)),
            pl.BlockSpec((1, wide, tm), lambda i: (i // nb, 0, i % nb)),
            pl.BlockSpec((1, vrows, tm), lambda i: (i // nb, 0, i % nb)),
        ],
        out_shape=[
            jax.ShapeDtypeStruct((t, wide), BF16),
            jax.ShapeDtypeStruct((bsz, wide, seq), BF16),
            jax.ShapeDtypeStruct((bsz, vrows, seq), BF16),
        ],
        scratch_shapes=[pltpu.VMEM((SUBLANES, LANES), F32)],
        compiler_params=_cparams(("arbitrary",)),
        name="fox_proj",
    )(x, mod, wf, bf, wq_t, wk, wv_t, jnp.asarray(sel_k.reshape(GATE_PARTS * LANES, wide), BF16), jnp.asarray(one_k),
      jnp.asarray(np.concatenate(list(sel_q), axis=1), BF16), jnp.asarray(one_q), jnp.asarray(one_v))


def _flash_kernel(qi_ref, kj_ref, k_ref, q_ref, v_ref, x_ref, mod_ref, wout_ref, g_ref, b_ref, *rest,
                  heads, hd, blk, alpha, n_exp, region):
    p_idx = pl.program_id(1)
    if n_exp:
        wr_ref, br_ref, o_ref, hp_ref, meta_ref, cnt_ref, m_ref, acc_ref, s_ref, carry_ref = rest

        @pl.when(jnp.logical_and(pl.program_id(0) == 0, p_idx == 0))
        def _():
            carry_ref[...] = jnp.zeros_like(carry_ref)
    else:
        o_ref, m_ref, acc_ref, s_ref = rest
    qi = qi_ref[p_idx]
    kj = kj_ref[p_idx]
    vp = hd + VAL_PAD

    @pl.when(kj == 0)
    def _():
        m_ref[...] = jnp.full(m_ref.shape, -jnp.inf, F32)
        acc_ref[...] = jnp.zeros_like(acc_ref)

    def step(diagonal):
        half = blk // 2
        tiles = [(slice(0, half), slice(0, half)), (slice(half, blk), slice(0, blk))] if diagonal else \
                [(slice(0, blk), slice(0, blk))]
        if diagonal:
            causal = lambda rows, first_q: (lax.broadcasted_iota(jnp.int32, (rows, half), 0)
                                            <= lax.broadcasted_iota(jnp.int32, (rows, half), 1) + first_q)
            keep = [causal(half, 0), causal(blk, half)]

        def scores(h):
            hs = slice(h * HEAD_PAD, (h + 1) * HEAD_PAD)
            tops = []
            for n, (qs, ks) in enumerate(tiles):
                s = _dot(k_ref[ks, hs], q_ref[0, hs, qs])
                if diagonal:
                    s = jnp.where(keep[n], s, -jnp.inf)
                s_ref[h % depth, ks, qs] = s
                tops.append(jnp.max(s, axis=0, keepdims=True))
            return tops

        depth = s_ref.shape[0]
        ahead = [scores(h) for h in range(depth - 1)]
        for h in range(heads):
            if h + depth - 1 < heads:
                ahead.append(scores(h + depth - 1))
            rs = slice(h * vp, (h + 1) * vp)
            for (qs, ks), m_cur in zip(tiles, ahead.pop(0)):
                m_prev = m_ref[h:h + 1, qs]
                m_new = jnp.maximum(m_prev, m_cur)
                m_ref[h:h + 1, qs] = m_new
                p = jnp.exp2(s_ref[h % depth, ks, qs] - m_new).astype(BF16)
                acc_ref[rs, qs] = acc_ref[rs, qs] * jnp.exp2(m_prev - m_new) + _dot(v_ref[0, rs, ks], p)

    @pl.when(kj < qi)
    def _():
        step(False)

    @pl.when(kj == qi)
    def _():
        step(True)
        per = LANES // hd
        cols = []
        for g in range(heads // per):
            parts = []
            for n in range(per):
                r0 = (g * per + n) * vp
                parts.append(acc_ref[r0:r0 + hd, :] / acc_ref[r0 + hd:r0 + hd + 1, :])
            cols.append(jnp.concatenate(parts, axis=0).T.astype(BF16))
        y = _dot(jnp.concatenate(cols, axis=1), wout_ref[...])
        x_new = _res_ln(x_ref[...], y, mod_ref[0], 0, g_ref[...], b_ref[...], alpha)
        o_ref[...] = x_new
        if n_exp:
            hp_ref[...], meta_ref[...] = _route_rows(x_new, mod_ref[0], wr_ref, br_ref, carry_ref, n_exp, region)
            cnt_ref[...] = carry_ref[...]


def _flash(k_aug, q_aug_t, v_t, x, mod, w_out, ln_g, ln_b, router=None, *, bsz, seq, heads, alpha, blk=512):
    t, wide = k_aug.shape
    vrows = v_t.shape[1]
    hd = vrows // heads - VAL_PAD
    d = heads * hd
    nq = seq // blk
    pairs = [(i, j) for i in range(nq) for j in range(i + 1)]
    qi = jnp.asarray(np.array([p[0] for p in pairs], np.int32))
    kj = jnp.asarray(np.array([p[1] for p in pairs], np.int32))
    n_exp = router[0].shape[1] if router else 0
    row_map = lambda b, p, qi, kj: (b * nq + qi[p], 0)
    const = lambda shape: pl.BlockSpec(shape, lambda b, p, qi, kj: (0, 0), pipeline_mode=pl.Buffered(1))
    out_specs, out_shape = [pl.BlockSpec((blk, d), row_map)], [jax.ShapeDtypeStruct((t, d), F32)]
    scratch = [pltpu.VMEM((heads, blk), F32), pltpu.VMEM((vrows, blk), F32), pltpu.VMEM((3, blk, blk), F32)]
    extra_specs, extra_args = [], []
    if router:
        r_specs, r_shapes = _router_out(t, d, blk, row_map)
        out_specs, out_shape = out_specs + r_specs, out_shape + r_shapes
        scratch.append(pltpu.VMEM((ROUTE_ROWS, LANES), F32))
        extra_specs = _router_specs(d, const)
        extra_args = list(_router_operands(*router))
    grid_spec = pltpu.PrefetchScalarGridSpec(
        num_scalar_prefetch=2,
        grid=(bsz, len(pairs)),
        in_specs=[
            pl.BlockSpec((blk, wide), lambda b, p, qi, kj: (b * nq + kj[p], 0)),
            pl.BlockSpec((1, wide, blk), lambda b, p, qi, kj: (b, 0, qi[p])),
            pl.BlockSpec((1, vrows, blk), lambda b, p, qi, kj: (b, 0, kj[p])),
            pl.BlockSpec((blk, d), row_map),
            pl.BlockSpec((1, 6, d), lambda b, p, qi, kj: (b, 0, 0)),
            const((d, d)),
            const((1, d)),
            const((1, d)),
        ] + extra_specs,
        out_specs=out_specs,
        scratch_shapes=scratch,
    )
    outs = pl.pallas_call(
        functools.partial(_flash_kernel, heads=heads, hd=hd, blk=blk, alpha=alpha, n_exp=n_exp, region=t),
        grid_spec=grid_spec,
        out_shape=out_shape,
        compiler_params=_cparams(("arbitrary", "arbitrary")),
        name="fox_attention",
    )(qi, kj, k_aug, q_aug_t, v_t, x, mod, w_out.astype(BF16), ln_g.reshape(1, d), ln_b.reshape(1, d), *extra_args)
    return outs if router else outs[0]


def _fox_layer(x, mod, w_in, b_f, w_out, ln_g, ln_b, router=None, *, bsz, seq, alpha):
    heads = b_f.shape[0]
    k_aug, q_aug_t, v_t = _fox_proj(x, mod, w_in, b_f, bsz=bsz, seq=seq, heads=heads)
    return _flash(k_aug, q_aug_t, v_t, x, mod, w_out, ln_g, ln_b, router, bsz=bsz, seq=seq, heads=heads,
                  alpha=alpha)


def _pool_kernel(x_ref, xh_ref, mod_ref, win_ref, wg_ref, ls_ref, wout_ref, g_ref, b_ref, o_ref, pooled_ref,
                 *, tm, sub, seq, alpha):
    pos0 = (pl.program_id(0) * tm) % seq
    m = mod_ref[0]
    n = tm // sub
    halo_ok = jnp.where(pos0 > 0, jnp.float32(1.0), jnp.float32(0.0))
    rows = lax.broadcasted_iota(jnp.int32, (sub + POOL_HALO, 1), 0)

    def project(s):
        x = x_ref[s * sub:(s + 1) * sub, :]
        halo = xh_ref[...] if s == 0 else x_ref[s * sub - POOL_HALO:s * sub, :]
        ze = _dot(_modulate(jnp.concatenate([halo, x], axis=0), m, 0).astype(BF16), win_ref[...])
        if s == 0:
            ze = ze * jnp.where(rows < POOL_HALO, halo_ok, jnp.float32(1.0))
        return x, ze

    def pool(s, ze):
        pos = (pos0 + s * sub + lax.broadcasted_iota(jnp.int32, (sub, 1), 0) + 1).astype(F32)
        gd = ze.shape[1] // len(POOL_WINDOWS)
        out_rows = slice(s * sub, (s + 1) * sub)
        for g, win in enumerate(POOL_WINDOWS):
            cs = slice(g * gd, (g + 1) * gd)
            zg = ze[:, cs]
            acc = zg
            span = 1
            while span < win:
                acc = acc + pltpu.roll(acc, span, 0)
                span *= 2
            count = jnp.minimum(pos, jnp.float32(win))
            pooled = acc[POOL_HALO:, :] / count - zg[POOL_HALO:, :]
            pooled_ref[out_rows, cs] = (_dot(pooled.astype(BF16), wg_ref[g]) * ls_ref[:, cs]).astype(BF16)
        return _dot(pooled_ref[out_rows, :], wout_ref[...])

    nxt = project(0)
    for s in range(n):
        x, ze = nxt
        if s + 1 < n:
            nxt = project(s + 1)
        y = pool(s, ze)
        o_ref[s * sub:(s + 1) * sub, :] = _res_ln(x, y, m, 0, g_ref[...], b_ref[...], alpha)


def _pool_layer(x, mod, w_in, w_grp, scale, w_out, ln_g, ln_b, *, seq, alpha, tm=1024, sub=256):
    t, d = x.shape
    ng, gd, _ = w_grp.shape
    hb = tm // POOL_HALO
    return pl.pallas_call(
        functools.partial(_pool_kernel, tm=tm, sub=sub, seq=seq, alpha=alpha),
        grid=(t // tm,),
        in_specs=[
            pl.BlockSpec((tm, d), lambda i: (i, 0)),
            pl.BlockSpec((POOL_HALO, d), lambda i: (jnp.maximum(i * hb - 1, 0), 0)),
            pl.BlockSpec((1, 6, d), lambda i: (i * tm // seq, 0, 0)),
            _const_spec((d, d)),
            _const_spec((ng, gd, gd)),
            _const_spec((1, d)),
            _const_spec((d, d)),
            _const_spec((1, d)),
            _const_spec((1, d)),
        ],
        out_specs=pl.BlockSpec((tm, d), lambda i: (i, 0)),
        out_shape=jax.ShapeDtypeStruct((t, d), F32),
        scratch_shapes=[pltpu.VMEM((tm, d), BF16)],
        compiler_params=_cparams(("parallel",)),
        name="pool_mixer",
    )(x, x, mod, w_in.astype(BF16), w_grp.astype(BF16), scale.reshape(1, d), w_out.astype(BF16),
      ln_g.reshape(1, d), ln_b.reshape(1, d))


def kernel(x, c, mod_w, mod_b, ln_g, ln_b, gm_w_in, gm_ln_g, gm_ln_b, gm_w_s, gm_b_s, gm_w_out, fox_w_in,
           fox_b_f, fox_w_out, pool_w_in, pool_w_grp, pool_scale, pool_w_out, ffn_w13, ffn_w2, moe_w_router,
           moe_b_router, moe_w13, moe_w2):
    bsz, seq, d = x.shape
    depth = mod_w.shape[0]
    alpha = (2 * depth) ** 0.25
    n_mixers = 3
    mod_all = _modulation(c, mod_w, mod_b)
    xt = x.reshape(bsz * seq, d)
    for i in range(depth):
        mod = mod_all[i]
        kind, j = i % n_mixers, i // n_mixers
        router = (moe_w_router[i // 2], moe_b_router[i // 2]) if i % 2 else None
        if kind == 0:
            xt = _gmlp_layer(xt, mod, gm_w_in[j], gm_ln_g[j], gm_ln_b[j], gm_w_s[j], gm_b_s[j], gm_w_out[j],
                             ln_g[i, 0], ln_b[i, 0], router, seq=seq, alpha=alpha)
        elif kind == 1:
            xt = _fox_layer(xt, mod, fox_w_in[j], fox_b_f[j], fox_w_out[j], ln_g[i, 0], ln_b[i, 0], router,
                            bsz=bsz, seq=seq, alpha=alpha)
        else:
            assert router is None, "the pooling mixer has no routing epilogue"
            xt = _pool_layer(xt, mod, pool_w_in[j], pool_w_grp[j], pool_scale[j], pool_w_out[j],
                             ln_g[i, 0], ln_b[i, 0], seq=seq, alpha=alpha)
        if router is None:
            xt = _ffn_layer(xt, mod, ffn_w13, ffn_w2, i // 2, ln_g[i, 1], ln_b[i, 1],
                            seq=seq, alpha=alpha)
        else:
            xt = _moe_layer(xt, mod, moe_w13, moe_w2, i // 2, ln_g[i, 1], ln_b[i, 1], seq=seq, alpha=alpha)
    return xt.reshape(bsz, seq, d)
```

```python
import functools

import numpy as np
import jax
import jax.numpy as jnp
from jax import lax
from jax.experimental import pallas as pl
from jax.experimental.pallas import tpu as pltpu
from jax.experimental.pallas import tpu_sc as plsc

F32 = jnp.float32
BF16 = jnp.bfloat16

POOL_WINDOWS = (2, 4, 8, 16)
TOP_K = 2
EXPERT_BLOCK = 512
LN_EPS = 1e-5
LANES = 128
SUBLANES = 8
POOL_HALO = 16
VMEM_LIMIT = 56 * 1024 * 1024
SC_CORES = 2
SC_SUBCORES = 16
SC_ROWS = 128


def _cparams(sem, vmem_limit=VMEM_LIMIT):
    return pltpu.CompilerParams(dimension_semantics=sem, vmem_limit_bytes=vmem_limit)


def _const_spec(shape):
    nd = len(shape)
    return pl.BlockSpec(shape, lambda *_: (0,) * nd, pipeline_mode=pl.Buffered(1))


def _layer_norm(r, g, b):
    mu = jnp.mean(r, axis=-1, keepdims=True)
    xc = r - mu
    var = jnp.mean(xc * xc, axis=-1, keepdims=True)
    return xc * lax.rsqrt(var + LN_EPS) * g + b


def _modulate(x, m, off):
    return x * (1.0 + m[off + 1:off + 2]) + m[off:off + 1]


def _res_ln(x, y, m, off, g, b, alpha):
    return _layer_norm(alpha * x + (1.0 + m[off + 2:off + 3]) * y, g, b)


def _split3(a):
    hi = a.astype(BF16)
    r1 = a - hi.astype(F32)
    mid = r1.astype(BF16)
    lo = (r1 - mid.astype(F32)).astype(BF16)
    return hi, mid, lo


def _dot(a, b):
    return jnp.dot(a, b, preferred_element_type=F32)


def _dot_split(a_hi, a_mid, w):
    n = w.shape[1]
    w_hi = w.astype(BF16)
    w_lo = (w - w_hi.astype(F32)).astype(BF16)
    both = _dot(a_hi, jnp.concatenate([w_hi, w_lo], axis=1))
    return both[:, :n] + both[:, n:] + _dot(a_mid, w_hi)


def _pack_bf16_pair(x):
    n = x.shape[1] // 2
    hi = lax.bitcast_convert_type(x[:, :n].astype(BF16).astype(F32), jnp.uint32)
    lo = lax.bitcast_convert_type(x[:, n:].astype(BF16).astype(F32), jnp.uint32)
    return hi | (lo >> 16)


def _unpack_bf16_pair(w):
    hi = lax.bitcast_convert_type(w & jnp.uint32(0xFFFF0000), F32)
    lo = lax.bitcast_convert_type(w << 16, F32)
    return jnp.concatenate([hi, lo], axis=1)


def _mod_kernel(c_ref, w_ref, b_ref, o_ref):
    c = c_ref[...]
    s = c / (1.0 + jnp.exp(-c))
    s_hi, s_mid, _ = _split3(s)
    w = w_ref[0]
    w_hi = w.astype(BF16)
    w_lo = (w - w_hi.astype(F32)).astype(BF16)
    acc = _dot(s_hi, w_hi) + _dot(s_mid, w_hi) + _dot(s_hi, w_lo)
    o_ref[0] = acc + b_ref[0]


def _modulation(c, mod_w, mod_b):
    depth, d, n = mod_w.shape
    bsz = c.shape[0]
    rows = SUBLANES
    tn = n // 2
    c_pad = jnp.zeros((rows, d), F32).at[:bsz].set(c)
    out = pl.pallas_call(
        _mod_kernel,
        grid=(depth, n // tn),
        in_specs=[
            pl.BlockSpec((rows, d), lambda l, j: (0, 0)),
            pl.BlockSpec((1, d, tn), lambda l, j: (l, 0, j)),
            pl.BlockSpec((1, 1, tn), lambda l, j: (l, 0, j)),
        ],
        out_specs=pl.BlockSpec((1, rows, tn), lambda l, j: (l, 0, j)),
        out_shape=jax.ShapeDtypeStruct((depth, rows, n), F32),
        compiler_params=_cparams(("parallel", "parallel")),
        name="adaln_mod",
    )(c_pad, mod_w, mod_b.reshape(depth, 1, n))
    return out[:, :bsz, :].reshape(depth, bsz, 6, d)


def _gmlp_kernel(x_ref, mod_ref, win_ref, vg_ref, vb_ref, ws_ref, bst_ref, wout_ref, g_ref, b_ref,
                 *rest, tm, sub, chunk, groups, alpha, n_exp, region):
    if n_exp:
        wr_ref, br_ref, o_ref, hp_ref, meta_ref, cnt_ref, gated_ref, winb_ref, woutb_ref, carry_ref = rest

        @pl.when(pl.program_id(0) == 0)
        def _():
            carry_ref[...] = jnp.zeros_like(carry_ref)
    else:
        o_ref, gated_ref, winb_ref, woutb_ref = rest

    @pl.when(pl.program_id(0) == 0)
    def _():
        winb_ref[...] = win_ref[...].astype(BF16)
        woutb_ref[...] = wout_ref[...].astype(BF16)

    m = mod_ref[0]
    row = lax.broadcasted_iota(jnp.int32, (chunk, chunk), 0)
    col = lax.broadcasted_iota(jnp.int32, (chunk, chunk), 1)
    causal = row >= col
    bst = bst_ref[...]
    w_mix = [jnp.where(causal, ws_ref[g], 0.0).astype(BF16) for g in range(groups)]
    n = tm // sub

    def project(s):
        x = x_ref[s * sub:(s + 1) * sub, :]
        return x, _dot(_modulate(x, m, 0).astype(BF16), winb_ref[...])

    def activate(z):
        z = 0.5 * z * (1.0 + lax.erf(z * (2.0 ** -0.5)))
        width = z.shape[1] // 2
        return z[:, :width], _layer_norm(z[:, width:], vg_ref[...], vb_ref[...]).astype(BF16)

    def mix(s, u, v):
        gd = u.shape[1] // groups
        for g in range(groups):
            cs = slice(g * gd, (g + 1) * gd)
            for c in range(sub // chunk):
                rs = slice(c * chunk, (c + 1) * chunk)
                mixed = _dot(w_mix[g], v[rs, cs]) + bst[:, g:g + 1]
                gated_ref[s * sub + c * chunk:s * sub + (c + 1) * chunk, cs] = (u[rs, cs] * mixed).astype(BF16)
        return _dot(gated_ref[s * sub:(s + 1) * sub, :], woutb_ref[...])

    def finish(s, x, y):
        rows = slice(s * sub, (s + 1) * sub)
        x_new = _res_ln(x, y, m, 0, g_ref[...], b_ref[...], alpha)
        o_ref[rows, :] = x_new
        if n_exp:
            hp_ref[rows, :], meta_ref[:, rows] = _route_rows(x_new, m, wr_ref, br_ref, carry_ref, n_exp, region)

    nxt = project(0)
    pending = None
    for s in range(n):
        x, z = nxt
        if s + 1 < n:
            nxt = project(s + 1)
        u, v = activate(z)
        if pending is not None:
            finish(*pending)
        pending = (s, x, mix(s, u, v))
    finish(*pending)
    if n_exp:
        cnt_ref[...] = carry_ref[...]


def _gmlp_layer(x, mod, w_in, v_g, v_b, w_s, b_s, w_out, layer, ln_g, ln_b, router=None, *, seq, alpha, tm=512,
                sub=256):
    t, d = x.shape
    groups, chunk, _ = w_s.shape
    width = w_out.shape[1]
    layer_spec = lambda r, c: pl.BlockSpec((None, r, c), lambda i: (layer, 0, 0), pipeline_mode=pl.Buffered(1))
    n_exp = router[0].shape[1] if router else 0
    kern = functools.partial(_gmlp_kernel, tm=tm, sub=sub, chunk=chunk, groups=groups, alpha=alpha, n_exp=n_exp,
                             region=t)
    row_map = lambda i: (i, 0)
    out_specs, out_shape = [pl.BlockSpec((tm, d), row_map)], [jax.ShapeDtypeStruct((t, d), F32)]
    scratch = [pltpu.VMEM((tm, width), BF16), pltpu.VMEM((d, 2 * width), BF16), pltpu.VMEM((width, d), BF16)]
    extra_specs, extra_args = [], []
    if router:
        r_specs, r_shapes = _router_out(t, d, tm, row_map)
        out_specs, out_shape = out_specs + r_specs, out_shape + r_shapes
        scratch.append(pltpu.VMEM((ROUTE_ROWS, LANES), F32))
        extra_specs = _router_specs(d, _const_spec)
        extra_args = list(_router_operands(*router))
    outs = pl.pallas_call(
        kern,
        grid=(t // tm,),
        in_specs=[
            pl.BlockSpec((tm, d), lambda i: (i, 0)),
            pl.BlockSpec((1, 6, d), lambda i: (i * tm // seq, 0, 0)),
            layer_spec(d, 2 * width),
            _const_spec((1, width)),
            _const_spec((1, width)),
            _const_spec((groups, chunk, chunk)),
            _const_spec((chunk, groups)),
            layer_spec(width, d),
            _const_spec((1, d)),
            _const_spec((1, d)),
        ] + extra_specs,
        out_specs=out_specs,
        out_shape=out_shape,
        scratch_shapes=scratch,
        compiler_params=_cparams(("arbitrary",), VMEM_LIMIT + 4 * 1024 * 1024),
        name="gmlp_mixer",
    )(x, mod, w_in, v_g.reshape(1, width), v_b.reshape(1, width), w_s, b_s.T,
      w_out, ln_g.reshape(1, d), ln_b.reshape(1, d), *extra_args)
    return outs if router else outs[0]


def _swiglu_pipelined(hs, w1, w3, w2, emit):
    up = lambda h: (_dot(h, w1), _dot(h, w3))
    nxt = up(hs[0])
    for s in range(len(hs)):
        a, b = nxt
        if s + 1 < len(hs):
            nxt = up(hs[s + 1])
        t = (a / (1.0 + jnp.exp(-a)) * b).astype(BF16)
        emit(s, _dot(t, w2))


def _ffn_kernel(x_ref, mod_ref, w1_ref, w3_ref, w2_ref, g_ref, b_ref, o_ref, h_ref, wb1_ref, wb3_ref, wb2_ref, *,
                alpha, sub):
    j = pl.program_id(1)
    last = pl.num_programs(1) - 1
    subs = [slice(s, s + sub) for s in range(0, x_ref.shape[0], sub)]

    @pl.when(pl.program_id(0) == 0)
    def _():
        wb1_ref[j] = w1_ref[...].astype(BF16)
        wb3_ref[j] = w3_ref[...].astype(BF16)
        wb2_ref[j] = w2_ref[...].astype(BF16)

    weights = lambda: (wb1_ref[j], wb3_ref[j], wb2_ref[j])

    @pl.when(j == 0)
    def _():
        hs = []
        for rows in subs:
            h = _modulate(x_ref[rows, :], mod_ref[0], 3).astype(BF16)
            h_ref[rows, :] = h
            hs.append(h)

        def emit(s, part):
            o_ref[subs[s], :] = part

        _swiglu_pipelined(hs, *weights(), emit)

    @pl.when(jnp.logical_and(j > 0, j < last))
    def _():
        def emit(s, part):
            o_ref[subs[s], :] += part

        _swiglu_pipelined([h_ref[rows, :] for rows in subs], *weights(), emit)

    @pl.when(j == last)
    def _():
        def emit(s, part):
            rows = subs[s]
            y = o_ref[rows, :] + part
            o_ref[rows, :] = _res_ln(x_ref[rows, :], y, mod_ref[0], 3, g_ref[...], b_ref[...], alpha)

        _swiglu_pipelined([h_ref[rows, :] for rows in subs], *weights(), emit)


def _ffn_layer(x, mod, w13, w2, layer, ln_g, ln_b, *, seq, alpha, tm=1024, tf=512):
    t, d = x.shape
    f = w2.shape[1]
    nf = f // tf
    assert nf >= 2
    tile = lambda i, j: jnp.where(i == 0, j, nf - 1)
    return pl.pallas_call(
        functools.partial(_ffn_kernel, alpha=alpha, sub=tm // 4),
        grid=(t // tm, nf),
        in_specs=[
            pl.BlockSpec((tm, d), lambda i, j: (i, 0)),
            pl.BlockSpec((1, 6, d), lambda i, j: (i * tm // seq, 0, 0)),
            pl.BlockSpec((None, d, tf), lambda i, j: (layer, 0, tile(i, j))),
            pl.BlockSpec((None, d, tf), lambda i, j: (layer, 0, nf + tile(i, j))),
            pl.BlockSpec((None, tf, d), lambda i, j: (layer, tile(i, j), 0)),
            _const_spec((1, d)),
            _const_spec((1, d)),
        ],
        out_specs=pl.BlockSpec((tm, d), lambda i, j: (i, 0)),
        out_shape=jax.ShapeDtypeStruct((t, d), F32),
        scratch_shapes=[pltpu.VMEM((tm, d), BF16), pltpu.VMEM((nf, d, tf), BF16), pltpu.VMEM((nf, d, tf), BF16),
                        pltpu.VMEM((nf, tf, d), BF16)],
        compiler_params=_cparams(("arbitrary", "arbitrary")),
        name="swiglu_dense",
    )(x, mod, w13, w13, w2, ln_g.reshape(1, d), ln_b.reshape(1, d))


MOE_GROUP = 2


def _moe_ffn_kernel(ge_ref, gb_ref, gr_ref, h_ref, w1_ref, w3_ref, w2_ref, o_ref, acc_ref, *, sub):
    g = pl.program_id(0)
    j = pl.program_id(1)
    last = pl.num_programs(1) - 1
    n_rows = gr_ref[g]
    nv = (n_rows + EXPERT_BLOCK - 1) // EXPERT_BLOCK

    def run(n_blocks, phase):
        subs = [slice(s, s + sub) for s in range(0, n_blocks * EXPERT_BLOCK, sub)]
        row = lax.broadcasted_iota(jnp.int32, (sub, 1), 0)

        def rows_in(rows):
            w = jnp.where(row + rows.start < n_rows, h_ref[rows, :], jnp.uint32(0))
            return _unpack_bf16_pair(w).astype(BF16)

        def emit(s, part):
            if phase == "first":
                acc_ref[subs[s], :] = part
            elif phase == "middle":
                acc_ref[subs[s], :] += part
            else:
                o_ref[subs[s], :] = _pack_bf16_pair(acc_ref[subs[s], :] + part)

        _swiglu_pipelined([rows_in(rows) for rows in subs], w1_ref[0, 0].astype(BF16), w3_ref[0, 0].astype(BF16),
                          w2_ref[0, 0].astype(BF16), emit)

    for n_blocks in range(1, MOE_GROUP + 1):
        @pl.when(jnp.logical_and(nv == n_blocks, j == 0))
        def _():
            run(n_blocks, "first")

        @pl.when(jnp.logical_and(nv == n_blocks, jnp.logical_and(j > 0, j < last)))
        def _():
            run(n_blocks, "middle")

        @pl.when(jnp.logical_and(nv == n_blocks, j == last))
        def _():
            run(n_blocks, "last")

    for blk in range(MOE_GROUP):
        @pl.when(jnp.logical_and(blk >= nv, j == last))
        def _():
            o_ref[blk * EXPERT_BLOCK:(blk + 1) * EXPERT_BLOCK, :] = jnp.zeros((EXPERT_BLOCK, o_ref.shape[1]),
                                                                              jnp.uint32)


def _moe_ffn(h_slots, group_expert, group_block, group_rows, w13, w2, layer, *, tf=512):
    half = h_slots.shape[1]
    d = 2 * half
    f = w2.shape[2]
    nf = f // tf
    assert nf >= 2
    tm = MOE_GROUP * EXPERT_BLOCK

    def jj(g, j, gr):
        return jnp.where(gr[g] > 0, j, nf - 1)

    grid_spec = pltpu.PrefetchScalarGridSpec(
        num_scalar_prefetch=3,
        grid=(group_expert.shape[0], nf),
        in_specs=[
            pl.BlockSpec((tm, half), lambda g, j, ge, gb, gr: (gb[g], 0)),
            pl.BlockSpec((1, 1, d, tf), lambda g, j, ge, gb, gr: (layer, ge[g], 0, jj(g, j, gr))),
            pl.BlockSpec((1, 1, d, tf), lambda g, j, ge, gb, gr: (layer, ge[g], 0, nf + jj(g, j, gr))),
            pl.BlockSpec((1, 1, tf, d), lambda g, j, ge, gb, gr: (layer, ge[g], jj(g, j, gr), 0)),
        ],
        out_specs=pl.BlockSpec((tm, half), lambda g, j, ge, gb, gr: (gb[g], 0)),
        scratch_shapes=[pltpu.VMEM((tm, d), F32)],
    )
    return pl.pallas_call(
        functools.partial(_moe_ffn_kernel, sub=EXPERT_BLOCK // 2),
        grid_spec=grid_spec,
        out_shape=jax.ShapeDtypeStruct(h_slots.shape, jnp.uint32),
        compiler_params=_cparams(("arbitrary", "arbitrary")),
        name="swiglu_experts",
    )(group_expert, group_block, group_rows, h_slots, w13, w13, w2)


ROUTE_ROWS = 16


def _route_rows(x_rows, m, wrt_ref, brt_ref, carry_ref, n_exp, region):
    tm = x_rows.shape[0]
    h = _modulate(x_rows, m, 3)
    h_hi, h_mid, _ = _split3(h)
    w = wrt_ref[...]
    w_hi = w.astype(BF16)
    w_lo = (w - w_hi.astype(F32)).astype(BF16)
    nt = (((1,), (1,)), ((), ()))
    logits = (lax.dot_general(w_hi, h_hi, nt, preferred_element_type=F32)
              + lax.dot_general(w_lo, h_hi, nt, preferred_element_type=F32)
              + lax.dot_general(w_hi, h_mid, nt, preferred_element_type=F32) + brt_ref[...])
    erow = lax.broadcasted_iota(jnp.int32, (ROUTE_ROWS, tm), 0)
    neg = jnp.float32(-jnp.inf)
    logits = jnp.where(erow < n_exp, logits, neg)
    v0 = jnp.max(logits, axis=0, keepdims=True)
    e0 = jnp.min(jnp.where(logits == v0, erow, ROUTE_ROWS), axis=0, keepdims=True)
    rest = jnp.where(erow == e0, neg, logits)
    v1 = jnp.max(rest, axis=0, keepdims=True)
    e1 = jnp.min(jnp.where(rest == v1, erow, ROUTE_ROWS), axis=0, keepdims=True)
    p = jnp.exp(v1 - v0)
    g0 = 1.0 / (1.0 + p)
    g1 = p / (1.0 + p)
    oh0 = erow == e0
    oh1 = erow == e1
    onehot = jnp.where(jnp.logical_or(oh0, oh1), 1.0, 0.0).astype(BF16)
    r = lax.broadcasted_iota(jnp.int32, (tm, tm), 0)
    c = lax.broadcasted_iota(jnp.int32, (tm, tm), 1)
    earlier = jnp.where(r < c, 1.0, 0.0).astype(BF16)
    carry = carry_ref[:, 0:1]
    before = _dot(onehot, earlier) + carry
    rank0 = jnp.sum(jnp.where(oh0, before, 0.0), axis=0, keepdims=True)
    rank1 = jnp.sum(jnp.where(oh1, before, 0.0), axis=0, keepdims=True)
    total = carry + jnp.sum(onehot.astype(F32), axis=1, keepdims=True)
    carry_ref[...] = jnp.broadcast_to(total, carry_ref.shape)
    mrow = lax.broadcasted_iota(jnp.int32, (SUBLANES, tm), 0)
    e0f = e0.astype(F32)
    e1f = e1.astype(F32)
    meta = jnp.where(mrow == 0, e0f, 0.0)
    meta = jnp.where(mrow == 1, e1f, meta)
    meta = jnp.where(mrow == 2, g0, meta)
    meta = jnp.where(mrow == 3, g1, meta)
    meta = jnp.where(mrow == 4, e0f * region + rank0, meta)
    meta = jnp.where(mrow == 5, e1f * region + rank1, meta)
    return _pack_bf16_pair(h), meta


def _router_operands(w_router, b_router):
    d, n_exp = w_router.shape
    wrt = jnp.zeros((ROUTE_ROWS, d), F32).at[:n_exp].set(w_router.T)
    brt = jnp.zeros((ROUTE_ROWS, 1), F32).at[:n_exp, 0].set(b_router)
    return wrt, brt


def _router_specs(d, const):
    return [const((ROUTE_ROWS, d)), const((ROUTE_ROWS, 1))]


def _router_out(t, d, tm, row_map):
    specs = [pl.BlockSpec((tm, d // 2), row_map),
             pl.BlockSpec((SUBLANES, tm), lambda *a: (0, row_map(*a)[0])),
             pl.BlockSpec((ROUTE_ROWS, LANES), lambda *_: (0, 0))]
    shapes = [jax.ShapeDtypeStruct((t, d // 2), jnp.uint32), jax.ShapeDtypeStruct((SUBLANES, t), F32),
              jax.ShapeDtypeStruct((ROUTE_ROWS, LANES), F32)]
    return specs, shapes


def _combine_kernel(x_ref, ya_ref, yb_ref, meta_ref, mod_ref, g_ref, b_ref, o_ref, *, alpha):
    record = meta_ref[...]
    cols = jnp.concatenate([record, jnp.zeros((LANES - record.shape[0], record.shape[1]), F32)], axis=0).T
    y = cols[:, 2:3] * _unpack_bf16_pair(ya_ref[...]) + cols[:, 3:4] * _unpack_bf16_pair(yb_ref[...])
    o_ref[...] = _res_ln(x_ref[...], y, mod_ref[0], 3, g_ref[...], b_ref[...], alpha)


def _combine(x, ya, yb, meta, mod, ln_g, ln_b, *, seq, alpha, tm=1024):
    t, d = x.shape
    row = pl.BlockSpec((tm, d), lambda i: (i, 0))
    packed = pl.BlockSpec((tm, d // 2), lambda i: (i, 0))
    return pl.pallas_call(
        functools.partial(_combine_kernel, alpha=alpha),
        grid=(t // tm,),
        in_specs=[row, packed, packed,
                  pl.BlockSpec((SUBLANES, tm), lambda i: (0, i)),
                  pl.BlockSpec((1, 6, d), lambda i: (i * tm // seq, 0, 0)),
                  _const_spec((1, d)), _const_spec((1, d))],
        out_specs=row,
        out_shape=jax.ShapeDtypeStruct((t, d), F32),
        compiler_params=_cparams(("parallel",)),
        name="moe_combine",
    )(x, ya, yb, meta, mod, ln_g.reshape(1, d), ln_b.reshape(1, d))


def _sc_mesh():
    return plsc.VectorSubcoreMesh(core_axis_name="c", subcore_axis_name="s", num_cores=SC_CORES,
                                  num_subcores=SC_SUBCORES)


def _sc_worker_share(n):
    workers = SC_CORES * SC_SUBCORES
    per_worker = n // workers
    steps = per_worker // SC_ROWS
    assert steps * SC_ROWS * workers == n
    return per_worker, steps


def _sc_gather_rows(table, idx_a, idx_b):
    n = idx_a.shape[0]
    width = table.shape[1]
    per_worker, steps = _sc_worker_share(n)

    def body(table_hbm, ia_hbm, ib_hbm, oa_hbm, ob_hbm, ia_v, ib_v, rows_v, sem):
        wid = lax.axis_index("s") * SC_CORES + lax.axis_index("c")
        base = pl.multiple_of(wid * per_worker, SC_ROWS)
        pltpu.sync_copy(ia_hbm.at[pl.ds(base, per_worker)], ia_v)
        pltpu.sync_copy(ib_hbm.at[pl.ds(base, per_worker)], ib_v)

        @pl.loop(0, steps)
        def _(j):
            lo = pl.multiple_of(j * SC_ROWS, SC_ROWS)
            for idx_v, out_hbm in ((ia_v, oa_hbm), (ib_v, ob_hbm)):
                pltpu.async_copy(table_hbm.at[idx_v.at[pl.ds(lo, SC_ROWS)]], rows_v, sem).wait()
                pltpu.sync_copy(rows_v, out_hbm.at[pl.ds(base + lo, SC_ROWS)])

    out = jax.ShapeDtypeStruct((n, width), table.dtype)
    return pl.kernel(
        body,
        out_type=(out, out),
        mesh=_sc_mesh(),
        scratch_types=[
            pltpu.VMEM((per_worker,), jnp.int32),
            pltpu.VMEM((per_worker,), jnp.int32),
            pltpu.VMEM((SC_ROWS, width), table.dtype),
            pltpu.SemaphoreType.DMA,
        ],
        name="sc_row_gather",
    )(table, idx_a, idx_b)


def _sc_scatter_rows(rows, dest_a, dest_b, n_out):
    n, width = rows.shape
    per_worker, steps = _sc_worker_share(n)

    def body(rows_hbm, da_hbm, db_hbm, out_hbm, ia_v, ib_v, rows_v, sem):
        wid = lax.axis_index("s") * SC_CORES + lax.axis_index("c")

        @pl.loop(0, steps)
        def _(j):
            off = pl.multiple_of(wid * per_worker + j * SC_ROWS, SC_ROWS)
            pltpu.sync_copy(da_hbm.at[pl.ds(off, SC_ROWS)], ia_v)
            pltpu.sync_copy(db_hbm.at[pl.ds(off, SC_ROWS)], ib_v)
            pltpu.sync_copy(rows_hbm.at[pl.ds(off, SC_ROWS)], rows_v)
            pltpu.async_copy(rows_v, out_hbm.at[ia_v], sem).wait()
            pltpu.async_copy(rows_v, out_hbm.at[ib_v], sem).wait()

    return pl.kernel(
        body,
        out_type=jax.ShapeDtypeStruct((n_out, width), rows.dtype),
        mesh=_sc_mesh(),
        scratch_types=[
            pltpu.VMEM((SC_ROWS,), jnp.int32),
            pltpu.VMEM((SC_ROWS,), jnp.int32),
            pltpu.VMEM((SC_ROWS, width), rows.dtype),
            pltpu.SemaphoreType.DMA,
        ],
        name="sc_row_scatter",
    )(rows, dest_a, dest_b)


def _moe_layer(routed, mod, w13, w2, layer, ln_g, ln_b, *, seq, alpha):
    x, h, meta, cnt = routed
    t, d = x.shape
    n_exp = w13.shape[1]
    dest0 = meta[4].astype(jnp.int32)
    dest1 = meta[5].astype(jnp.int32)
    counts = cnt[:n_exp, 0].astype(jnp.int32)
    group_rows = MOE_GROUP * EXPERT_BLOCK
    assert t % group_rows == 0
    region_groups = t // group_rows
    groups_per_expert = (counts + group_rows - 1) // group_rows
    group_end = jnp.cumsum(groups_per_expert)
    g = jnp.arange(t * TOP_K // group_rows + n_exp, dtype=jnp.int32)
    expert = jnp.minimum(jnp.sum(group_end[None, :] <= g[:, None], axis=1), n_exp - 1).astype(jnp.int32)
    blk_in_expert = g - (group_end - groups_per_expert)[expert]
    used = g < group_end[-1]
    spare_block = n_exp * region_groups
    group_block = jnp.where(used, expert * region_groups + blk_in_expert, spare_block).astype(jnp.int32)
    rows_in_group = jnp.where(used, jnp.clip(counts[expert] - blk_in_expert * group_rows, 0, group_rows),
                              0).astype(jnp.int32)
    h_slots = _sc_scatter_rows(h, dest0, dest1, (spare_block + 1) * group_rows)
    y_slots = _moe_ffn(h_slots, expert, group_block, rows_in_group, w13, w2, layer)
    ya, yb = _sc_gather_rows(y_slots, dest0, dest1)
    return _combine(x, ya, yb, meta, mod, ln_g, ln_b, seq=seq, alpha=alpha)


HEAD_PAD = LANES
GATE_PARTS = 3
GATE_ROWS = 16
VAL_PAD = 16
LOG2E = 1.4426950408889634


def _head_lanes(h, hd):
    k0 = (h % (HEAD_PAD // hd)) * hd
    return k0, (hd if k0 == 0 else 0)


def _fox_proj_kernel(x_ref, mod_ref, wf_ref, bf_ref, wq_ref, wk_ref, wv_ref, selk_ref, onek_ref, selq_ref,
                     oneq_ref, onev_ref, k_ref, q_ref, v_ref, carry_ref, *, q_scale, heads, hd, nb):
    tm = x_ref.shape[0]

    @pl.when(pl.program_id(0) % nb == 0)
    def _():
        carry_ref[...] = jnp.zeros_like(carry_ref)

    h_hi, h_mid, _ = _split3(_modulate(x_ref[...], mod_ref[0], 0))

    f = _dot_split(h_hi, h_mid, wf_ref[...]) + bf_ref[...]
    lf = jnp.minimum(f, 0.0) - jnp.log(1.0 + jnp.exp(-jnp.abs(f)))
    r = lax.broadcasted_iota(jnp.int32, (tm, tm), 0)
    c = lax.broadcasted_iota(jnp.int32, (tm, tm), 1)
    tri = jnp.where(r >= c, 1.0, 0.0).astype(BF16)
    sums = _dot(tri, jnp.concatenate(_split3(lf), axis=1))
    cs = carry_ref[0:1, :] + sum(sums[:, n * LANES:(n + 1) * LANES] for n in range(GATE_PARTS))
    carry_ref[...] = jnp.broadcast_to(cs[tm - 1:tm, :], carry_ref.shape)
    cs = cs * LOG2E

    gate_k = onek_ref[...] + _dot(jnp.concatenate(_split3(cs), axis=1), selk_ref[...])
    kc = _dot(h_hi, wk_ref[...])
    lane = lax.broadcasted_iota(jnp.int32, (tm, HEAD_PAD), 1)
    for h in range(heads):
        k0, _ = _head_lanes(h, hd)
        src = (h * hd // HEAD_PAD) * HEAD_PAD
        is_k = jnp.logical_and(lane >= k0, lane < k0 + hd)
        hs = slice(h * HEAD_PAD, (h + 1) * HEAD_PAD)
        k_ref[:, hs] = jnp.where(is_k, kc[:, src:src + HEAD_PAD], gate_k[:, hs]).astype(BF16)

    nt = (((1,), (1,)), ((), ()))
    qc = lax.dot_general(wq_ref[...], h_hi, nt, preferred_element_type=F32) * (q_scale * LOG2E)
    gate_q = oneq_ref[...] + _dot(selq_ref[...], jnp.concatenate(_split3(cs.T), axis=0))
    fill = HEAD_PAD - hd - GATE_ROWS
    for h in range(heads):
        k0, g0 = _head_lanes(h, hd)
        base = h * HEAD_PAD
        q_ref[0, base + k0:base + k0 + hd, :] = qc[h * hd:(h + 1) * hd, :].astype(BF16)
        q_ref[0, base + g0:base + g0 + GATE_ROWS, :] = gate_q[h * GATE_ROWS:(h + 1) * GATE_ROWS, :].astype(BF16)
        q_ref[0, base + g0 + GATE_ROWS:base + g0 + GATE_ROWS + fill, :] = jnp.zeros((fill, tm), BF16)

    v = lax.dot_general(wv_ref[...], h_hi, nt, preferred_element_type=F32) + onev_ref[...]
    v_ref[0] = v.astype(BF16)


def _fox_proj(x, mod, w_in, b_f, *, bsz, seq, heads, tm=512):
    t, d = x.shape
    hd = d // heads
    assert HEAD_PAD == 2 * hd and GATE_ROWS >= 2 * GATE_PARTS
    nb = seq // tm
    wide = heads * HEAD_PAD
    wq_t = w_in[:, :d].T.astype(BF16)
    wk = w_in[:, d:2 * d].astype(BF16)
    vrows = heads * (hd + VAL_PAD)
    wv_t = jnp.pad(w_in[:, 2 * d:3 * d].reshape(d, heads, hd),
                   ((0, 0), (0, 0), (0, VAL_PAD))).reshape(d, vrows).T.astype(BF16)
    wf = jnp.zeros((d, LANES), F32).at[:, :heads].set(w_in[:, 3 * d:])
    bf = jnp.zeros((1, LANES), F32).at[0, :heads].set(b_f)
    sel_k = np.zeros((GATE_PARTS, LANES, wide), np.float32)
    one_k = np.zeros((1, wide), np.float32)
    sel_q = np.zeros((GATE_PARTS, heads * GATE_ROWS, LANES), np.float32)
    one_q = np.zeros((heads * GATE_ROWS, 1), np.float32)
    one_v = np.zeros((vrows, 1), np.float32)
    for hh in range(heads):
        _, g0 = _head_lanes(hh, hd)
        one_v[hh * (hd + VAL_PAD) + hd, 0] = 1.0
        for n in range(GATE_PARTS):
            sel_k[n, hh, hh * HEAD_PAD + g0 + n] = -1.0
            one_q[hh * GATE_ROWS + n, 0] = 1.0
            one_k[0, hh * HEAD_PAD + g0 + GATE_PARTS + n] = 1.0
            sel_q[n, hh * GATE_ROWS + GATE_PARTS + n, hh] = 1.0
    return pl.pallas_call(
        functools.partial(_fox_proj_kernel, q_scale=hd ** -0.5, heads=heads, hd=hd, nb=nb),
        grid=(t // tm,),
        in_specs=[
            pl.BlockSpec((tm, d), lambda i: (i, 0)),
            pl.BlockSpec((1, 6, d), lambda i: (i * tm // seq, 0, 0)),
            _const_spec((d, LANES)),
            _const_spec((1, LANES)),
            _const_spec((d, d)),
            _const_spec((d, d)),
            _const_spec((vrows, d)),
            _const_spec((GATE_PARTS * LANES, wide)),
            _const_spec((1, wide)),
            _const_spec((heads * GATE_ROWS, GATE_PARTS * LANES)),
            _const_spec((heads * GATE_ROWS, 1)),
            _const_spec((vrows, 1)),
        ],
        out_specs=[
            pl.BlockSpec((tm, wide), lambda i: (i, 0)),
            pl.BlockSpec((1, wide, tm), lambda i: (i // nb, 0, i % nb)),
            pl.BlockSpec((1, vrows, tm), lambda i: (i // nb, 0, i % nb)),
        ],
        out_shape=[
            jax.ShapeDtypeStruct((t, wide), BF16),
            jax.ShapeDtypeStruct((bsz, wide, seq), BF16),
            jax.ShapeDtypeStruct((bsz, vrows, seq), BF16),
        ],
        scratch_shapes=[pltpu.VMEM((SUBLANES, LANES), F32)],
        compiler_params=_cparams(("arbitrary",)),
        name="fox_proj",
    )(x, mod, wf, bf, wq_t, wk, wv_t, jnp.asarray(sel_k.reshape(GATE_PARTS * LANES, wide), BF16), jnp.asarray(one_k),
      jnp.asarray(np.concatenate(list(sel_q), axis=1), BF16), jnp.asarray(one_q), jnp.asarray(one_v))


def _flash_kernel(qi_ref, kj_ref, k_ref, q_ref, v_ref, x_ref, mod_ref, wout_ref, g_ref, b_ref, *rest,
                  heads, hd, blk, alpha, n_exp, region):
    p_idx = pl.program_id(1)
    if n_exp:
        wr_ref, br_ref, o_ref, hp_ref, meta_ref, cnt_ref, m_ref, acc_ref, s_ref, carry_ref = rest

        @pl.when(jnp.logical_and(pl.program_id(0) == 0, p_idx == 0))
        def _():
            carry_ref[...] = jnp.zeros_like(carry_ref)
    else:
        o_ref, m_ref, acc_ref, s_ref = rest
    qi = qi_ref[p_idx]
    kj = kj_ref[p_idx]
    vp = hd + VAL_PAD

    @pl.when(kj == 0)
    def _():
        m_ref[...] = jnp.full(m_ref.shape, -jnp.inf, F32)
        acc_ref[...] = jnp.zeros_like(acc_ref)

    def step(diagonal):
        half = blk // 2
        tiles = [(slice(0, half), slice(0, half)), (slice(half, blk), slice(0, blk))] if diagonal else \
                [(slice(0, blk), slice(0, blk))]
        if diagonal:
            causal = lambda rows, first_q: (lax.broadcasted_iota(jnp.int32, (rows, half), 0)
                                            <= lax.broadcasted_iota(jnp.int32, (rows, half), 1) + first_q)
            keep = [causal(half, 0), causal(blk, half)]

        def scores(h):
            hs = slice(h * HEAD_PAD, (h + 1) * HEAD_PAD)
            tops = []
            for n, (qs, ks) in enumerate(tiles):
                s = _dot(k_ref[ks, hs], q_ref[0, hs, qs])
                if diagonal:
                    s = jnp.where(keep[n], s, -jnp.inf)
                s_ref[h % depth, ks, qs] = s
                tops.append(jnp.max(s, axis=0, keepdims=True))
            return tops

        depth = s_ref.shape[0]
        ahead = [scores(h) for h in range(depth - 1)]
        for h in range(heads):
            if h + depth - 1 < heads:
                ahead.append(scores(h + depth - 1))
            rs = slice(h * vp, (h + 1) * vp)
            for (qs, ks), m_cur in zip(tiles, ahead.pop(0)):
                m_prev = m_ref[h:h + 1, qs]
                m_new = jnp.maximum(m_prev, m_cur)
                m_ref[h:h + 1, qs] = m_new
                p = jnp.exp2(s_ref[h % depth, ks, qs] - m_new).astype(BF16)
                acc_ref[rs, qs] = acc_ref[rs, qs] * jnp.exp2(m_prev - m_new) + _dot(v_ref[0, rs, ks], p)

    @pl.when(kj < qi)
    def _():
        step(False)

    @pl.when(kj == qi)
    def _():
        step(True)
        per = LANES // hd
        cols = []
        for g in range(heads // per):
            parts = []
            for n in range(per):
                r0 = (g * per + n) * vp
                parts.append(acc_ref[r0:r0 + hd, :] / acc_ref[r0 + hd:r0 + hd + 1, :])
            cols.append(jnp.concatenate(parts, axis=0).T.astype(BF16))
        y = _dot(jnp.concatenate(cols, axis=1), wout_ref[...])
        x_new = _res_ln(x_ref[...], y, mod_ref[0], 0, g_ref[...], b_ref[...], alpha)
        o_ref[...] = x_new
        if n_exp:
            hp_ref[...], meta_ref[...] = _route_rows(x_new, mod_ref[0], wr_ref, br_ref, carry_ref, n_exp, region)
            cnt_ref[...] = carry_ref[...]


def _flash(k_aug, q_aug_t, v_t, x, mod, w_out, ln_g, ln_b, router=None, *, bsz, seq, heads, alpha, blk=512):
    t, wide = k_aug.shape
    vrows = v_t.shape[1]
    hd = vrows // heads - VAL_PAD
    d = heads * hd
    nq = seq // blk
    pairs = [(i, j) for i in range(nq) for j in range(i + 1)]
    qi = jnp.asarray(np.array([p[0] for p in pairs], np.int32))
    kj = jnp.asarray(np.array([p[1] for p in pairs], np.int32))
    n_exp = router[0].shape[1] if router else 0
    row_map = lambda b, p, qi, kj: (b * nq + qi[p], 0)
    const = lambda shape: pl.BlockSpec(shape, lambda b, p, qi, kj: (0, 0), pipeline_mode=pl.Buffered(1))
    out_specs, out_shape = [pl.BlockSpec((blk, d), row_map)], [jax.ShapeDtypeStruct((t, d), F32)]
    scratch = [pltpu.VMEM((heads, blk), F32), pltpu.VMEM((vrows, blk), F32), pltpu.VMEM((3, blk, blk), F32)]
    extra_specs, extra_args = [], []
    if router:
        r_specs, r_shapes = _router_out(t, d, blk, row_map)
        out_specs, out_shape = out_specs + r_specs, out_shape + r_shapes
        scratch.append(pltpu.VMEM((ROUTE_ROWS, LANES), F32))
        extra_specs = _router_specs(d, const)
        extra_args = list(_router_operands(*router))
    grid_spec = pltpu.PrefetchScalarGridSpec(
        num_scalar_prefetch=2,
        grid=(bsz, len(pairs)),
        in_specs=[
            pl.BlockSpec((blk, wide), lambda b, p, qi, kj: (b * nq + kj[p], 0)),
            pl.BlockSpec((1, wide, blk), lambda b, p, qi, kj: (b, 0, qi[p])),
            pl.BlockSpec((1, vrows, blk), lambda b, p, qi, kj: (b, 0, kj[p])),
            pl.BlockSpec((blk, d), row_map),
            pl.BlockSpec((1, 6, d), lambda b, p, qi, kj: (b, 0, 0)),
            const((d, d)),
            const((1, d)),
            const((1, d)),
        ] + extra_specs,
        out_specs=out_specs,
        scratch_shapes=scratch,
    )
    outs = pl.pallas_call(
        functools.partial(_flash_kernel, heads=heads, hd=hd, blk=blk, alpha=alpha, n_exp=n_exp, region=t),
        grid_spec=grid_spec,
        out_shape=out_shape,
        compiler_params=_cparams(("arbitrary", "arbitrary")),
        name="fox_attention",
    )(qi, kj, k_aug, q_aug_t, v_t, x, mod, w_out.astype(BF16), ln_g.reshape(1, d), ln_b.reshape(1, d), *extra_args)
    return outs if router else outs[0]


def _fox_layer(x, mod, w_in, b_f, w_out, ln_g, ln_b, router=None, *, bsz, seq, alpha):
    heads = b_f.shape[0]
    k_aug, q_aug_t, v_t = _fox_proj(x, mod, w_in, b_f, bsz=bsz, seq=seq, heads=heads)
    return _flash(k_aug, q_aug_t, v_t, x, mod, w_out, ln_g, ln_b, router, bsz=bsz, seq=seq, heads=heads,
                  alpha=alpha)


def _pool_kernel(x_ref, xh_ref, mod_ref, win_ref, wg_ref, ls_ref, wout_ref, g_ref, b_ref, o_ref, pooled_ref,
                 *, tm, sub, seq, alpha):
    pos0 = (pl.program_id(0) * tm) % seq
    m = mod_ref[0]
    n = tm // sub
    halo_ok = jnp.where(pos0 > 0, jnp.float32(1.0), jnp.float32(0.0))
    rows = lax.broadcasted_iota(jnp.int32, (sub + POOL_HALO, 1), 0)

    def project(s):
        x = x_ref[s * sub:(s + 1) * sub, :]
        halo = xh_ref[...] if s == 0 else x_ref[s * sub - POOL_HALO:s * sub, :]
        ze = _dot(_modulate(jnp.concatenate([halo, x], axis=0), m, 0).astype(BF16), win_ref[...])
        if s == 0:
            ze = ze * jnp.where(rows < POOL_HALO, halo_ok, jnp.float32(1.0))
        return x, ze

    def pool(s, ze):
        pos = (pos0 + s * sub + lax.broadcasted_iota(jnp.int32, (sub, 1), 0) + 1).astype(F32)
        gd = ze.shape[1] // len(POOL_WINDOWS)
        out_rows = slice(s * sub, (s + 1) * sub)
        for g, win in enumerate(POOL_WINDOWS):
            cs = slice(g * gd, (g + 1) * gd)
            zg = ze[:, cs]
            acc = zg
            span = 1
            while span < win:
                acc = acc + pltpu.roll(acc, span, 0)
                span *= 2
            count = jnp.minimum(pos, jnp.float32(win))
            pooled = acc[POOL_HALO:, :] / count - zg[POOL_HALO:, :]
            pooled_ref[out_rows, cs] = (_dot(pooled.astype(BF16), wg_ref[g]) * ls_ref[:, cs]).astype(BF16)
        return _dot(pooled_ref[out_rows, :], wout_ref[...])

    nxt = project(0)
    for s in range(n):
        x, ze = nxt
        if s + 1 < n:
            nxt = project(s + 1)
        y = pool(s, ze)
        o_ref[s * sub:(s + 1) * sub, :] = _res_ln(x, y, m, 0, g_ref[...], b_ref[...], alpha)


def _pool_layer(x, mod, w_in, w_grp, scale, w_out, ln_g, ln_b, *, seq, alpha, tm=1024, sub=256):
    t, d = x.shape
    ng, gd, _ = w_grp.shape
    hb = tm // POOL_HALO
    return pl.pallas_call(
        functools.partial(_pool_kernel, tm=tm, sub=sub, seq=seq, alpha=alpha),
        grid=(t // tm,),
        in_specs=[
            pl.BlockSpec((tm, d), lambda i: (i, 0)),
            pl.BlockSpec((POOL_HALO, d), lambda i: (jnp.maximum(i * hb - 1, 0), 0)),
            pl.BlockSpec((1, 6, d), lambda i: (i * tm // seq, 0, 0)),
            _const_spec((d, d)),
            _const_spec((ng, gd, gd)),
            _const_spec((1, d)),
            _const_spec((d, d)),
            _const_spec((1, d)),
            _const_spec((1, d)),
        ],
        out_specs=pl.BlockSpec((tm, d), lambda i: (i, 0)),
        out_shape=jax.ShapeDtypeStruct((t, d), F32),
        scratch_shapes=[pltpu.VMEM((tm, d), BF16)],
        compiler_params=_cparams(("parallel",)),
        name="pool_mixer",
    )(x, x, mod, w_in.astype(BF16), w_grp.astype(BF16), scale.reshape(1, d), w_out.astype(BF16),
      ln_g.reshape(1, d), ln_b.reshape(1, d))


def kernel(x, c, mod_w, mod_b, ln_g, ln_b, gm_w_in, gm_ln_g, gm_ln_b, gm_w_s, gm_b_s, gm_w_out, fox_w_in,
           fox_b_f, fox_w_out, pool_w_in, pool_w_grp, pool_scale, pool_w_out, ffn_w13, ffn_w2, moe_w_router,
           moe_b_router, moe_w13, moe_w2):
    bsz, seq, d = x.shape
    depth = mod_w.shape[0]
    alpha = (2 * depth) ** 0.25
    n_mixers = 3
    mod_all = _modulation(c, mod_w, mod_b)
    xt = x.reshape(bsz * seq, d)
    for i in range(depth):
        mod = mod_all[i]
        kind, j = i % n_mixers, i // n_mixers
        router = (moe_w_router[i // 2], moe_b_router[i // 2]) if i % 2 else None
        if kind == 0:
            xt = _gmlp_layer(xt, mod, gm_w_in, gm_ln_g[j], gm_ln_b[j], gm_w_s[j], gm_b_s[j], gm_w_out, j,
                             ln_g[i, 0], ln_b[i, 0], router, seq=seq, alpha=alpha)
        elif kind == 1:
            xt = _fox_layer(xt, mod, fox_w_in[j], fox_b_f[j], fox_w_out[j], ln_g[i, 0], ln_b[i, 0], router,
                            bsz=bsz, seq=seq, alpha=alpha)
        else:
            assert router is None, "the pooling mixer has no routing epilogue"
            xt = _pool_layer(xt, mod, pool_w_in[j], pool_w_grp[j], pool_scale[j], pool_w_out[j],
                             ln_g[i, 0], ln_b[i, 0], seq=seq, alpha=alpha)
        if router is None:
            xt = _ffn_layer(xt, mod, ffn_w13, ffn_w2, i // 2, ln_g[i, 1], ln_b[i, 1],
                            seq=seq, alpha=alpha)
        else:
            xt = _moe_layer(xt, mod, moe_w13, moe_w2, i // 2, ln_g[i, 1], ln_b[i, 1], seq=seq, alpha=alpha)
    return xt.reshape(bsz, seq, d)
```
